```python
import jax, jax.numpy as jnp
from jax import lax
import numpy as np

D_MODEL = 2048
BATCH = 16
SEQ = 2048
DEPTH = 4
DEC_BATCH = 32
DEC_SEQ = 32
PAST_LEN = 1024

CHUNK = 64
N_A_LAYERS = DEPTH // 2
N_B_LAYERS = DEPTH - N_A_LAYERS
RW_HEAD = 64
RW_HEADS = D_MODEL // RW_HEAD
RW_DECAY_LORA = 96
RW_AAA_LORA = 96
RW_MV_LORA = 64
RW_GATE_LORA = 256
GN_EPS = 64e-5
MLA_HEADS = 32
Q_RANK = 512
KV_RANK = 512
NOPE_DIM = 128
ROPE_DIM = 64
V_DIM = 128
ROPE_THETA = 10000.0
MLA_SCALE = (NOPE_DIM + ROPE_DIM) ** -0.5
D_FF = 4 * D_MODEL
Q_BLOCK = 128
NORM_EPS = 1e-6
N_MOD = 6

kernel_name = 'yoco_rwkv7_mla_stream_encoder'


def rmsnorm(x, g):
    xf = x.astype(jnp.float32)
    y = xf * lax.rsqrt(jnp.mean(xf * xf, axis=-1, keepdims=True) + NORM_EPS)
    return (y * g.astype(jnp.float32)).astype(x.dtype)


def modulate(h, shift, scale):
    return h * (1 + scale[:, None, :]) + shift[:, None, :]


def rope(x, pos):
    half = ROPE_DIM // 2
    inv = ROPE_THETA ** (-jnp.arange(half, dtype=jnp.float32) / half)
    ang = pos.astype(jnp.float32)[:, None] * inv[None, :]
    shape = (ang.shape[0],) + (1,) * (x.ndim - 3) + (half,)
    cos, sin = jnp.cos(ang).reshape(shape), jnp.sin(ang).reshape(shape)
    xf = x.astype(jnp.float32)
    x1, x2 = xf[..., :half], xf[..., half:]
    return jnp.concatenate([x1 * cos - x2 * sin, x2 * cos + x1 * sin], axis=-1).astype(x.dtype)


def rwkv7_time_mix(h, h_prev, s0, v_first, p, v_lora):
    B, T, D = h.shape
    xx = jnp.concatenate([h_prev[:, None, :].astype(h.dtype), h[:, :-1, :]], axis=1) - h
    mu = p['mu']
    r = (h + xx * mu[0]) @ p['wr']
    w = -jax.nn.softplus(-(p['w0'] + jnp.tanh((h + xx * mu[1]) @ p['w1']) @ p['w2'])) - 0.5
    k = (h + xx * mu[2]) @ p['wk']
    xv = h + xx * mu[3]
    v = xv @ p['wv']
    if v_lora is None:
        v_first = v
    else:
        v0, v1, v2 = v_lora
        v = v + (v_first - v) * jax.nn.sigmoid(v0 + (xv @ v1) @ v2)
    a = jax.nn.sigmoid(p['a0'] + ((h + xx * mu[4]) @ p['a1']) @ p['a2'])
    g = jax.nn.sigmoid((h + xx * mu[5]) @ p['g1']) @ p['g2']
    heads = lambda t: t.astype(jnp.float32).reshape(B, T, RW_HEADS, RW_HEAD)
    kk = heads(k * p['k_k'])
    kk = kk / jnp.maximum(jnp.sqrt(jnp.sum(kk * kk, axis=-1, keepdims=True)), 1e-12)
    k = k * (1 + (a - 1) * p['k_a'])
    decay = jnp.exp(-jnp.exp(heads(w)))
    tb = lambda t: jnp.swapaxes(t, 0, 1)
    xs = (tb(heads(r)), tb(decay), tb(heads(k)), tb(heads(v)), tb(kk), tb(kk * heads(a)))

    def step(S, inp):
        r_t, d_t, k_t, v_t, kk_t, b_t = inp
        sa = jnp.einsum('bhvk,bhk->bhv', S, kk_t)
        S = S * d_t[:, :, None, :] - sa[..., None] * b_t[:, :, None, :] + v_t[..., None] * k_t[:, :, None, :]
        return S, jnp.einsum('bhvk,bhk->bhv', S, r_t)

    s_fin, y = lax.scan(step, s0.astype(jnp.float32), xs)
    y = tb(y)
    mean = jnp.mean(y, -1, keepdims=True)
    var = jnp.mean(jnp.square(y - mean), -1, keepdims=True)
    yn = ((y - mean) * lax.rsqrt(var + GN_EPS)).reshape(B, T, D) * p['lnx_g'] + p['lnx_b']
    bonus = jnp.sum(heads(r * k * p['r_k']), -1, keepdims=True) * heads(v)
    out = ((yn + bonus.reshape(B, T, D)).astype(h.dtype) * g) @ p['wo']
    return out, v_first, s_fin, h[:, -1, :]


def sq_relu_mlp(h, w1, w2):
    return jnp.square(jax.nn.relu(h @ w1)) @ w2


def shared_latent(x, cs, pos, W):
    mod = cs @ W['kv_ada_w'] + W['kv_ada_b']
    h = modulate(rmsnorm(x, W['kv_norm_g']), mod[:, :D_MODEL], mod[:, D_MODEL:])
    kv = h @ W['kv_wd']
    ckv = rmsnorm(kv[..., :KV_RANK], W['kv_lat_g'])
    kpe = rope(kv[..., KV_RANK:], pos)
    return ckv, kpe


def mla_queries(h, pos, wdq, q_g, wuq):
    B, T, _ = h.shape
    cq = rmsnorm(h @ wdq, q_g)
    q = (cq @ wuq).reshape(B, T, MLA_HEADS, NOPE_DIM + ROPE_DIM)
    return q[..., :NOPE_DIM], rope(q[..., NOPE_DIM:], pos)


def latent_attention(q_nope, q_pe, q_pos, ckv, kpe, k_pos, w_uk, w_uv):
    q_lat = jnp.einsum('bqhd,chd->bqhc', q_nope, w_uk)
    s = jnp.einsum('bqhc,bkc->bhqk', q_lat, ckv) + jnp.einsum('bqhr,bkr->bhqk', q_pe, kpe)
    s = s.astype(jnp.float32) * MLA_SCALE
    visible = (k_pos[None, :] // CHUNK) <= (q_pos[:, None] // CHUNK)
    s = jnp.where(visible[None, None], s, -jnp.inf)
    pr = jax.nn.softmax(s, axis=-1).astype(ckv.dtype)
    o_lat = jnp.einsum('bhqk,bkc->bqhc', pr, ckv)
    return jnp.einsum('bqhc,chd->bqhd', o_lat, w_uv)


def prompt_attention(q_nope, q_pe, ckv, kpe, w_uk, w_uv):
    B, T = q_nope.shape[:2]
    nb = T // Q_BLOCK
    pos = jnp.arange(T)
    blocks = lambda t: jnp.moveaxis(t.reshape((B, nb, Q_BLOCK) + t.shape[2:]), 1, 0)

    def one(args):
        qn, qp, qpos = args
        return latent_attention(qn, qp, qpos, ckv, kpe, pos, w_uk, w_uv)

    o = lax.map(one, (blocks(q_nope), blocks(q_pe), pos.reshape(nb, Q_BLOCK)))
    return jnp.moveaxis(o, 0, 1).reshape(B, T, MLA_HEADS, V_DIM)


def trunk(x, c, pos, h_prev, s0, past_ckv, past_kpe, W):
    B, T, D = x.shape
    cs = jax.nn.silu(c)
    v_first = None
    shifts, states = [], []
    ckv = kpe = keys_ckv = keys_kpe = k_pos = None
    for l in range(DEPTH):
        mod = (cs @ W['ada_w'][l] + W['ada_b'][l]).reshape(B, N_MOD, D)
        h = modulate(rmsnorm(x, W['norm_mix_g'][l]), mod[:, 0], mod[:, 1])
        if l < N_A_LAYERS:
            p = dict(mu=W['rw_mu'][l], w0=W['rw_w0'][l], w1=W['rw_w1'][l], w2=W['rw_w2'][l],
                     a0=W['rw_a0'][l], a1=W['rw_a1'][l], a2=W['rw_a2'][l],
                     g1=W['rw_g1'][l], g2=W['rw_g2'][l], wr=W['rw_wr'][l], wk=W['rw_wk'][l],
                     wv=W['rw_wv'][l], wo=W['rw_wo'][l], k_k=W['rw_kk'][l], k_a=W['rw_ka'][l],
                     r_k=W['rw_rk'][l], lnx_g=W['rw_lnx_g'][l], lnx_b=W['rw_lnx_b'][l])
            v_lora = None if l == 0 else (W['rw_v0'][l - 1], W['rw_v1'][l - 1], W['rw_v2'][l - 1])
            out, v_first, s_fin, h_last = rwkv7_time_mix(h, h_prev[l], s0[l], v_first, p, v_lora)
            shifts.append(h_last)
            states.append(s_fin)
        else:
            j = l - N_A_LAYERS
            q_nope, q_pe = mla_queries(h, pos, W['mla_wdq'][j], W['mla_q_g'][j], W['mla_wuq'][j])
            if past_ckv is None:
                o = prompt_attention(q_nope, q_pe, ckv, kpe, W['kv_wuk'], W['kv_wuv'])
            else:
                o = latent_attention(q_nope, q_pe, pos, keys_ckv, keys_kpe, k_pos, W['kv_wuk'], W['kv_wuv'])
            out = o.reshape(B, T, MLA_HEADS * V_DIM) @ W['mla_wo'][j]
        x = x + mod[:, 2][:, None, :] * out
        h = modulate(rmsnorm(x, W['norm_mlp_g'][l]), mod[:, 3], mod[:, 4])
        x = x + mod[:, 5][:, None, :] * sq_relu_mlp(h, W['mlp_w1'][l], W['mlp_w2'][l])
        if l == N_A_LAYERS - 1:
            ckv, kpe = shared_latent(x, cs, pos, W)
            if past_ckv is not None:
                keys_ckv = jnp.concatenate([past_ckv.astype(ckv.dtype), ckv], axis=1)
                keys_kpe = jnp.concatenate([past_kpe.astype(kpe.dtype), kpe], axis=1)
                k_pos = jnp.arange(keys_ckv.shape[1])
    y = rmsnorm(x, W['final_g'])
    return y, ckv, kpe, jnp.stack(states), jnp.stack(shifts)


def setup_inputs(seed: int = 0) -> dict:
    key = jax.random.key(seed)
    ks = iter(jax.random.split(key, 64))

    def nrm(shape, scale=1.0, offset=0.0):
        return offset + scale * jax.random.normal(next(ks), shape, jnp.float32)

    D, NA, NB, H = D_MODEL, N_A_LAYERS, N_B_LAYERS, MLA_HEADS
    return {
        'x_prompt': nrm((BATCH, SEQ, D)),
        'x_sample': nrm((DEC_BATCH, DEC_SEQ, D)),
        'cache_ckv': nrm((DEC_BATCH, PAST_LEN, KV_RANK)),
        'cache_kpe': nrm((DEC_BATCH, PAST_LEN, ROPE_DIM)),
        'state_wkv': nrm((NA, DEC_BATCH, RW_HEADS, RW_HEAD, RW_HEAD), 0.3),
        'state_shift': nrm((NA, DEC_BATCH, D)),
        'c_prompt': nrm((BATCH, D)),
        'c_sample': nrm((DEC_BATCH, D)),
        'ada_w': nrm((DEPTH, D, N_MOD * D), 0.5 * D ** -0.5),
        'ada_b': nrm((DEPTH, N_MOD * D), 0.01),
        'norm_mix_g': nrm((DEPTH, D), 0.05, 1.0),
        'norm_mlp_g': nrm((DEPTH, D), 0.05, 1.0),
        'mlp_w1': nrm((DEPTH, D, D_FF), D ** -0.5),
        'mlp_w2': nrm((DEPTH, D_FF, D), D_FF ** -0.5),
        'rw_mu': nrm((NA, 6, D), 0.2, 0.5),
        'rw_w0': nrm((NA, D), 0.5, -2.0),
        'rw_w1': nrm((NA, D, RW_DECAY_LORA), D ** -0.5),
        'rw_w2': nrm((NA, RW_DECAY_LORA, D), 0.3 * RW_DECAY_LORA ** -0.5),
        'rw_a0': nrm((NA, D), 0.2),
        'rw_a1': nrm((NA, D, RW_AAA_LORA), D ** -0.5),
        'rw_a2': nrm((NA, RW_AAA_LORA, D), RW_AAA_LORA ** -0.5),
        'rw_v0': nrm((NA - 1, D), 0.2),
        'rw_v1': nrm((NA - 1, D, RW_MV_LORA), D ** -0.5),
        'rw_v2': nrm((NA - 1, RW_MV_LORA, D), RW_MV_LORA ** -0.5),
        'rw_g1': nrm((NA, D, RW_GATE_LORA), D ** -0.5),
        'rw_g2': nrm((NA, RW_GATE_LORA, D), RW_GATE_LORA ** -0.5),
        'rw_wr': nrm((NA, D, D), D ** -0.5),
        'rw_wk': nrm((NA, D, D), D ** -0.5),
        'rw_wv': nrm((NA, D, D), D ** -0.5),
        'rw_wo': nrm((NA, D, D), D ** -0.5),
        'rw_kk': nrm((NA, D), 0.05, 0.85),
        'rw_ka': nrm((NA, D), 0.05, 1.0),
        'rw_rk': nrm((NA, D), 0.1),
        'rw_lnx_g': nrm((NA, D), 0.05, 1.0),
        'rw_lnx_b': nrm((NA, D), 0.02),
        'kv_ada_w': nrm((D, 2 * D), 0.5 * D ** -0.5),
        'kv_ada_b': nrm((2 * D,), 0.01),
        'kv_norm_g': nrm((D,), 0.05, 1.0),
        'kv_wd': nrm((D, KV_RANK + ROPE_DIM), D ** -0.5),
        'kv_lat_g': nrm((KV_RANK,), 0.05, 1.0),
        'kv_wuk': nrm((KV_RANK, H, NOPE_DIM), KV_RANK ** -0.5),
        'kv_wuv': nrm((KV_RANK, H, V_DIM), KV_RANK ** -0.5),
        'mla_wdq': nrm((NB, D, Q_RANK), D ** -0.5),
        'mla_q_g': nrm((NB, Q_RANK), 0.05, 1.0),
        'mla_wuq': nrm((NB, Q_RANK, H * (NOPE_DIM + ROPE_DIM)), Q_RANK ** -0.5),
        'mla_wo': nrm((NB, H * V_DIM, D), (H * V_DIM) ** -0.5),
        'final_g': nrm((D,), 0.05, 1.0),
    }


def reference(x_prompt, x_sample, cache_ckv, cache_kpe, state_wkv, state_shift, c_prompt, c_sample,
              ada_w, ada_b, norm_mix_g, norm_mlp_g, mlp_w1, mlp_w2,
              rw_mu, rw_w0, rw_w1, rw_w2, rw_a0, rw_a1, rw_a2, rw_v0, rw_v1, rw_v2, rw_g1, rw_g2,
              rw_wr, rw_wk, rw_wv, rw_wo, rw_kk, rw_ka, rw_rk, rw_lnx_g, rw_lnx_b,
              kv_ada_w, kv_ada_b, kv_norm_g, kv_wd, kv_lat_g, kv_wuk, kv_wuv,
              mla_wdq, mla_q_g, mla_wuq, mla_wo, final_g):
    W = dict(ada_w=ada_w, ada_b=ada_b, norm_mix_g=norm_mix_g, norm_mlp_g=norm_mlp_g,
             mlp_w1=mlp_w1, mlp_w2=mlp_w2, rw_mu=rw_mu, rw_w0=rw_w0, rw_w1=rw_w1, rw_w2=rw_w2,
             rw_a0=rw_a0, rw_a1=rw_a1, rw_a2=rw_a2, rw_v0=rw_v0, rw_v1=rw_v1, rw_v2=rw_v2,
             rw_g1=rw_g1, rw_g2=rw_g2, rw_wr=rw_wr, rw_wk=rw_wk, rw_wv=rw_wv, rw_wo=rw_wo,
             rw_kk=rw_kk, rw_ka=rw_ka, rw_rk=rw_rk, rw_lnx_g=rw_lnx_g, rw_lnx_b=rw_lnx_b,
             kv_ada_w=kv_ada_w, kv_ada_b=kv_ada_b, kv_norm_g=kv_norm_g, kv_wd=kv_wd,
             kv_lat_g=kv_lat_g, kv_wuk=kv_wuk, kv_wuv=kv_wuv, mla_wdq=mla_wdq, mla_q_g=mla_q_g,
             mla_wuq=mla_wuq, mla_wo=mla_wo, final_g=final_g)
    Bp, Tp = x_prompt.shape[0], x_prompt.shape[1]
    pos_p = jnp.arange(Tp)
    h0 = jnp.zeros((N_A_LAYERS, Bp, D_MODEL), x_prompt.dtype)
    s0 = jnp.zeros((N_A_LAYERS, Bp, RW_HEADS, RW_HEAD, RW_HEAD), jnp.float32)
    y_prompt, ckv_prompt, kpe_prompt, wkv_prompt, shift_prompt = trunk(
        x_prompt, c_prompt, pos_p, h0, s0, None, None, W)
    pos_s = cache_ckv.shape[1] + jnp.arange(x_sample.shape[1])
    y_sample, ckv_sample, kpe_sample, wkv_sample, shift_sample = trunk(
        x_sample, c_sample, pos_s, state_shift, state_wkv, cache_ckv, cache_kpe, W)
    return (y_prompt, y_sample, ckv_prompt, kpe_prompt, wkv_prompt, shift_prompt,
            ckv_sample, kpe_sample, wkv_sample, shift_sample)
```

```python
import functools

import jax
import jax.numpy as jnp
from jax import lax
from jax.experimental import pallas as pl
from jax.experimental.pallas import tpu as pltpu

F32, BF16 = jnp.float32, jnp.bfloat16

RW_HEAD = 64
CHUNK = 64
GN_EPS = 64e-5
NOPE_DIM = 128
ROPE_DIM = 64
V_DIM = 128
ROPE_THETA = 10000.0
MLA_SCALE = (NOPE_DIM + ROPE_DIM) ** -0.5
NORM_EPS = 1e-6
N_MOD = 6

LANES = 128
VMEM_LIMIT = 48 * 1024 * 1024
LORA_PAD = 128


def _params(sem, vmem=VMEM_LIMIT):
    return pltpu.CompilerParams(dimension_semantics=sem, vmem_limit_bytes=vmem)


def _dot(a, b):
    return jnp.dot(a, b, preferred_element_type=F32)


def _dot_nt(a, b):
    return lax.dot_general(a, b, (((1,), (1,)), ((), ())), preferred_element_type=F32)


def _dot_tn(a, b):
    return lax.dot_general(a, b, (((0,), (0,)), ((), ())), preferred_element_type=F32)


def _split2(x):
    hi = x.astype(BF16)
    return hi, (x - hi.astype(F32)).astype(BF16)


def _split3(x):
    hi = x.astype(BF16)
    r1 = x - hi.astype(F32)
    mid = r1.astype(BF16)
    return hi, mid, (r1 - mid.astype(F32)).astype(BF16)


def _sigmoid(x):
    return 1.0 / (1.0 + jnp.exp(-x))


def _blk(i, n):
    assert n & (n - 1) == 0
    return i >> (n.bit_length() - 1)


def _off(i, n):
    assert n & (n - 1) == 0
    return i & (n - 1)


def _lower_left(ri, cj, s):
    return (_blk(ri, 2 * s) == _blk(cj, 2 * s)) & (_off(ri, 2 * s) >= s) & (_off(cj, 2 * s) < s)


def _rms(x, g):
    return x * lax.rsqrt(jnp.mean(x * x, axis=-1, keepdims=True) + NORM_EPS) * g


def _ada_kernel(c_ref, w_ref, b_ref, o_ref):
    c = c_ref[...]
    cs = c * _sigmoid(c)
    o_ref[...] = _dot(cs.astype(BF16), w_ref[...].astype(BF16)) + b_ref[...]


def ada_linear(c, w, b):
    L, K, N = w.shape
    M = c.shape[0]
    bn = min(512, N)
    return pl.pallas_call(
        _ada_kernel,
        grid=(L, N // bn),
        in_specs=[pl.BlockSpec((M, K), lambda l, j: (0, 0)),
                  pl.BlockSpec((None, K, bn), lambda l, j: (l, 0, j)),
                  pl.BlockSpec((None, 1, bn), lambda l, j: (l, 0, j))],
        out_specs=pl.BlockSpec((None, M, bn), lambda l, j: (l, 0, j)),
        out_shape=jax.ShapeDtypeStruct((L, M, N), F32),
        compiler_params=_params(("parallel", "parallel")),
        name="ada_linear",
    )(c, w, b.reshape(L, 1, N))


def _norm_mod_kernel(x_ref, g_ref, sh_ref, sc_ref, o_ref):
    y = _rms(x_ref[...], g_ref[...])
    o_ref[...] = (y * (1.0 + sc_ref[...]) + sh_ref[...]).astype(o_ref.dtype)


def norm_mod(x, g, shift, scale):
    B, T, D = x.shape
    bt = min(512, T)
    vec = pl.BlockSpec((None, 1, D), lambda b, t: (b, 0, 0))
    return pl.pallas_call(
        _norm_mod_kernel,
        grid=(B, T // bt),
        in_specs=[pl.BlockSpec((None, bt, D), lambda b, t: (b, t, 0)),
                  pl.BlockSpec((1, D), lambda b, t: (0, 0)), vec, vec],
        out_specs=pl.BlockSpec((None, bt, D), lambda b, t: (b, t, 0)),
        out_shape=jax.ShapeDtypeStruct((B, T, D), BF16),
        compiler_params=_params(("parallel", "parallel")),
        name="norm_mod",
    )(x, g.reshape(1, D), shift.reshape(B, 1, D), scale.reshape(B, 1, D))


def _norm_kernel(x_ref, g_ref, o_ref):
    o_ref[...] = _rms(x_ref[...], g_ref[...])


def final_norm(x, g):
    B, T, D = x.shape
    bt = min(512, T)
    return pl.pallas_call(
        _norm_kernel,
        grid=(B, T // bt),
        in_specs=[pl.BlockSpec((None, bt, D), lambda b, t: (b, t, 0)),
                  pl.BlockSpec((1, D), lambda b, t: (0, 0))],
        out_specs=pl.BlockSpec((None, bt, D), lambda b, t: (b, t, 0)),
        out_shape=jax.ShapeDtypeStruct((B, T, D), F32),
        compiler_params=_params(("parallel", "parallel")),
        name="final_norm",
    )(x, g.reshape(1, D))


def _rwkv_pre_kernel(*refs, has_v):
    if has_v:
        (x_ref, g_ref, sh_ref, sc_ref, hp_ref, mu_ref, w1_ref, a1_ref, g1_ref, v1_ref,
         xr_ref, xk_ref, xv_ref, tw_ref, av_ref, gg_ref, vv_ref, hl_ref, prev) = refs
    else:
        (x_ref, g_ref, sh_ref, sc_ref, hp_ref, mu_ref, w1_ref, a1_ref, g1_ref,
         xr_ref, xk_ref, xv_ref, tw_ref, av_ref, gg_ref, hl_ref, prev) = refs
    bt = x_ref.shape[0]
    h = _rms(x_ref[...], g_ref[...]) * (1.0 + sc_ref[...]) + sh_ref[...]

    @pl.when(pl.program_id(1) == 0)
    def _():
        prev[...] = hp_ref[...]

    row = lax.broadcasted_iota(jnp.int32, h.shape, 0)
    xx = jnp.where(row == 0, prev[...], pltpu.roll(h, 1, 0)) - h
    last = h[bt - 1:bt, :]
    prev[...] = last
    hl_ref[...] = last
    mu = mu_ref[...]

    def mix(i):
        return (h + xx * mu[i:i + 1, :]).astype(BF16)

    xr_ref[...] = mix(0)
    tw_ref[...] = jnp.tanh(_dot(mix(1), w1_ref[...])).astype(BF16)
    xk_ref[...] = mix(2)
    xv = mix(3)
    xv_ref[...] = xv
    if has_v:
        vv_ref[...] = _dot(xv, v1_ref[...]).astype(BF16)
    av_ref[...] = _dot(mix(4), a1_ref[...]).astype(BF16)
    gg_ref[...] = _sigmoid(_dot(mix(5), g1_ref[...])).astype(BF16)


def rwkv_pre(x, g, shift, scale, h_prev, mu, w1, a1, g1, v1):
    B, T, D = x.shape
    bt = min(256, T)
    has_v = v1 is not None
    row = lambda n: pl.BlockSpec((None, bt, n), lambda b, t: (b, t, 0))
    vec = pl.BlockSpec((None, 1, D), lambda b, t: (b, 0, 0))
    full = lambda a: pl.BlockSpec(a.shape, lambda b, t: (0, 0))
    lora = [w1, a1, g1] + ([v1] if has_v else [])
    outs = [(D, BF16)] * 3 + [(w1.shape[1], BF16), (a1.shape[1], BF16), (g1.shape[1], BF16)]
    if has_v:
        outs.append((v1.shape[1], BF16))
    res = pl.pallas_call(
        functools.partial(_rwkv_pre_kernel, has_v=has_v),
        grid=(B, T // bt),
        in_specs=[row(D), pl.BlockSpec((1, D), lambda b, t: (0, 0)), vec, vec, vec, full(mu)]
                 + [full(a) for a in lora],
        out_specs=[row(n) for n, _ in outs] + [vec],
        out_shape=[jax.ShapeDtypeStruct((B, T, n), dt) for n, dt in outs]
                  + [jax.ShapeDtypeStruct((B, 1, D), F32)],
        scratch_shapes=[pltpu.VMEM((1, D), F32)],
        compiler_params=_params(("parallel", "arbitrary")),
        name="rwkv_pre",
    )(x, g.reshape(1, D), shift.reshape(B, 1, D), scale.reshape(B, 1, D), h_prev.reshape(B, 1, D), mu, *lora)
    return res


def _rwkv_proj_kernel(*refs, has_v):
    if has_v:
        (xr_ref, xk_ref, xv_ref, tw_ref, av_ref, gg_ref, vv_ref, vf_ref,
         wr_ref, wk_ref, wv_ref, w2_ref, a2_ref, g2_ref, v2_ref, vec_ref, ones_ref,
         r_ref, ld_ref, k_ref, v_ref, kk_ref, b_ref, g_ref) = refs
    else:
        (xr_ref, xk_ref, xv_ref, tw_ref, av_ref, gg_ref,
         wr_ref, wk_ref, wv_ref, w2_ref, a2_ref, g2_ref, vec_ref, ones_ref,
         r_ref, ld_ref, k_ref, v_ref, kk_ref, b_ref, g_ref) = refs
    vec = vec_ref[...]
    w0, a0, v0, k_k, k_a = (vec[i:i + 1, :] for i in range(5))
    r_ref[...] = _dot(xr_ref[...], wr_ref[...])
    kraw = _dot(xk_ref[...], wk_ref[...])
    v = _dot(xv_ref[...], wv_ref[...])
    nz = -(w0 + _dot(tw_ref[...], w2_ref[...]))
    softplus = jnp.maximum(nz, 0.0) + jnp.log(1.0 + jnp.exp(-jnp.abs(nz)))
    ld_ref[...] = -jnp.exp(-softplus - 0.5)
    a = _sigmoid(a0 + _dot(av_ref[...], a2_ref[...]))
    if has_v:
        v = v + (vf_ref[...] - v) * _sigmoid(v0 + _dot(vv_ref[...], v2_ref[...]))
    v_ref[...] = v
    g_ref[...] = _dot(gg_ref[...], g2_ref[...])
    kk = kraw * k_k
    hi, lo = _split2(kk * kk)
    ss = _dot(hi, ones_ref[...]) + _dot(lo, ones_ref[...])
    kk = kk / jnp.maximum(jnp.sqrt(ss), 1e-12)
    kk_ref[...] = kk
    b_ref[...] = kk * a
    k_ref[...] = kraw * (1.0 + (a - 1.0) * k_a)


def rwkv_proj(pre, v_first, wr, wk, wv, w2, a2, g2, v2, vecs, ones_bd):
    has_v = v2 is not None
    xr = pre[0]
    M, D = xr.shape
    HL = ones_bd.shape[0]
    bm = min(512, M)
    row = lambda a: pl.BlockSpec((bm, a.shape[1]), lambda i, j: (i, 0))
    tile = pl.BlockSpec((bm, HL), lambda i, j: (i, j))
    col = lambda a: pl.BlockSpec((a.shape[0], HL), lambda i, j: (0, j))
    acts = list(pre[:6]) + ([pre[6]] if has_v else [])
    weights = [wr, wk, wv, w2, a2, g2] + ([v2] if has_v else [])
    return pl.pallas_call(
        functools.partial(_rwkv_proj_kernel, has_v=has_v),
        grid=(M // bm, D // HL),
        in_specs=[row(a) for a in acts] + ([tile] if has_v else []) + [col(w) for w in weights]
                 + [col(vecs), pl.BlockSpec((HL, HL), lambda i, j: (0, 0))],
        out_specs=[tile] * 7,
        out_shape=[jax.ShapeDtypeStruct((M, D), F32)] * 7,
        compiler_params=_params(("parallel", "arbitrary")),
        name="rwkv_proj",
    )(*acts, *([v_first] if has_v else []), *weights, vecs, ones_bd)


def _recur_kernel(r_ref, ld_ref, k_ref, v_ref, kk_ref, b_ref, g_ref, s0_ref, vec_ref,
                  z_ref, sout_ref, s_scr, *, C, HP):
    ci = pl.program_id(2)
    C2 = 2 * C

    @pl.when(ci == 0)
    def _():
        s_scr[...] = s0_ref[...]

    ti = lax.broadcasted_iota(jnp.int32, (C, C), 0)
    tj = lax.broadcasted_iota(jnp.int32, (C, C), 1)
    tri = jnp.where(ti >= tj, 1.0, 0.0).astype(BF16)
    ld = ld_ref[...]
    cl = sum(_dot(tri, part) for part in _split3(ld))
    cl_end = cl[C - 1:C, :]
    p_in = jnp.exp(cl)
    p_inv = jnp.exp(-cl)
    p_rem = jnp.exp(cl_end - cl)
    p_end = jnp.exp(cl_end)
    r, k, v, b = r_ref[...], k_ref[...], v_ref[...], b_ref[...]
    a_t = -(jnp.exp(cl - ld) * kk_ref[...])
    r_t = p_in * r
    b_t = p_inv * b
    k_t = p_inv * k
    b_h = p_rem * b
    k_h = p_rem * k
    vec = vec_ref[...]
    r_k, lnx_g, lnx_b = (vec[i:i + 1, :] for i in range(3))
    rk = r * k * r_k
    g = g_ref[...]

    lane = lax.broadcasted_iota(jnp.int32, (1, LANES), 1)
    first = lane < RW_HEAD

    def stack(x):
        return jnp.concatenate([jnp.where(first, x, 0.0), jnp.where(first, 0.0, x)], axis=0).astype(BF16)

    def fold(x):
        return x[0:C, :] + x[C:C2, :]

    ri = lax.broadcasted_iota(jnp.int32, (C2, C2), 0)
    cj = lax.broadcasted_iota(jnp.int32, (C2, C2), 1)
    same = _blk(ri, C) == _blk(cj, C)
    strict = same & (_off(ri, C) > _off(cj, C))
    incl = same & (_off(ri, C) >= _off(cj, C))
    eye = jnp.where(ri == cj, 1.0, 0.0)
    same_head = (_blk(lax.broadcasted_iota(jnp.int32, (LANES, LANES), 0), RW_HEAD)
                 == _blk(lax.broadcasted_iota(jnp.int32, (LANES, LANES), 1), RW_HEAD))
    ones_bd = jnp.where(same_head, 1.0, 0.0).astype(BF16)

    for p in range(HP):
        sl = slice(p * LANES, (p + 1) * LANES)
        a_p, r_p, v_p = a_t[:, sl], r_t[:, sl], v[:, sl]
        q_st = jnp.concatenate([stack(a_p), stack(r_p)], axis=0)
        b_st = jnp.concatenate([b_t[:, sl], b_t[:, sl]], axis=0).astype(BF16)
        k_st = jnp.concatenate([k_t[:, sl], k_t[:, sl]], axis=0).astype(BF16)
        gb = _dot_nt(q_st, b_st)
        gk = _dot_nt(q_st, k_st)
        low = jnp.where(strict, gb[0:C2, :], 0.0)
        ak_f = fold(jnp.where(strict, gk[0:C2, :], 0.0)).astype(BF16)
        rb_f = fold(jnp.where(incl, gb[C2:, :], 0.0)).astype(BF16)
        rk_f = fold(jnp.where(incl, gk[C2:, :], 0.0)).astype(BF16)

        t_inv = eye + jnp.where(_lower_left(ri, cj, 1), low, 0.0)
        s = 2
        while s < C:
            sel = _lower_left(ri, cj, s)
            tb = t_inv.astype(BF16)
            mid = _dot(tb, jnp.where(sel, low, 0.0).astype(BF16))
            t_inv = t_inv + _dot(mid.astype(BF16), tb)
            s *= 2
        t_f = fold(t_inv).astype(BF16)

        state = s_scr[p]
        state_b = state.astype(BF16)
        v_st = stack(v_p)
        x = _dot_nt(a_p.astype(BF16), state_b) + _dot(ak_f, v_st)
        u = _dot(t_f, stack(x))
        y = _dot_nt(r_p.astype(BF16), state_b) + _dot(rb_f, stack(u)) + _dot(rk_f, v_st)
        uv = jnp.concatenate([u, v_p], axis=0).astype(BF16)
        bk = jnp.concatenate([b_h[:, sl], k_h[:, sl]], axis=0).astype(BF16)
        s_scr[p] = state * p_end[:, sl] + jnp.where(same_head, _dot_tn(uv, bk), 0.0)

        def seg(t):
            hi, lo = _split2(t)
            return _dot(hi, ones_bd) + _dot(lo, ones_bd)

        mean = seg(y) * (1.0 / RW_HEAD)
        d = y - mean
        var = seg(d * d) * (1.0 / RW_HEAD)
        yn = d * lax.rsqrt(var + GN_EPS) * lnx_g[:, sl] + lnx_b[:, sl]
        bonus = seg(rk[:, sl]) * v_p
        z_ref[:, sl] = ((yn + bonus) * g[:, sl]).astype(z_ref.dtype)

    @pl.when(ci == pl.num_programs(2) - 1)
    def _():
        sout_ref[...] = s_scr[...]


def rwkv_recur(ops, s0_bd, vecs, B, T):
    D = ops[0].shape[1]
    HL = min(512, D)
    HP = HL // LANES
    C = min(CHUNK, T)
    ops3 = [o.reshape(B, T, D) for o in ops]
    tile = pl.BlockSpec((None, C, HL), lambda b, j, c: (b, c, j))
    st = pl.BlockSpec((None, HP, LANES, LANES), lambda b, j, c: (b, j, 0, 0))
    z, s_fin = pl.pallas_call(
        functools.partial(_recur_kernel, C=C, HP=HP),
        grid=(B, D // HL, T // C),
        in_specs=[tile] * 7 + [st, pl.BlockSpec((vecs.shape[0], HL), lambda b, j, c: (0, j))],
        out_specs=[tile, st],
        out_shape=[jax.ShapeDtypeStruct((B, T, D), BF16), jax.ShapeDtypeStruct(s0_bd.shape, F32)],
        scratch_shapes=[pltpu.VMEM((HP, LANES, LANES), F32)],
        compiler_params=_params(("parallel", "parallel", "arbitrary")),
        name="rwkv_recur",
    )(*ops3, s0_bd, vecs)
    return z.reshape(B * T, D), s_fin


def _gate_operand(gate, B, T, bm, bn):
    N = gate.shape[1]
    col = (lambda j: j) if bn < N else (lambda j: 0)
    if T % bm == 0:
        return gate.reshape(B, 1, N), pl.BlockSpec((None, 1, bn), lambda i, j: ((i * bm) // T, 0, col(j)))
    rows = jnp.broadcast_to(gate[:, None, :], (B, T, N)).reshape(B * T, N)
    return rows, pl.BlockSpec((bm, bn), lambda i, j: (i, col(j)))


def _matmul_res_kernel(a_ref, w_ref, x_ref, gate_ref, o_ref):
    o_ref[...] = x_ref[...] + gate_ref[...] * _dot(a_ref[...], w_ref[...])


def matmul_res(a, w, x, gate, B, T):
    M, K = a.shape
    N = w.shape[1]
    bm, bn = min(1024, M), min(512, N)
    gate_arr, gate_spec = _gate_operand(gate, B, T, bm, bn)
    return pl.pallas_call(
        _matmul_res_kernel,
        grid=(M // bm, N // bn),
        in_specs=[pl.BlockSpec((bm, K), lambda i, j: (i, 0)),
                  pl.BlockSpec((K, bn), lambda i, j: (0, j)),
                  pl.BlockSpec((bm, bn), lambda i, j: (i, j)), gate_spec],
        out_specs=pl.BlockSpec((bm, bn), lambda i, j: (i, j)),
        out_shape=jax.ShapeDtypeStruct((M, N), F32),
        compiler_params=_params(("parallel", "arbitrary")),
        name="matmul_res",
    )(a, w, x, gate_arr)


def _mlp_kernel(h_ref, w1_ref, w2_ref, x_ref, gate_ref, o_ref, acc):
    f = pl.program_id(1)
    t = jnp.square(jnp.maximum(_dot(h_ref[...], w1_ref[...]), 0.0)).astype(BF16)
    part = _dot(t, w2_ref[...])

    @pl.when(f == 0)
    def _():
        acc[...] = part

    @pl.when(f > 0)
    def _():
        acc[...] += part

    @pl.when(f == pl.num_programs(1) - 1)
    def _():
        o_ref[...] = x_ref[...] + gate_ref[...] * acc[...]


def mlp_res(h, w1, w2, x, gate, B, T):
    M, D = h.shape
    F = w1.shape[1]
    bm, bf = min(512, M), min(512, F)
    gate_arr, gate_spec = _gate_operand(gate, B, T, bm, D)
    return pl.pallas_call(
        _mlp_kernel,
        grid=(M // bm, F // bf),
        in_specs=[pl.BlockSpec((bm, D), lambda i, f: (i, 0)),
                  pl.BlockSpec((D, bf), lambda i, f: (0, f)),
                  pl.BlockSpec((bf, D), lambda i, f: (f, 0)),
                  pl.BlockSpec((bm, D), lambda i, f: (i, 0)), gate_spec],
        out_specs=pl.BlockSpec((bm, D), lambda i, f: (i, 0)),
        out_shape=jax.ShapeDtypeStruct((M, D), F32),
        scratch_shapes=[pltpu.VMEM((bm, D), F32)],
        compiler_params=_params(("parallel", "arbitrary")),
        name="mlp_res",
    )(h, w1, w2, x, gate_arr)


def _rope_pair(t, tab):
    prod = t * tab
    return (prod + pltpu.roll(prod, ROPE_DIM, 1))[:, :ROPE_DIM]


def _latent_kernel(h_ref, w_ref, g_ref, tab_ref, ckv_ref, kpe_ref):
    acc = _dot(h_ref[...], w_ref[...])
    R = g_ref.shape[1]
    ckv_ref[...] = _rms(acc[:, :R], g_ref[...])
    kpe_ref[...] = _rope_pair(acc[:, R:R + 2 * ROPE_DIM], tab_ref[...])


def latent(h, w, g, tab):
    B, T, D = h.shape
    R = g.shape[0]
    bt = min(512, T)
    return pl.pallas_call(
        _latent_kernel,
        grid=(B, T // bt),
        in_specs=[pl.BlockSpec((None, bt, D), lambda b, t: (b, t, 0)),
                  pl.BlockSpec(w.shape, lambda b, t: (0, 0)),
                  pl.BlockSpec((1, R), lambda b, t: (0, 0)),
                  pl.BlockSpec((bt, 2 * ROPE_DIM), lambda b, t: (t, 0))],
        out_specs=[pl.BlockSpec((None, bt, R), lambda b, t: (b, t, 0)),
                   pl.BlockSpec((None, bt, ROPE_DIM), lambda b, t: (b, t, 0))],
        out_shape=[jax.ShapeDtypeStruct((B, T, R), F32), jax.ShapeDtypeStruct((B, T, ROPE_DIM), F32)],
        compiler_params=_params(("parallel", "parallel")),
        name="latent",
    )(h, w, g.reshape(1, R), tab)


def _kv_expand_kernel(c_ref, pe_ref, wuk_ref, wuv_ref, k_ref, v_ref):
    c = c_ref[...].astype(BF16)
    kn = _dot(c, wuk_ref[...])
    vv = _dot(c, wuv_ref[...])
    pe = pe_ref[...].astype(BF16)
    for h in range(k_ref.shape[0]):
        k_ref[h, :, 0:NOPE_DIM] = kn[:, h * NOPE_DIM:(h + 1) * NOPE_DIM].astype(BF16)
        k_ref[h, :, NOPE_DIM:NOPE_DIM + ROPE_DIM] = pe
        v_ref[h] = vv[:, h * V_DIM:(h + 1) * V_DIM].astype(BF16)


def kv_expand(ckv, kpe, wuk, wuv, H):
    B, S, R = ckv.shape
    bt = 512 if S % 512 == 0 else S
    hb = min(8, H)
    DK = NOPE_DIM + ROPE_DIM
    return pl.pallas_call(
        _kv_expand_kernel,
        grid=(B, S // bt, H // hb),
        in_specs=[pl.BlockSpec((None, bt, R), lambda b, t, j: (b, t, 0)),
                  pl.BlockSpec((None, bt, ROPE_DIM), lambda b, t, j: (b, t, 0)),
                  pl.BlockSpec((R, hb * NOPE_DIM), lambda b, t, j: (0, j)),
                  pl.BlockSpec((R, hb * V_DIM), lambda b, t, j: (0, j))],
        out_specs=[pl.BlockSpec((None, hb, bt, DK), lambda b, t, j: (b, j, t, 0)),
                   pl.BlockSpec((None, hb, bt, V_DIM), lambda b, t, j: (b, j, t, 0))],
        out_shape=[jax.ShapeDtypeStruct((B, H, S, DK), BF16), jax.ShapeDtypeStruct((B, H, S, V_DIM), BF16)],
        compiler_params=_params(("parallel", "parallel", "arbitrary")),
        name="kv_expand",
    )(ckv, kpe, wuk, wuv)


def _wdq_kernel(h_ref, w_ref, g_ref, o_ref):
    o_ref[...] = _rms(_dot(h_ref[...], w_ref[...]), g_ref[...]).astype(BF16)


def wdq_norm(h, w, g):
    M, D = h.shape
    R = w.shape[1]
    bm = min(1024, M)
    return pl.pallas_call(
        _wdq_kernel,
        grid=(M // bm,),
        in_specs=[pl.BlockSpec((bm, D), lambda i: (i, 0)), pl.BlockSpec((D, R), lambda i: (0, 0)),
                  pl.BlockSpec((1, R), lambda i: (0, 0))],
        out_specs=pl.BlockSpec((bm, R), lambda i: (i, 0)),
        out_shape=jax.ShapeDtypeStruct((M, R), BF16),
        compiler_params=_params(("parallel",)),
        name="wdq_norm",
    )(h, w, g.reshape(1, R))


def _wuq_kernel(c_ref, w_ref, tab_ref, q_ref):
    acc = _dot(c_ref[...], w_ref[...])
    tab = tab_ref[...]
    W = NOPE_DIM + 2 * ROPE_DIM
    for h in range(q_ref.shape[0]):
        q_ref[h, :, 0:NOPE_DIM] = (acc[:, h * W:h * W + NOPE_DIM] * MLA_SCALE).astype(BF16)
        pe = _rope_pair(acc[:, h * W + NOPE_DIM:(h + 1) * W], tab)
        q_ref[h, :, NOPE_DIM:NOPE_DIM + ROPE_DIM] = (pe * MLA_SCALE).astype(BF16)


def wuq_rope(cq, w, tab, H):
    B, T, R = cq.shape
    bt = min(512, T)
    hb = min(8, H)
    W = NOPE_DIM + 2 * ROPE_DIM
    DK = NOPE_DIM + ROPE_DIM
    return pl.pallas_call(
        _wuq_kernel,
        grid=(B, T // bt, H // hb),
        in_specs=[pl.BlockSpec((None, bt, R), lambda b, t, j: (b, t, 0)),
                  pl.BlockSpec((R, hb * W), lambda b, t, j: (0, j)),
                  pl.BlockSpec((bt, 2 * ROPE_DIM), lambda b, t, j: (t, 0))],
        out_specs=pl.BlockSpec((None, hb, bt, DK), lambda b, t, j: (b, j, t, 0)),
        out_shape=jax.ShapeDtypeStruct((B, H, T, DK), BF16),
        compiler_params=_params(("parallel", "parallel", "arbitrary")),
        name="wuq_rope",
    )(cq, w, tab)


def _visible(qpos, kpos):
    return _blk(kpos, CHUNK) <= _blk(qpos, CHUNK)


def _flash_kernel(q_ref, k_ref, v_ref, o_ref, *, tq, tk):
    qi = pl.program_id(2)
    q = q_ref[...]
    qpos = qi * tq + lax.broadcasted_iota(jnp.int32, (tq, 1), 0)

    def body(ki, carry):
        m, l, acc = carry
        start = pl.multiple_of(ki * tk, tk)
        k = k_ref[pl.ds(start, tk), :]
        v = v_ref[pl.ds(start, tk), :]
        kpos = ki * tk + lax.broadcasted_iota(jnp.int32, (1, tk), 1)
        s = jnp.where(_visible(qpos, kpos), _dot_nt(q, k), -jnp.inf)
        m_new = jnp.maximum(m, jnp.max(s, axis=-1, keepdims=True))
        alpha = jnp.exp(m - m_new)
        p = jnp.exp(s - m_new)
        l = alpha * l + jnp.sum(p, axis=-1, keepdims=True)
        acc = alpha * acc + _dot(p.astype(BF16), v)
        return m_new, l, acc

    nk = ((qi + 1) * tq + tk - 1) // tk
    init = (jnp.full((tq, 1), -jnp.inf, F32), jnp.zeros((tq, 1), F32), jnp.zeros((tq, V_DIM), F32))
    _, l, acc = lax.fori_loop(0, nk, body, init)
    o_ref[...] = (acc / l).astype(o_ref.dtype)


def flash_prompt(q, k, v):
    B, H, T, DK = q.shape
    tq = tk = min(256, T)
    return pl.pallas_call(
        functools.partial(_flash_kernel, tq=tq, tk=tk),
        grid=(B, H, T // tq),
        in_specs=[pl.BlockSpec((None, None, tq, DK), lambda b, h, i: (b, h, i, 0)),
                  pl.BlockSpec((None, None, T, DK), lambda b, h, i: (b, h, 0, 0)),
                  pl.BlockSpec((None, None, T, V_DIM), lambda b, h, i: (b, h, 0, 0))],
        out_specs=pl.BlockSpec((None, tq, V_DIM), lambda b, h, i: (b, i, h)),
        out_shape=jax.ShapeDtypeStruct((B, T, H * V_DIM), BF16),
        compiler_params=_params(("parallel", "parallel", "arbitrary")),
        name="flash_prompt",
    )(q, k, v)


def _attn_full_kernel(q_ref, k_ref, v_ref, o_ref, *, q0):
    tq, S = q_ref.shape[0], k_ref.shape[0]
    qpos = q0 + lax.broadcasted_iota(jnp.int32, (tq, 1), 0)
    kpos = lax.broadcasted_iota(jnp.int32, (1, S), 1)
    s = jnp.where(_visible(qpos, kpos), _dot_nt(q_ref[...], k_ref[...]), -jnp.inf)
    p = jnp.exp(s - jnp.max(s, axis=-1, keepdims=True))
    l = jnp.sum(p, axis=-1, keepdims=True)
    o_ref[...] = (_dot(p.astype(BF16), v_ref[...]) / l).astype(o_ref.dtype)


def attn_full(q, k, v, q0):
    B, H, T, DK = q.shape
    S = k.shape[2]
    return pl.pallas_call(
        functools.partial(_attn_full_kernel, q0=q0),
        grid=(B, H),
        in_specs=[pl.BlockSpec((None, None, T, DK), lambda b, h: (b, h, 0, 0)),
                  pl.BlockSpec((None, None, S, DK), lambda b, h: (b, h, 0, 0)),
                  pl.BlockSpec((None, None, S, V_DIM), lambda b, h: (b, h, 0, 0))],
        out_specs=pl.BlockSpec((None, T, V_DIM), lambda b, h: (b, 0, h)),
        out_shape=jax.ShapeDtypeStruct((B, T, H * V_DIM), BF16),
        compiler_params=_params(("parallel", "parallel")),
        name="attn_full",
    )(q, k, v)


def _pad_cols(w, n):
    return jnp.pad(w, ((0, 0), (0, n - w.shape[1])))


def _pad_rows(w, n):
    return jnp.pad(w, ((0, n - w.shape[0]), (0, 0)))


def _rotate_half_cols(w):
    half = ROPE_DIM // 2
    return jnp.concatenate([-w[..., half:], w[..., :half]], axis=-1)


def _rope_table(pos):
    half = ROPE_DIM // 2
    inv = ROPE_THETA ** (-jnp.arange(half, dtype=F32) / half)
    ang = pos.astype(F32)[:, None] * inv[None, :]
    cos, sin = jnp.cos(ang), jnp.sin(ang)
    return jnp.concatenate([cos, cos, sin, sin], axis=-1)


def _block_diag_states(s):
    B, H, N, _ = s.shape
    s = s.reshape(B, H // 2, 2, N, N)
    z = jnp.zeros_like(s[:, :, 0])
    top = jnp.concatenate([s[:, :, 0], z], axis=-1)
    bot = jnp.concatenate([z, s[:, :, 1]], axis=-1)
    return jnp.concatenate([top, bot], axis=-2)


def _diag_states(s):
    B, HP = s.shape[:2]
    N = RW_HEAD
    return jnp.stack([s[:, :, :N, :N], s[:, :, N:, N:]], axis=2).reshape(B, 2 * HP, N, N)


def _prepare(W):
    D = W['rw_wr'].shape[1]
    P = {}
    bf = lambda a: a.astype(BF16)
    P['mlp_w1'], P['mlp_w2'] = bf(W['mlp_w1']), bf(W['mlp_w2'])
    for n in ('rw_wr', 'rw_wk', 'rw_wv', 'rw_wo', 'rw_g1', 'rw_g2', 'mla_wdq', 'mla_wo'):
        P[n] = bf(W[n])
    NA = W['rw_wr'].shape[0]
    P['rw_w1'] = [bf(_pad_cols(W['rw_w1'][l], LORA_PAD)) for l in range(NA)]
    P['rw_w2'] = [bf(_pad_rows(W['rw_w2'][l], LORA_PAD)) for l in range(NA)]
    P['rw_a1'] = [bf(_pad_cols(W['rw_a1'][l], LORA_PAD)) for l in range(NA)]
    P['rw_a2'] = [bf(_pad_rows(W['rw_a2'][l], LORA_PAD)) for l in range(NA)]
    P['rw_v1'] = [bf(_pad_cols(W['rw_v1'][l], LORA_PAD)) for l in range(NA - 1)]
    P['rw_v2'] = [bf(_pad_rows(W['rw_v2'][l], LORA_PAD)) for l in range(NA - 1)]
    zeros = jnp.zeros((D,), F32)
    P['proj_vecs'], P['recur_vecs'] = [], []
    for l in range(NA):
        v0 = W['rw_v0'][l - 1] if l > 0 else zeros
        P['proj_vecs'].append(jnp.stack([W['rw_w0'][l], W['rw_a0'][l], v0, W['rw_kk'][l], W['rw_ka'][l],
                                         zeros, zeros, zeros]))
        P['recur_vecs'].append(jnp.stack([W['rw_rk'][l], W['rw_lnx_g'][l], W['rw_lnx_b'][l]] + [zeros] * 5))
    HL = min(512, D)
    head = jnp.arange(HL) // RW_HEAD
    P['ones_bd'] = (head[:, None] == head[None, :]).astype(BF16)
    R = W['kv_lat_g'].shape[0]
    wd = W['kv_wd']
    P['kv_wd'] = bf(jnp.concatenate([wd, _rotate_half_cols(wd[:, R:])], axis=1))
    H = W['kv_wuk'].shape[1]
    P['kv_wuk'] = bf(W['kv_wuk'].reshape(R, H * NOPE_DIM))
    P['kv_wuv'] = bf(W['kv_wuv'].reshape(R, H * V_DIM))
    NB, Q = W['mla_wuq'].shape[:2]
    wuq = W['mla_wuq'].reshape(NB, Q, H, NOPE_DIM + ROPE_DIM)
    pe = wuq[..., NOPE_DIM:]
    P['mla_wuq'] = bf(jnp.concatenate([wuq, _rotate_half_cols(pe)], axis=-1).reshape(NB, Q, -1))
    return P


def _trunk(x, mod, kv_mod, pos0, h_prev, s0, past_ckv, past_kpe, W, P):
    B, T, D = x.shape
    M = B * T
    depth = W['ada_w'].shape[0]
    NA = W['rw_wr'].shape[0]
    H = W['kv_wuk'].shape[1]
    tab = _rope_table(pos0 + jnp.arange(T))
    xf = x.reshape(M, D)
    shifts, states = [], []
    v_first = None
    keys = vals = ckv = kpe = None
    for l in range(depth):
        m = mod[l]
        if l < NA:
            has_v = l > 0
            pre = rwkv_pre(xf.reshape(B, T, D), W['norm_mix_g'][l], m[:, 0], m[:, 1], h_prev[l], W['rw_mu'][l],
                           P['rw_w1'][l], P['rw_a1'][l], P['rw_g1'][l], P['rw_v1'][l - 1] if has_v else None)
            shifts.append(pre[-1].reshape(B, D))
            acts = [a.reshape(M, a.shape[-1]) for a in pre[:-1]]
            ops = rwkv_proj(acts, v_first, P['rw_wr'][l], P['rw_wk'][l], P['rw_wv'][l], P['rw_w2'][l],
                            P['rw_a2'][l], P['rw_g2'][l], P['rw_v2'][l - 1] if has_v else None,
                            P['proj_vecs'][l], P['ones_bd'])
            if l == 0:
                v_first = ops[3]
            z, s_fin = rwkv_recur(ops, _block_diag_states(s0[l]), P['recur_vecs'][l], B, T)
            states.append(_diag_states(s_fin))
            xf = matmul_res(z, P['rw_wo'][l], xf, m[:, 2], B, T)
        else:
            j = l - NA
            h = norm_mod(xf.reshape(B, T, D), W['norm_mix_g'][l], m[:, 0], m[:, 1])
            cq = wdq_norm(h.reshape(M, D), P['mla_wdq'][j], W['mla_q_g'][j])
            q = wuq_rope(cq.reshape(B, T, -1), P['mla_wuq'][j], tab, H)
            if past_ckv is None:
                o = flash_prompt(q, keys, vals)
            else:
                o = attn_full(q, keys, vals, pos0)
            xf = matmul_res(o.reshape(M, H * V_DIM), P['mla_wo'][j], xf, m[:, 2], B, T)
        h = norm_mod(xf.reshape(B, T, D), W['norm_mlp_g'][l], m[:, 3], m[:, 4])
        xf = mlp_res(h.reshape(M, D), P['mlp_w1'][l], P['mlp_w2'][l], xf, m[:, 5], B, T)
        if l == NA - 1:
            hkv = norm_mod(xf.reshape(B, T, D), W['kv_norm_g'], kv_mod[:, 0], kv_mod[:, 1])
            ckv, kpe = latent(hkv, P['kv_wd'], W['kv_lat_g'], tab)
            if past_ckv is None:
                keys, vals = kv_expand(ckv, kpe, P['kv_wuk'], P['kv_wuv'], H)
            else:
                keys, vals = kv_expand(jnp.concatenate([past_ckv, ckv], axis=1),
                                       jnp.concatenate([past_kpe, kpe], axis=1), P['kv_wuk'], P['kv_wuv'], H)
    y = final_norm(xf.reshape(B, T, D), W['final_g'])
    return y, ckv, kpe, jnp.stack(states), jnp.stack(shifts)


def kernel(x_prompt, x_sample, cache_ckv, cache_kpe, state_wkv, state_shift, c_prompt, c_sample, ada_w, ada_b, norm_mix_g, norm_mlp_g, mlp_w1, mlp_w2, rw_mu, rw_w0, rw_w1, rw_w2, rw_a0, rw_a1, rw_a2, rw_v0, rw_v1, rw_v2, rw_g1, rw_g2, rw_wr, rw_wk, rw_wv, rw_wo, rw_kk, rw_ka, rw_rk, rw_lnx_g, rw_lnx_b, kv_ada_w, kv_ada_b, kv_norm_g, kv_wd, kv_lat_g, kv_wuk, kv_wuv, mla_wdq, mla_q_g, mla_wuq, mla_wo, final_g):
    W = dict(ada_w=ada_w, ada_b=ada_b, norm_mix_g=norm_mix_g, norm_mlp_g=norm_mlp_g,
             mlp_w1=mlp_w1, mlp_w2=mlp_w2, rw_mu=rw_mu, rw_w0=rw_w0, rw_w1=rw_w1, rw_w2=rw_w2,
             rw_a0=rw_a0, rw_a1=rw_a1, rw_a2=rw_a2, rw_v0=rw_v0, rw_v1=rw_v1, rw_v2=rw_v2,
             rw_g1=rw_g1, rw_g2=rw_g2, rw_wr=rw_wr, rw_wk=rw_wk, rw_wv=rw_wv, rw_wo=rw_wo,
             rw_kk=rw_kk, rw_ka=rw_ka, rw_rk=rw_rk, rw_lnx_g=rw_lnx_g, rw_lnx_b=rw_lnx_b,
             kv_ada_w=kv_ada_w, kv_ada_b=kv_ada_b, kv_norm_g=kv_norm_g, kv_wd=kv_wd,
             kv_lat_g=kv_lat_g, kv_wuk=kv_wuk, kv_wuv=kv_wuv, mla_wdq=mla_wdq, mla_q_g=mla_q_g,
             mla_wuq=mla_wuq, mla_wo=mla_wo, final_g=final_g)
    P = _prepare(W)
    Bp, Tp, D = x_prompt.shape
    Bs = x_sample.shape[0]
    depth = ada_w.shape[0]
    NA = rw_wr.shape[0]
    c_all = jnp.concatenate([c_prompt, c_sample], axis=0)
    mod = ada_linear(c_all, ada_w, ada_b).reshape(depth, Bp + Bs, N_MOD, D)
    kv_mod = ada_linear(c_all, kv_ada_w[None], kv_ada_b[None]).reshape(Bp + Bs, 2, D)
    h0 = jnp.zeros((NA, Bp, D), F32)
    s0 = jnp.zeros((NA, Bp, D // RW_HEAD, RW_HEAD, RW_HEAD), F32)
    out_p = _trunk(x_prompt, mod[:, :Bp], kv_mod[:Bp], 0, h0, s0, None, None, W, P)
    out_s = _trunk(x_sample, mod[:, Bp:], kv_mod[Bp:], cache_ckv.shape[1], state_shift, state_wkv,
                   cache_ckv, cache_kpe, W, P)
    return (out_p[0], out_s[0]) + out_p[1:] + out_s[1:]
```

```python
import functools

import jax
import jax.numpy as jnp
from jax import lax
from jax.experimental import pallas as pl
from jax.experimental.pallas import tpu as pltpu

F32, BF16 = jnp.float32, jnp.bfloat16

RW_HEAD = 64
CHUNK = 64
GN_EPS = 64e-5
NOPE_DIM = 128
ROPE_DIM = 64
V_DIM = 128
ROPE_THETA = 10000.0
MLA_SCALE = (NOPE_DIM + ROPE_DIM) ** -0.5
NORM_EPS = 1e-6
N_MOD = 6

LANES = 128
VMEM_LIMIT = 48 * 1024 * 1024
LORA_PAD = 128


def _params(sem, vmem=VMEM_LIMIT):
    return pltpu.CompilerParams(dimension_semantics=sem, vmem_limit_bytes=vmem)


def _dot(a, b):
    return jnp.dot(a, b, preferred_element_type=F32)


def _dot_nt(a, b):
    return lax.dot_general(a, b, (((1,), (1,)), ((), ())), preferred_element_type=F32)


def _dot_tn(a, b):
    return lax.dot_general(a, b, (((0,), (0,)), ((), ())), preferred_element_type=F32)


def _split2(x):
    hi = x.astype(BF16)
    return hi, (x - hi.astype(F32)).astype(BF16)


def _split3(x):
    hi = x.astype(BF16)
    r1 = x - hi.astype(F32)
    mid = r1.astype(BF16)
    return hi, mid, (r1 - mid.astype(F32)).astype(BF16)


def _sigmoid(x):
    return 1.0 / (1.0 + jnp.exp(-x))


def _blk(i, n):
    assert n & (n - 1) == 0
    return i >> (n.bit_length() - 1)


def _off(i, n):
    assert n & (n - 1) == 0
    return i & (n - 1)


def _lower_left(ri, cj, s):
    return (_blk(ri, 2 * s) == _blk(cj, 2 * s)) & (_off(ri, 2 * s) >= s) & (_off(cj, 2 * s) < s)


def _rms(x, g):
    return x * lax.rsqrt(jnp.mean(x * x, axis=-1, keepdims=True) + NORM_EPS) * g


def _ada_kernel(c_ref, w_ref, b_ref, o_ref):
    c = c_ref[...]
    cs = c * _sigmoid(c)
    o_ref[...] = _dot(cs.astype(BF16), w_ref[...].astype(BF16)) + b_ref[...]


def ada_linear(c, w, b):
    L, K, N = w.shape
    M = c.shape[0]
    bn = min(512, N)
    return pl.pallas_call(
        _ada_kernel,
        grid=(L, N // bn),
        in_specs=[pl.BlockSpec((M, K), lambda l, j: (0, 0)),
                  pl.BlockSpec((None, K, bn), lambda l, j: (l, 0, j)),
                  pl.BlockSpec((None, 1, bn), lambda l, j: (l, 0, j))],
        out_specs=pl.BlockSpec((None, M, bn), lambda l, j: (l, 0, j)),
        out_shape=jax.ShapeDtypeStruct((L, M, N), F32),
        compiler_params=_params(("parallel", "parallel")),
        name="ada_linear",
    )(c, w, b.reshape(L, 1, N))


def _norm_mod_kernel(x_ref, g_ref, sh_ref, sc_ref, o_ref):
    y = _rms(x_ref[...], g_ref[...])
    o_ref[...] = (y * (1.0 + sc_ref[...]) + sh_ref[...]).astype(o_ref.dtype)


def norm_mod(x, g, shift, scale):
    B, T, D = x.shape
    bt = min(512, T)
    vec = pl.BlockSpec((None, 1, D), lambda b, t: (b, 0, 0))
    return pl.pallas_call(
        _norm_mod_kernel,
        grid=(B, T // bt),
        in_specs=[pl.BlockSpec((None, bt, D), lambda b, t: (b, t, 0)),
                  pl.BlockSpec((1, D), lambda b, t: (0, 0)), vec, vec],
        out_specs=pl.BlockSpec((None, bt, D), lambda b, t: (b, t, 0)),
        out_shape=jax.ShapeDtypeStruct((B, T, D), BF16),
        compiler_params=_params(("parallel", "parallel")),
        name="norm_mod",
    )(x, g.reshape(1, D), shift.reshape(B, 1, D), scale.reshape(B, 1, D))


def _norm_kernel(x_ref, g_ref, o_ref):
    o_ref[...] = _rms(x_ref[...], g_ref[...])


def final_norm(x, g):
    B, T, D = x.shape
    bt = min(512, T)
    return pl.pallas_call(
        _norm_kernel,
        grid=(B, T // bt),
        in_specs=[pl.BlockSpec((None, bt, D), lambda b, t: (b, t, 0)),
                  pl.BlockSpec((1, D), lambda b, t: (0, 0))],
        out_specs=pl.BlockSpec((None, bt, D), lambda b, t: (b, t, 0)),
        out_shape=jax.ShapeDtypeStruct((B, T, D), F32),
        compiler_params=_params(("parallel", "parallel")),
        name="final_norm",
    )(x, g.reshape(1, D))


def _rwkv_pre_kernel(*refs, has_v):
    if has_v:
        (x_ref, g_ref, sh_ref, sc_ref, hp_ref, mu_ref, w1_ref, a1_ref, g1_ref, v1_ref,
         xr_ref, xk_ref, xv_ref, tw_ref, av_ref, gg_ref, vv_ref, hl_ref, prev) = refs
    else:
        (x_ref, g_ref, sh_ref, sc_ref, hp_ref, mu_ref, w1_ref, a1_ref, g1_ref,
         xr_ref, xk_ref, xv_ref, tw_ref, av_ref, gg_ref, hl_ref, prev) = refs
    bt = x_ref.shape[0]
    h = _rms(x_ref[...], g_ref[...]) * (1.0 + sc_ref[...]) + sh_ref[...]

    @pl.when(pl.program_id(1) == 0)
    def _():
        prev[...] = hp_ref[...]

    row = lax.broadcasted_iota(jnp.int32, h.shape, 0)
    xx = jnp.where(row == 0, prev[...], pltpu.roll(h, 1, 0)) - h
    last = h[bt - 1:bt, :]
    prev[...] = last
    hl_ref[...] = last
    mu = mu_ref[...]

    def mix(i):
        return (h + xx * mu[i:i + 1, :]).astype(BF16)

    xr_ref[...] = mix(0)
    tw_ref[...] = jnp.tanh(_dot(mix(1), w1_ref[...])).astype(BF16)
    xk_ref[...] = mix(2)
    xv = mix(3)
    xv_ref[...] = xv
    if has_v:
        vv_ref[...] = _dot(xv, v1_ref[...]).astype(BF16)
    av_ref[...] = _dot(mix(4), a1_ref[...]).astype(BF16)
    gg_ref[...] = _sigmoid(_dot(mix(5), g1_ref[...])).astype(BF16)


def rwkv_pre(x, g, shift, scale, h_prev, mu, w1, a1, g1, v1):
    B, T, D = x.shape
    bt = min(256, T)
    has_v = v1 is not None
    row = lambda n: pl.BlockSpec((None, bt, n), lambda b, t: (b, t, 0))
    vec = pl.BlockSpec((None, 1, D), lambda b, t: (b, 0, 0))
    full = lambda a: pl.BlockSpec(a.shape, lambda b, t: (0, 0))
    lora = [w1, a1, g1] + ([v1] if has_v else [])
    outs = [(D, BF16)] * 3 + [(w1.shape[1], BF16), (a1.shape[1], BF16), (g1.shape[1], BF16)]
    if has_v:
        outs.append((v1.shape[1], BF16))
    res = pl.pallas_call(
        functools.partial(_rwkv_pre_kernel, has_v=has_v),
        grid=(B, T // bt),
        in_specs=[row(D), pl.BlockSpec((1, D), lambda b, t: (0, 0)), vec, vec, vec, full(mu)]
                 + [full(a) for a in lora],
        out_specs=[row(n) for n, _ in outs] + [vec],
        out_shape=[jax.ShapeDtypeStruct((B, T, n), dt) for n, dt in outs]
                  + [jax.ShapeDtypeStruct((B, 1, D), F32)],
        scratch_shapes=[pltpu.VMEM((1, D), F32)],
        compiler_params=_params(("parallel", "arbitrary")),
        name="rwkv_pre",
    )(x, g.reshape(1, D), shift.reshape(B, 1, D), scale.reshape(B, 1, D), h_prev.reshape(B, 1, D), mu, *lora)
    return res


def _rwkv_proj_kernel(*refs, has_v):
    if has_v:
        (xr_ref, xk_ref, xv_ref, tw_ref, av_ref, gg_ref, vv_ref, vf_ref,
         wr_ref, wk_ref, wv_ref, w2_ref, a2_ref, g2_ref, v2_ref, vec_ref, ones_ref,
         r_ref, ld_ref, k_ref, v_ref, kk_ref, b_ref, g_ref) = refs
    else:
        (xr_ref, xk_ref, xv_ref, tw_ref, av_ref, gg_ref,
         wr_ref, wk_ref, wv_ref, w2_ref, a2_ref, g2_ref, vec_ref, ones_ref,
         r_ref, ld_ref, k_ref, v_ref, kk_ref, b_ref, g_ref) = refs
    vec = vec_ref[...]
    w0, a0, v0, k_k, k_a = (vec[i:i + 1, :] for i in range(5))
    r_ref[...] = _dot(xr_ref[...], wr_ref[...])
    kraw = _dot(xk_ref[...], wk_ref[...])
    v = _dot(xv_ref[...], wv_ref[...])
    nz = -(w0 + _dot(tw_ref[...], w2_ref[...]))
    softplus = jnp.maximum(nz, 0.0) + jnp.log(1.0 + jnp.exp(-jnp.abs(nz)))
    ld_ref[...] = -jnp.exp(-softplus - 0.5)
    a = _sigmoid(a0 + _dot(av_ref[...], a2_ref[...]))
    if has_v:
        v = v + (vf_ref[...] - v) * _sigmoid(v0 + _dot(vv_ref[...], v2_ref[...]))
    v_ref[...] = v
    g_ref[...] = _dot(gg_ref[...], g2_ref[...])
    kk = kraw * k_k
    hi, lo = _split2(kk * kk)
    ss = _dot(hi, ones_ref[...]) + _dot(lo, ones_ref[...])
    kk = kk / jnp.maximum(jnp.sqrt(ss), 1e-12)
    kk_ref[...] = kk
    b_ref[...] = kk * a
    k_ref[...] = kraw * (1.0 + (a - 1.0) * k_a)


def rwkv_proj(pre, v_first, wr, wk, wv, w2, a2, g2, v2, vecs, ones_bd):
    has_v = v2 is not None
    xr = pre[0]
    M, D = xr.shape
    HL = ones_bd.shape[0]
    bm = min(512, M)
    row = lambda a: pl.BlockSpec((bm, a.shape[1]), lambda i, j: (i, 0))
    tile = pl.BlockSpec((bm, HL), lambda i, j: (i, j))
    col = lambda a: pl.BlockSpec((a.shape[0], HL), lambda i, j: (0, j))
    acts = list(pre[:6]) + ([pre[6]] if has_v else [])
    weights = [wr, wk, wv, w2, a2, g2] + ([v2] if has_v else [])
    return pl.pallas_call(
        functools.partial(_rwkv_proj_kernel, has_v=has_v),
        grid=(M // bm, D // HL),
        in_specs=[row(a) for a in acts] + ([tile] if has_v else []) + [col(w) for w in weights]
                 + [col(vecs), pl.BlockSpec((HL, HL), lambda i, j: (0, 0))],
        out_specs=[tile] * 7,
        out_shape=[jax.ShapeDtypeStruct((M, D), F32)] * 7,
        compiler_params=_params(("parallel", "arbitrary")),
        name="rwkv_proj",
    )(*acts, *([v_first] if has_v else []), *weights, vecs, ones_bd)


def _recur_kernel(r_ref, ld_ref, k_ref, v_ref, kk_ref, b_ref, g_ref, s0_ref, vec_ref,
                  z_ref, sout_ref, s_scr, *, C, NC, HP):
    ci = pl.program_id(2)
    C2 = 2 * C
    RB = NC * C

    @pl.when(ci == 0)
    def _():
        s_scr[...] = s0_ref[...]

    ti = lax.broadcasted_iota(jnp.int32, (RB, RB), 0)
    tj = lax.broadcasted_iota(jnp.int32, (RB, RB), 1)
    tri = jnp.where((_blk(ti, C) == _blk(tj, C)) & (ti >= tj), 1.0, 0.0).astype(BF16)
    ld = ld_ref[...]
    cl = sum(_dot(tri, part) for part in _split3(ld))
    p_in = jnp.exp(cl)
    p_inv = jnp.exp(-cl)
    r, k, v, b = r_ref[...], k_ref[...], v_ref[...], b_ref[...]
    a_t = -(jnp.exp(cl - ld) * kk_ref[...])
    r_t = p_in * r
    b_t = p_inv * b
    k_t = p_inv * k
    rows = [slice(c * C, (c + 1) * C) for c in range(NC)]
    cl_end = [cl[(c + 1) * C - 1:(c + 1) * C, :] for c in range(NC)]
    p_rem = [jnp.exp(cl_end[c] - cl[rows[c], :]) for c in range(NC)]
    p_end = [jnp.exp(cl_end[c]) for c in range(NC)]
    b_h = [p_rem[c] * b[rows[c], :] for c in range(NC)]
    k_h = [p_rem[c] * k[rows[c], :] for c in range(NC)]
    vec = vec_ref[...]
    r_k, lnx_g, lnx_b = (vec[i:i + 1, :] for i in range(3))
    rk = r * k * r_k
    g = g_ref[...]

    lane = lax.broadcasted_iota(jnp.int32, (1, LANES), 1)
    first = lane < RW_HEAD

    def stack(x):
        return jnp.concatenate([jnp.where(first, x, 0.0), jnp.where(first, 0.0, x)], axis=0).astype(BF16)

    def fold(x):
        return x[0:C, :] + x[C:C2, :]

    ri = lax.broadcasted_iota(jnp.int32, (C2, C2), 0)
    cj = lax.broadcasted_iota(jnp.int32, (C2, C2), 1)
    same = _blk(ri, C) == _blk(cj, C)
    strict = same & (_off(ri, C) > _off(cj, C))
    incl = same & (_off(ri, C) >= _off(cj, C))
    eye = jnp.where(ri == cj, 1.0, 0.0)
    same_head = (_blk(lax.broadcasted_iota(jnp.int32, (LANES, LANES), 0), RW_HEAD)
                 == _blk(lax.broadcasted_iota(jnp.int32, (LANES, LANES), 1), RW_HEAD))
    ones_bd = jnp.where(same_head, 1.0, 0.0).astype(BF16)

    lanes = [slice(p * LANES, (p + 1) * LANES) for p in range(HP)]
    probs = [(c, p) for c in range(NC) for p in range(HP)]
    a_b = {cp: a_t[rows[cp[0]], lanes[cp[1]]].astype(BF16) for cp in probs}
    r_b = {cp: r_t[rows[cp[0]], lanes[cp[1]]].astype(BF16) for cp in probs}
    v_st = {cp: stack(v[rows[cp[0]], lanes[cp[1]]]) for cp in probs}
    gb, gk = {}, {}
    for c, p in probs:
        q_st = jnp.concatenate([stack(a_t[rows[c], lanes[p]]), stack(r_t[rows[c], lanes[p]])], axis=0)
        b_c = b_t[rows[c], lanes[p]].astype(BF16)
        k_c = k_t[rows[c], lanes[p]].astype(BF16)
        gb[c, p] = _dot_nt(q_st, jnp.concatenate([b_c, b_c], axis=0))
        gk[c, p] = _dot_nt(q_st, jnp.concatenate([k_c, k_c], axis=0))
    low = {cp: jnp.where(strict, gb[cp][0:C2, :], 0.0) for cp in probs}
    rb_f = {cp: fold(jnp.where(incl, gb[cp][C2:, :], 0.0)).astype(BF16) for cp in probs}
    akv = {cp: _dot(fold(jnp.where(strict, gk[cp][0:C2, :], 0.0)).astype(BF16), v_st[cp]) for cp in probs}
    rkv = {cp: _dot(fold(jnp.where(incl, gk[cp][C2:, :], 0.0)).astype(BF16), v_st[cp]) for cp in probs}

    first_level = _lower_left(ri, cj, 1)
    t_inv = {cp: eye + jnp.where(first_level, low[cp], 0.0) for cp in probs}
    s = 2
    while s < C:
        sel = _lower_left(ri, cj, s)
        tb = {cp: t_inv[cp].astype(BF16) for cp in probs}
        mid = {cp: _dot(tb[cp], jnp.where(sel, low[cp], 0.0).astype(BF16)).astype(BF16) for cp in probs}
        t_inv = {cp: t_inv[cp] + _dot(mid[cp], tb[cp]) for cp in probs}
        s *= 2
    t_f = {cp: fold(t_inv[cp]).astype(BF16) for cp in probs}

    state = [s_scr[p] for p in range(HP)]
    y = {}
    for c in range(NC):
        state_b = [st.astype(BF16) for st in state]
        x = [_dot_nt(a_b[c, p], state_b[p]) + akv[c, p] for p in range(HP)]
        y0 = [_dot_nt(r_b[c, p], state_b[p]) + rkv[c, p] for p in range(HP)]
        u = [_dot(t_f[c, p], stack(x[p])) for p in range(HP)]
        for p in range(HP):
            y[c, p] = y0[p] + _dot(rb_f[c, p], stack(u[p]))
        upd = []
        for p in range(HP):
            uv = jnp.concatenate([u[p], v[rows[c], lanes[p]]], axis=0).astype(BF16)
            bk = jnp.concatenate([b_h[c][:, lanes[p]], k_h[c][:, lanes[p]]], axis=0).astype(BF16)
            upd.append(_dot_tn(uv, bk))
        state = [state[p] * p_end[c][:, lanes[p]] + jnp.where(same_head, upd[p], 0.0) for p in range(HP)]
    for p in range(HP):
        s_scr[p] = state[p]

    def seg(t):
        hi, lo = _split2(t)
        return _dot(hi, ones_bd) + _dot(lo, ones_bd)

    y_all = [jnp.concatenate([y[c, p] for c in range(NC)], axis=0) for p in range(HP)]
    mean = [seg(y_all[p]) * (1.0 / RW_HEAD) for p in range(HP)]
    bonus = [seg(rk[:, lanes[p]]) * v[:, lanes[p]] for p in range(HP)]
    d = [y_all[p] - mean[p] for p in range(HP)]
    var = [seg(d[p] * d[p]) * (1.0 / RW_HEAD) for p in range(HP)]
    for p in range(HP):
        yn = d[p] * lax.rsqrt(var[p] + GN_EPS) * lnx_g[:, lanes[p]] + lnx_b[:, lanes[p]]
        z_ref[:, lanes[p]] = ((yn + bonus[p]) * g[:, lanes[p]]).astype(z_ref.dtype)

    @pl.when(ci == pl.num_programs(2) - 1)
    def _():
        sout_ref[...] = s_scr[...]


def rwkv_recur(ops, s0_bd, vecs, B, T):
    D = ops[0].shape[1]
    HL = min(512, D)
    HP = HL // LANES
    C = min(CHUNK, T)
    NC = 2 if T % (2 * C) == 0 else 1
    ops3 = [o.reshape(B, T, D) for o in ops]
    tile = pl.BlockSpec((None, NC * C, HL), lambda b, j, c: (b, c, j))
    st = pl.BlockSpec((None, HP, LANES, LANES), lambda b, j, c: (b, j, 0, 0))
    z, s_fin = pl.pallas_call(
        functools.partial(_recur_kernel, C=C, NC=NC, HP=HP),
        grid=(B, D // HL, T // (NC * C)),
        in_specs=[tile] * 7 + [st, pl.BlockSpec((vecs.shape[0], HL), lambda b, j, c: (0, j))],
        out_specs=[tile, st],
        out_shape=[jax.ShapeDtypeStruct((B, T, D), BF16), jax.ShapeDtypeStruct(s0_bd.shape, F32)],
        scratch_shapes=[pltpu.VMEM((HP, LANES, LANES), F32)],
        compiler_params=_params(("parallel", "parallel", "arbitrary")),
        name="rwkv_recur",
    )(*ops3, s0_bd, vecs)
    return z.reshape(B * T, D), s_fin


def _gate_operand(gate, B, T, bm, bn):
    N = gate.shape[1]
    col = (lambda j: j) if bn < N else (lambda j: 0)
    if T % bm == 0:
        return gate.reshape(B, 1, N), pl.BlockSpec((None, 1, bn), lambda i, j: ((i * bm) // T, 0, col(j)))
    rows = jnp.broadcast_to(gate[:, None, :], (B, T, N)).reshape(B * T, N)
    return rows, pl.BlockSpec((bm, bn), lambda i, j: (i, col(j)))


def _matmul_res_kernel(a_ref, w_ref, x_ref, gate_ref, o_ref):
    o_ref[...] = x_ref[...] + gate_ref[...] * _dot(a_ref[...], w_ref[...])


def matmul_res(a, w, x, gate, B, T):
    M, K = a.shape
    N = w.shape[1]
    bm, bn = min(1024, M), min(512, N)
    gate_arr, gate_spec = _gate_operand(gate, B, T, bm, bn)
    return pl.pallas_call(
        _matmul_res_kernel,
        grid=(M // bm, N // bn),
        in_specs=[pl.BlockSpec((bm, K), lambda i, j: (i, 0)),
                  pl.BlockSpec((K, bn), lambda i, j: (0, j)),
                  pl.BlockSpec((bm, bn), lambda i, j: (i, j)), gate_spec],
        out_specs=pl.BlockSpec((bm, bn), lambda i, j: (i, j)),
        out_shape=jax.ShapeDtypeStruct((M, N), F32),
        compiler_params=_params(("parallel", "arbitrary")),
        name="matmul_res",
    )(a, w, x, gate_arr)


def _mlp_kernel(h_ref, w1_ref, w2_ref, x_ref, gate_ref, o_ref, acc):
    f = pl.program_id(1)
    t = jnp.square(jnp.maximum(_dot(h_ref[...], w1_ref[...]), 0.0)).astype(BF16)
    part = _dot(t, w2_ref[...])

    @pl.when(f == 0)
    def _():
        acc[...] = part

    @pl.when(f > 0)
    def _():
        acc[...] += part

    @pl.when(f == pl.num_programs(1) - 1)
    def _():
        o_ref[...] = x_ref[...] + gate_ref[...] * acc[...]


def mlp_res(h, w1, w2, x, gate, B, T):
    M, D = h.shape
    F = w1.shape[1]
    bm, bf = min(512, M), min(512, F)
    gate_arr, gate_spec = _gate_operand(gate, B, T, bm, D)
    return pl.pallas_call(
        _mlp_kernel,
        grid=(M // bm, F // bf),
        in_specs=[pl.BlockSpec((bm, D), lambda i, f: (i, 0)),
                  pl.BlockSpec((D, bf), lambda i, f: (0, f)),
                  pl.BlockSpec((bf, D), lambda i, f: (f, 0)),
                  pl.BlockSpec((bm, D), lambda i, f: (i, 0)), gate_spec],
        out_specs=pl.BlockSpec((bm, D), lambda i, f: (i, 0)),
        out_shape=jax.ShapeDtypeStruct((M, D), F32),
        scratch_shapes=[pltpu.VMEM((bm, D), F32)],
        compiler_params=_params(("parallel", "arbitrary")),
        name="mlp_res",
    )(h, w1, w2, x, gate_arr)


def _rope_pair(t, tab):
    prod = t * tab
    return (prod + pltpu.roll(prod, ROPE_DIM, 1))[:, :ROPE_DIM]


def _latent_kernel(h_ref, w_ref, g_ref, tab_ref, ckv_ref, kpe_ref):
    acc = _dot(h_ref[...], w_ref[...])
    R = g_ref.shape[1]
    ckv_ref[...] = _rms(acc[:, :R], g_ref[...])
    kpe_ref[...] = _rope_pair(acc[:, R:R + 2 * ROPE_DIM], tab_ref[...])


def latent(h, w, g, tab):
    B, T, D = h.shape
    R = g.shape[0]
    bt = min(512, T)
    return pl.pallas_call(
        _latent_kernel,
        grid=(B, T // bt),
        in_specs=[pl.BlockSpec((None, bt, D), lambda b, t: (b, t, 0)),
                  pl.BlockSpec(w.shape, lambda b, t: (0, 0)),
                  pl.BlockSpec((1, R), lambda b, t: (0, 0)),
                  pl.BlockSpec((bt, 2 * ROPE_DIM), lambda b, t: (t, 0))],
        out_specs=[pl.BlockSpec((None, bt, R), lambda b, t: (b, t, 0)),
                   pl.BlockSpec((None, bt, ROPE_DIM), lambda b, t: (b, t, 0))],
        out_shape=[jax.ShapeDtypeStruct((B, T, R), F32), jax.ShapeDtypeStruct((B, T, ROPE_DIM), F32)],
        compiler_params=_params(("parallel", "parallel")),
        name="latent",
    )(h, w, g.reshape(1, R), tab)


def _kv_expand_kernel(c_ref, pe_ref, wuk_ref, wuv_ref, k_ref, v_ref, *, v_transposed):
    c = c_ref[...].astype(BF16)
    kn = _dot(c, wuk_ref[...])
    vv = _dot(c, wuv_ref[...])
    pe = pe_ref[...].astype(BF16)
    for h in range(k_ref.shape[0]):
        k_ref[h, :, 0:NOPE_DIM] = kn[:, h * NOPE_DIM:(h + 1) * NOPE_DIM].astype(BF16)
        k_ref[h, :, NOPE_DIM:NOPE_DIM + ROPE_DIM] = pe
        v_h = vv[:, h * V_DIM:(h + 1) * V_DIM]
        v_ref[h] = (v_h.T if v_transposed else v_h).astype(BF16)


def kv_expand(ckv, kpe, wuk, wuv, H, v_transposed):
    B, S, R = ckv.shape
    bt = 512 if S % 512 == 0 else S
    hb = min(8, H)
    DK = NOPE_DIM + ROPE_DIM
    if v_transposed:
        v_spec = pl.BlockSpec((None, hb, V_DIM, bt), lambda b, t, j: (b, j, 0, t))
        v_shape = (B, H, V_DIM, S)
    else:
        v_spec = pl.BlockSpec((None, hb, bt, V_DIM), lambda b, t, j: (b, j, t, 0))
        v_shape = (B, H, S, V_DIM)
    return pl.pallas_call(
        functools.partial(_kv_expand_kernel, v_transposed=v_transposed),
        grid=(B, S // bt, H // hb),
        in_specs=[pl.BlockSpec((None, bt, R), lambda b, t, j: (b, t, 0)),
                  pl.BlockSpec((None, bt, ROPE_DIM), lambda b, t, j: (b, t, 0)),
                  pl.BlockSpec((R, hb * NOPE_DIM), lambda b, t, j: (0, j)),
                  pl.BlockSpec((R, hb * V_DIM), lambda b, t, j: (0, j))],
        out_specs=[pl.BlockSpec((None, hb, bt, DK), lambda b, t, j: (b, j, t, 0)), v_spec],
        out_shape=[jax.ShapeDtypeStruct((B, H, S, DK), BF16), jax.ShapeDtypeStruct(v_shape, BF16)],
        compiler_params=_params(("parallel", "parallel", "arbitrary")),
        name="kv_expand",
    )(ckv, kpe, wuk, wuv)


def _wdq_kernel(h_ref, w_ref, g_ref, o_ref):
    o_ref[...] = _rms(_dot(h_ref[...], w_ref[...]), g_ref[...]).astype(BF16)


def wdq_norm(h, w, g):
    M, D = h.shape
    R = w.shape[1]
    bm = min(1024, M)
    return pl.pallas_call(
        _wdq_kernel,
        grid=(M // bm,),
        in_specs=[pl.BlockSpec((bm, D), lambda i: (i, 0)), pl.BlockSpec((D, R), lambda i: (0, 0)),
                  pl.BlockSpec((1, R), lambda i: (0, 0))],
        out_specs=pl.BlockSpec((bm, R), lambda i: (i, 0)),
        out_shape=jax.ShapeDtypeStruct((M, R), BF16),
        compiler_params=_params(("parallel",)),
        name="wdq_norm",
    )(h, w, g.reshape(1, R))


def _wuq_kernel(c_ref, w_ref, tab_ref, q_ref):
    acc = _dot(c_ref[...], w_ref[...])
    tab = tab_ref[...]
    W = NOPE_DIM + 2 * ROPE_DIM
    for h in range(q_ref.shape[0]):
        q_ref[h, :, 0:NOPE_DIM] = (acc[:, h * W:h * W + NOPE_DIM] * MLA_SCALE).astype(BF16)
        pe = _rope_pair(acc[:, h * W + NOPE_DIM:(h + 1) * W], tab)
        q_ref[h, :, NOPE_DIM:NOPE_DIM + ROPE_DIM] = (pe * MLA_SCALE).astype(BF16)


def wuq_rope(cq, w, tab, H):
    B, T, R = cq.shape
    bt = min(512, T)
    hb = min(8, H)
    W = NOPE_DIM + 2 * ROPE_DIM
    DK = NOPE_DIM + ROPE_DIM
    return pl.pallas_call(
        _wuq_kernel,
        grid=(B, T // bt, H // hb),
        in_specs=[pl.BlockSpec((None, bt, R), lambda b, t, j: (b, t, 0)),
                  pl.BlockSpec((R, hb * W), lambda b, t, j: (0, j)),
                  pl.BlockSpec((bt, 2 * ROPE_DIM), lambda b, t, j: (t, 0))],
        out_specs=pl.BlockSpec((None, hb, bt, DK), lambda b, t, j: (b, j, t, 0)),
        out_shape=jax.ShapeDtypeStruct((B, H, T, DK), BF16),
        compiler_params=_params(("parallel", "parallel", "arbitrary")),
        name="wuq_rope",
    )(cq, w, tab)


def _visible(qpos, kpos):
    return _blk(kpos, CHUNK) <= _blk(qpos, CHUNK)


def _flash_kernel(q_ref, k_ref, vt_ref, o_ref, *, tile):
    hb = q_ref.shape[0]
    qi = pl.program_id(2)
    qpos = qi * tile + lax.broadcasted_iota(jnp.int32, (1, tile), 1)

    def step(ki, carry, masked):
        start = pl.multiple_of(ki * tile, tile)
        scores = [_dot_nt(k_ref[h, pl.ds(start, tile), :], q_ref[h]) for h in range(hb)]
        out = []
        for h in range(hb):
            m, l, acc = carry[h]
            s = scores[h]
            if masked:
                kpos = ki * tile + lax.broadcasted_iota(jnp.int32, (tile, 1), 0)
                s = jnp.where(_visible(qpos, kpos), s, -jnp.inf)
            m_new = jnp.maximum(m, jnp.max(s, axis=0, keepdims=True))
            alpha = jnp.exp(m - m_new)
            p = jnp.exp(s - m_new)
            l = alpha * l + jnp.sum(p, axis=0, keepdims=True)
            acc = alpha * acc + _dot(vt_ref[h, :, pl.ds(start, tile)], p.astype(BF16))
            out.append((m_new, l, acc))
        return tuple(out)

    init = tuple((jnp.full((1, tile), -jnp.inf, F32), jnp.zeros((1, tile), F32), jnp.zeros((V_DIM, tile), F32))
                 for _ in range(hb))
    carry = lax.fori_loop(0, qi, lambda ki, c: step(ki, c, False), init)
    carry = step(qi, carry, True)
    for h in range(hb):
        _, l, acc = carry[h]
        o_ref[:, h * V_DIM:(h + 1) * V_DIM] = (acc / l).T.astype(o_ref.dtype)


def flash_prompt(q, k, vt):
    B, H, T, DK = q.shape
    tile = min(256, T)
    assert tile % CHUNK == 0 and T % tile == 0
    hb = 4 if H % 4 == 0 else 1
    return pl.pallas_call(
        functools.partial(_flash_kernel, tile=tile),
        grid=(B, H // hb, T // tile),
        in_specs=[pl.BlockSpec((None, hb, tile, DK), lambda b, j, i: (b, j, i, 0)),
                  pl.BlockSpec((None, hb, T, DK), lambda b, j, i: (b, j, 0, 0)),
                  pl.BlockSpec((None, hb, V_DIM, T), lambda b, j, i: (b, j, 0, 0))],
        out_specs=pl.BlockSpec((None, tile, hb * V_DIM), lambda b, j, i: (b, i, j)),
        out_shape=jax.ShapeDtypeStruct((B, T, H * V_DIM), BF16),
        compiler_params=_params(("parallel", "parallel", "arbitrary")),
        name="flash_prompt",
    )(q, k, vt)


def _attn_full_kernel(q_ref, k_ref, v_ref, o_ref, *, q0):
    tq, S = q_ref.shape[0], k_ref.shape[0]
    qpos = q0 + lax.broadcasted_iota(jnp.int32, (tq, 1), 0)
    kpos = lax.broadcasted_iota(jnp.int32, (1, S), 1)
    s = jnp.where(_visible(qpos, kpos), _dot_nt(q_ref[...], k_ref[...]), -jnp.inf)
    p = jnp.exp(s - jnp.max(s, axis=-1, keepdims=True))
    l = jnp.sum(p, axis=-1, keepdims=True)
    o_ref[...] = (_dot(p.astype(BF16), v_ref[...]) / l).astype(o_ref.dtype)


def attn_full(q, k, v, q0):
    B, H, T, DK = q.shape
    S = k.shape[2]
    return pl.pallas_call(
        functools.partial(_attn_full_kernel, q0=q0),
        grid=(B, H),
        in_specs=[pl.BlockSpec((None, None, T, DK), lambda b, h: (b, h, 0, 0)),
                  pl.BlockSpec((None, None, S, DK), lambda b, h: (b, h, 0, 0)),
                  pl.BlockSpec((None, None, S, V_DIM), lambda b, h: (b, h, 0, 0))],
        out_specs=pl.BlockSpec((None, T, V_DIM), lambda b, h: (b, 0, h)),
        out_shape=jax.ShapeDtypeStruct((B, T, H * V_DIM), BF16),
        compiler_params=_params(("parallel", "parallel")),
        name="attn_full",
    )(q, k, v)


def _pad_cols(w, n):
    return jnp.pad(w, ((0, 0), (0, n - w.shape[1])))


def _pad_rows(w, n):
    return jnp.pad(w, ((0, n - w.shape[0]), (0, 0)))


def _rotate_half_cols(w):
    half = ROPE_DIM // 2
    return jnp.concatenate([-w[..., half:], w[..., :half]], axis=-1)


def _rope_table(pos):
    half = ROPE_DIM // 2
    inv = ROPE_THETA ** (-jnp.arange(half, dtype=F32) / half)
    ang = pos.astype(F32)[:, None] * inv[None, :]
    cos, sin = jnp.cos(ang), jnp.sin(ang)
    return jnp.concatenate([cos, cos, sin, sin], axis=-1)


def _block_diag_states(s):
    B, H, N, _ = s.shape
    s = s.reshape(B, H // 2, 2, N, N)
    z = jnp.zeros_like(s[:, :, 0])
    top = jnp.concatenate([s[:, :, 0], z], axis=-1)
    bot = jnp.concatenate([z, s[:, :, 1]], axis=-1)
    return jnp.concatenate([top, bot], axis=-2)


def _diag_states(s):
    B, HP = s.shape[:2]
    N = RW_HEAD
    return jnp.stack([s[:, :, :N, :N], s[:, :, N:, N:]], axis=2).reshape(B, 2 * HP, N, N)


def _prepare(W):
    D = W['rw_wr'].shape[1]
    P = {}
    bf = lambda a: a.astype(BF16)
    P['mlp_w1'], P['mlp_w2'] = bf(W['mlp_w1']), bf(W['mlp_w2'])
    for n in ('rw_wr', 'rw_wk', 'rw_wv', 'rw_wo', 'rw_g1', 'rw_g2', 'mla_wdq', 'mla_wo'):
        P[n] = bf(W[n])
    NA = W['rw_wr'].shape[0]
    P['rw_w1'] = [bf(_pad_cols(W['rw_w1'][l], LORA_PAD)) for l in range(NA)]
    P['rw_w2'] = [bf(_pad_rows(W['rw_w2'][l], LORA_PAD)) for l in range(NA)]
    P['rw_a1'] = [bf(_pad_cols(W['rw_a1'][l], LORA_PAD)) for l in range(NA)]
    P['rw_a2'] = [bf(_pad_rows(W['rw_a2'][l], LORA_PAD)) for l in range(NA)]
    P['rw_v1'] = [bf(_pad_cols(W['rw_v1'][l], LORA_PAD)) for l in range(NA - 1)]
    P['rw_v2'] = [bf(_pad_rows(W['rw_v2'][l], LORA_PAD)) for l in range(NA - 1)]
    zeros = jnp.zeros((D,), F32)
    P['proj_vecs'], P['recur_vecs'] = [], []
    for l in range(NA):
        v0 = W['rw_v0'][l - 1] if l > 0 else zeros
        P['proj_vecs'].append(jnp.stack([W['rw_w0'][l], W['rw_a0'][l], v0, W['rw_kk'][l], W['rw_ka'][l],
                                         zeros, zeros, zeros]))
        P['recur_vecs'].append(jnp.stack([W['rw_rk'][l], W['rw_lnx_g'][l], W['rw_lnx_b'][l]] + [zeros] * 5))
    HL = min(512, D)
    head = jnp.arange(HL) // RW_HEAD
    P['ones_bd'] = (head[:, None] == head[None, :]).astype(BF16)
    R = W['kv_lat_g'].shape[0]
    wd = W['kv_wd']
    P['kv_wd'] = bf(jnp.concatenate([wd, _rotate_half_cols(wd[:, R:])], axis=1))
    H = W['kv_wuk'].shape[1]
    P['kv_wuk'] = bf(W['kv_wuk'].reshape(R, H * NOPE_DIM))
    P['kv_wuv'] = bf(W['kv_wuv'].reshape(R, H * V_DIM))
    NB, Q = W['mla_wuq'].shape[:2]
    wuq = W['mla_wuq'].reshape(NB, Q, H, NOPE_DIM + ROPE_DIM)
    pe = wuq[..., NOPE_DIM:]
    P['mla_wuq'] = bf(jnp.concatenate([wuq, _rotate_half_cols(pe)], axis=-1).reshape(NB, Q, -1))
    return P


def _trunk(x, mod, kv_mod, pos0, h_prev, s0, past_ckv, past_kpe, W, P):
    B, T, D = x.shape
    M = B * T
    depth = W['ada_w'].shape[0]
    NA = W['rw_wr'].shape[0]
    H = W['kv_wuk'].shape[1]
    tab = _rope_table(pos0 + jnp.arange(T))
    xf = x.reshape(M, D)
    shifts, states = [], []
    v_first = None
    keys = vals = ckv = kpe = None
    for l in range(depth):
        m = mod[l]
        if l < NA:
            has_v = l > 0
            pre = rwkv_pre(xf.reshape(B, T, D), W['norm_mix_g'][l], m[:, 0], m[:, 1], h_prev[l], W['rw_mu'][l],
                           P['rw_w1'][l], P['rw_a1'][l], P['rw_g1'][l], P['rw_v1'][l - 1] if has_v else None)
            shifts.append(pre[-1].reshape(B, D))
            acts = [a.reshape(M, a.shape[-1]) for a in pre[:-1]]
            ops = rwkv_proj(acts, v_first, P['rw_wr'][l], P['rw_wk'][l], P['rw_wv'][l], P['rw_w2'][l],
                            P['rw_a2'][l], P['rw_g2'][l], P['rw_v2'][l - 1] if has_v else None,
                            P['proj_vecs'][l], P['ones_bd'])
            if l == 0:
                v_first = ops[3]
            z, s_fin = rwkv_recur(ops, _block_diag_states(s0[l]), P['recur_vecs'][l], B, T)
            states.append(_diag_states(s_fin))
            xf = matmul_res(z, P['rw_wo'][l], xf, m[:, 2], B, T)
        else:
            j = l - NA
            h = norm_mod(xf.reshape(B, T, D), W['norm_mix_g'][l], m[:, 0], m[:, 1])
            cq = wdq_norm(h.reshape(M, D), P['mla_wdq'][j], W['mla_q_g'][j])
            q = wuq_rope(cq.reshape(B, T, -1), P['mla_wuq'][j], tab, H)
            if past_ckv is None:
                o = flash_prompt(q, keys, vals)
            else:
                o = attn_full(q, keys, vals, pos0)
            xf = matmul_res(o.reshape(M, H * V_DIM), P['mla_wo'][j], xf, m[:, 2], B, T)
        h = norm_mod(xf.reshape(B, T, D), W['norm_mlp_g'][l], m[:, 3], m[:, 4])
        xf = mlp_res(h.reshape(M, D), P['mlp_w1'][l], P['mlp_w2'][l], xf, m[:, 5], B, T)
        if l == NA - 1:
            hkv = norm_mod(xf.reshape(B, T, D), W['kv_norm_g'], kv_mod[:, 0], kv_mod[:, 1])
            ckv, kpe = latent(hkv, P['kv_wd'], W['kv_lat_g'], tab)
            if past_ckv is None:
                keys, vals = kv_expand(ckv, kpe, P['kv_wuk'], P['kv_wuv'], H, True)
            else:
                keys, vals = kv_expand(jnp.concatenate([past_ckv, ckv], axis=1),
                                       jnp.concatenate([past_kpe, kpe], axis=1), P['kv_wuk'], P['kv_wuv'], H, False)
    y = final_norm(xf.reshape(B, T, D), W['final_g'])
    return y, ckv, kpe, jnp.stack(states), jnp.stack(shifts)


def kernel(x_prompt, x_sample, cache_ckv, cache_kpe, state_wkv, state_shift, c_prompt, c_sample, ada_w, ada_b, norm_mix_g, norm_mlp_g, mlp_w1, mlp_w2, rw_mu, rw_w0, rw_w1, rw_w2, rw_a0, rw_a1, rw_a2, rw_v0, rw_v1, rw_v2, rw_g1, rw_g2, rw_wr, rw_wk, rw_wv, rw_wo, rw_kk, rw_ka, rw_rk, rw_lnx_g, rw_lnx_b, kv_ada_w, kv_ada_b, kv_norm_g, kv_wd, kv_lat_g, kv_wuk, kv_wuv, mla_wdq, mla_q_g, mla_wuq, mla_wo, final_g):
    W = dict(ada_w=ada_w, ada_b=ada_b, norm_mix_g=norm_mix_g, norm_mlp_g=norm_mlp_g,
             mlp_w1=mlp_w1, mlp_w2=mlp_w2, rw_mu=rw_mu, rw_w0=rw_w0, rw_w1=rw_w1, rw_w2=rw_w2,
             rw_a0=rw_a0, rw_a1=rw_a1, rw_a2=rw_a2, rw_v0=rw_v0, rw_v1=rw_v1, rw_v2=rw_v2,
             rw_g1=rw_g1, rw_g2=rw_g2, rw_wr=rw_wr, rw_wk=rw_wk, rw_wv=rw_wv, rw_wo=rw_wo,
             rw_kk=rw_kk, rw_ka=rw_ka, rw_rk=rw_rk, rw_lnx_g=rw_lnx_g, rw_lnx_b=rw_lnx_b,
             kv_ada_w=kv_ada_w, kv_ada_b=kv_ada_b, kv_norm_g=kv_norm_g, kv_wd=kv_wd,
             kv_lat_g=kv_lat_g, kv_wuk=kv_wuk, kv_wuv=kv_wuv, mla_wdq=mla_wdq, mla_q_g=mla_q_g,
             mla_wuq=mla_wuq, mla_wo=mla_wo, final_g=final_g)
    P = _prepare(W)
    Bp, Tp, D = x_prompt.shape
    Bs = x_sample.shape[0]
    depth = ada_w.shape[0]
    NA = rw_wr.shape[0]
    c_all = jnp.concatenate([c_prompt, c_sample], axis=0)
    mod = ada_linear(c_all, ada_w, ada_b).reshape(depth, Bp + Bs, N_MOD, D)
    kv_mod = ada_linear(c_all, kv_ada_w[None], kv_ada_b[None]).reshape(Bp + Bs, 2, D)
    h0 = jnp.zeros((NA, Bp, D), F32)
    s0 = jnp.zeros((NA, Bp, D // RW_HEAD, RW_HEAD, RW_HEAD), F32)
    out_p = _trunk(x_prompt, mod[:, :Bp], kv_mod[:Bp], 0, h0, s0, None, None, W, P)
    out_s = _trunk(x_sample, mod[:, Bp:], kv_mod[Bp:], cache_ckv.shape[1], state_shift, state_wkv,
                   cache_ckv, cache_kpe, W, P)
    return (out_p[0], out_s[0]) + out_p[1:] + out_s[1:]
```

```python
import functools

import jax
import jax.numpy as jnp
from jax import lax
from jax.experimental import pallas as pl
from jax.experimental.pallas import tpu as pltpu

F32, BF16 = jnp.float32, jnp.bfloat16

RW_HEAD = 64
CHUNK = 64
GN_EPS = 64e-5
NOPE_DIM = 128
ROPE_DIM = 64
V_DIM = 128
ROPE_THETA = 10000.0
MLA_SCALE = (NOPE_DIM + ROPE_DIM) ** -0.5
NORM_EPS = 1e-6
N_MOD = 6

LANES = 128
VMEM_LIMIT = 56 * 1024 * 1024
LORA_PAD = 128


def _params(sem, vmem=VMEM_LIMIT):
    return pltpu.CompilerParams(dimension_semantics=sem, vmem_limit_bytes=vmem)


def _dot(a, b):
    return jnp.dot(a, b, preferred_element_type=F32)


def _dot_nt(a, b):
    return lax.dot_general(a, b, (((1,), (1,)), ((), ())), preferred_element_type=F32)


def _dot_tn(a, b):
    return lax.dot_general(a, b, (((0,), (0,)), ((), ())), preferred_element_type=F32)


def _split2(x):
    hi = x.astype(BF16)
    return hi, (x - hi.astype(F32)).astype(BF16)


def _split3(x):
    hi = x.astype(BF16)
    r1 = x - hi.astype(F32)
    mid = r1.astype(BF16)
    return hi, mid, (r1 - mid.astype(F32)).astype(BF16)


def _sigmoid(x):
    return 1.0 / (1.0 + jnp.exp(-x))


def _blk(i, n):
    assert n & (n - 1) == 0
    return i >> (n.bit_length() - 1)


def _off(i, n):
    assert n & (n - 1) == 0
    return i & (n - 1)


def _lower_left(ri, cj, s):
    return (_blk(ri, 2 * s) == _blk(cj, 2 * s)) & (_off(ri, 2 * s) >= s) & (_off(cj, 2 * s) < s)


def _rms(x, g):
    return x * lax.rsqrt(jnp.mean(x * x, axis=-1, keepdims=True) + NORM_EPS) * g


def _ada_kernel(c_ref, w_ref, b_ref, o_ref):
    c = c_ref[...]
    cs = c * _sigmoid(c)
    o_ref[...] = _dot(cs.astype(BF16), w_ref[...].astype(BF16)) + b_ref[...]


def ada_linear(c, w, b):
    L, K, N = w.shape
    M = c.shape[0]
    bn = min(512, N)
    return pl.pallas_call(
        _ada_kernel,
        grid=(L, N // bn),
        in_specs=[pl.BlockSpec((M, K), lambda l, j: (0, 0)),
                  pl.BlockSpec((None, K, bn), lambda l, j: (l, 0, j)),
                  pl.BlockSpec((None, 1, bn), lambda l, j: (l, 0, j))],
        out_specs=pl.BlockSpec((None, M, bn), lambda l, j: (l, 0, j)),
        out_shape=jax.ShapeDtypeStruct((L, M, N), F32),
        compiler_params=_params(("parallel", "parallel")),
        name="ada_linear",
    )(c, w, b.reshape(L, 1, N))


def _norm_kernel(x_ref, g_ref, o_ref):
    o_ref[...] = _rms(x_ref[...], g_ref[...])


def final_norm(x, g):
    B, T, D = x.shape
    bt = min(512, T)
    return pl.pallas_call(
        _norm_kernel,
        grid=(B, T // bt),
        in_specs=[pl.BlockSpec((None, bt, D), lambda b, t: (b, t, 0)),
                  pl.BlockSpec((1, D), lambda b, t: (0, 0))],
        out_specs=pl.BlockSpec((None, bt, D), lambda b, t: (b, t, 0)),
        out_shape=jax.ShapeDtypeStruct((B, T, D), F32),
        compiler_params=_params(("parallel", "parallel")),
        name="final_norm",
    )(x, g.reshape(1, D))


def _rwkv_pre_kernel(*refs, has_v):
    if has_v:
        (x_ref, g_ref, sh_ref, sc_ref, hp_ref, mu_ref, w1_ref, a1_ref, g1_ref, v1_ref,
         xr_ref, xk_ref, xv_ref, tw_ref, av_ref, gg_ref, vv_ref, hl_ref, prev) = refs
    else:
        (x_ref, g_ref, sh_ref, sc_ref, hp_ref, mu_ref, w1_ref, a1_ref, g1_ref,
         xr_ref, xk_ref, xv_ref, tw_ref, av_ref, gg_ref, hl_ref, prev) = refs
    bt = x_ref.shape[0]
    h = _rms(x_ref[...], g_ref[...]) * (1.0 + sc_ref[...]) + sh_ref[...]

    @pl.when(pl.program_id(1) == 0)
    def _():
        prev[...] = hp_ref[...]

    row = lax.broadcasted_iota(jnp.int32, h.shape, 0)
    xx = jnp.where(row == 0, prev[...], pltpu.roll(h, 1, 0)) - h
    last = h[bt - 1:bt, :]
    prev[...] = last
    hl_ref[...] = last
    mu = mu_ref[...]

    def mix(i):
        return (h + xx * mu[i:i + 1, :]).astype(BF16)

    xr_ref[...] = mix(0)
    tw_ref[...] = jnp.tanh(_dot(mix(1), w1_ref[...])).astype(BF16)
    xk_ref[...] = mix(2)
    xv = mix(3)
    xv_ref[...] = xv
    if has_v:
        vv_ref[...] = _dot(xv, v1_ref[...]).astype(BF16)
    av_ref[...] = _dot(mix(4), a1_ref[...]).astype(BF16)
    gg_ref[...] = _sigmoid(_dot(mix(5), g1_ref[...])).astype(BF16)


def rwkv_pre(x, g, shift, scale, h_prev, mu, w1, a1, g1, v1):
    B, T, D = x.shape
    bt = min(256, T)
    has_v = v1 is not None
    row = lambda n: pl.BlockSpec((None, bt, n), lambda b, t: (b, t, 0))
    vec = pl.BlockSpec((None, 1, D), lambda b, t: (b, 0, 0))
    full = lambda a: pl.BlockSpec(a.shape, lambda b, t: (0, 0))
    lora = [w1, a1, g1] + ([v1] if has_v else [])
    outs = [(D, BF16)] * 3 + [(w1.shape[1], BF16), (a1.shape[1], BF16), (g1.shape[1], BF16)]
    if has_v:
        outs.append((v1.shape[1], BF16))
    res = pl.pallas_call(
        functools.partial(_rwkv_pre_kernel, has_v=has_v),
        grid=(B, T // bt),
        in_specs=[row(D), pl.BlockSpec((1, D), lambda b, t: (0, 0)), vec, vec, vec, full(mu)]
                 + [full(a) for a in lora],
        out_specs=[row(n) for n, _ in outs] + [vec],
        out_shape=[jax.ShapeDtypeStruct((B, T, n), dt) for n, dt in outs]
                  + [jax.ShapeDtypeStruct((B, 1, D), F32)],
        scratch_shapes=[pltpu.VMEM((1, D), F32)],
        compiler_params=_params(("parallel", "arbitrary")),
        name="rwkv_pre",
    )(x, g.reshape(1, D), shift.reshape(B, 1, D), scale.reshape(B, 1, D), h_prev.reshape(B, 1, D), mu, *lora)
    return res


def _rwkv_proj_kernel(*refs, has_v):
    if has_v:
        (xr_ref, xk_ref, xv_ref, tw_ref, av_ref, gg_ref, vv_ref, vf_ref,
         wr_ref, wk_ref, wv_ref, w2_ref, a2_ref, g2_ref, v2_ref, vec_ref, ones_ref,
         r_ref, ld_ref, k_ref, v_ref, kk_ref, b_ref, g_ref) = refs
    else:
        (xr_ref, xk_ref, xv_ref, tw_ref, av_ref, gg_ref,
         wr_ref, wk_ref, wv_ref, w2_ref, a2_ref, g2_ref, vec_ref, ones_ref,
         r_ref, ld_ref, k_ref, v_ref, kk_ref, b_ref, g_ref) = refs
    vec = vec_ref[...]
    w0, a0, v0, k_k, k_a = (vec[i:i + 1, :] for i in range(5))
    r_ref[...] = _dot(xr_ref[...], wr_ref[...])
    kraw = _dot(xk_ref[...], wk_ref[...])
    v = _dot(xv_ref[...], wv_ref[...])
    nz = -(w0 + _dot(tw_ref[...], w2_ref[...]))
    softplus = jnp.maximum(nz, 0.0) + jnp.log(1.0 + jnp.exp(-jnp.abs(nz)))
    ld_ref[...] = -jnp.exp(-softplus - 0.5)
    a = _sigmoid(a0 + _dot(av_ref[...], a2_ref[...]))
    if has_v:
        v = v + (vf_ref[...] - v) * _sigmoid(v0 + _dot(vv_ref[...], v2_ref[...]))
    v_ref[...] = v
    g_ref[...] = _dot(gg_ref[...], g2_ref[...])
    kk = kraw * k_k
    hi, lo = _split2(kk * kk)
    ss = _dot(hi, ones_ref[...]) + _dot(lo, ones_ref[...])
    kk = kk / jnp.maximum(jnp.sqrt(ss), 1e-12)
    kk_ref[...] = kk
    b_ref[...] = kk * a
    k_ref[...] = kraw * (1.0 + (a - 1.0) * k_a)


def rwkv_proj(pre, v_first, wr, wk, wv, w2, a2, g2, v2, vecs, ones_bd):
    has_v = v2 is not None
    xr = pre[0]
    M, D = xr.shape
    HL = ones_bd.shape[0]
    bm = min(512, M)
    row = lambda a: pl.BlockSpec((bm, a.shape[1]), lambda i, j: (i, 0))
    tile = pl.BlockSpec((bm, HL), lambda i, j: (i, j))
    col = lambda a: pl.BlockSpec((a.shape[0], HL), lambda i, j: (0, j))
    acts = list(pre[:6]) + ([pre[6]] if has_v else [])
    weights = [wr, wk, wv, w2, a2, g2] + ([v2] if has_v else [])
    return pl.pallas_call(
        functools.partial(_rwkv_proj_kernel, has_v=has_v),
        grid=(M // bm, D // HL),
        in_specs=[row(a) for a in acts] + ([tile] if has_v else []) + [col(w) for w in weights]
                 + [col(vecs), pl.BlockSpec((HL, HL), lambda i, j: (0, 0))],
        out_specs=[tile] * 7,
        out_shape=[jax.ShapeDtypeStruct((M, D), F32)] * 7,
        compiler_params=_params(("parallel", "arbitrary")),
        name="rwkv_proj",
    )(*acts, *([v_first] if has_v else []), *weights, vecs, ones_bd)


def _recur_kernel(r_ref, ld_ref, k_ref, v_ref, kk_ref, b_ref, g_ref, s0_ref, vec_ref,
                  z_ref, sout_ref, s_scr, *, C, NC, HP):
    ci = pl.program_id(2)
    C2 = 2 * C
    RB = NC * C

    @pl.when(ci == 0)
    def _():
        s_scr[...] = s0_ref[...]

    ti = lax.broadcasted_iota(jnp.int32, (RB, RB), 0)
    tj = lax.broadcasted_iota(jnp.int32, (RB, RB), 1)
    tri = jnp.where((_blk(ti, C) == _blk(tj, C)) & (ti >= tj), 1.0, 0.0).astype(BF16)
    ld = ld_ref[...]
    cl = sum(_dot(tri, part) for part in _split3(ld))
    p_in = jnp.exp(cl)
    p_inv = jnp.exp(-cl)
    r, k, v, b = r_ref[...], k_ref[...], v_ref[...], b_ref[...]
    a_t = -(jnp.exp(cl - ld) * kk_ref[...])
    r_t = p_in * r
    b_t = p_inv * b
    k_t = p_inv * k
    rows = [slice(c * C, (c + 1) * C) for c in range(NC)]
    cl_end = [cl[(c + 1) * C - 1:(c + 1) * C, :] for c in range(NC)]
    p_rem = [jnp.exp(cl_end[c] - cl[rows[c], :]) for c in range(NC)]
    p_end = [jnp.exp(cl_end[c]) for c in range(NC)]
    b_h = [p_rem[c] * b[rows[c], :] for c in range(NC)]
    k_h = [p_rem[c] * k[rows[c], :] for c in range(NC)]
    vec = vec_ref[...]
    r_k, lnx_g, lnx_b = (vec[i:i + 1, :] for i in range(3))
    rk = r * k * r_k
    g = g_ref[...]

    lane = lax.broadcasted_iota(jnp.int32, (1, LANES), 1)
    first = lane < RW_HEAD

    def stack(x):
        return jnp.concatenate([jnp.where(first, x, 0.0), jnp.where(first, 0.0, x)], axis=0).astype(BF16)

    def fold(x):
        return x[0:C, :] + x[C:C2, :]

    ri = lax.broadcasted_iota(jnp.int32, (C2, C2), 0)
    cj = lax.broadcasted_iota(jnp.int32, (C2, C2), 1)
    same = _blk(ri, C) == _blk(cj, C)
    strict = same & (_off(ri, C) > _off(cj, C))
    incl = same & (_off(ri, C) >= _off(cj, C))
    eye = jnp.where(ri == cj, 1.0, 0.0)
    same_head = (_blk(lax.broadcasted_iota(jnp.int32, (LANES, LANES), 0), RW_HEAD)
                 == _blk(lax.broadcasted_iota(jnp.int32, (LANES, LANES), 1), RW_HEAD))
    ones_bd = jnp.where(same_head, 1.0, 0.0).astype(BF16)

    lanes = [slice(p * LANES, (p + 1) * LANES) for p in range(HP)]
    probs = [(c, p) for c in range(NC) for p in range(HP)]
    a_b = {cp: a_t[rows[cp[0]], lanes[cp[1]]].astype(BF16) for cp in probs}
    r_b = {cp: r_t[rows[cp[0]], lanes[cp[1]]].astype(BF16) for cp in probs}
    v_st = {cp: stack(v[rows[cp[0]], lanes[cp[1]]]) for cp in probs}
    gb, gk = {}, {}
    for c, p in probs:
        q_st = jnp.concatenate([stack(a_t[rows[c], lanes[p]]), stack(r_t[rows[c], lanes[p]])], axis=0)
        b_c = b_t[rows[c], lanes[p]].astype(BF16)
        k_c = k_t[rows[c], lanes[p]].astype(BF16)
        if C2 % LANES == 0:
            both = _dot_nt(q_st, jnp.concatenate([b_c, b_c, k_c, k_c], axis=0))
            gb[c, p], gk[c, p] = both[:, 0:C2], both[:, C2:]
        else:
            gb[c, p] = _dot_nt(q_st, jnp.concatenate([b_c, b_c], axis=0))
            gk[c, p] = _dot_nt(q_st, jnp.concatenate([k_c, k_c], axis=0))
    low = {cp: jnp.where(strict, gb[cp][0:C2, :], 0.0) for cp in probs}
    rb_f = {cp: fold(jnp.where(incl, gb[cp][C2:, :], 0.0)).astype(BF16) for cp in probs}
    akv = {cp: _dot(fold(jnp.where(strict, gk[cp][0:C2, :], 0.0)).astype(BF16), v_st[cp]) for cp in probs}
    rkv = {cp: _dot(fold(jnp.where(incl, gk[cp][C2:, :], 0.0)).astype(BF16), v_st[cp]) for cp in probs}

    first_level = _lower_left(ri, cj, 1)
    t_inv = {cp: eye + jnp.where(first_level, low[cp], 0.0) for cp in probs}
    s = 2
    while s < C:
        sel = _lower_left(ri, cj, s)
        tb = {cp: t_inv[cp].astype(BF16) for cp in probs}
        mid = {cp: _dot(tb[cp], jnp.where(sel, low[cp], 0.0).astype(BF16)).astype(BF16) for cp in probs}
        t_inv = {cp: t_inv[cp] + _dot(mid[cp], tb[cp]) for cp in probs}
        s *= 2
    t_f = {cp: fold(t_inv[cp]).astype(BF16) for cp in probs}

    state = [s_scr[p] for p in range(HP)]
    y = {}
    for c in range(NC):
        state_b = [st.astype(BF16) for st in state]
        x = [_dot_nt(a_b[c, p], state_b[p]) + akv[c, p] for p in range(HP)]
        y0 = [_dot_nt(r_b[c, p], state_b[p]) + rkv[c, p] for p in range(HP)]
        u = [_dot(t_f[c, p], stack(x[p])) for p in range(HP)]
        for p in range(HP):
            y[c, p] = y0[p] + _dot(rb_f[c, p], stack(u[p]))
        upd = []
        for p in range(HP):
            uv = jnp.concatenate([u[p], v[rows[c], lanes[p]]], axis=0).astype(BF16)
            bk = jnp.concatenate([b_h[c][:, lanes[p]], k_h[c][:, lanes[p]]], axis=0).astype(BF16)
            upd.append(_dot_tn(uv, bk))
        state = [state[p] * p_end[c][:, lanes[p]] + jnp.where(same_head, upd[p], 0.0) for p in range(HP)]
    for p in range(HP):
        s_scr[p] = state[p]

    def seg(t, slices):
        parts = _split2(t) if slices == 2 else (t.astype(BF16),)
        return sum(_dot(part, ones_bd) for part in parts)

    y_all = [jnp.concatenate([y[c, p] for c in range(NC)], axis=0) for p in range(HP)]
    mean = [seg(y_all[p], 2) * (1.0 / RW_HEAD) for p in range(HP)]
    bonus = [seg(rk[:, lanes[p]], 1) * v[:, lanes[p]] for p in range(HP)]
    d = [y_all[p] - mean[p] for p in range(HP)]
    var = [seg(d[p] * d[p], 1) * (1.0 / RW_HEAD) for p in range(HP)]
    for p in range(HP):
        yn = d[p] * lax.rsqrt(var[p] + GN_EPS) * lnx_g[:, lanes[p]] + lnx_b[:, lanes[p]]
        z_ref[:, lanes[p]] = ((yn + bonus[p]) * g[:, lanes[p]]).astype(z_ref.dtype)

    @pl.when(ci == pl.num_programs(2) - 1)
    def _():
        sout_ref[...] = s_scr[...]


def rwkv_recur(ops, s0_bd, vecs, B, T):
    D = ops[0].shape[1]
    HL = min(1024, D)
    HP = HL // LANES
    C = min(CHUNK, T)
    NC = 2 if T % (2 * C) == 0 else 1
    ops3 = [o.reshape(B, T, D) for o in ops]
    tile = pl.BlockSpec((None, NC * C, HL), lambda b, j, c: (b, c, j))
    st = pl.BlockSpec((None, HP, LANES, LANES), lambda b, j, c: (b, j, 0, 0))
    z, s_fin = pl.pallas_call(
        functools.partial(_recur_kernel, C=C, NC=NC, HP=HP),
        grid=(B, D // HL, T // (NC * C)),
        in_specs=[tile] * 7 + [st, pl.BlockSpec((vecs.shape[0], HL), lambda b, j, c: (0, j))],
        out_specs=[tile, st],
        out_shape=[jax.ShapeDtypeStruct((B, T, D), BF16), jax.ShapeDtypeStruct(s0_bd.shape, F32)],
        scratch_shapes=[pltpu.VMEM((HP, LANES, LANES), F32)],
        compiler_params=_params(("parallel", "parallel", "arbitrary")),
        name="rwkv_recur",
    )(*ops3, s0_bd, vecs)
    return z.reshape(B * T, D), s_fin


def _gate_operand(gate, B, T, bm, bn):
    N = gate.shape[1]
    col = (lambda j: j) if bn < N else (lambda j: 0)
    if T % bm == 0:
        return gate.reshape(B, 1, N), pl.BlockSpec((None, 1, bn), lambda i, j: ((i * bm) // T, 0, col(j)))
    rows = jnp.broadcast_to(gate[:, None, :], (B, T, N)).reshape(B * T, N)
    return rows, pl.BlockSpec((bm, bn), lambda i, j: (i, col(j)))


def _matmul_res_kernel(a_ref, w_ref, x_ref, gate_ref, o_ref):
    o_ref[...] = x_ref[...] + gate_ref[...] * _dot(a_ref[...], w_ref[...])


def matmul_res(a, w, x, gate, B, T):
    M, K = a.shape
    N = w.shape[1]
    bm, bn = min(1024, M), min(512, N)
    gate_arr, gate_spec = _gate_operand(gate, B, T, bm, bn)
    return pl.pallas_call(
        _matmul_res_kernel,
        grid=(M // bm, N // bn),
        in_specs=[pl.BlockSpec((bm, K), lambda i, j: (i, 0)),
                  pl.BlockSpec((K, bn), lambda i, j: (0, j)),
                  pl.BlockSpec((bm, bn), lambda i, j: (i, j)), gate_spec],
        out_specs=pl.BlockSpec((bm, bn), lambda i, j: (i, j)),
        out_shape=jax.ShapeDtypeStruct((M, N), F32),
        compiler_params=_params(("parallel", "arbitrary")),
        name="matmul_res",
    )(a, w, x, gate_arr)


def _mlp_kernel(x_ref, g_ref, sh_ref, sc_ref, gate_ref, w1_ref, w2_ref, o_ref, h_scr, acc):
    f = pl.program_id(1)

    @pl.when(f == 0)
    def _():
        h_scr[...] = (_rms(x_ref[...], g_ref[...]) * (1.0 + sc_ref[...]) + sh_ref[...]).astype(BF16)
        acc[...] = jnp.zeros_like(acc)

    t = jnp.square(jnp.maximum(_dot(h_scr[...], w1_ref[...]), 0.0)).astype(BF16)
    acc[...] += _dot(t, w2_ref[...])

    @pl.when(f == pl.num_programs(1) - 1)
    def _():
        o_ref[...] = x_ref[...] + gate_ref[...] * acc[...]


def mlp_res(x, g, shift, scale, gate, w1, w2, B, T):
    M, D = x.shape
    F = w1.shape[1]
    bm = 512 if T % 512 == 0 else min(256, M)
    bf = min(1024, F)
    (sh_arr, vec_spec), (sc_arr, _), (gate_arr, _) = (_gate_operand(v, B, T, bm, D) for v in (shift, scale, gate))
    return pl.pallas_call(
        _mlp_kernel,
        grid=(M // bm, F // bf),
        in_specs=[pl.BlockSpec((bm, D), lambda i, f: (i, 0)),
                  pl.BlockSpec((1, D), lambda i, f: (0, 0)), vec_spec, vec_spec, vec_spec,
                  pl.BlockSpec((D, bf), lambda i, f: (0, f)),
                  pl.BlockSpec((bf, D), lambda i, f: (f, 0))],
        out_specs=pl.BlockSpec((bm, D), lambda i, f: (i, 0)),
        out_shape=jax.ShapeDtypeStruct((M, D), F32),
        scratch_shapes=[pltpu.VMEM((bm, D), BF16), pltpu.VMEM((bm, D), F32)],
        compiler_params=_params(("parallel", "arbitrary")),
        name="mlp_res",
    )(x, g.reshape(1, D), sh_arr, sc_arr, gate_arr, w1, w2)


def _rope_pair(t, tab):
    prod = t * tab
    return (prod + pltpu.roll(prod, ROPE_DIM, 1))[:, :ROPE_DIM]


def _latent_kernel(x_ref, gx_ref, sh_ref, sc_ref, w_ref, g_ref, tab_ref, ckv_ref, kpe_ref):
    h = (_rms(x_ref[...], gx_ref[...]) * (1.0 + sc_ref[...]) + sh_ref[...]).astype(BF16)
    acc = _dot(h, w_ref[...])
    R = g_ref.shape[1]
    ckv_ref[...] = _rms(acc[:, :R], g_ref[...])
    kpe_ref[...] = _rope_pair(acc[:, R:R + 2 * ROPE_DIM], tab_ref[...])


def latent(x, gx, shift, scale, w, g, tab):
    B, T, D = x.shape
    R = g.shape[0]
    bt = min(512, T)
    vec = pl.BlockSpec((None, 1, D), lambda b, t: (b, 0, 0))
    return pl.pallas_call(
        _latent_kernel,
        grid=(B, T // bt),
        in_specs=[pl.BlockSpec((None, bt, D), lambda b, t: (b, t, 0)),
                  pl.BlockSpec((1, D), lambda b, t: (0, 0)), vec, vec,
                  pl.BlockSpec(w.shape, lambda b, t: (0, 0)),
                  pl.BlockSpec((1, R), lambda b, t: (0, 0)),
                  pl.BlockSpec((bt, 2 * ROPE_DIM), lambda b, t: (t, 0))],
        out_specs=[pl.BlockSpec((None, bt, R), lambda b, t: (b, t, 0)),
                   pl.BlockSpec((None, bt, ROPE_DIM), lambda b, t: (b, t, 0))],
        out_shape=[jax.ShapeDtypeStruct((B, T, R), F32), jax.ShapeDtypeStruct((B, T, ROPE_DIM), F32)],
        compiler_params=_params(("parallel", "parallel")),
        name="latent",
    )(x, gx.reshape(1, D), shift.reshape(B, 1, D), scale.reshape(B, 1, D), w, g.reshape(1, R), tab)


def _kv_expand_kernel(c_ref, pe_ref, wuk_ref, wuv_ref, k_ref, vt_ref):
    c = c_ref[...].astype(BF16)
    kn = _dot(c, wuk_ref[...])
    vv = _dot(c, wuv_ref[...])
    pe = pe_ref[...].astype(BF16)
    for h in range(k_ref.shape[0]):
        k_ref[h, :, 0:NOPE_DIM] = kn[:, h * NOPE_DIM:(h + 1) * NOPE_DIM].astype(BF16)
        k_ref[h, :, NOPE_DIM:NOPE_DIM + ROPE_DIM] = pe
        vt_ref[h] = vv[:, h * V_DIM:(h + 1) * V_DIM].T.astype(BF16)


def kv_expand(ckv, kpe, wuk, wuv, H):
    B, S, R = ckv.shape
    bt = min(512, S)
    hb = min(8, H)
    DK = NOPE_DIM + ROPE_DIM
    return pl.pallas_call(
        _kv_expand_kernel,
        grid=(B, S // bt, H // hb),
        in_specs=[pl.BlockSpec((None, bt, R), lambda b, t, j: (b, t, 0)),
                  pl.BlockSpec((None, bt, ROPE_DIM), lambda b, t, j: (b, t, 0)),
                  pl.BlockSpec((R, hb * NOPE_DIM), lambda b, t, j: (0, j)),
                  pl.BlockSpec((R, hb * V_DIM), lambda b, t, j: (0, j))],
        out_specs=[pl.BlockSpec((None, hb, bt, DK), lambda b, t, j: (b, j, t, 0)),
                   pl.BlockSpec((None, hb, V_DIM, bt), lambda b, t, j: (b, j, 0, t))],
        out_shape=[jax.ShapeDtypeStruct((B, H, S, DK), BF16), jax.ShapeDtypeStruct((B, H, V_DIM, S), BF16)],
        compiler_params=_params(("parallel", "parallel", "arbitrary")),
        name="kv_expand",
    )(ckv, kpe, wuk, wuv)


def _wdq_kernel(x_ref, gx_ref, sh_ref, sc_ref, w_ref, g_ref, o_ref):
    h = (_rms(x_ref[...], gx_ref[...]) * (1.0 + sc_ref[...]) + sh_ref[...]).astype(BF16)
    o_ref[...] = _rms(_dot(h, w_ref[...]), g_ref[...]).astype(BF16)


def wdq_norm(x, gx, shift, scale, w, g):
    B, T, D = x.shape
    R = w.shape[1]
    bt = min(1024, T)
    vec = pl.BlockSpec((None, 1, D), lambda b, t: (b, 0, 0))
    return pl.pallas_call(
        _wdq_kernel,
        grid=(B, T // bt),
        in_specs=[pl.BlockSpec((None, bt, D), lambda b, t: (b, t, 0)),
                  pl.BlockSpec((1, D), lambda b, t: (0, 0)), vec, vec,
                  pl.BlockSpec((D, R), lambda b, t: (0, 0)),
                  pl.BlockSpec((1, R), lambda b, t: (0, 0))],
        out_specs=pl.BlockSpec((None, bt, R), lambda b, t: (b, t, 0)),
        out_shape=jax.ShapeDtypeStruct((B, T, R), BF16),
        compiler_params=_params(("parallel", "parallel")),
        name="wdq_norm",
    )(x, gx.reshape(1, D), shift.reshape(B, 1, D), scale.reshape(B, 1, D), w, g.reshape(1, R))


def _wuq_kernel(c_ref, w_ref, tab_ref, q_ref):
    acc = _dot(c_ref[...], w_ref[...])
    tab = tab_ref[...]
    W = NOPE_DIM + 2 * ROPE_DIM
    for h in range(q_ref.shape[0]):
        q_ref[h, :, 0:NOPE_DIM] = (acc[:, h * W:h * W + NOPE_DIM] * MLA_SCALE).astype(BF16)
        pe = _rope_pair(acc[:, h * W + NOPE_DIM:(h + 1) * W], tab)
        q_ref[h, :, NOPE_DIM:NOPE_DIM + ROPE_DIM] = (pe * MLA_SCALE).astype(BF16)


def wuq_rope(cq, w, tab, H, head_major):
    B, T, R = cq.shape
    bt = min(512, T)
    nt = T // bt
    hb = min(8, H)
    W = NOPE_DIM + 2 * ROPE_DIM
    DK = NOPE_DIM + ROPE_DIM
    if head_major:
        out_spec = pl.BlockSpec((hb, bt, DK), lambda b, t, j: (j, b * nt + t, 0))
        out_shape = (H, B * T, DK)
    else:
        out_spec = pl.BlockSpec((None, hb, bt, DK), lambda b, t, j: (b, j, t, 0))
        out_shape = (B, H, T, DK)
    return pl.pallas_call(
        _wuq_kernel,
        grid=(B, nt, H // hb),
        in_specs=[pl.BlockSpec((None, bt, R), lambda b, t, j: (b, t, 0)),
                  pl.BlockSpec((R, hb * W), lambda b, t, j: (0, j)),
                  pl.BlockSpec((bt, 2 * ROPE_DIM), lambda b, t, j: (t, 0))],
        out_specs=out_spec,
        out_shape=jax.ShapeDtypeStruct(out_shape, BF16),
        compiler_params=_params(("parallel", "parallel", "arbitrary")),
        name="wuq_rope",
    )(cq, w, tab)


def _visible(qpos, kpos):
    return _blk(kpos, CHUNK) <= _blk(qpos, CHUNK)


def _flash_kernel(q_ref, k_ref, vt_ref, o_ref, *, tile):
    hb = q_ref.shape[0]
    qi = pl.program_id(2)
    qpos = qi * tile + lax.broadcasted_iota(jnp.int32, (1, tile), 1)

    def step(ki, carry, masked):
        start = pl.multiple_of(ki * tile, tile)
        scores = [_dot_nt(k_ref[h, pl.ds(start, tile), :], q_ref[h]) for h in range(hb)]
        out = []
        for h in range(hb):
            m, l, acc = carry[h]
            s = scores[h]
            if masked:
                kpos = ki * tile + lax.broadcasted_iota(jnp.int32, (tile, 1), 0)
                s = jnp.where(_visible(qpos, kpos), s, -jnp.inf)
            m_new = jnp.maximum(m, jnp.max(s, axis=0, keepdims=True))
            alpha = jnp.exp(m - m_new)
            p = jnp.exp(s - m_new)
            l = alpha * l + jnp.sum(p, axis=0, keepdims=True)
            acc = alpha * acc + _dot(vt_ref[h, :, pl.ds(start, tile)], p.astype(BF16))
            out.append((m_new, l, acc))
        return tuple(out)

    init = tuple((jnp.full((1, tile), -jnp.inf, F32), jnp.zeros((1, tile), F32), jnp.zeros((V_DIM, tile), F32))
                 for _ in range(hb))
    carry = lax.fori_loop(0, qi, lambda ki, c: step(ki, c, False), init)
    carry = step(qi, carry, True)
    for h in range(hb):
        _, l, acc = carry[h]
        o_ref[:, h * V_DIM:(h + 1) * V_DIM] = (acc / l).T.astype(o_ref.dtype)


def flash_prompt(q, k, vt):
    B, H, T, DK = q.shape
    tile = min(256, T)
    assert tile % CHUNK == 0 and T % tile == 0
    hb = 8 if H % 8 == 0 else 1
    return pl.pallas_call(
        functools.partial(_flash_kernel, tile=tile),
        grid=(B, H // hb, T // tile),
        in_specs=[pl.BlockSpec((None, hb, tile, DK), lambda b, j, i: (b, j, i, 0)),
                  pl.BlockSpec((None, hb, T, DK), lambda b, j, i: (b, j, 0, 0)),
                  pl.BlockSpec((None, hb, V_DIM, T), lambda b, j, i: (b, j, 0, 0))],
        out_specs=pl.BlockSpec((None, tile, hb * V_DIM), lambda b, j, i: (b, i, j)),
        out_shape=jax.ShapeDtypeStruct((B, T, H * V_DIM), BF16),
        compiler_params=_params(("parallel", "parallel", "arbitrary")),
        name="flash_prompt",
    )(q, k, vt)


def _q_absorb_kernel(q_ref, wuk_ref, o_ref):
    o_ref[...] = _dot_nt(q_ref[:, 0:NOPE_DIM], wuk_ref[...]).astype(BF16)


def _attn_latent_kernel(ql_ref, q_ref, qpos_ref, c_ref, pe_ref, o_ref):
    H, T, R = ql_ref.shape
    S = c_ref.shape[0]
    c = c_ref[...].astype(BF16)
    s = (_dot_nt(ql_ref[...].reshape(H * T, R), c)
         + _dot_nt(q_ref[:, :, NOPE_DIM:NOPE_DIM + ROPE_DIM].reshape(H * T, ROPE_DIM), pe_ref[...].astype(BF16)))
    kpos = lax.broadcasted_iota(jnp.int32, (1, S), 1)
    s = jnp.where(_visible(qpos_ref[...], kpos), s, -jnp.inf)
    p = jnp.exp(s - jnp.max(s, axis=-1, keepdims=True))
    l = jnp.sum(p, axis=-1, keepdims=True)
    o_ref[...] = (_dot(p.astype(BF16), c) / l).astype(BF16).reshape(H, T, R)


def _o_expand_kernel(o_ref, wuv_ref, out_ref):
    out_ref[...] = _dot(o_ref[...], wuv_ref[...]).astype(BF16)


def attn_latent(q, ckv, kpe, wuk, wuv, B, T, q0):
    H, M, DK = q.shape
    S, R = ckv.shape[1:]
    q_lat = pl.pallas_call(
        _q_absorb_kernel,
        grid=(H,),
        in_specs=[pl.BlockSpec((None, M, DK), lambda h: (h, 0, 0)),
                  pl.BlockSpec((R, NOPE_DIM), lambda h: (0, h))],
        out_specs=pl.BlockSpec((None, M, R), lambda h: (h, 0, 0)),
        out_shape=jax.ShapeDtypeStruct((H, M, R), BF16),
        compiler_params=_params(("parallel",)),
        name="q_absorb",
    )(q, wuk)
    qpos = jnp.tile(q0 + jnp.arange(T, dtype=jnp.int32), H)[:, None]
    o_lat = pl.pallas_call(
        _attn_latent_kernel,
        grid=(B,),
        in_specs=[pl.BlockSpec((H, T, R), lambda b: (0, b, 0)),
                  pl.BlockSpec((H, T, DK), lambda b: (0, b, 0)),
                  pl.BlockSpec((H * T, 1), lambda b: (0, 0)),
                  pl.BlockSpec((None, S, R), lambda b: (b, 0, 0)),
                  pl.BlockSpec((None, S, ROPE_DIM), lambda b: (b, 0, 0))],
        out_specs=pl.BlockSpec((H, T, R), lambda b: (0, b, 0)),
        out_shape=jax.ShapeDtypeStruct((H, M, R), BF16),
        compiler_params=_params(("parallel",)),
        name="attn_latent",
    )(q_lat, q, qpos, ckv, kpe)
    return pl.pallas_call(
        _o_expand_kernel,
        grid=(H,),
        in_specs=[pl.BlockSpec((None, M, R), lambda h: (h, 0, 0)),
                  pl.BlockSpec((R, V_DIM), lambda h: (0, h))],
        out_specs=pl.BlockSpec((M, V_DIM), lambda h: (0, h)),
        out_shape=jax.ShapeDtypeStruct((M, H * V_DIM), BF16),
        compiler_params=_params(("parallel",)),
        name="o_expand",
    )(o_lat, wuv)


def _pad_cols(w, n):
    return jnp.pad(w, ((0, 0), (0, n - w.shape[1])))


def _pad_rows(w, n):
    return jnp.pad(w, ((0, n - w.shape[0]), (0, 0)))


def _rotate_half_cols(w):
    half = ROPE_DIM // 2
    return jnp.concatenate([-w[..., half:], w[..., :half]], axis=-1)


def _rope_table(pos):
    half = ROPE_DIM // 2
    inv = ROPE_THETA ** (-jnp.arange(half, dtype=F32) / half)
    ang = pos.astype(F32)[:, None] * inv[None, :]
    cos, sin = jnp.cos(ang), jnp.sin(ang)
    return jnp.concatenate([cos, cos, sin, sin], axis=-1)


def _block_diag_states(s):
    B, H, N, _ = s.shape
    s = s.reshape(B, H // 2, 2, N, N)
    z = jnp.zeros_like(s[:, :, 0])
    top = jnp.concatenate([s[:, :, 0], z], axis=-1)
    bot = jnp.concatenate([z, s[:, :, 1]], axis=-1)
    return jnp.concatenate([top, bot], axis=-2)


def _diag_states(s):
    B, HP = s.shape[:2]
    N = RW_HEAD
    return jnp.stack([s[:, :, :N, :N], s[:, :, N:, N:]], axis=2).reshape(B, 2 * HP, N, N)


def _prepare(W):
    D = W['rw_wr'].shape[1]
    P = {}
    bf = lambda a: a.astype(BF16)
    P['mlp_w1'], P['mlp_w2'] = bf(W['mlp_w1']), bf(W['mlp_w2'])
    for n in ('rw_wr', 'rw_wk', 'rw_wv', 'rw_wo', 'rw_g1', 'rw_g2', 'mla_wdq', 'mla_wo'):
        P[n] = bf(W[n])
    NA = W['rw_wr'].shape[0]
    P['rw_w1'] = [bf(_pad_cols(W['rw_w1'][l], LORA_PAD)) for l in range(NA)]
    P['rw_w2'] = [bf(_pad_rows(W['rw_w2'][l], LORA_PAD)) for l in range(NA)]
    P['rw_a1'] = [bf(_pad_cols(W['rw_a1'][l], LORA_PAD)) for l in range(NA)]
    P['rw_a2'] = [bf(_pad_rows(W['rw_a2'][l], LORA_PAD)) for l in range(NA)]
    P['rw_v1'] = [bf(_pad_cols(W['rw_v1'][l], LORA_PAD)) for l in range(NA - 1)]
    P['rw_v2'] = [bf(_pad_rows(W['rw_v2'][l], LORA_PAD)) for l in range(NA - 1)]
    zeros = jnp.zeros((D,), F32)
    P['proj_vecs'], P['recur_vecs'] = [], []
    for l in range(NA):
        v0 = W['rw_v0'][l - 1] if l > 0 else zeros
        P['proj_vecs'].append(jnp.stack([W['rw_w0'][l], W['rw_a0'][l], v0, W['rw_kk'][l], W['rw_ka'][l],
                                         zeros, zeros, zeros]))
        P['recur_vecs'].append(jnp.stack([W['rw_rk'][l], W['rw_lnx_g'][l], W['rw_lnx_b'][l]] + [zeros] * 5))
    HL = min(512, D)
    head = jnp.arange(HL) // RW_HEAD
    P['ones_bd'] = (head[:, None] == head[None, :]).astype(BF16)
    R = W['kv_lat_g'].shape[0]
    wd = W['kv_wd']
    P['kv_wd'] = bf(jnp.concatenate([wd, _rotate_half_cols(wd[:, R:])], axis=1))
    H = W['kv_wuk'].shape[1]
    P['kv_wuk'] = bf(W['kv_wuk'].reshape(R, H * NOPE_DIM))
    P['kv_wuv'] = bf(W['kv_wuv'].reshape(R, H * V_DIM))
    NB, Q = W['mla_wuq'].shape[:2]
    wuq = W['mla_wuq'].reshape(NB, Q, H, NOPE_DIM + ROPE_DIM)
    pe = wuq[..., NOPE_DIM:]
    P['mla_wuq'] = bf(jnp.concatenate([wuq, _rotate_half_cols(pe)], axis=-1).reshape(NB, Q, -1))
    return P


def _trunk(x, mod, kv_mod, pos0, h_prev, s0, past_ckv, past_kpe, W, P):
    B, T, D = x.shape
    M = B * T
    depth = W['ada_w'].shape[0]
    NA = W['rw_wr'].shape[0]
    H = W['kv_wuk'].shape[1]
    tab = _rope_table(pos0 + jnp.arange(T))
    xf = x.reshape(M, D)
    shifts, states = [], []
    v_first = None
    keys = vals = ckv = kpe = None
    for l in range(depth):
        m = mod[l]
        if l < NA:
            has_v = l > 0
            pre = rwkv_pre(xf.reshape(B, T, D), W['norm_mix_g'][l], m[:, 0], m[:, 1], h_prev[l], W['rw_mu'][l],
                           P['rw_w1'][l], P['rw_a1'][l], P['rw_g1'][l], P['rw_v1'][l - 1] if has_v else None)
            shifts.append(pre[-1].reshape(B, D))
            acts = [a.reshape(M, a.shape[-1]) for a in pre[:-1]]
            ops = rwkv_proj(acts, v_first, P['rw_wr'][l], P['rw_wk'][l], P['rw_wv'][l], P['rw_w2'][l],
                            P['rw_a2'][l], P['rw_g2'][l], P['rw_v2'][l - 1] if has_v else None,
                            P['proj_vecs'][l], P['ones_bd'])
            if l == 0:
                v_first = ops[3]
            z, s_fin = rwkv_recur(ops, _block_diag_states(s0[l]), P['recur_vecs'][l], B, T)
            states.append(_diag_states(s_fin))
            xf = matmul_res(z, P['rw_wo'][l], xf, m[:, 2], B, T)
        else:
            j = l - NA
            cq = wdq_norm(xf.reshape(B, T, D), W['norm_mix_g'][l], m[:, 0], m[:, 1], P['mla_wdq'][j], W['mla_q_g'][j])
            if past_ckv is None:
                q = wuq_rope(cq, P['mla_wuq'][j], tab, H, False)
                o = flash_prompt(q, keys, vals).reshape(M, H * V_DIM)
            else:
                q = wuq_rope(cq, P['mla_wuq'][j], tab, H, True)
                o = attn_latent(q, keys, vals, P['kv_wuk'], P['kv_wuv'], B, T, pos0)
            xf = matmul_res(o, P['mla_wo'][j], xf, m[:, 2], B, T)
        xf = mlp_res(xf, W['norm_mlp_g'][l], m[:, 3], m[:, 4], m[:, 5], P['mlp_w1'][l], P['mlp_w2'][l], B, T)
        if l == NA - 1:
            ckv, kpe = latent(xf.reshape(B, T, D), W['kv_norm_g'], kv_mod[:, 0], kv_mod[:, 1],
                              P['kv_wd'], W['kv_lat_g'], tab)
            if past_ckv is None:
                keys, vals = kv_expand(ckv, kpe, P['kv_wuk'], P['kv_wuv'], H)
            else:
                keys = jnp.concatenate([past_ckv, ckv], axis=1)
                vals = jnp.concatenate([past_kpe, kpe], axis=1)
    y = final_norm(xf.reshape(B, T, D), W['final_g'])
    return y, ckv, kpe, jnp.stack(states), jnp.stack(shifts)


def kernel(x_prompt, x_sample, cache_ckv, cache_kpe, state_wkv, state_shift, c_prompt, c_sample, ada_w, ada_b, norm_mix_g, norm_mlp_g, mlp_w1, mlp_w2, rw_mu, rw_w0, rw_w1, rw_w2, rw_a0, rw_a1, rw_a2, rw_v0, rw_v1, rw_v2, rw_g1, rw_g2, rw_wr, rw_wk, rw_wv, rw_wo, rw_kk, rw_ka, rw_rk, rw_lnx_g, rw_lnx_b, kv_ada_w, kv_ada_b, kv_norm_g, kv_wd, kv_lat_g, kv_wuk, kv_wuv, mla_wdq, mla_q_g, mla_wuq, mla_wo, final_g):
    W = dict(ada_w=ada_w, ada_b=ada_b, norm_mix_g=norm_mix_g, norm_mlp_g=norm_mlp_g,
             mlp_w1=mlp_w1, mlp_w2=mlp_w2, rw_mu=rw_mu, rw_w0=rw_w0, rw_w1=rw_w1, rw_w2=rw_w2,
             rw_a0=rw_a0, rw_a1=rw_a1, rw_a2=rw_a2, rw_v0=rw_v0, rw_v1=rw_v1, rw_v2=rw_v2,
             rw_g1=rw_g1, rw_g2=rw_g2, rw_wr=rw_wr, rw_wk=rw_wk, rw_wv=rw_wv, rw_wo=rw_wo,
             rw_kk=rw_kk, rw_ka=rw_ka, rw_rk=rw_rk, rw_lnx_g=rw_lnx_g, rw_lnx_b=rw_lnx_b,
             kv_ada_w=kv_ada_w, kv_ada_b=kv_ada_b, kv_norm_g=kv_norm_g, kv_wd=kv_wd,
             kv_lat_g=kv_lat_g, kv_wuk=kv_wuk, kv_wuv=kv_wuv, mla_wdq=mla_wdq, mla_q_g=mla_q_g,
             mla_wuq=mla_wuq, mla_wo=mla_wo, final_g=final_g)
    P = _prepare(W)
    Bp, Tp, D = x_prompt.shape
    Bs = x_sample.shape[0]
    depth = ada_w.shape[0]
    NA = rw_wr.shape[0]
    c_all = jnp.concatenate([c_prompt, c_sample], axis=0)
    mod = ada_linear(c_all, ada_w, ada_b).reshape(depth, Bp + Bs, N_MOD, D)
    kv_mod = ada_linear(c_all, kv_ada_w[None], kv_ada_b[None]).reshape(Bp + Bs, 2, D)
    h0 = jnp.zeros((NA, Bp, D), F32)
    s0 = jnp.zeros((NA, Bp, D // RW_HEAD, RW_HEAD, RW_HEAD), F32)
    out_p = _trunk(x_prompt, mod[:, :Bp], kv_mod[:Bp], 0, h0, s0, None, None, W, P)
    out_s = _trunk(x_sample, mod[:, Bp:], kv_mod[Bp:], cache_ckv.shape[1], state_shift, state_wkv,
                   cache_ckv, cache_kpe, W, P)
    return (out_p[0], out_s[0]) + out_p[1:] + out_s[1:]
```

```python
import functools

import jax
import jax.numpy as jnp
from jax import lax
from jax.experimental import pallas as pl
from jax.experimental.pallas import tpu as pltpu

F32, BF16 = jnp.float32, jnp.bfloat16

RW_HEAD = 64
CHUNK = 64
GN_EPS = 64e-5
NOPE_DIM = 128
ROPE_DIM = 64
V_DIM = 128
ROPE_THETA = 10000.0
MLA_SCALE = (NOPE_DIM + ROPE_DIM) ** -0.5
LOG2E = 1.4426950408889634
V_ROWS = V_DIM + 16
NORM_EPS = 1e-6
N_MOD = 6

LANES = 128
VMEM_LIMIT = 56 * 1024 * 1024
LORA_PAD = 128


def _params(sem, vmem=VMEM_LIMIT):
    return pltpu.CompilerParams(dimension_semantics=sem, vmem_limit_bytes=vmem)


def _dot(a, b):
    return jnp.dot(a, b, preferred_element_type=F32)


def _dot_nt(a, b):
    return lax.dot_general(a, b, (((1,), (1,)), ((), ())), preferred_element_type=F32)


def _dot_tn(a, b):
    return lax.dot_general(a, b, (((0,), (0,)), ((), ())), preferred_element_type=F32)


def _split2(x):
    hi = x.astype(BF16)
    return hi, (x - hi.astype(F32)).astype(BF16)


def _split3(x):
    hi = x.astype(BF16)
    r1 = x - hi.astype(F32)
    mid = r1.astype(BF16)
    return hi, mid, (r1 - mid.astype(F32)).astype(BF16)


def _sigmoid(x):
    return 1.0 / (1.0 + jnp.exp(-x))


def _blk(i, n):
    assert n & (n - 1) == 0
    return i >> (n.bit_length() - 1)


def _off(i, n):
    assert n & (n - 1) == 0
    return i & (n - 1)


def _lower_left(ri, cj, s):
    return (_blk(ri, 2 * s) == _blk(cj, 2 * s)) & (_off(ri, 2 * s) >= s) & (_off(cj, 2 * s) < s)


def _rms(x, g):
    return x * lax.rsqrt(jnp.mean(x * x, axis=-1, keepdims=True) + NORM_EPS) * g


def _ada_kernel(c_ref, w_ref, b_ref, o_ref):
    c = c_ref[...]
    cs = c * _sigmoid(c)
    o_ref[...] = _dot(cs.astype(BF16), w_ref[...].astype(BF16)) + b_ref[...]


def ada_linear(c, w, b):
    L, K, N = w.shape
    M = c.shape[0]
    bn = min(512, N)
    return pl.pallas_call(
        _ada_kernel,
        grid=(L, N // bn),
        in_specs=[pl.BlockSpec((M, K), lambda l, j: (0, 0)),
                  pl.BlockSpec((None, K, bn), lambda l, j: (l, 0, j)),
                  pl.BlockSpec((None, 1, bn), lambda l, j: (l, 0, j))],
        out_specs=pl.BlockSpec((None, M, bn), lambda l, j: (l, 0, j)),
        out_shape=jax.ShapeDtypeStruct((L, M, N), F32),
        compiler_params=_params(("parallel", "parallel")),
        name="ada_linear",
    )(c, w, b.reshape(L, 1, N))


def _norm_kernel(x_ref, g_ref, o_ref):
    o_ref[...] = _rms(x_ref[...], g_ref[...])


def final_norm(x, g):
    B, T, D = x.shape
    bt = min(512, T)
    return pl.pallas_call(
        _norm_kernel,
        grid=(B, T // bt),
        in_specs=[pl.BlockSpec((None, bt, D), lambda b, t: (b, t, 0)),
                  pl.BlockSpec((1, D), lambda b, t: (0, 0))],
        out_specs=pl.BlockSpec((None, bt, D), lambda b, t: (b, t, 0)),
        out_shape=jax.ShapeDtypeStruct((B, T, D), F32),
        compiler_params=_params(("parallel", "parallel")),
        name="final_norm",
    )(x, g.reshape(1, D))


def _rwkv_pre_kernel(*refs, has_v):
    if has_v:
        (x_ref, g_ref, sh_ref, sc_ref, hp_ref, mu_ref, w1_ref, a1_ref, g1_ref, v1_ref,
         xr_ref, xk_ref, xv_ref, tw_ref, av_ref, gg_ref, vv_ref, hl_ref, prev) = refs
    else:
        (x_ref, g_ref, sh_ref, sc_ref, hp_ref, mu_ref, w1_ref, a1_ref, g1_ref,
         xr_ref, xk_ref, xv_ref, tw_ref, av_ref, gg_ref, hl_ref, prev) = refs
    bt = x_ref.shape[0]
    h = _rms(x_ref[...], g_ref[...]) * (1.0 + sc_ref[...]) + sh_ref[...]

    @pl.when(pl.program_id(1) == 0)
    def _():
        prev[...] = hp_ref[...]

    row = lax.broadcasted_iota(jnp.int32, h.shape, 0)
    xx = jnp.where(row == 0, prev[...], pltpu.roll(h, 1, 0)) - h
    last = h[bt - 1:bt, :]
    prev[...] = last
    hl_ref[...] = last
    mu = mu_ref[...]

    def mix(i):
        return (h + xx * mu[i:i + 1, :]).astype(BF16)

    xr_ref[...] = mix(0)
    tw_ref[...] = jnp.tanh(_dot(mix(1), w1_ref[...])).astype(BF16)
    xk_ref[...] = mix(2)
    xv = mix(3)
    xv_ref[...] = xv
    if has_v:
        vv_ref[...] = _dot(xv, v1_ref[...]).astype(BF16)
    av_ref[...] = _dot(mix(4), a1_ref[...]).astype(BF16)
    gg_ref[...] = _sigmoid(_dot(mix(5), g1_ref[...])).astype(BF16)


def rwkv_pre(x, g, shift, scale, h_prev, mu, w1, a1, g1, v1):
    B, T, D = x.shape
    bt = min(256, T)
    has_v = v1 is not None
    row = lambda n: pl.BlockSpec((None, bt, n), lambda b, t: (b, t, 0))
    vec = pl.BlockSpec((None, 1, D), lambda b, t: (b, 0, 0))
    full = lambda a: pl.BlockSpec(a.shape, lambda b, t: (0, 0))
    lora = [w1, a1, g1] + ([v1] if has_v else [])
    outs = [(D, BF16)] * 3 + [(w1.shape[1], BF16), (a1.shape[1], BF16), (g1.shape[1], BF16)]
    if has_v:
        outs.append((v1.shape[1], BF16))
    res = pl.pallas_call(
        functools.partial(_rwkv_pre_kernel, has_v=has_v),
        grid=(B, T // bt),
        in_specs=[row(D), pl.BlockSpec((1, D), lambda b, t: (0, 0)), vec, vec, vec, full(mu)]
                 + [full(a) for a in lora],
        out_specs=[row(n) for n, _ in outs] + [vec],
        out_shape=[jax.ShapeDtypeStruct((B, T, n), dt) for n, dt in outs]
                  + [jax.ShapeDtypeStruct((B, 1, D), F32)],
        scratch_shapes=[pltpu.VMEM((1, D), F32)],
        compiler_params=_params(("parallel", "arbitrary")),
        name="rwkv_pre",
    )(x, g.reshape(1, D), shift.reshape(B, 1, D), scale.reshape(B, 1, D), h_prev.reshape(B, 1, D), mu, *lora)
    return res


def _rwkv_proj_kernel(*refs, has_v):
    if has_v:
        (xr_ref, xk_ref, xv_ref, tw_ref, av_ref, gg_ref, vv_ref, vf_ref,
         wr_ref, wk_ref, wv_ref, w2_ref, a2_ref, g2_ref, v2_ref, vec_ref, ones_ref,
         r_ref, ld_ref, k_ref, v_ref, kk_ref, b_ref, g_ref) = refs
    else:
        (xr_ref, xk_ref, xv_ref, tw_ref, av_ref, gg_ref,
         wr_ref, wk_ref, wv_ref, w2_ref, a2_ref, g2_ref, vec_ref, ones_ref,
         r_ref, ld_ref, k_ref, v_ref, kk_ref, b_ref, g_ref) = refs
    vec = vec_ref[...]
    w0, a0, v0, k_k, k_a = (vec[i:i + 1, :] for i in range(5))
    r_ref[...] = _dot(xr_ref[...], wr_ref[...])
    kraw = _dot(xk_ref[...], wk_ref[...])
    v = _dot(xv_ref[...], wv_ref[...])
    nz = -(w0 + _dot(tw_ref[...], w2_ref[...]))
    softplus = jnp.maximum(nz, 0.0) + jnp.log(1.0 + jnp.exp(-jnp.abs(nz)))
    ld_ref[...] = -jnp.exp(-softplus - 0.5)
    a = _sigmoid(a0 + _dot(av_ref[...], a2_ref[...]))
    if has_v:
        v = v + (vf_ref[...] - v) * _sigmoid(v0 + _dot(vv_ref[...], v2_ref[...]))
    v_ref[...] = v
    g_ref[...] = _dot(gg_ref[...], g2_ref[...])
    kk = kraw * k_k
    ss = _dot((kk * kk).astype(BF16), ones_ref[...])
    kk = kk / jnp.maximum(jnp.sqrt(ss), 1e-12)
    kk_ref[...] = kk
    b_ref[...] = kk * a
    k_ref[...] = kraw * (1.0 + (a - 1.0) * k_a)


def rwkv_proj(pre, v_first, wr, wk, wv, w2, a2, g2, v2, vecs, ones_bd):
    has_v = v2 is not None
    xr = pre[0]
    M, D = xr.shape
    HL = ones_bd.shape[0]
    bm = min(512, M)
    row = lambda a: pl.BlockSpec((bm, a.shape[1]), lambda i, j: (i, 0))
    tile = pl.BlockSpec((bm, HL), lambda i, j: (i, j))
    col = lambda a: pl.BlockSpec((a.shape[0], HL), lambda i, j: (0, j))
    acts = list(pre[:6]) + ([pre[6]] if has_v else [])
    weights = [wr, wk, wv, w2, a2, g2] + ([v2] if has_v else [])
    return pl.pallas_call(
        functools.partial(_rwkv_proj_kernel, has_v=has_v),
        grid=(M // bm, D // HL),
        in_specs=[row(a) for a in acts] + ([tile] if has_v else []) + [col(w) for w in weights]
                 + [col(vecs), pl.BlockSpec((HL, HL), lambda i, j: (0, 0))],
        out_specs=[tile] * 7,
        out_shape=[jax.ShapeDtypeStruct((M, D), F32)] * 7,
        compiler_params=_params(("parallel", "arbitrary")),
        name="rwkv_proj",
    )(*acts, *([v_first] if has_v else []), *weights, vecs, ones_bd)


def _recur_kernel(r_ref, ld_ref, k_ref, v_ref, kk_ref, b_ref, g_ref, s0_ref, vec_ref,
                  z_ref, sout_ref, s_scr, *, C, NC, HP):
    ci = pl.program_id(2)
    C2 = 2 * C
    RB = NC * C

    @pl.when(ci == 0)
    def _():
        s_scr[...] = s0_ref[...]

    ti = lax.broadcasted_iota(jnp.int32, (RB, RB), 0)
    tj = lax.broadcasted_iota(jnp.int32, (RB, RB), 1)
    tri = jnp.where((_blk(ti, C) == _blk(tj, C)) & (ti >= tj), 1.0, 0.0).astype(BF16)
    ld = ld_ref[...]
    cl = sum(_dot(tri, part) for part in _split3(ld))
    p_in = jnp.exp(cl)
    p_inv = jnp.exp(-cl)
    r, k, v, b = r_ref[...], k_ref[...], v_ref[...], b_ref[...]
    a_t = -(jnp.exp(cl - ld) * kk_ref[...])
    r_t = p_in * r
    b_t = p_inv * b
    k_t = p_inv * k
    rows = [slice(c * C, (c + 1) * C) for c in range(NC)]
    cl_end = [cl[(c + 1) * C - 1:(c + 1) * C, :] for c in range(NC)]
    p_rem = [jnp.exp(cl_end[c] - cl[rows[c], :]) for c in range(NC)]
    p_end = [jnp.exp(cl_end[c]) for c in range(NC)]
    b_h = [p_rem[c] * b[rows[c], :] for c in range(NC)]
    k_h = [p_rem[c] * k[rows[c], :] for c in range(NC)]
    vec = vec_ref[...]
    r_k, lnx_g, lnx_b = (vec[i:i + 1, :] for i in range(3))
    rk = r * k * r_k
    g = g_ref[...]

    lane = lax.broadcasted_iota(jnp.int32, (1, LANES), 1)
    first = lane < RW_HEAD

    def stack(x):
        return jnp.concatenate([jnp.where(first, x, 0.0), jnp.where(first, 0.0, x)], axis=0).astype(BF16)

    def fold(x):
        return x[0:C, :] + x[C:C2, :]

    ri = lax.broadcasted_iota(jnp.int32, (C2, C2), 0)
    cj = lax.broadcasted_iota(jnp.int32, (C2, C2), 1)
    same = _blk(ri, C) == _blk(cj, C)
    strict = same & (_off(ri, C) > _off(cj, C))
    incl = same & (_off(ri, C) >= _off(cj, C))
    eye = jnp.where(ri == cj, 1.0, 0.0)
    same_head = (_blk(lax.broadcasted_iota(jnp.int32, (LANES, LANES), 0), RW_HEAD)
                 == _blk(lax.broadcasted_iota(jnp.int32, (LANES, LANES), 1), RW_HEAD))
    ones_bd = jnp.where(same_head, 1.0, 0.0).astype(BF16)

    lanes = [slice(p * LANES, (p + 1) * LANES) for p in range(HP)]
    probs = [(c, p) for c in range(NC) for p in range(HP)]
    a_b = {cp: a_t[rows[cp[0]], lanes[cp[1]]].astype(BF16) for cp in probs}
    r_b = {cp: r_t[rows[cp[0]], lanes[cp[1]]].astype(BF16) for cp in probs}
    v_st = {cp: stack(v[rows[cp[0]], lanes[cp[1]]]) for cp in probs}
    gb, gk = {}, {}
    for c, p in probs:
        q_st = jnp.concatenate([stack(a_t[rows[c], lanes[p]]), stack(r_t[rows[c], lanes[p]])], axis=0)
        b_c = b_t[rows[c], lanes[p]].astype(BF16)
        k_c = k_t[rows[c], lanes[p]].astype(BF16)
        if C2 % LANES == 0:
            both = _dot_nt(q_st, jnp.concatenate([b_c, b_c, k_c, k_c], axis=0))
            gb[c, p], gk[c, p] = both[:, 0:C2], both[:, C2:]
        else:
            gb[c, p] = _dot_nt(q_st, jnp.concatenate([b_c, b_c], axis=0))
            gk[c, p] = _dot_nt(q_st, jnp.concatenate([k_c, k_c], axis=0))
    low = {cp: jnp.where(strict, gb[cp][0:C2, :], 0.0) for cp in probs}
    rb_f = {cp: fold(jnp.where(incl, gb[cp][C2:, :], 0.0)).astype(BF16) for cp in probs}
    akv = {cp: _dot(fold(jnp.where(strict, gk[cp][0:C2, :], 0.0)).astype(BF16), v_st[cp]) for cp in probs}
    rkv = {cp: _dot(fold(jnp.where(incl, gk[cp][C2:, :], 0.0)).astype(BF16), v_st[cp]) for cp in probs}

    first_level = _lower_left(ri, cj, 1)
    t_inv = {cp: eye + jnp.where(first_level, low[cp], 0.0) for cp in probs}
    s = 2
    while s < C:
        sel = _lower_left(ri, cj, s)
        tb = {cp: t_inv[cp].astype(BF16) for cp in probs}
        mid = {cp: _dot(tb[cp], jnp.where(sel, low[cp], 0.0).astype(BF16)).astype(BF16) for cp in probs}
        t_inv = {cp: t_inv[cp] + _dot(mid[cp], tb[cp]) for cp in probs}
        s *= 2
    t_f = {cp: fold(t_inv[cp]).astype(BF16) for cp in probs}

    state = [s_scr[p] for p in range(HP)]
    y = {}
    for c in range(NC):
        state_b = [st.astype(BF16) for st in state]
        x = [_dot_nt(a_b[c, p], state_b[p]) + akv[c, p] for p in range(HP)]
        y0 = [_dot_nt(r_b[c, p], state_b[p]) + rkv[c, p] for p in range(HP)]
        u = [_dot(t_f[c, p], stack(x[p])) for p in range(HP)]
        for p in range(HP):
            y[c, p] = y0[p] + _dot(rb_f[c, p], stack(u[p]))
        upd = []
        for p in range(HP):
            uv = jnp.concatenate([u[p], v[rows[c], lanes[p]]], axis=0).astype(BF16)
            bk = jnp.concatenate([b_h[c][:, lanes[p]], k_h[c][:, lanes[p]]], axis=0).astype(BF16)
            upd.append(_dot_tn(uv, bk))
        state = [state[p] * p_end[c][:, lanes[p]] + jnp.where(same_head, upd[p], 0.0) for p in range(HP)]
    for p in range(HP):
        s_scr[p] = state[p]

    def seg(t, slices):
        parts = _split2(t) if slices == 2 else (t.astype(BF16),)
        return sum(_dot(part, ones_bd) for part in parts)

    y_all = [jnp.concatenate([y[c, p] for c in range(NC)], axis=0) for p in range(HP)]
    mean = [seg(y_all[p], 2) * (1.0 / RW_HEAD) for p in range(HP)]
    bonus = [seg(rk[:, lanes[p]], 1) * v[:, lanes[p]] for p in range(HP)]
    d = [y_all[p] - mean[p] for p in range(HP)]
    var = [seg(d[p] * d[p], 1) * (1.0 / RW_HEAD) for p in range(HP)]
    for p in range(HP):
        yn = d[p] * lax.rsqrt(var[p] + GN_EPS) * lnx_g[:, lanes[p]] + lnx_b[:, lanes[p]]
        z_ref[:, lanes[p]] = ((yn + bonus[p]) * g[:, lanes[p]]).astype(z_ref.dtype)

    @pl.when(ci == pl.num_programs(2) - 1)
    def _():
        sout_ref[...] = s_scr[...]


def rwkv_recur(ops, s0_bd, vecs, B, T):
    D = ops[0].shape[1]
    HL = min(1024, D)
    HP = HL // LANES
    C = min(CHUNK, T)
    NC = 2 if T % (2 * C) == 0 else 1
    ops3 = [o.reshape(B, T, D) for o in ops]
    tile = pl.BlockSpec((None, NC * C, HL), lambda b, j, c: (b, c, j))
    st = pl.BlockSpec((None, HP, LANES, LANES), lambda b, j, c: (b, j, 0, 0))
    z, s_fin = pl.pallas_call(
        functools.partial(_recur_kernel, C=C, NC=NC, HP=HP),
        grid=(B, D // HL, T // (NC * C)),
        in_specs=[tile] * 7 + [st, pl.BlockSpec((vecs.shape[0], HL), lambda b, j, c: (0, j))],
        out_specs=[tile, st],
        out_shape=[jax.ShapeDtypeStruct((B, T, D), BF16), jax.ShapeDtypeStruct(s0_bd.shape, F32)],
        scratch_shapes=[pltpu.VMEM((HP, LANES, LANES), F32)],
        compiler_params=_params(("parallel", "parallel", "arbitrary")),
        name="rwkv_recur",
    )(*ops3, s0_bd, vecs)
    return z.reshape(B * T, D), s_fin


def _gate_operand(gate, B, T, bm, bn):
    N = gate.shape[1]
    col = (lambda j: j) if bn < N else (lambda j: 0)
    if T % bm == 0:
        return gate.reshape(B, 1, N), pl.BlockSpec((None, 1, bn), lambda i, j: ((i * bm) // T, 0, col(j)))
    rows = jnp.broadcast_to(gate[:, None, :], (B, T, N)).reshape(B * T, N)
    return rows, pl.BlockSpec((bm, bn), lambda i, j: (i, col(j)))


def _matmul_res_kernel(a_ref, w_ref, x_ref, gate_ref, o_ref):
    o_ref[...] = x_ref[...] + gate_ref[...] * _dot(a_ref[...], w_ref[...])


def matmul_res(a, w, x, gate, B, T):
    M, K = a.shape
    N = w.shape[1]
    bm, bn = min(1024, M), min(512, N)
    gate_arr, gate_spec = _gate_operand(gate, B, T, bm, bn)
    return pl.pallas_call(
        _matmul_res_kernel,
        grid=(M // bm, N // bn),
        in_specs=[pl.BlockSpec((bm, K), lambda i, j: (i, 0)),
                  pl.BlockSpec((K, bn), lambda i, j: (0, j)),
                  pl.BlockSpec((bm, bn), lambda i, j: (i, j)), gate_spec],
        out_specs=pl.BlockSpec((bm, bn), lambda i, j: (i, j)),
        out_shape=jax.ShapeDtypeStruct((M, N), F32),
        compiler_params=_params(("parallel", "arbitrary")),
        name="matmul_res",
    )(a, w, x, gate_arr)


def _mlp_kernel(x_ref, g_ref, sh_ref, sc_ref, gate_ref, w1_ref, w2_ref, o_ref, h_scr, acc):
    f = pl.program_id(1)

    @pl.when(f == 0)
    def _():
        h_scr[...] = (_rms(x_ref[...], g_ref[...]) * (1.0 + sc_ref[...]) + sh_ref[...]).astype(BF16)
        acc[...] = jnp.zeros_like(acc)

    t = jnp.square(jnp.maximum(_dot(h_scr[...], w1_ref[...]), 0.0)).astype(BF16)
    acc[...] += _dot(t, w2_ref[...])

    @pl.when(f == pl.num_programs(1) - 1)
    def _():
        o_ref[...] = x_ref[...] + gate_ref[...] * acc[...]


def mlp_res(x, g, shift, scale, gate, w1, w2, B, T):
    M, D = x.shape
    F = w1.shape[1]
    bm = 512 if T % 512 == 0 else min(256, M)
    bf = min(1024, F)
    (sh_arr, vec_spec), (sc_arr, _), (gate_arr, _) = (_gate_operand(v, B, T, bm, D) for v in (shift, scale, gate))
    return pl.pallas_call(
        _mlp_kernel,
        grid=(M // bm, F // bf),
        in_specs=[pl.BlockSpec((bm, D), lambda i, f: (i, 0)),
                  pl.BlockSpec((1, D), lambda i, f: (0, 0)), vec_spec, vec_spec, vec_spec,
                  pl.BlockSpec((D, bf), lambda i, f: (0, f)),
                  pl.BlockSpec((bf, D), lambda i, f: (f, 0))],
        out_specs=pl.BlockSpec((bm, D), lambda i, f: (i, 0)),
        out_shape=jax.ShapeDtypeStruct((M, D), F32),
        scratch_shapes=[pltpu.VMEM((bm, D), BF16), pltpu.VMEM((bm, D), F32)],
        compiler_params=_params(("parallel", "arbitrary")),
        name="mlp_res",
    )(x, g.reshape(1, D), sh_arr, sc_arr, gate_arr, w1, w2)


def _rope_pair(t, tab):
    prod = t * tab
    return (prod + pltpu.roll(prod, ROPE_DIM, 1))[:, :ROPE_DIM]


def _latent_kernel(x_ref, gx_ref, sh_ref, sc_ref, w_ref, g_ref, tab_ref, ckv_ref, kpe_ref):
    h = (_rms(x_ref[...], gx_ref[...]) * (1.0 + sc_ref[...]) + sh_ref[...]).astype(BF16)
    acc = _dot(h, w_ref[...])
    R = g_ref.shape[1]
    ckv_ref[...] = _rms(acc[:, :R], g_ref[...])
    kpe_ref[...] = _rope_pair(acc[:, R:R + 2 * ROPE_DIM], tab_ref[...])


def latent(x, gx, shift, scale, w, g, tab):
    B, T, D = x.shape
    R = g.shape[0]
    bt = min(512, T)
    vec = pl.BlockSpec((None, 1, D), lambda b, t: (b, 0, 0))
    return pl.pallas_call(
        _latent_kernel,
        grid=(B, T // bt),
        in_specs=[pl.BlockSpec((None, bt, D), lambda b, t: (b, t, 0)),
                  pl.BlockSpec((1, D), lambda b, t: (0, 0)), vec, vec,
                  pl.BlockSpec(w.shape, lambda b, t: (0, 0)),
                  pl.BlockSpec((1, R), lambda b, t: (0, 0)),
                  pl.BlockSpec((bt, 2 * ROPE_DIM), lambda b, t: (t, 0))],
        out_specs=[pl.BlockSpec((None, bt, R), lambda b, t: (b, t, 0)),
                   pl.BlockSpec((None, bt, ROPE_DIM), lambda b, t: (b, t, 0))],
        out_shape=[jax.ShapeDtypeStruct((B, T, R), F32), jax.ShapeDtypeStruct((B, T, ROPE_DIM), F32)],
        compiler_params=_params(("parallel", "parallel")),
        name="latent",
    )(x, gx.reshape(1, D), shift.reshape(B, 1, D), scale.reshape(B, 1, D), w, g.reshape(1, R), tab)


def _kv_expand_kernel(c_ref, pe_ref, wuk_ref, wuv_ref, k_ref, vt_ref):
    c = c_ref[...].astype(BF16)
    kn = _dot(c, wuk_ref[...])
    vv = _dot(c, wuv_ref[...])
    pe = pe_ref[...].astype(BF16)
    for h in range(k_ref.shape[0]):
        k_ref[h, :, 0:NOPE_DIM] = kn[:, h * NOPE_DIM:(h + 1) * NOPE_DIM].astype(BF16)
        k_ref[h, :, NOPE_DIM:NOPE_DIM + ROPE_DIM] = pe
        vt_ref[h, 0:V_DIM, :] = vv[:, h * V_DIM:(h + 1) * V_DIM].T.astype(BF16)
        vt_ref[h, V_DIM:V_ROWS, :] = jnp.ones((V_ROWS - V_DIM, c.shape[0]), BF16)


def kv_expand(ckv, kpe, wuk, wuv, H):
    B, S, R = ckv.shape
    bt = min(512, S)
    hb = min(8, H)
    DK = NOPE_DIM + ROPE_DIM
    return pl.pallas_call(
        _kv_expand_kernel,
        grid=(B, S // bt, H // hb),
        in_specs=[pl.BlockSpec((None, bt, R), lambda b, t, j: (b, t, 0)),
                  pl.BlockSpec((None, bt, ROPE_DIM), lambda b, t, j: (b, t, 0)),
                  pl.BlockSpec((R, hb * NOPE_DIM), lambda b, t, j: (0, j)),
                  pl.BlockSpec((R, hb * V_DIM), lambda b, t, j: (0, j))],
        out_specs=[pl.BlockSpec((None, hb, bt, DK), lambda b, t, j: (b, j, t, 0)),
                   pl.BlockSpec((None, hb, V_ROWS, bt), lambda b, t, j: (b, j, 0, t))],
        out_shape=[jax.ShapeDtypeStruct((B, H, S, DK), BF16), jax.ShapeDtypeStruct((B, H, V_ROWS, S), BF16)],
        compiler_params=_params(("parallel", "parallel", "arbitrary")),
        name="kv_expand",
    )(ckv, kpe, wuk, wuv)


def _wdq_kernel(x_ref, gx_ref, sh_ref, sc_ref, w_ref, g_ref, o_ref):
    h = (_rms(x_ref[...], gx_ref[...]) * (1.0 + sc_ref[...]) + sh_ref[...]).astype(BF16)
    o_ref[...] = _rms(_dot(h, w_ref[...]), g_ref[...]).astype(BF16)


def wdq_norm(x, gx, shift, scale, w, g):
    B, T, D = x.shape
    R = w.shape[1]
    bt = min(1024, T)
    vec = pl.BlockSpec((None, 1, D), lambda b, t: (b, 0, 0))
    return pl.pallas_call(
        _wdq_kernel,
        grid=(B, T // bt),
        in_specs=[pl.BlockSpec((None, bt, D), lambda b, t: (b, t, 0)),
                  pl.BlockSpec((1, D), lambda b, t: (0, 0)), vec, vec,
                  pl.BlockSpec((D, R), lambda b, t: (0, 0)),
                  pl.BlockSpec((1, R), lambda b, t: (0, 0))],
        out_specs=pl.BlockSpec((None, bt, R), lambda b, t: (b, t, 0)),
        out_shape=jax.ShapeDtypeStruct((B, T, R), BF16),
        compiler_params=_params(("parallel", "parallel")),
        name="wdq_norm",
    )(x, gx.reshape(1, D), shift.reshape(B, 1, D), scale.reshape(B, 1, D), w, g.reshape(1, R))


def _wuq_kernel(c_ref, w_ref, tab_ref, q_ref, *, scale):
    acc = _dot(c_ref[...], w_ref[...])
    tab = tab_ref[...]
    W = NOPE_DIM + 2 * ROPE_DIM
    for h in range(q_ref.shape[0]):
        q_ref[h, :, 0:NOPE_DIM] = (acc[:, h * W:h * W + NOPE_DIM] * scale).astype(BF16)
        pe = _rope_pair(acc[:, h * W + NOPE_DIM:(h + 1) * W], tab)
        q_ref[h, :, NOPE_DIM:NOPE_DIM + ROPE_DIM] = (pe * scale).astype(BF16)


def wuq_rope(cq, w, tab, H, head_major):
    B, T, R = cq.shape
    bt = min(512, T)
    nt = T // bt
    hb = min(8, H)
    W = NOPE_DIM + 2 * ROPE_DIM
    DK = NOPE_DIM + ROPE_DIM
    if head_major:
        out_spec = pl.BlockSpec((hb, bt, DK), lambda b, t, j: (j, b * nt + t, 0))
        out_shape = (H, B * T, DK)
    else:
        out_spec = pl.BlockSpec((None, hb, bt, DK), lambda b, t, j: (b, j, t, 0))
        out_shape = (B, H, T, DK)
    return pl.pallas_call(
        functools.partial(_wuq_kernel, scale=MLA_SCALE if head_major else MLA_SCALE * LOG2E),
        grid=(B, nt, H // hb),
        in_specs=[pl.BlockSpec((None, bt, R), lambda b, t, j: (b, t, 0)),
                  pl.BlockSpec((R, hb * W), lambda b, t, j: (0, j)),
                  pl.BlockSpec((bt, 2 * ROPE_DIM), lambda b, t, j: (t, 0))],
        out_specs=out_spec,
        out_shape=jax.ShapeDtypeStruct(out_shape, BF16),
        compiler_params=_params(("parallel", "parallel", "arbitrary")),
        name="wuq_rope",
    )(cq, w, tab)


def _visible(qpos, kpos):
    return _blk(kpos, CHUNK) <= _blk(qpos, CHUNK)


def _flash_kernel(q_ref, k_ref, vt_ref, o_ref, s_a, s_b, m_scr, acc_scr, *, tile):
    hb = q_ref.shape[0]
    ha = hb // 2
    first, second = range(0, ha), range(ha, hb)
    qi = pl.program_id(2)
    qpos = qi * tile + lax.broadcasted_iota(jnp.int32, (1, tile), 1)
    m_scr[...] = jnp.full(m_scr.shape, -jnp.inf, F32)
    acc_scr[...] = jnp.zeros(acc_scr.shape, F32)

    def scores(ki, heads, s_ref):
        start = pl.multiple_of(ki * tile, tile)
        for h in heads:
            s_ref[h - heads[0]] = _dot_nt(k_ref[h, pl.ds(start, tile), :], q_ref[h])

    def softmax_pv(ki, heads, s_ref, masked):
        start = pl.multiple_of(ki * tile, tile)
        for h in heads:
            s = s_ref[h - heads[0]]
            if masked:
                kpos = ki * tile + lax.broadcasted_iota(jnp.int32, (tile, 1), 0)
                s = jnp.where(_visible(qpos, kpos), s, -jnp.inf)
            m = m_scr[h]
            m_new = jnp.maximum(m, jnp.max(s, axis=0, keepdims=True))
            p = jnp.exp2(s - m_new).astype(BF16)
            acc_scr[h] = jnp.exp2(m - m_new) * acc_scr[h] + _dot(vt_ref[h, :, pl.ds(start, tile)], p)
            m_scr[h] = m_new

    def body(ki, carry):
        scores(ki, second, s_b)
        softmax_pv(ki, first, s_a, False)
        scores(ki + 1, first, s_a)
        softmax_pv(ki, second, s_b, False)
        return carry

    scores(0, first, s_a)
    lax.fori_loop(0, qi, body, 0)
    scores(qi, second, s_b)
    softmax_pv(qi, first, s_a, True)
    softmax_pv(qi, second, s_b, True)
    for h in range(hb):
        acc = acc_scr[h]
        o_ref[:, h * V_DIM:(h + 1) * V_DIM] = (acc[0:V_DIM, :] / acc[V_DIM:V_DIM + 1, :]).T.astype(o_ref.dtype)


def flash_prompt(q, k, vt):
    B, H, T, DK = q.shape
    tile = min(256, T)
    assert tile % CHUNK == 0 and T % tile == 0 and H % 2 == 0
    hb = next(n for n in (8, 4, 2) if H % n == 0)
    return pl.pallas_call(
        functools.partial(_flash_kernel, tile=tile),
        grid=(B, H // hb, T // tile),
        in_specs=[pl.BlockSpec((None, hb, tile, DK), lambda b, j, i: (b, j, i, 0)),
                  pl.BlockSpec((None, hb, T, DK), lambda b, j, i: (b, j, 0, 0)),
                  pl.BlockSpec((None, hb, V_ROWS, T), lambda b, j, i: (b, j, 0, 0))],
        out_specs=pl.BlockSpec((None, tile, hb * V_DIM), lambda b, j, i: (b, i, j)),
        out_shape=jax.ShapeDtypeStruct((B, T, H * V_DIM), BF16),
        scratch_shapes=[pltpu.VMEM((hb // 2, tile, tile), F32), pltpu.VMEM((hb // 2, tile, tile), F32),
                        pltpu.VMEM((hb, 1, tile), F32), pltpu.VMEM((hb, V_ROWS, tile), F32)],
        compiler_params=_params(("parallel", "parallel", "arbitrary")),
        name="flash_prompt",
    )(q, k, vt)


def _q_absorb_kernel(q_ref, wuk_ref, o_ref):
    o_ref[...] = _dot_nt(q_ref[:, 0:NOPE_DIM], wuk_ref[...]).astype(BF16)


def _attn_latent_kernel(ql_ref, q_ref, qpos_ref, c_ref, pe_ref, o_ref):
    H, T, R = ql_ref.shape
    S = c_ref.shape[0]
    c = c_ref[...].astype(BF16)
    s = (_dot_nt(ql_ref[...].reshape(H * T, R), c)
         + _dot_nt(q_ref[:, :, NOPE_DIM:NOPE_DIM + ROPE_DIM].reshape(H * T, ROPE_DIM), pe_ref[...].astype(BF16)))
    kpos = lax.broadcasted_iota(jnp.int32, (1, S), 1)
    s = jnp.where(_visible(qpos_ref[...], kpos), s, -jnp.inf)
    p = jnp.exp(s - jnp.max(s, axis=-1, keepdims=True))
    l = jnp.sum(p, axis=-1, keepdims=True)
    o_ref[...] = (_dot(p.astype(BF16), c) / l).astype(BF16).reshape(H, T, R)


def _o_expand_kernel(o_ref, wuv_ref, out_ref):
    out_ref[...] = _dot(o_ref[...], wuv_ref[...]).astype(BF16)


def attn_latent(q, ckv, kpe, wuk, wuv, B, T, q0):
    H, M, DK = q.shape
    S, R = ckv.shape[1:]
    q_lat = pl.pallas_call(
        _q_absorb_kernel,
        grid=(H,),
        in_specs=[pl.BlockSpec((None, M, DK), lambda h: (h, 0, 0)),
                  pl.BlockSpec((R, NOPE_DIM), lambda h: (0, h))],
        out_specs=pl.BlockSpec((None, M, R), lambda h: (h, 0, 0)),
        out_shape=jax.ShapeDtypeStruct((H, M, R), BF16),
        compiler_params=_params(("parallel",)),
        name="q_absorb",
    )(q, wuk)
    qpos = jnp.tile(q0 + jnp.arange(T, dtype=jnp.int32), H)[:, None]
    o_lat = pl.pallas_call(
        _attn_latent_kernel,
        grid=(B,),
        in_specs=[pl.BlockSpec((H, T, R), lambda b: (0, b, 0)),
                  pl.BlockSpec((H, T, DK), lambda b: (0, b, 0)),
                  pl.BlockSpec((H * T, 1), lambda b: (0, 0)),
                  pl.BlockSpec((None, S, R), lambda b: (b, 0, 0)),
                  pl.BlockSpec((None, S, ROPE_DIM), lambda b: (b, 0, 0))],
        out_specs=pl.BlockSpec((H, T, R), lambda b: (0, b, 0)),
        out_shape=jax.ShapeDtypeStruct((H, M, R), BF16),
        compiler_params=_params(("parallel",)),
        name="attn_latent",
    )(q_lat, q, qpos, ckv, kpe)
    return pl.pallas_call(
        _o_expand_kernel,
        grid=(H,),
        in_specs=[pl.BlockSpec((None, M, R), lambda h: (h, 0, 0)),
                  pl.BlockSpec((R, V_DIM), lambda h: (0, h))],
        out_specs=pl.BlockSpec((M, V_DIM), lambda h: (0, h)),
        out_shape=jax.ShapeDtypeStruct((M, H * V_DIM), BF16),
        compiler_params=_params(("parallel",)),
        name="o_expand",
    )(o_lat, wuv)


def _pad_cols(w, n):
    return jnp.pad(w, ((0, 0), (0, n - w.shape[1])))


def _pad_rows(w, n):
    return jnp.pad(w, ((0, n - w.shape[0]), (0, 0)))


def _rotate_half_cols(w):
    half = ROPE_DIM // 2
    return jnp.concatenate([-w[..., half:], w[..., :half]], axis=-1)


def _rope_table(pos):
    half = ROPE_DIM // 2
    inv = ROPE_THETA ** (-jnp.arange(half, dtype=F32) / half)
    ang = pos.astype(F32)[:, None] * inv[None, :]
    cos, sin = jnp.cos(ang), jnp.sin(ang)
    return jnp.concatenate([cos, cos, sin, sin], axis=-1)


def _block_diag_states(s):
    B, H, N, _ = s.shape
    s = s.reshape(B, H // 2, 2, N, N)
    z = jnp.zeros_like(s[:, :, 0])
    top = jnp.concatenate([s[:, :, 0], z], axis=-1)
    bot = jnp.concatenate([z, s[:, :, 1]], axis=-1)
    return jnp.concatenate([top, bot], axis=-2)


def _diag_states(s):
    B, HP = s.shape[:2]
    N = RW_HEAD
    return jnp.stack([s[:, :, :N, :N], s[:, :, N:, N:]], axis=2).reshape(B, 2 * HP, N, N)


def _prepare(W):
    D = W['rw_wr'].shape[1]
    P = {}
    bf = lambda a: a.astype(BF16)
    P['mlp_w1'], P['mlp_w2'] = bf(W['mlp_w1']), bf(W['mlp_w2'])
    for n in ('rw_wr', 'rw_wk', 'rw_wv', 'rw_wo', 'rw_g1', 'rw_g2', 'mla_wdq', 'mla_wo'):
        P[n] = bf(W[n])
    NA = W['rw_wr'].shape[0]
    P['rw_w1'] = [bf(_pad_cols(W['rw_w1'][l], LORA_PAD)) for l in range(NA)]
    P['rw_w2'] = [bf(_pad_rows(W['rw_w2'][l], LORA_PAD)) for l in range(NA)]
    P['rw_a1'] = [bf(_pad_cols(W['rw_a1'][l], LORA_PAD)) for l in range(NA)]
    P['rw_a2'] = [bf(_pad_rows(W['rw_a2'][l], LORA_PAD)) for l in range(NA)]
    P['rw_v1'] = [bf(_pad_cols(W['rw_v1'][l], LORA_PAD)) for l in range(NA - 1)]
    P['rw_v2'] = [bf(_pad_rows(W['rw_v2'][l], LORA_PAD)) for l in range(NA - 1)]
    zeros = jnp.zeros((D,), F32)
    P['proj_vecs'], P['recur_vecs'] = [], []
    for l in range(NA):
        v0 = W['rw_v0'][l - 1] if l > 0 else zeros
        P['proj_vecs'].append(jnp.stack([W['rw_w0'][l], W['rw_a0'][l], v0, W['rw_kk'][l], W['rw_ka'][l],
                                         zeros, zeros, zeros]))
        P['recur_vecs'].append(jnp.stack([W['rw_rk'][l], W['rw_lnx_g'][l], W['rw_lnx_b'][l]] + [zeros] * 5))
    HL = min(512, D)
    head = jnp.arange(HL) // RW_HEAD
    P['ones_bd'] = (head[:, None] == head[None, :]).astype(BF16)
    R = W['kv_lat_g'].shape[0]
    wd = W['kv_wd']
    P['kv_wd'] = bf(jnp.concatenate([wd, _rotate_half_cols(wd[:, R:])], axis=1))
    H = W['kv_wuk'].shape[1]
    P['kv_wuk'] = bf(W['kv_wuk'].reshape(R, H * NOPE_DIM))
    P['kv_wuv'] = bf(W['kv_wuv'].reshape(R, H * V_DIM))
    NB, Q = W['mla_wuq'].shape[:2]
    wuq = W['mla_wuq'].reshape(NB, Q, H, NOPE_DIM + ROPE_DIM)
    pe = wuq[..., NOPE_DIM:]
    P['mla_wuq'] = bf(jnp.concatenate([wuq, _rotate_half_cols(pe)], axis=-1).reshape(NB, Q, -1))
    return P


def _trunk(x, mod, kv_mod, pos0, h_prev, s0, past_ckv, past_kpe, W, P):
    B, T, D = x.shape
    M = B * T
    depth = W['ada_w'].shape[0]
    NA = W['rw_wr'].shape[0]
    H = W['kv_wuk'].shape[1]
    tab = _rope_table(pos0 + jnp.arange(T))
    xf = x.reshape(M, D)
    shifts, states = [], []
    v_first = None
    keys = vals = ckv = kpe = None
    for l in range(depth):
        m = mod[l]
        if l < NA:
            has_v = l > 0
            pre = rwkv_pre(xf.reshape(B, T, D), W['norm_mix_g'][l], m[:, 0], m[:, 1], h_prev[l], W['rw_mu'][l],
                           P['rw_w1'][l], P['rw_a1'][l], P['rw_g1'][l], P['rw_v1'][l - 1] if has_v else None)
            shifts.append(pre[-1].reshape(B, D))
            acts = [a.reshape(M, a.shape[-1]) for a in pre[:-1]]
            ops = rwkv_proj(acts, v_first, P['rw_wr'][l], P['rw_wk'][l], P['rw_wv'][l], P['rw_w2'][l],
                            P['rw_a2'][l], P['rw_g2'][l], P['rw_v2'][l - 1] if has_v else None,
                            P['proj_vecs'][l], P['ones_bd'])
            if l == 0:
                v_first = ops[3]
            z, s_fin = rwkv_recur(ops, _block_diag_states(s0[l]), P['recur_vecs'][l], B, T)
            states.append(_diag_states(s_fin))
            xf = matmul_res(z, P['rw_wo'][l], xf, m[:, 2], B, T)
        else:
            j = l - NA
            cq = wdq_norm(xf.reshape(B, T, D), W['norm_mix_g'][l], m[:, 0], m[:, 1], P['mla_wdq'][j], W['mla_q_g'][j])
            if past_ckv is None:
                q = wuq_rope(cq, P['mla_wuq'][j], tab, H, False)
                o = flash_prompt(q, keys, vals).reshape(M, H * V_DIM)
            else:
                q = wuq_rope(cq, P['mla_wuq'][j], tab, H, True)
                o = attn_latent(q, keys, vals, P['kv_wuk'], P['kv_wuv'], B, T, pos0)
            xf = matmul_res(o, P['mla_wo'][j], xf, m[:, 2], B, T)
        xf = mlp_res(xf, W['norm_mlp_g'][l], m[:, 3], m[:, 4], m[:, 5], P['mlp_w1'][l], P['mlp_w2'][l], B, T)
        if l == NA - 1:
            ckv, kpe = latent(xf.reshape(B, T, D), W['kv_norm_g'], kv_mod[:, 0], kv_mod[:, 1],
                              P['kv_wd'], W['kv_lat_g'], tab)
            if past_ckv is None:
                keys, vals = kv_expand(ckv, kpe, P['kv_wuk'], P['kv_wuv'], H)
            else:
                keys = jnp.concatenate([past_ckv, ckv], axis=1)
                vals = jnp.concatenate([past_kpe, kpe], axis=1)
    y = final_norm(xf.reshape(B, T, D), W['final_g'])
    return y, ckv, kpe, jnp.stack(states), jnp.stack(shifts)


def kernel(x_prompt, x_sample, cache_ckv, cache_kpe, state_wkv, state_shift, c_prompt, c_sample, ada_w, ada_b, norm_mix_g, norm_mlp_g, mlp_w1, mlp_w2, rw_mu, rw_w0, rw_w1, rw_w2, rw_a0, rw_a1, rw_a2, rw_v0, rw_v1, rw_v2, rw_g1, rw_g2, rw_wr, rw_wk, rw_wv, rw_wo, rw_kk, rw_ka, rw_rk, rw_lnx_g, rw_lnx_b, kv_ada_w, kv_ada_b, kv_norm_g, kv_wd, kv_lat_g, kv_wuk, kv_wuv, mla_wdq, mla_q_g, mla_wuq, mla_wo, final_g):
    W = dict(ada_w=ada_w, ada_b=ada_b, norm_mix_g=norm_mix_g, norm_mlp_g=norm_mlp_g,
             mlp_w1=mlp_w1, mlp_w2=mlp_w2, rw_mu=rw_mu, rw_w0=rw_w0, rw_w1=rw_w1, rw_w2=rw_w2,
             rw_a0=rw_a0, rw_a1=rw_a1, rw_a2=rw_a2, rw_v0=rw_v0, rw_v1=rw_v1, rw_v2=rw_v2,
             rw_g1=rw_g1, rw_g2=rw_g2, rw_wr=rw_wr, rw_wk=rw_wk, rw_wv=rw_wv, rw_wo=rw_wo,
             rw_kk=rw_kk, rw_ka=rw_ka, rw_rk=rw_rk, rw_lnx_g=rw_lnx_g, rw_lnx_b=rw_lnx_b,
             kv_ada_w=kv_ada_w, kv_ada_b=kv_ada_b, kv_norm_g=kv_norm_g, kv_wd=kv_wd,
             kv_lat_g=kv_lat_g, kv_wuk=kv_wuk, kv_wuv=kv_wuv, mla_wdq=mla_wdq, mla_q_g=mla_q_g,
             mla_wuq=mla_wuq, mla_wo=mla_wo, final_g=final_g)
    P = _prepare(W)
    Bp, Tp, D = x_prompt.shape
    Bs = x_sample.shape[0]
    depth = ada_w.shape[0]
    NA = rw_wr.shape[0]
    c_all = jnp.concatenate([c_prompt, c_sample], axis=0)
    mod = ada_linear(c_all, ada_w, ada_b).reshape(depth, Bp + Bs, N_MOD, D)
    kv_mod = ada_linear(c_all, kv_ada_w[None], kv_ada_b[None]).reshape(Bp + Bs, 2, D)
    h0 = jnp.zeros((NA, Bp, D), F32)
    s0 = jnp.zeros((NA, Bp, D // RW_HEAD, RW_HEAD, RW_HEAD), F32)
    out_p = _trunk(x_prompt, mod[:, :Bp], kv_mod[:Bp], 0, h0, s0, None, None, W, P)
    out_s = _trunk(x_sample, mod[:, Bp:], kv_mod[Bp:], cache_ckv.shape[1], state_shift, state_wkv,
                   cache_ckv, cache_kpe, W, P)
    return (out_p[0], out_s[0]) + out_p[1:] + out_s[1:]
```

```python
import functools

import jax
import jax.numpy as jnp
from jax import lax
from jax.experimental import pallas as pl
from jax.experimental.pallas import tpu as pltpu

F32, BF16 = jnp.float32, jnp.bfloat16

RW_HEAD = 64
CHUNK = 64
GN_EPS = 64e-5
NOPE_DIM = 128
ROPE_DIM = 64
V_DIM = 128
ROPE_THETA = 10000.0
MLA_SCALE = (NOPE_DIM + ROPE_DIM) ** -0.5
LOG2E = 1.4426950408889634
V_ROWS = V_DIM + 16
NORM_EPS = 1e-6
N_MOD = 6

LANES = 128
VMEM_LIMIT = 56 * 1024 * 1024
LORA_PAD = 128


def _params(sem, vmem=VMEM_LIMIT):
    return pltpu.CompilerParams(dimension_semantics=sem, vmem_limit_bytes=vmem)


def _dot(a, b):
    return jnp.dot(a, b, preferred_element_type=F32)


def _dot_nt(a, b):
    return lax.dot_general(a, b, (((1,), (1,)), ((), ())), preferred_element_type=F32)


def _dot_tn(a, b):
    return lax.dot_general(a, b, (((0,), (0,)), ((), ())), preferred_element_type=F32)


def _split2(x):
    hi = x.astype(BF16)
    return hi, (x - hi.astype(F32)).astype(BF16)


def _split3(x):
    hi = x.astype(BF16)
    r1 = x - hi.astype(F32)
    mid = r1.astype(BF16)
    return hi, mid, (r1 - mid.astype(F32)).astype(BF16)


def _sigmoid(x):
    return 1.0 / (1.0 + jnp.exp(-x))


def _blk(i, n):
    assert n & (n - 1) == 0
    return i >> (n.bit_length() - 1)


def _off(i, n):
    assert n & (n - 1) == 0
    return i & (n - 1)


def _lower_left(ri, cj, s):
    return (_blk(ri, 2 * s) == _blk(cj, 2 * s)) & (_off(ri, 2 * s) >= s) & (_off(cj, 2 * s) < s)


def _rms(x, g):
    return x * lax.rsqrt(jnp.mean(x * x, axis=-1, keepdims=True) + NORM_EPS) * g


def _ada_kernel(c_ref, w_ref, b_ref, o_ref):
    c = c_ref[...]
    cs = c * _sigmoid(c)
    o_ref[...] = _dot(cs.astype(BF16), w_ref[...].astype(BF16)) + b_ref[...]


def ada_linear(c, w, b):
    L, K, N = w.shape
    M = c.shape[0]
    bn = min(512, N)
    return pl.pallas_call(
        _ada_kernel,
        grid=(L, N // bn),
        in_specs=[pl.BlockSpec((M, K), lambda l, j: (0, 0)),
                  pl.BlockSpec((None, K, bn), lambda l, j: (l, 0, j)),
                  pl.BlockSpec((None, 1, bn), lambda l, j: (l, 0, j))],
        out_specs=pl.BlockSpec((None, M, bn), lambda l, j: (l, 0, j)),
        out_shape=jax.ShapeDtypeStruct((L, M, N), F32),
        compiler_params=_params(("parallel", "parallel")),
        name="ada_linear",
    )(c, w, b.reshape(L, 1, N))


def _norm_kernel(x_ref, g_ref, o_ref):
    o_ref[...] = _rms(x_ref[...], g_ref[...])


def final_norm(x, g):
    B, T, D = x.shape
    bt = min(512, T)
    return pl.pallas_call(
        _norm_kernel,
        grid=(B, T // bt),
        in_specs=[pl.BlockSpec((None, bt, D), lambda b, t: (b, t, 0)),
                  pl.BlockSpec((1, D), lambda b, t: (0, 0))],
        out_specs=pl.BlockSpec((None, bt, D), lambda b, t: (b, t, 0)),
        out_shape=jax.ShapeDtypeStruct((B, T, D), F32),
        compiler_params=_params(("parallel", "parallel")),
        name="final_norm",
    )(x, g.reshape(1, D))


def _rwkv_pre_kernel(*refs, has_v):
    if has_v:
        (x_ref, g_ref, sh_ref, sc_ref, hp_ref, mu_ref, w1_ref, a1_ref, g1_ref, v1_ref,
         xr_ref, xk_ref, xv_ref, tw_ref, av_ref, gg_ref, vv_ref, hl_ref, prev) = refs
    else:
        (x_ref, g_ref, sh_ref, sc_ref, hp_ref, mu_ref, w1_ref, a1_ref, g1_ref,
         xr_ref, xk_ref, xv_ref, tw_ref, av_ref, gg_ref, hl_ref, prev) = refs
    bt = x_ref.shape[0]
    h = _rms(x_ref[...], g_ref[...]) * (1.0 + sc_ref[...]) + sh_ref[...]

    @pl.when(pl.program_id(1) == 0)
    def _():
        prev[...] = hp_ref[...]

    row = lax.broadcasted_iota(jnp.int32, h.shape, 0)
    xx = jnp.where(row == 0, prev[...], pltpu.roll(h, 1, 0)) - h
    last = h[bt - 1:bt, :]
    prev[...] = last
    hl_ref[...] = last
    mu = mu_ref[...]

    def mix(i):
        return (h + xx * mu[i:i + 1, :]).astype(BF16)

    xr_ref[...] = mix(0)
    tw_ref[...] = jnp.tanh(_dot(mix(1), w1_ref[...])).astype(BF16)
    xk_ref[...] = mix(2)
    xv = mix(3)
    xv_ref[...] = xv
    if has_v:
        vv_ref[...] = _dot(xv, v1_ref[...]).astype(BF16)
    av_ref[...] = _dot(mix(4), a1_ref[...]).astype(BF16)
    gg_ref[...] = _sigmoid(_dot(mix(5), g1_ref[...])).astype(BF16)


def rwkv_pre(x, g, shift, scale, h_prev, mu, w1, a1, g1, v1):
    B, T, D = x.shape
    bt = min(256, T)
    has_v = v1 is not None
    row = lambda n: pl.BlockSpec((None, bt, n), lambda b, t: (b, t, 0))
    vec = pl.BlockSpec((None, 1, D), lambda b, t: (b, 0, 0))
    full = lambda a: pl.BlockSpec(a.shape, lambda b, t: (0, 0))
    lora = [w1, a1, g1] + ([v1] if has_v else [])
    outs = [(D, BF16)] * 3 + [(w1.shape[1], BF16), (a1.shape[1], BF16), (g1.shape[1], BF16)]
    if has_v:
        outs.append((v1.shape[1], BF16))
    res = pl.pallas_call(
        functools.partial(_rwkv_pre_kernel, has_v=has_v),
        grid=(B, T // bt),
        in_specs=[row(D), pl.BlockSpec((1, D), lambda b, t: (0, 0)), vec, vec, vec, full(mu)]
                 + [full(a) for a in lora],
        out_specs=[row(n) for n, _ in outs] + [vec],
        out_shape=[jax.ShapeDtypeStruct((B, T, n), dt) for n, dt in outs]
                  + [jax.ShapeDtypeStruct((B, 1, D), F32)],
        scratch_shapes=[pltpu.VMEM((1, D), F32)],
        compiler_params=_params(("parallel", "arbitrary")),
        name="rwkv_pre",
    )(x, g.reshape(1, D), shift.reshape(B, 1, D), scale.reshape(B, 1, D), h_prev.reshape(B, 1, D), mu, *lora)
    return res


def _rwkv_proj_kernel(*refs, has_v):
    if has_v:
        (xr_ref, xk_ref, xv_ref, tw_ref, av_ref, gg_ref, vv_ref, vf_ref,
         wr_ref, wk_ref, wv_ref, w2_ref, a2_ref, g2_ref, v2_ref, vec_ref, ones_ref,
         r_ref, ld_ref, k_ref, v_ref, kk_ref, b_ref, g_ref) = refs
    else:
        (xr_ref, xk_ref, xv_ref, tw_ref, av_ref, gg_ref,
         wr_ref, wk_ref, wv_ref, w2_ref, a2_ref, g2_ref, vec_ref, ones_ref,
         r_ref, ld_ref, k_ref, v_ref, kk_ref, b_ref, g_ref) = refs
    vec = vec_ref[...]
    w0, a0, v0, k_k, k_a = (vec[i:i + 1, :] for i in range(5))
    r_ref[...] = _dot(xr_ref[...], wr_ref[...])
    kraw = _dot(xk_ref[...], wk_ref[...])
    v = _dot(xv_ref[...], wv_ref[...])
    nz = -(w0 + _dot(tw_ref[...], w2_ref[...]))
    softplus = jnp.maximum(nz, 0.0) + jnp.log(1.0 + jnp.exp(-jnp.abs(nz)))
    ld_ref[...] = -jnp.exp(-softplus - 0.5)
    a = _sigmoid(a0 + _dot(av_ref[...], a2_ref[...]))
    if has_v:
        v = v + (vf_ref[...] - v) * _sigmoid(v0 + _dot(vv_ref[...], v2_ref[...]))
    v_ref[...] = v
    g_ref[...] = _dot(gg_ref[...], g2_ref[...])
    kk = kraw * k_k
    ss = _dot((kk * kk).astype(BF16), ones_ref[...])
    kk = kk / jnp.maximum(jnp.sqrt(ss), 1e-12)
    kk_ref[...] = kk
    b_ref[...] = kk * a
    k_ref[...] = kraw * (1.0 + (a - 1.0) * k_a)


def rwkv_proj(pre, v_first, wr, wk, wv, w2, a2, g2, v2, vecs, ones_bd):
    has_v = v2 is not None
    xr = pre[0]
    M, D = xr.shape
    HL = ones_bd.shape[0]
    bm = min(512, M)
    row = lambda a: pl.BlockSpec((bm, a.shape[1]), lambda i, j: (i, 0))
    tile = pl.BlockSpec((bm, HL), lambda i, j: (i, j))
    col = lambda a: pl.BlockSpec((a.shape[0], HL), lambda i, j: (0, j))
    acts = list(pre[:6]) + ([pre[6]] if has_v else [])
    weights = [wr, wk, wv, w2, a2, g2] + ([v2] if has_v else [])
    return pl.pallas_call(
        functools.partial(_rwkv_proj_kernel, has_v=has_v),
        grid=(M // bm, D // HL),
        in_specs=[row(a) for a in acts] + ([tile] if has_v else []) + [col(w) for w in weights]
                 + [col(vecs), pl.BlockSpec((HL, HL), lambda i, j: (0, 0))],
        out_specs=[tile] * 7,
        out_shape=[jax.ShapeDtypeStruct((M, D), F32)] * 7,
        compiler_params=_params(("parallel", "arbitrary")),
        name="rwkv_proj",
    )(*acts, *([v_first] if has_v else []), *weights, vecs, ones_bd)


def _recur_kernel(r_ref, ld_ref, k_ref, v_ref, kk_ref, b_ref, g_ref, s0_ref, vec_ref,
                  z_ref, sout_ref, s_scr, *, C, NC, HP):
    ci = pl.program_id(2)
    C2 = 2 * C
    RB = NC * C

    @pl.when(ci == 0)
    def _():
        s_scr[...] = s0_ref[...]

    ti = lax.broadcasted_iota(jnp.int32, (RB, RB), 0)
    tj = lax.broadcasted_iota(jnp.int32, (RB, RB), 1)
    tri = jnp.where((_blk(ti, C) == _blk(tj, C)) & (ti >= tj), 1.0, 0.0).astype(BF16)
    ld = ld_ref[...]
    cl = sum(_dot(tri, part) for part in _split3(ld))
    p_in = jnp.exp(cl)
    p_inv = jnp.exp(-cl)
    r, k, v, b = r_ref[...], k_ref[...], v_ref[...], b_ref[...]
    a_t = -(jnp.exp(cl - ld) * kk_ref[...])
    r_t = p_in * r
    b_t = p_inv * b
    k_t = p_inv * k
    rows = [slice(c * C, (c + 1) * C) for c in range(NC)]
    cl_end = [cl[(c + 1) * C - 1:(c + 1) * C, :] for c in range(NC)]
    p_rem = [jnp.exp(cl_end[c] - cl[rows[c], :]) for c in range(NC)]
    p_end = [jnp.exp(cl_end[c]) for c in range(NC)]
    b_h = [p_rem[c] * b[rows[c], :] for c in range(NC)]
    k_h = [p_rem[c] * k[rows[c], :] for c in range(NC)]
    vec = vec_ref[...]
    r_k, lnx_g, lnx_b = (vec[i:i + 1, :] for i in range(3))
    rk = r * k * r_k
    g = g_ref[...]

    lane = lax.broadcasted_iota(jnp.int32, (1, LANES), 1)
    first = lane < RW_HEAD

    def stack(x):
        return jnp.concatenate([jnp.where(first, x, 0.0), jnp.where(first, 0.0, x)], axis=0).astype(BF16)

    def fold(x):
        return x[0:C, :] + x[C:C2, :]

    ri = lax.broadcasted_iota(jnp.int32, (C2, C2), 0)
    cj = lax.broadcasted_iota(jnp.int32, (C2, C2), 1)
    same = _blk(ri, C) == _blk(cj, C)
    strict = same & (_off(ri, C) > _off(cj, C))
    incl = same & (_off(ri, C) >= _off(cj, C))
    eye = jnp.where(ri == cj, 1.0, 0.0)
    same_head = (_blk(lax.broadcasted_iota(jnp.int32, (LANES, LANES), 0), RW_HEAD)
                 == _blk(lax.broadcasted_iota(jnp.int32, (LANES, LANES), 1), RW_HEAD))
    ones_bd = jnp.where(same_head, 1.0, 0.0).astype(BF16)

    lanes = [slice(p * LANES, (p + 1) * LANES) for p in range(HP)]
    probs = [(c, p) for c in range(NC) for p in range(HP)]
    a_b = {cp: a_t[rows[cp[0]], lanes[cp[1]]].astype(BF16) for cp in probs}
    r_b = {cp: r_t[rows[cp[0]], lanes[cp[1]]].astype(BF16) for cp in probs}
    v_st = {cp: stack(v[rows[cp[0]], lanes[cp[1]]]) for cp in probs}
    gb, gk = {}, {}
    for c, p in probs:
        q_st = jnp.concatenate([stack(a_t[rows[c], lanes[p]]), stack(r_t[rows[c], lanes[p]])], axis=0)
        b_c = b_t[rows[c], lanes[p]].astype(BF16)
        k_c = k_t[rows[c], lanes[p]].astype(BF16)
        if C2 % LANES == 0:
            both = _dot_nt(q_st, jnp.concatenate([b_c, b_c, k_c, k_c], axis=0))
            gb[c, p], gk[c, p] = both[:, 0:C2], both[:, C2:]
        else:
            gb[c, p] = _dot_nt(q_st, jnp.concatenate([b_c, b_c], axis=0))
            gk[c, p] = _dot_nt(q_st, jnp.concatenate([k_c, k_c], axis=0))
    low = {cp: jnp.where(strict, gb[cp][0:C2, :], 0.0) for cp in probs}
    rb_f = {cp: fold(jnp.where(incl, gb[cp][C2:, :], 0.0)).astype(BF16) for cp in probs}
    akv = {cp: _dot(fold(jnp.where(strict, gk[cp][0:C2, :], 0.0)).astype(BF16), v_st[cp]) for cp in probs}
    rkv = {cp: _dot(fold(jnp.where(incl, gk[cp][C2:, :], 0.0)).astype(BF16), v_st[cp]) for cp in probs}

    first_level = _lower_left(ri, cj, 1)
    t_inv = {cp: eye + jnp.where(first_level, low[cp], 0.0) for cp in probs}
    s = 2
    while s < C:
        sel = _lower_left(ri, cj, s)
        tb = {cp: t_inv[cp].astype(BF16) for cp in probs}
        mid = {cp: _dot(tb[cp], jnp.where(sel, low[cp], 0.0).astype(BF16)).astype(BF16) for cp in probs}
        t_inv = {cp: t_inv[cp] + _dot(mid[cp], tb[cp]) for cp in probs}
        s *= 2
    t_f = {cp: fold(t_inv[cp]).astype(BF16) for cp in probs}

    state = [s_scr[p] for p in range(HP)]
    y = {}
    for c in range(NC):
        state_b = [st.astype(BF16) for st in state]
        x = [_dot_nt(a_b[c, p], state_b[p]) + akv[c, p] for p in range(HP)]
        y0 = [_dot_nt(r_b[c, p], state_b[p]) + rkv[c, p] for p in range(HP)]
        u = [_dot(t_f[c, p], stack(x[p])) for p in range(HP)]
        for p in range(HP):
            y[c, p] = y0[p] + _dot(rb_f[c, p], stack(u[p]))
        upd = []
        for p in range(HP):
            uv = jnp.concatenate([u[p], v[rows[c], lanes[p]]], axis=0).astype(BF16)
            bk = jnp.concatenate([b_h[c][:, lanes[p]], k_h[c][:, lanes[p]]], axis=0).astype(BF16)
            upd.append(_dot_tn(uv, bk))
        state = [state[p] * p_end[c][:, lanes[p]] + jnp.where(same_head, upd[p], 0.0) for p in range(HP)]
    for p in range(HP):
        s_scr[p] = state[p]

    def seg(t, slices):
        parts = _split2(t) if slices == 2 else (t.astype(BF16),)
        return sum(_dot(part, ones_bd) for part in parts)

    y_all = [jnp.concatenate([y[c, p] for c in range(NC)], axis=0) for p in range(HP)]
    mean = [seg(y_all[p], 2) * (1.0 / RW_HEAD) for p in range(HP)]
    bonus = [seg(rk[:, lanes[p]], 1) * v[:, lanes[p]] for p in range(HP)]
    d = [y_all[p] - mean[p] for p in range(HP)]
    var = [seg(d[p] * d[p], 1) * (1.0 / RW_HEAD) for p in range(HP)]
    for p in range(HP):
        yn = d[p] * lax.rsqrt(var[p] + GN_EPS) * lnx_g[:, lanes[p]] + lnx_b[:, lanes[p]]
        z_ref[:, lanes[p]] = ((yn + bonus[p]) * g[:, lanes[p]]).astype(z_ref.dtype)

    @pl.when(ci == pl.num_programs(2) - 1)
    def _():
        sout_ref[...] = s_scr[...]


def rwkv_recur(ops, s0_bd, vecs, B, T):
    D = ops[0].shape[1]
    HL = min(1024, D)
    HP = HL // LANES
    C = min(CHUNK, T)
    NC = 2 if T % (2 * C) == 0 else 1
    ops3 = [o.reshape(B, T, D) for o in ops]
    tile = pl.BlockSpec((None, NC * C, HL), lambda b, j, c: (b, c, j))
    st = pl.BlockSpec((None, HP, LANES, LANES), lambda b, j, c: (b, j, 0, 0))
    z, s_fin = pl.pallas_call(
        functools.partial(_recur_kernel, C=C, NC=NC, HP=HP),
        grid=(B, D // HL, T // (NC * C)),
        in_specs=[tile] * 7 + [st, pl.BlockSpec((vecs.shape[0], HL), lambda b, j, c: (0, j))],
        out_specs=[tile, st],
        out_shape=[jax.ShapeDtypeStruct((B, T, D), BF16), jax.ShapeDtypeStruct(s0_bd.shape, F32)],
        scratch_shapes=[pltpu.VMEM((HP, LANES, LANES), F32)],
        compiler_params=_params(("parallel", "parallel", "arbitrary")),
        name="rwkv_recur",
    )(*ops3, s0_bd, vecs)
    return z.reshape(B * T, D), s_fin


def _gate_operand(gate, B, T, bm, bn):
    N = gate.shape[1]
    col = (lambda j: j) if bn < N else (lambda j: 0)
    if T % bm == 0:
        return gate.reshape(B, 1, N), pl.BlockSpec((None, 1, bn), lambda i, j: ((i * bm) // T, 0, col(j)))
    rows = jnp.broadcast_to(gate[:, None, :], (B, T, N)).reshape(B * T, N)
    return rows, pl.BlockSpec((bm, bn), lambda i, j: (i, col(j)))


def _matmul_res_kernel(a_ref, w_ref, x_ref, gate_ref, o_ref):
    o_ref[...] = x_ref[...] + gate_ref[...] * _dot(a_ref[...], w_ref[...])


def matmul_res(a, w, x, gate, B, T):
    M, K = a.shape
    N = w.shape[1]
    bm, bn = min(1024, M), min(512, N)
    gate_arr, gate_spec = _gate_operand(gate, B, T, bm, bn)
    return pl.pallas_call(
        _matmul_res_kernel,
        grid=(M // bm, N // bn),
        in_specs=[pl.BlockSpec((bm, K), lambda i, j: (i, 0)),
                  pl.BlockSpec((K, bn), lambda i, j: (0, j)),
                  pl.BlockSpec((bm, bn), lambda i, j: (i, j)), gate_spec],
        out_specs=pl.BlockSpec((bm, bn), lambda i, j: (i, j)),
        out_shape=jax.ShapeDtypeStruct((M, N), F32),
        compiler_params=_params(("parallel", "arbitrary")),
        name="matmul_res",
    )(a, w, x, gate_arr)


def _mlp_kernel(x_ref, g_ref, sh_ref, sc_ref, gate_ref, w1_ref, w2_ref, o_ref, h_scr, acc):
    f = pl.program_id(1)

    @pl.when(f == 0)
    def _():
        h_scr[...] = (_rms(x_ref[...], g_ref[...]) * (1.0 + sc_ref[...]) + sh_ref[...]).astype(BF16)
        acc[...] = jnp.zeros_like(acc)

    t = jnp.square(jnp.maximum(_dot(h_scr[...], w1_ref[...]), 0.0)).astype(BF16)
    acc[...] += _dot(t, w2_ref[...])

    @pl.when(f == pl.num_programs(1) - 1)
    def _():
        o_ref[...] = x_ref[...] + gate_ref[...] * acc[...]


def mlp_res(x, g, shift, scale, gate, w1, w2, B, T):
    M, D = x.shape
    F = w1.shape[1]
    bm = 512 if T % 512 == 0 else min(256, M)
    bf = min(1024, F)
    (sh_arr, vec_spec), (sc_arr, _), (gate_arr, _) = (_gate_operand(v, B, T, bm, D) for v in (shift, scale, gate))
    return pl.pallas_call(
        _mlp_kernel,
        grid=(M // bm, F // bf),
        in_specs=[pl.BlockSpec((bm, D), lambda i, f: (i, 0)),
                  pl.BlockSpec((1, D), lambda i, f: (0, 0)), vec_spec, vec_spec, vec_spec,
                  pl.BlockSpec((D, bf), lambda i, f: (0, f)),
                  pl.BlockSpec((bf, D), lambda i, f: (f, 0))],
        out_specs=pl.BlockSpec((bm, D), lambda i, f: (i, 0)),
        out_shape=jax.ShapeDtypeStruct((M, D), F32),
        scratch_shapes=[pltpu.VMEM((bm, D), BF16), pltpu.VMEM((bm, D), F32)],
        compiler_params=_params(("parallel", "arbitrary")),
        name="mlp_res",
    )(x, g.reshape(1, D), sh_arr, sc_arr, gate_arr, w1, w2)


def _rope_pair(t, tab):
    prod = t * tab
    return (prod + pltpu.roll(prod, ROPE_DIM, 1))[:, :ROPE_DIM]


def _latent_kernel(x_ref, gx_ref, sh_ref, sc_ref, w_ref, g_ref, tab_ref, ckv_ref, kpe_ref):
    h = (_rms(x_ref[...], gx_ref[...]) * (1.0 + sc_ref[...]) + sh_ref[...]).astype(BF16)
    acc = _dot(h, w_ref[...])
    R = g_ref.shape[1]
    ckv_ref[...] = _rms(acc[:, :R], g_ref[...])
    kpe_ref[...] = _rope_pair(acc[:, R:R + 2 * ROPE_DIM], tab_ref[...])


def latent(x, gx, shift, scale, w, g, tab):
    B, T, D = x.shape
    R = g.shape[0]
    bt = min(512, T)
    vec = pl.BlockSpec((None, 1, D), lambda b, t: (b, 0, 0))
    return pl.pallas_call(
        _latent_kernel,
        grid=(B, T // bt),
        in_specs=[pl.BlockSpec((None, bt, D), lambda b, t: (b, t, 0)),
                  pl.BlockSpec((1, D), lambda b, t: (0, 0)), vec, vec,
                  pl.BlockSpec(w.shape, lambda b, t: (0, 0)),
                  pl.BlockSpec((1, R), lambda b, t: (0, 0)),
                  pl.BlockSpec((bt, 2 * ROPE_DIM), lambda b, t: (t, 0))],
        out_specs=[pl.BlockSpec((None, bt, R), lambda b, t: (b, t, 0)),
                   pl.BlockSpec((None, bt, ROPE_DIM), lambda b, t: (b, t, 0))],
        out_shape=[jax.ShapeDtypeStruct((B, T, R), F32), jax.ShapeDtypeStruct((B, T, ROPE_DIM), F32)],
        compiler_params=_params(("parallel", "parallel")),
        name="latent",
    )(x, gx.reshape(1, D), shift.reshape(B, 1, D), scale.reshape(B, 1, D), w, g.reshape(1, R), tab)


def _kv_expand_kernel(c_ref, pe_ref, wuk_ref, wuv_ref, k_ref, vt_ref):
    c = c_ref[...].astype(BF16)
    kn = _dot(c, wuk_ref[...])
    vv = _dot(c, wuv_ref[...])
    pe = pe_ref[...].astype(BF16)
    for h in range(k_ref.shape[0]):
        k_ref[h, :, 0:NOPE_DIM] = kn[:, h * NOPE_DIM:(h + 1) * NOPE_DIM].astype(BF16)
        k_ref[h, :, NOPE_DIM:NOPE_DIM + ROPE_DIM] = pe
        vt_ref[h, 0:V_DIM, :] = vv[:, h * V_DIM:(h + 1) * V_DIM].T.astype(BF16)
        vt_ref[h, V_DIM:V_ROWS, :] = jnp.ones((V_ROWS - V_DIM, c.shape[0]), BF16)


def kv_expand(ckv, kpe, wuk, wuv, H):
    B, S, R = ckv.shape
    bt = min(512, S)
    hb = min(8, H)
    DK = NOPE_DIM + ROPE_DIM
    return pl.pallas_call(
        _kv_expand_kernel,
        grid=(H // hb, B, S // bt),
        in_specs=[pl.BlockSpec((None, bt, R), lambda j, b, t: (b, t, 0)),
                  pl.BlockSpec((None, bt, ROPE_DIM), lambda j, b, t: (b, t, 0)),
                  pl.BlockSpec((R, hb * NOPE_DIM), lambda j, b, t: (0, j)),
                  pl.BlockSpec((R, hb * V_DIM), lambda j, b, t: (0, j))],
        out_specs=[pl.BlockSpec((None, hb, bt, DK), lambda j, b, t: (b, j, t, 0)),
                   pl.BlockSpec((None, hb, V_ROWS, bt), lambda j, b, t: (b, j, 0, t))],
        out_shape=[jax.ShapeDtypeStruct((B, H, S, DK), BF16), jax.ShapeDtypeStruct((B, H, V_ROWS, S), BF16)],
        compiler_params=_params(("parallel", "parallel", "arbitrary")),
        name="kv_expand",
    )(ckv, kpe, wuk, wuv)


def _wdq_kernel(x_ref, gx_ref, sh_ref, sc_ref, w_ref, g_ref, o_ref):
    h = (_rms(x_ref[...], gx_ref[...]) * (1.0 + sc_ref[...]) + sh_ref[...]).astype(BF16)
    o_ref[...] = _rms(_dot(h, w_ref[...]), g_ref[...]).astype(BF16)


def wdq_norm(x, gx, shift, scale, w, g):
    B, T, D = x.shape
    R = w.shape[1]
    bt = min(1024, T)
    vec = pl.BlockSpec((None, 1, D), lambda b, t: (b, 0, 0))
    return pl.pallas_call(
        _wdq_kernel,
        grid=(B, T // bt),
        in_specs=[pl.BlockSpec((None, bt, D), lambda b, t: (b, t, 0)),
                  pl.BlockSpec((1, D), lambda b, t: (0, 0)), vec, vec,
                  pl.BlockSpec((D, R), lambda b, t: (0, 0)),
                  pl.BlockSpec((1, R), lambda b, t: (0, 0))],
        out_specs=pl.BlockSpec((None, bt, R), lambda b, t: (b, t, 0)),
        out_shape=jax.ShapeDtypeStruct((B, T, R), BF16),
        compiler_params=_params(("parallel", "parallel")),
        name="wdq_norm",
    )(x, gx.reshape(1, D), shift.reshape(B, 1, D), scale.reshape(B, 1, D), w, g.reshape(1, R))


def _wuq_kernel(c_ref, w_ref, tab_ref, q_ref, *, scale):
    acc = _dot(c_ref[...], w_ref[...])
    tab = tab_ref[...]
    W = NOPE_DIM + 2 * ROPE_DIM
    for h in range(q_ref.shape[0]):
        q_ref[h, :, 0:NOPE_DIM] = (acc[:, h * W:h * W + NOPE_DIM] * scale).astype(BF16)
        pe = _rope_pair(acc[:, h * W + NOPE_DIM:(h + 1) * W], tab)
        q_ref[h, :, NOPE_DIM:NOPE_DIM + ROPE_DIM] = (pe * scale).astype(BF16)


def wuq_rope(cq, w, tab, H, head_major):
    B, T, R = cq.shape
    bt = min(512, T)
    nt = T // bt
    hb = min(8, H)
    W = NOPE_DIM + 2 * ROPE_DIM
    DK = NOPE_DIM + ROPE_DIM
    if head_major:
        out_spec = pl.BlockSpec((hb, bt, DK), lambda j, b, t: (j, b * nt + t, 0))
        out_shape = (H, B * T, DK)
    else:
        out_spec = pl.BlockSpec((None, hb, bt, DK), lambda j, b, t: (b, j, t, 0))
        out_shape = (B, H, T, DK)
    return pl.pallas_call(
        functools.partial(_wuq_kernel, scale=MLA_SCALE if head_major else MLA_SCALE * LOG2E),
        grid=(H // hb, B, nt),
        in_specs=[pl.BlockSpec((None, bt, R), lambda j, b, t: (b, t, 0)),
                  pl.BlockSpec((R, hb * W), lambda j, b, t: (0, j)),
                  pl.BlockSpec((bt, 2 * ROPE_DIM), lambda j, b, t: (t, 0))],
        out_specs=out_spec,
        out_shape=jax.ShapeDtypeStruct(out_shape, BF16),
        compiler_params=_params(("parallel", "parallel", "parallel")),
        name="wuq_rope",
    )(cq, w, tab)


def _visible(qpos, kpos):
    return _blk(kpos, CHUNK) <= _blk(qpos, CHUNK)


def _flash_kernel(q_ref, k_ref, vt_ref, o_ref, s_a, s_b, m_scr, acc_scr, *, tile):
    hb = q_ref.shape[0]
    ha = hb // 2
    first, second = range(0, ha), range(ha, hb)
    qi = pl.program_id(2)
    qpos = qi * tile + lax.broadcasted_iota(jnp.int32, (1, tile), 1)
    m_scr[...] = jnp.full(m_scr.shape, -jnp.inf, F32)
    acc_scr[...] = jnp.zeros(acc_scr.shape, F32)

    def scores(ki, heads, s_ref):
        start = pl.multiple_of(ki * tile, tile)
        for h in heads:
            s_ref[h - heads[0]] = _dot_nt(k_ref[h, pl.ds(start, tile), :], q_ref[h])

    def softmax_pv(ki, heads, s_ref, masked):
        start = pl.multiple_of(ki * tile, tile)
        for h in heads:
            s = s_ref[h - heads[0]]
            if masked:
                kpos = ki * tile + lax.broadcasted_iota(jnp.int32, (tile, 1), 0)
                s = jnp.where(_visible(qpos, kpos), s, -jnp.inf)
            m = m_scr[h]
            m_new = jnp.maximum(m, jnp.max(s, axis=0, keepdims=True))
            p = jnp.exp2(s - m_new).astype(BF16)
            acc_scr[h] = jnp.exp2(m - m_new) * acc_scr[h] + _dot(vt_ref[h, :, pl.ds(start, tile)], p)
            m_scr[h] = m_new

    def body(ki, carry):
        scores(ki, second, s_b)
        softmax_pv(ki, first, s_a, False)
        scores(ki + 1, first, s_a)
        softmax_pv(ki, second, s_b, False)
        return carry

    scores(0, first, s_a)
    lax.fori_loop(0, qi, body, 0)
    scores(qi, second, s_b)
    softmax_pv(qi, first, s_a, True)
    softmax_pv(qi, second, s_b, True)
    for h in range(hb):
        acc = acc_scr[h]
        o_ref[:, h * V_DIM:(h + 1) * V_DIM] = (acc[0:V_DIM, :] / acc[V_DIM:V_DIM + 1, :]).T.astype(o_ref.dtype)


def flash_prompt(q, k, vt):
    B, H, T, DK = q.shape
    tile = min(256, T)
    assert tile % CHUNK == 0 and T % tile == 0 and H % 2 == 0
    hb = next(n for n in (8, 4, 2) if H % n == 0)
    return pl.pallas_call(
        functools.partial(_flash_kernel, tile=tile),
        grid=(B, H // hb, T // tile),
        in_specs=[pl.BlockSpec((None, hb, tile, DK), lambda b, j, i: (b, j, i, 0)),
                  pl.BlockSpec((None, hb, T, DK), lambda b, j, i: (b, j, 0, 0)),
                  pl.BlockSpec((None, hb, V_ROWS, T), lambda b, j, i: (b, j, 0, 0))],
        out_specs=pl.BlockSpec((None, tile, hb * V_DIM), lambda b, j, i: (b, i, j)),
        out_shape=jax.ShapeDtypeStruct((B, T, H * V_DIM), BF16),
        scratch_shapes=[pltpu.VMEM((hb // 2, tile, tile), F32), pltpu.VMEM((hb // 2, tile, tile), F32),
                        pltpu.VMEM((hb, 1, tile), F32), pltpu.VMEM((hb, V_ROWS, tile), F32)],
        compiler_params=_params(("parallel", "parallel", "arbitrary")),
        name="flash_prompt",
    )(q, k, vt)


def _q_absorb_kernel(q_ref, wuk_ref, o_ref):
    o_ref[...] = _dot_nt(q_ref[:, 0:NOPE_DIM], wuk_ref[...]).astype(BF16)


def _attn_latent_kernel(ql_ref, q_ref, qpos_ref, c_ref, pe_ref, o_ref):
    H, T, R = ql_ref.shape
    S = c_ref.shape[0]
    c = c_ref[...].astype(BF16)
    s = (_dot_nt(ql_ref[...].reshape(H * T, R), c)
         + _dot_nt(q_ref[:, :, NOPE_DIM:NOPE_DIM + ROPE_DIM].reshape(H * T, ROPE_DIM), pe_ref[...].astype(BF16)))
    kpos = lax.broadcasted_iota(jnp.int32, (1, S), 1)
    s = jnp.where(_visible(qpos_ref[...], kpos), s, -jnp.inf)
    p = jnp.exp(s - jnp.max(s, axis=-1, keepdims=True))
    l = jnp.sum(p, axis=-1, keepdims=True)
    o_ref[...] = (_dot(p.astype(BF16), c) / l).astype(BF16).reshape(H, T, R)


def _o_expand_kernel(o_ref, wuv_ref, out_ref):
    out_ref[...] = _dot(o_ref[...], wuv_ref[...]).astype(BF16)


def attn_latent(q, ckv, kpe, wuk, wuv, B, T, q0):
    H, M, DK = q.shape
    S, R = ckv.shape[1:]
    q_lat = pl.pallas_call(
        _q_absorb_kernel,
        grid=(H,),
        in_specs=[pl.BlockSpec((None, M, DK), lambda h: (h, 0, 0)),
                  pl.BlockSpec((R, NOPE_DIM), lambda h: (0, h))],
        out_specs=pl.BlockSpec((None, M, R), lambda h: (h, 0, 0)),
        out_shape=jax.ShapeDtypeStruct((H, M, R), BF16),
        compiler_params=_params(("parallel",)),
        name="q_absorb",
    )(q, wuk)
    qpos = jnp.tile(q0 + jnp.arange(T, dtype=jnp.int32), H)[:, None]
    o_lat = pl.pallas_call(
        _attn_latent_kernel,
        grid=(B,),
        in_specs=[pl.BlockSpec((H, T, R), lambda b: (0, b, 0)),
                  pl.BlockSpec((H, T, DK), lambda b: (0, b, 0)),
                  pl.BlockSpec((H * T, 1), lambda b: (0, 0)),
                  pl.BlockSpec((None, S, R), lambda b: (b, 0, 0)),
                  pl.BlockSpec((None, S, ROPE_DIM), lambda b: (b, 0, 0))],
        out_specs=pl.BlockSpec((H, T, R), lambda b: (0, b, 0)),
        out_shape=jax.ShapeDtypeStruct((H, M, R), BF16),
        compiler_params=_params(("parallel",)),
        name="attn_latent",
    )(q_lat, q, qpos, ckv, kpe)
    return pl.pallas_call(
        _o_expand_kernel,
        grid=(H,),
        in_specs=[pl.BlockSpec((None, M, R), lambda h: (h, 0, 0)),
                  pl.BlockSpec((R, V_DIM), lambda h: (0, h))],
        out_specs=pl.BlockSpec((M, V_DIM), lambda h: (0, h)),
        out_shape=jax.ShapeDtypeStruct((M, H * V_DIM), BF16),
        compiler_params=_params(("parallel",)),
        name="o_expand",
    )(o_lat, wuv)


def _pad_cols(w, n):
    return jnp.pad(w, ((0, 0), (0, n - w.shape[1])))


def _pad_rows(w, n):
    return jnp.pad(w, ((0, n - w.shape[0]), (0, 0)))


def _rotate_half_cols(w):
    half = ROPE_DIM // 2
    return jnp.concatenate([-w[..., half:], w[..., :half]], axis=-1)


def _rope_table(pos):
    half = ROPE_DIM // 2
    inv = ROPE_THETA ** (-jnp.arange(half, dtype=F32) / half)
    ang = pos.astype(F32)[:, None] * inv[None, :]
    cos, sin = jnp.cos(ang), jnp.sin(ang)
    return jnp.concatenate([cos, cos, sin, sin], axis=-1)


def _block_diag_states(s):
    B, H, N, _ = s.shape
    s = s.reshape(B, H // 2, 2, N, N)
    z = jnp.zeros_like(s[:, :, 0])
    top = jnp.concatenate([s[:, :, 0], z], axis=-1)
    bot = jnp.concatenate([z, s[:, :, 1]], axis=-1)
    return jnp.concatenate([top, bot], axis=-2)


def _diag_states(s):
    B, HP = s.shape[:2]
    N = RW_HEAD
    return jnp.stack([s[:, :, :N, :N], s[:, :, N:, N:]], axis=2).reshape(B, 2 * HP, N, N)


def _prepare(W):
    D = W['rw_wr'].shape[1]
    P = {}
    bf = lambda a: a.astype(BF16)
    for n in ('mlp_w1', 'mlp_w2', 'rw_wr', 'rw_wk', 'rw_wv', 'rw_wo', 'rw_g1', 'rw_g2', 'mla_wdq', 'mla_wo'):
        P[n] = [bf(W[n][l]) for l in range(W[n].shape[0])]
    NA = W['rw_wr'].shape[0]
    P['rw_w1'] = [bf(_pad_cols(W['rw_w1'][l], LORA_PAD)) for l in range(NA)]
    P['rw_w2'] = [bf(_pad_rows(W['rw_w2'][l], LORA_PAD)) for l in range(NA)]
    P['rw_a1'] = [bf(_pad_cols(W['rw_a1'][l], LORA_PAD)) for l in range(NA)]
    P['rw_a2'] = [bf(_pad_rows(W['rw_a2'][l], LORA_PAD)) for l in range(NA)]
    P['rw_v1'] = [bf(_pad_cols(W['rw_v1'][l], LORA_PAD)) for l in range(NA - 1)]
    P['rw_v2'] = [bf(_pad_rows(W['rw_v2'][l], LORA_PAD)) for l in range(NA - 1)]
    zeros = jnp.zeros((D,), F32)
    P['proj_vecs'], P['recur_vecs'] = [], []
    for l in range(NA):
        v0 = W['rw_v0'][l - 1] if l > 0 else zeros
        P['proj_vecs'].append(jnp.stack([W['rw_w0'][l], W['rw_a0'][l], v0, W['rw_kk'][l], W['rw_ka'][l],
                                         zeros, zeros, zeros]))
        P['recur_vecs'].append(jnp.stack([W['rw_rk'][l], W['rw_lnx_g'][l], W['rw_lnx_b'][l]] + [zeros] * 5))
    HL = min(512, D)
    head = jnp.arange(HL) // RW_HEAD
    P['ones_bd'] = (head[:, None] == head[None, :]).astype(BF16)
    R = W['kv_lat_g'].shape[0]
    wd = W['kv_wd']
    P['kv_wd'] = bf(jnp.concatenate([wd, _rotate_half_cols(wd[:, R:])], axis=1))
    H = W['kv_wuk'].shape[1]
    P['kv_wuk'] = bf(W['kv_wuk'].reshape(R, H * NOPE_DIM))
    P['kv_wuv'] = bf(W['kv_wuv'].reshape(R, H * V_DIM))
    NB, Q = W['mla_wuq'].shape[:2]
    wuq = W['mla_wuq'].reshape(NB, Q, H, NOPE_DIM + ROPE_DIM)
    pe = wuq[..., NOPE_DIM:]
    P['mla_wuq'] = bf(jnp.concatenate([wuq, _rotate_half_cols(pe)], axis=-1).reshape(NB, Q, -1))
    return P


def _trunk(x, mod, kv_mod, pos0, h_prev, s0, past_ckv, past_kpe, W, P):
    B, T, D = x.shape
    M = B * T
    depth = W['ada_w'].shape[0]
    NA = W['rw_wr'].shape[0]
    H = W['kv_wuk'].shape[1]
    tab = _rope_table(pos0 + jnp.arange(T))
    xf = x.reshape(M, D)
    shifts, states = [], []
    v_first = None
    keys = vals = ckv = kpe = None
    for l in range(depth):
        m = mod[l]
        if l < NA:
            has_v = l > 0
            pre = rwkv_pre(xf.reshape(B, T, D), W['norm_mix_g'][l], m[:, 0], m[:, 1], h_prev[l], W['rw_mu'][l],
                           P['rw_w1'][l], P['rw_a1'][l], P['rw_g1'][l], P['rw_v1'][l - 1] if has_v else None)
            shifts.append(pre[-1].reshape(B, D))
            acts = [a.reshape(M, a.shape[-1]) for a in pre[:-1]]
            ops = rwkv_proj(acts, v_first, P['rw_wr'][l], P['rw_wk'][l], P['rw_wv'][l], P['rw_w2'][l],
                            P['rw_a2'][l], P['rw_g2'][l], P['rw_v2'][l - 1] if has_v else None,
                            P['proj_vecs'][l], P['ones_bd'])
            if l == 0:
                v_first = ops[3]
            z, s_fin = rwkv_recur(ops, _block_diag_states(s0[l]), P['recur_vecs'][l], B, T)
            states.append(_diag_states(s_fin))
            xf = matmul_res(z, P['rw_wo'][l], xf, m[:, 2], B, T)
        else:
            j = l - NA
            cq = wdq_norm(xf.reshape(B, T, D), W['norm_mix_g'][l], m[:, 0], m[:, 1], P['mla_wdq'][j], W['mla_q_g'][j])
            if past_ckv is None:
                q = wuq_rope(cq, P['mla_wuq'][j], tab, H, False)
                o = flash_prompt(q, keys, vals).reshape(M, H * V_DIM)
            else:
                q = wuq_rope(cq, P['mla_wuq'][j], tab, H, True)
                o = attn_latent(q, keys, vals, P['kv_wuk'], P['kv_wuv'], B, T, pos0)
            xf = matmul_res(o, P['mla_wo'][j], xf, m[:, 2], B, T)
        xf = mlp_res(xf, W['norm_mlp_g'][l], m[:, 3], m[:, 4], m[:, 5], P['mlp_w1'][l], P['mlp_w2'][l], B, T)
        if l == NA - 1:
            ckv, kpe = latent(xf.reshape(B, T, D), W['kv_norm_g'], kv_mod[:, 0], kv_mod[:, 1],
                              P['kv_wd'], W['kv_lat_g'], tab)
            if past_ckv is None:
                keys, vals = kv_expand(ckv, kpe, P['kv_wuk'], P['kv_wuv'], H)
            else:
                keys = jnp.concatenate([past_ckv, ckv], axis=1)
                vals = jnp.concatenate([past_kpe, kpe], axis=1)
    y = final_norm(xf.reshape(B, T, D), W['final_g'])
    return y, ckv, kpe, jnp.stack(states), jnp.stack(shifts)


def kernel(x_prompt, x_sample, cache_ckv, cache_kpe, state_wkv, state_shift, c_prompt, c_sample, ada_w, ada_b, norm_mix_g, norm_mlp_g, mlp_w1, mlp_w2, rw_mu, rw_w0, rw_w1, rw_w2, rw_a0, rw_a1, rw_a2, rw_v0, rw_v1, rw_v2, rw_g1, rw_g2, rw_wr, rw_wk, rw_wv, rw_wo, rw_kk, rw_ka, rw_rk, rw_lnx_g, rw_lnx_b, kv_ada_w, kv_ada_b, kv_norm_g, kv_wd, kv_lat_g, kv_wuk, kv_wuv, mla_wdq, mla_q_g, mla_wuq, mla_wo, final_g):
    W = dict(ada_w=ada_w, ada_b=ada_b, norm_mix_g=norm_mix_g, norm_mlp_g=norm_mlp_g,
             mlp_w1=mlp_w1, mlp_w2=mlp_w2, rw_mu=rw_mu, rw_w0=rw_w0, rw_w1=rw_w1, rw_w2=rw_w2,
             rw_a0=rw_a0, rw_a1=rw_a1, rw_a2=rw_a2, rw_v0=rw_v0, rw_v1=rw_v1, rw_v2=rw_v2,
             rw_g1=rw_g1, rw_g2=rw_g2, rw_wr=rw_wr, rw_wk=rw_wk, rw_wv=rw_wv, rw_wo=rw_wo,
             rw_kk=rw_kk, rw_ka=rw_ka, rw_rk=rw_rk, rw_lnx_g=rw_lnx_g, rw_lnx_b=rw_lnx_b,
             kv_ada_w=kv_ada_w, kv_ada_b=kv_ada_b, kv_norm_g=kv_norm_g, kv_wd=kv_wd,
             kv_lat_g=kv_lat_g, kv_wuk=kv_wuk, kv_wuv=kv_wuv, mla_wdq=mla_wdq, mla_q_g=mla_q_g,
             mla_wuq=mla_wuq, mla_wo=mla_wo, final_g=final_g)
    P = _prepare(W)
    Bp, Tp, D = x_prompt.shape
    Bs = x_sample.shape[0]
    depth = ada_w.shape[0]
    NA = rw_wr.shape[0]
    c_all = jnp.concatenate([c_prompt, c_sample], axis=0)
    mod = ada_linear(c_all, ada_w, ada_b).reshape(depth, Bp + Bs, N_MOD, D)
    kv_mod = ada_linear(c_all, kv_ada_w[None], kv_ada_b[None]).reshape(Bp + Bs, 2, D)
    h0 = jnp.zeros((NA, Bp, D), F32)
    s0 = jnp.zeros((NA, Bp, D // RW_HEAD, RW_HEAD, RW_HEAD), F32)
    out_p = _trunk(x_prompt, mod[:, :Bp], kv_mod[:Bp], 0, h0, s0, None, None, W, P)
    out_s = _trunk(x_sample, mod[:, Bp:], kv_mod[Bp:], cache_ckv.shape[1], state_shift, state_wkv,
                   cache_ckv, cache_kpe, W, P)
    return (out_p[0], out_s[0]) + out_p[1:] + out_s[1:]
```

```python
import functools

import jax
import jax.numpy as jnp
from jax import lax
from jax.experimental import pallas as pl
from jax.experimental.pallas import tpu as pltpu

F32, BF16 = jnp.float32, jnp.bfloat16

RW_HEAD = 64
CHUNK = 64
GN_EPS = 64e-5
NOPE_DIM = 128
ROPE_DIM = 64
V_DIM = 128
ROPE_THETA = 10000.0
MLA_SCALE = (NOPE_DIM + ROPE_DIM) ** -0.5
LOG2E = 1.4426950408889634
V_ROWS = V_DIM + 16
NORM_EPS = 1e-6
N_MOD = 6

LANES = 128
VMEM_LIMIT = 56 * 1024 * 1024
LORA_PAD = 128


def _params(sem, vmem=VMEM_LIMIT):
    return pltpu.CompilerParams(dimension_semantics=sem, vmem_limit_bytes=vmem)


def _dot(a, b):
    return jnp.dot(a, b, preferred_element_type=F32)


def _dot_nt(a, b):
    return lax.dot_general(a, b, (((1,), (1,)), ((), ())), preferred_element_type=F32)


def _dot_tn(a, b):
    return lax.dot_general(a, b, (((0,), (0,)), ((), ())), preferred_element_type=F32)


def _split2(x):
    hi = x.astype(BF16)
    return hi, (x - hi.astype(F32)).astype(BF16)


def _split3(x):
    hi = x.astype(BF16)
    r1 = x - hi.astype(F32)
    mid = r1.astype(BF16)
    return hi, mid, (r1 - mid.astype(F32)).astype(BF16)


def _sigmoid(x):
    return 1.0 / (1.0 + jnp.exp(-x))


def _blk(i, n):
    assert n & (n - 1) == 0
    return i >> (n.bit_length() - 1)


def _off(i, n):
    assert n & (n - 1) == 0
    return i & (n - 1)


def _lower_left(ri, cj, s):
    return (_blk(ri, 2 * s) == _blk(cj, 2 * s)) & (_off(ri, 2 * s) >= s) & (_off(cj, 2 * s) < s)


def _rms(x, g):
    return x * lax.rsqrt(jnp.mean(x * x, axis=-1, keepdims=True) + NORM_EPS) * g


def _ada_kernel(c_ref, w_ref, b_ref, o_ref):
    c = c_ref[...]
    cs = c * _sigmoid(c)
    o_ref[...] = _dot(cs.astype(BF16), w_ref[...].astype(BF16)) + b_ref[...]


def ada_linear(c, w, b):
    L, K, N = w.shape
    M = c.shape[0]
    bn = min(512, N)
    return pl.pallas_call(
        _ada_kernel,
        grid=(L, N // bn),
        in_specs=[pl.BlockSpec((M, K), lambda l, j: (0, 0)),
                  pl.BlockSpec((None, K, bn), lambda l, j: (l, 0, j)),
                  pl.BlockSpec((None, 1, bn), lambda l, j: (l, 0, j))],
        out_specs=pl.BlockSpec((None, M, bn), lambda l, j: (l, 0, j)),
        out_shape=jax.ShapeDtypeStruct((L, M, N), F32),
        compiler_params=_params(("parallel", "parallel")),
        name="ada_linear",
    )(c, w, b.reshape(L, 1, N))


def _rwkv_pre_kernel(*refs, has_v):
    if has_v:
        (x_ref, g_ref, sh_ref, sc_ref, hp_ref, mu_ref, w1_ref, a1_ref, g1_ref, v1_ref,
         xr_ref, xk_ref, xv_ref, tw_ref, av_ref, gg_ref, vv_ref, hl_ref, prev) = refs
    else:
        (x_ref, g_ref, sh_ref, sc_ref, hp_ref, mu_ref, w1_ref, a1_ref, g1_ref,
         xr_ref, xk_ref, xv_ref, tw_ref, av_ref, gg_ref, hl_ref, prev) = refs
    bt = x_ref.shape[0]
    h = _rms(x_ref[...], g_ref[...]) * (1.0 + sc_ref[...]) + sh_ref[...]

    @pl.when(pl.program_id(1) == 0)
    def _():
        prev[...] = hp_ref[...]

    row = lax.broadcasted_iota(jnp.int32, h.shape, 0)
    xx = jnp.where(row == 0, prev[...], pltpu.roll(h, 1, 0)) - h
    last = h[bt - 1:bt, :]
    prev[...] = last
    hl_ref[...] = last
    mu = mu_ref[...]

    def mix(i):
        return (h + xx * mu[i:i + 1, :]).astype(BF16)

    xr_ref[...] = mix(0)
    tw_ref[...] = jnp.tanh(_dot(mix(1), w1_ref[...])).astype(BF16)
    xk_ref[...] = mix(2)
    xv = mix(3)
    xv_ref[...] = xv
    if has_v:
        vv_ref[...] = _dot(xv, v1_ref[...]).astype(BF16)
    av_ref[...] = _dot(mix(4), a1_ref[...]).astype(BF16)
    gg_ref[...] = _sigmoid(_dot(mix(5), g1_ref[...])).astype(BF16)


def rwkv_pre(x, g, shift, scale, h_prev, mu, w1, a1, g1, v1):
    B, T, D = x.shape
    bt = min(256, T)
    has_v = v1 is not None
    row = lambda n: pl.BlockSpec((None, bt, n), lambda b, t: (b, t, 0))
    vec = pl.BlockSpec((None, 1, D), lambda b, t: (b, 0, 0))
    full = lambda a: pl.BlockSpec(a.shape, lambda b, t: (0, 0))
    lora = [w1, a1, g1] + ([v1] if has_v else [])
    outs = [(D, BF16)] * 3 + [(w1.shape[1], BF16), (a1.shape[1], BF16), (g1.shape[1], BF16)]
    if has_v:
        outs.append((v1.shape[1], BF16))
    res = pl.pallas_call(
        functools.partial(_rwkv_pre_kernel, has_v=has_v),
        grid=(B, T // bt),
        in_specs=[row(D), pl.BlockSpec((1, D), lambda b, t: (0, 0)), vec, vec, vec, full(mu)]
                 + [full(a) for a in lora],
        out_specs=[row(n) for n, _ in outs] + [vec],
        out_shape=[jax.ShapeDtypeStruct((B, T, n), dt) for n, dt in outs]
                  + [jax.ShapeDtypeStruct((B, 1, D), F32)],
        scratch_shapes=[pltpu.VMEM((1, D), F32)],
        compiler_params=_params(("parallel", "arbitrary")),
        name="rwkv_pre",
    )(x, g.reshape(1, D), shift.reshape(B, 1, D), scale.reshape(B, 1, D), h_prev.reshape(B, 1, D), mu, *lora)
    return res


def _rwkv_proj_kernel(*refs, has_v):
    if has_v:
        (xr_ref, xk_ref, xv_ref, tw_ref, av_ref, gg_ref, vv_ref, vf_ref,
         wr_ref, wk_ref, wv_ref, w2_ref, a2_ref, g2_ref, v2_ref, vec_ref, ones_ref,
         r_ref, ld_ref, k_ref, v_ref, kk_ref, b_ref, g_ref) = refs
    else:
        (xr_ref, xk_ref, xv_ref, tw_ref, av_ref, gg_ref,
         wr_ref, wk_ref, wv_ref, w2_ref, a2_ref, g2_ref, vec_ref, ones_ref,
         r_ref, ld_ref, k_ref, v_ref, kk_ref, b_ref, g_ref) = refs
    vec = vec_ref[...]
    w0, a0, v0, k_k, k_a = (vec[i:i + 1, :] for i in range(5))
    r_ref[...] = _dot(xr_ref[...], wr_ref[...])
    kraw = _dot(xk_ref[...], wk_ref[...])
    v = _dot(xv_ref[...], wv_ref[...])
    nz = -(w0 + _dot(tw_ref[...], w2_ref[...]))
    softplus = jnp.maximum(nz, 0.0) + jnp.log(1.0 + jnp.exp(-jnp.abs(nz)))
    ld_ref[...] = -jnp.exp(-softplus - 0.5)
    a = _sigmoid(a0 + _dot(av_ref[...], a2_ref[...]))
    if has_v:
        v = v + (vf_ref[...] - v) * _sigmoid(v0 + _dot(vv_ref[...], v2_ref[...]))
    v_ref[...] = v
    g_ref[...] = _dot(gg_ref[...], g2_ref[...])
    kk = kraw * k_k
    ss = _dot((kk * kk).astype(BF16), ones_ref[...])
    kk = kk / jnp.maximum(jnp.sqrt(ss), 1e-12)
    kk_ref[...] = kk
    b_ref[...] = kk * a
    k_ref[...] = kraw * (1.0 + (a - 1.0) * k_a)


def rwkv_proj(pre, v_first, wr, wk, wv, w2, a2, g2, v2, vecs, ones_bd):
    has_v = v2 is not None
    xr = pre[0]
    M, D = xr.shape
    HL = ones_bd.shape[0]
    bm = min(512, M)
    row = lambda a: pl.BlockSpec((bm, a.shape[1]), lambda i, j: (i, 0))
    tile = pl.BlockSpec((bm, HL), lambda i, j: (i, j))
    col = lambda a: pl.BlockSpec((a.shape[0], HL), lambda i, j: (0, j))
    acts = list(pre[:6]) + ([pre[6]] if has_v else [])
    weights = [wr, wk, wv, w2, a2, g2] + ([v2] if has_v else [])
    return pl.pallas_call(
        functools.partial(_rwkv_proj_kernel, has_v=has_v),
        grid=(M // bm, D // HL),
        in_specs=[row(a) for a in acts] + ([tile] if has_v else []) + [col(w) for w in weights]
                 + [col(vecs), pl.BlockSpec((HL, HL), lambda i, j: (0, 0))],
        out_specs=[tile] * 7,
        out_shape=[jax.ShapeDtypeStruct((M, D), F32)] * 7,
        compiler_params=_params(("parallel", "arbitrary")),
        name="rwkv_proj",
    )(*acts, *([v_first] if has_v else []), *weights, vecs, ones_bd)


def _recur_kernel(r_ref, ld_ref, k_ref, v_ref, kk_ref, b_ref, g_ref, s0_ref, vec_ref,
                  z_ref, sout_ref, s_scr, *, C, NC, HP):
    ci = pl.program_id(2)
    C2 = 2 * C
    RB = NC * C

    @pl.when(ci == 0)
    def _():
        s_scr[...] = s0_ref[...]

    ti = lax.broadcasted_iota(jnp.int32, (RB, RB), 0)
    tj = lax.broadcasted_iota(jnp.int32, (RB, RB), 1)
    tri = jnp.where((_blk(ti, C) == _blk(tj, C)) & (ti >= tj), 1.0, 0.0).astype(BF16)
    ld = ld_ref[...]
    cl = sum(_dot(tri, part) for part in _split3(ld))
    p_in = jnp.exp(cl)
    p_inv = jnp.exp(-cl)
    r, k, v, b = r_ref[...], k_ref[...], v_ref[...], b_ref[...]
    a_t = -(jnp.exp(cl - ld) * kk_ref[...])
    r_t = p_in * r
    b_t = p_inv * b
    k_t = p_inv * k
    rows = [slice(c * C, (c + 1) * C) for c in range(NC)]
    cl_end = [cl[(c + 1) * C - 1:(c + 1) * C, :] for c in range(NC)]
    p_rem = [jnp.exp(cl_end[c] - cl[rows[c], :]) for c in range(NC)]
    p_end = [jnp.exp(cl_end[c]) for c in range(NC)]
    b_h = [p_rem[c] * b[rows[c], :] for c in range(NC)]
    k_h = [p_rem[c] * k[rows[c], :] for c in range(NC)]
    vec = vec_ref[...]
    r_k, lnx_g, lnx_b = (vec[i:i + 1, :] for i in range(3))
    rk = r * k * r_k
    g = g_ref[...]

    lane = lax.broadcasted_iota(jnp.int32, (1, LANES), 1)
    first = lane < RW_HEAD

    def stack(x):
        return jnp.concatenate([jnp.where(first, x, 0.0), jnp.where(first, 0.0, x)], axis=0).astype(BF16)

    def fold(x):
        return x[0:C, :] + x[C:C2, :]

    ri = lax.broadcasted_iota(jnp.int32, (C2, C2), 0)
    cj = lax.broadcasted_iota(jnp.int32, (C2, C2), 1)
    same = _blk(ri, C) == _blk(cj, C)
    strict = same & (_off(ri, C) > _off(cj, C))
    incl = same & (_off(ri, C) >= _off(cj, C))
    eye = jnp.where(ri == cj, 1.0, 0.0)
    same_head = (_blk(lax.broadcasted_iota(jnp.int32, (LANES, LANES), 0), RW_HEAD)
                 == _blk(lax.broadcasted_iota(jnp.int32, (LANES, LANES), 1), RW_HEAD))
    ones_bd = jnp.where(same_head, 1.0, 0.0).astype(BF16)

    lanes = [slice(p * LANES, (p + 1) * LANES) for p in range(HP)]
    probs = [(c, p) for c in range(NC) for p in range(HP)]
    a_b = {cp: a_t[rows[cp[0]], lanes[cp[1]]].astype(BF16) for cp in probs}
    r_b = {cp: r_t[rows[cp[0]], lanes[cp[1]]].astype(BF16) for cp in probs}
    v_st = {cp: stack(v[rows[cp[0]], lanes[cp[1]]]) for cp in probs}
    low, rb_f, ak_f, rk_f = {}, {}, {}, {}
    if C2 == LANES:
        t_f64 = lax.broadcasted_iota(jnp.int32, (C, LANES), 0)
        s_f64 = _off(lax.broadcasted_iota(jnp.int32, (C, LANES), 1), C)
        strict_f, incl_f = t_f64 > s_f64, t_f64 >= s_f64
        for c, p in probs:
            q_st = jnp.concatenate([stack(a_t[rows[c], lanes[p]]), stack(r_t[rows[c], lanes[p]])], axis=0)
            w_st = jnp.concatenate([b_t[rows[c], lanes[p]], k_t[rows[c], lanes[p]]], axis=0).astype(BF16)
            gm = _dot_nt(q_st, w_st)
            gr = pltpu.roll(gm, C, 1)
            a0, a1, r0, r1 = (gm[i * C:(i + 1) * C, :] for i in range(4))
            a0r, a1r, r0r, r1r = (gr[i * C:(i + 1) * C, :] for i in range(4))
            low[c, p] = jnp.concatenate([jnp.where(first & strict_f, a0, 0.0),
                                         jnp.where(strict_f & ~first, a1r, 0.0)], axis=0)
            ak_f[c, p] = jnp.where(strict_f, jnp.where(first, a0r, a1), 0.0).astype(BF16)
            rb_f[c, p] = jnp.where(incl_f, jnp.where(first, r0, r1r), 0.0).astype(BF16)
            rk_f[c, p] = jnp.where(incl_f, jnp.where(first, r0r, r1), 0.0).astype(BF16)
    else:
        for c, p in probs:
            q_st = jnp.concatenate([stack(a_t[rows[c], lanes[p]]), stack(r_t[rows[c], lanes[p]])], axis=0)
            b_c = b_t[rows[c], lanes[p]].astype(BF16)
            k_c = k_t[rows[c], lanes[p]].astype(BF16)
            gb = _dot_nt(q_st, jnp.concatenate([b_c, b_c], axis=0))
            gk = _dot_nt(q_st, jnp.concatenate([k_c, k_c], axis=0))
            low[c, p] = jnp.where(strict, gb[0:C2, :], 0.0)
            rb_f[c, p] = fold(jnp.where(incl, gb[C2:, :], 0.0)).astype(BF16)
            ak_f[c, p] = fold(jnp.where(strict, gk[0:C2, :], 0.0)).astype(BF16)
            rk_f[c, p] = fold(jnp.where(incl, gk[C2:, :], 0.0)).astype(BF16)
    akv = {cp: _dot(ak_f[cp], v_st[cp]) for cp in probs}
    rkv = {cp: _dot(rk_f[cp], v_st[cp]) for cp in probs}

    first_level = _lower_left(ri, cj, 1)
    t_inv = {cp: eye + jnp.where(first_level, low[cp], 0.0) for cp in probs}
    s = 2
    while s < C:
        sel = _lower_left(ri, cj, s)
        tb = {cp: t_inv[cp].astype(BF16) for cp in probs}
        mid = {cp: _dot(tb[cp], jnp.where(sel, low[cp], 0.0).astype(BF16)).astype(BF16) for cp in probs}
        t_inv = {cp: t_inv[cp] + _dot(mid[cp], tb[cp]) for cp in probs}
        s *= 2
    t_f = {cp: fold(t_inv[cp]).astype(BF16) for cp in probs}

    state = [s_scr[p] for p in range(HP)]
    y = {}
    for c in range(NC):
        state_b = [st.astype(BF16) for st in state]
        x = [_dot_nt(a_b[c, p], state_b[p]) + akv[c, p] for p in range(HP)]
        y0 = [_dot_nt(r_b[c, p], state_b[p]) + rkv[c, p] for p in range(HP)]
        u = [_dot(t_f[c, p], stack(x[p])) for p in range(HP)]
        for p in range(HP):
            y[c, p] = y0[p] + _dot(rb_f[c, p], stack(u[p]))
        upd = []
        for p in range(HP):
            uv = jnp.concatenate([u[p], v[rows[c], lanes[p]]], axis=0).astype(BF16)
            bk = jnp.concatenate([b_h[c][:, lanes[p]], k_h[c][:, lanes[p]]], axis=0).astype(BF16)
            upd.append(_dot_tn(uv, bk))
        state = [state[p] * p_end[c][:, lanes[p]] + jnp.where(same_head, upd[p], 0.0) for p in range(HP)]
    for p in range(HP):
        s_scr[p] = state[p]

    def seg(t, slices):
        parts = _split2(t) if slices == 2 else (t.astype(BF16),)
        return sum(_dot(part, ones_bd) for part in parts)

    y_all = [jnp.concatenate([y[c, p] for c in range(NC)], axis=0) for p in range(HP)]
    mean = [seg(y_all[p], 2) * (1.0 / RW_HEAD) for p in range(HP)]
    bonus = [seg(rk[:, lanes[p]], 1) * v[:, lanes[p]] for p in range(HP)]
    d = [y_all[p] - mean[p] for p in range(HP)]
    var = [seg(d[p] * d[p], 1) * (1.0 / RW_HEAD) for p in range(HP)]
    for p in range(HP):
        yn = d[p] * lax.rsqrt(var[p] + GN_EPS) * lnx_g[:, lanes[p]] + lnx_b[:, lanes[p]]
        z_ref[:, lanes[p]] = ((yn + bonus[p]) * g[:, lanes[p]]).astype(z_ref.dtype)

    @pl.when(ci == pl.num_programs(2) - 1)
    def _():
        sout_ref[...] = s_scr[...]


def rwkv_recur(ops, s0_bd, vecs, B, T):
    D = ops[0].shape[1]
    HL = min(1024, D)
    HP = HL // LANES
    C = min(CHUNK, T)
    NC = 2 if T % (2 * C) == 0 else 1
    ops3 = [o.reshape(B, T, D) for o in ops]
    tile = pl.BlockSpec((None, NC * C, HL), lambda b, j, c: (b, c, j))
    st = pl.BlockSpec((None, HP, LANES, LANES), lambda b, j, c: (b, j, 0, 0))
    z, s_fin = pl.pallas_call(
        functools.partial(_recur_kernel, C=C, NC=NC, HP=HP),
        grid=(B, D // HL, T // (NC * C)),
        in_specs=[tile] * 7 + [st, pl.BlockSpec((vecs.shape[0], HL), lambda b, j, c: (0, j))],
        out_specs=[tile, st],
        out_shape=[jax.ShapeDtypeStruct((B, T, D), BF16), jax.ShapeDtypeStruct(s0_bd.shape, F32)],
        scratch_shapes=[pltpu.VMEM((HP, LANES, LANES), F32)],
        compiler_params=_params(("parallel", "parallel", "arbitrary")),
        name="rwkv_recur",
    )(*ops3, s0_bd, vecs)
    return z.reshape(B * T, D), s_fin


def _gate_operand(gate, B, T, bm, bn):
    N = gate.shape[1]
    col = (lambda j: j) if bn < N else (lambda j: 0)
    if T % bm == 0:
        return gate.reshape(B, 1, N), pl.BlockSpec((None, 1, bn), lambda i, j: ((i * bm) // T, 0, col(j)))
    rows = jnp.broadcast_to(gate[:, None, :], (B, T, N)).reshape(B * T, N)
    return rows, pl.BlockSpec((bm, bn), lambda i, j: (i, col(j)))


def _matmul_res_kernel(a_ref, w_ref, x_ref, gate_ref, o_ref):
    o_ref[...] = x_ref[...] + gate_ref[...] * _dot(a_ref[...], w_ref[...])


def matmul_res(a, w, x, gate, B, T):
    M, K = a.shape
    N = w.shape[1]
    bm, bn = min(1024, M), min(512, N)
    gate_arr, gate_spec = _gate_operand(gate, B, T, bm, bn)
    return pl.pallas_call(
        _matmul_res_kernel,
        grid=(M // bm, N // bn),
        in_specs=[pl.BlockSpec((bm, K), lambda i, j: (i, 0)),
                  pl.BlockSpec((K, bn), lambda i, j: (0, j)),
                  pl.BlockSpec((bm, bn), lambda i, j: (i, j)), gate_spec],
        out_specs=pl.BlockSpec((bm, bn), lambda i, j: (i, j)),
        out_shape=jax.ShapeDtypeStruct((M, N), F32),
        compiler_params=_params(("parallel", "arbitrary")),
        name="matmul_res",
    )(a, w, x, gate_arr)


def _mlp_kernel(x_ref, g_ref, sh_ref, sc_ref, gate_ref, w1_ref, w2_ref, fg_ref, o_ref, h_scr, acc, *, final):
    f = pl.program_id(1)
    bm = x_ref.shape[0]

    def rows_of(ref, rs):
        return ref[...] if ref.shape[0] == 1 else ref[rs, :]

    def mlp(h):
        return _dot(jnp.square(jnp.maximum(_dot(h, w1_ref[...]), 0.0)).astype(BF16), w2_ref[...])

    @pl.when(f == 0)
    def _():
        parts = 4 if bm % 64 == 0 else 1
        for c in range(parts):
            rs = slice(c * bm // parts, (c + 1) * bm // parts)
            h = (_rms(x_ref[rs, :], g_ref[...]) * (1.0 + rows_of(sc_ref, rs)) + rows_of(sh_ref, rs)).astype(BF16)
            h_scr[rs, :] = h
            acc[rs, :] = mlp(h)

    @pl.when(f > 0)
    def _():
        acc[...] += mlp(h_scr[...])

    @pl.when(f == pl.num_programs(1) - 1)
    def _():
        y = x_ref[...] + gate_ref[...] * acc[...]
        o_ref[...] = _rms(y, fg_ref[...]) if final else y


def mlp_res(x, g, shift, scale, gate, w1, w2, layer, final_g, final, B, T):
    M, D = x.shape
    F = w1.shape[2]
    bm = 512 if T % 512 == 0 else min(256, M)
    bf = min(1024, F)
    (sh_arr, vec_spec), (sc_arr, _), (gate_arr, _) = (_gate_operand(v, B, T, bm, D) for v in (shift, scale, gate))
    return pl.pallas_call(
        functools.partial(_mlp_kernel, final=final),
        grid=(M // bm, F // bf),
        in_specs=[pl.BlockSpec((bm, D), lambda i, f: (i, 0)),
                  pl.BlockSpec((1, D), lambda i, f: (0, 0)), vec_spec, vec_spec, vec_spec,
                  pl.BlockSpec((None, D, bf), lambda i, f: (layer, 0, f)),
                  pl.BlockSpec((None, bf, D), lambda i, f: (layer, f, 0)),
                  pl.BlockSpec((1, D), lambda i, f: (0, 0))],
        out_specs=pl.BlockSpec((bm, D), lambda i, f: (i, 0)),
        out_shape=jax.ShapeDtypeStruct((M, D), F32),
        scratch_shapes=[pltpu.VMEM((bm, D), BF16), pltpu.VMEM((bm, D), F32)],
        compiler_params=_params(("parallel", "arbitrary")),
        name="mlp_res",
    )(x, g.reshape(1, D), sh_arr, sc_arr, gate_arr, w1, w2, final_g.reshape(1, D))


def _rope_pair(t, tab):
    prod = t * tab
    return (prod + pltpu.roll(prod, ROPE_DIM, 1))[:, :ROPE_DIM]


def _latent_kernel(x_ref, gx_ref, sh_ref, sc_ref, w_ref, g_ref, tab_ref, ckv_ref, kpe_ref):
    h = (_rms(x_ref[...], gx_ref[...]) * (1.0 + sc_ref[...]) + sh_ref[...]).astype(BF16)
    acc = _dot(h, w_ref[...])
    R = g_ref.shape[1]
    ckv_ref[...] = _rms(acc[:, :R], g_ref[...])
    kpe_ref[...] = _rope_pair(acc[:, R:R + 2 * ROPE_DIM], tab_ref[...])


def latent(x, gx, shift, scale, w, g, tab):
    B, T, D = x.shape
    R = g.shape[0]
    bt = min(512, T)
    vec = pl.BlockSpec((None, 1, D), lambda b, t: (b, 0, 0))
    return pl.pallas_call(
        _latent_kernel,
        grid=(B, T // bt),
        in_specs=[pl.BlockSpec((None, bt, D), lambda b, t: (b, t, 0)),
                  pl.BlockSpec((1, D), lambda b, t: (0, 0)), vec, vec,
                  pl.BlockSpec(w.shape, lambda b, t: (0, 0)),
                  pl.BlockSpec((1, R), lambda b, t: (0, 0)),
                  pl.BlockSpec((bt, 2 * ROPE_DIM), lambda b, t: (t, 0))],
        out_specs=[pl.BlockSpec((None, bt, R), lambda b, t: (b, t, 0)),
                   pl.BlockSpec((None, bt, ROPE_DIM), lambda b, t: (b, t, 0))],
        out_shape=[jax.ShapeDtypeStruct((B, T, R), F32), jax.ShapeDtypeStruct((B, T, ROPE_DIM), F32)],
        compiler_params=_params(("parallel", "parallel")),
        name="latent",
    )(x, gx.reshape(1, D), shift.reshape(B, 1, D), scale.reshape(B, 1, D), w, g.reshape(1, R), tab)


def _kv_expand_kernel(c_ref, pe_ref, wuk_ref, wuv_ref, k_ref, vt_ref):
    c = c_ref[...].astype(BF16)
    kn = _dot(c, wuk_ref[...])
    vv = _dot(c, wuv_ref[...])
    pe = pe_ref[...].astype(BF16)
    for h in range(k_ref.shape[0]):
        k_ref[h, :, 0:NOPE_DIM] = kn[:, h * NOPE_DIM:(h + 1) * NOPE_DIM].astype(BF16)
        k_ref[h, :, NOPE_DIM:NOPE_DIM + ROPE_DIM] = pe
        vt_ref[h, 0:V_DIM, :] = vv[:, h * V_DIM:(h + 1) * V_DIM].T.astype(BF16)
        vt_ref[h, V_DIM:V_ROWS, :] = jnp.ones((V_ROWS - V_DIM, c.shape[0]), BF16)


def kv_expand(ckv, kpe, wuk, wuv, H):
    B, S, R = ckv.shape
    bt = min(512, S)
    hb = min(8, H)
    DK = NOPE_DIM + ROPE_DIM
    return pl.pallas_call(
        _kv_expand_kernel,
        grid=(H // hb, B, S // bt),
        in_specs=[pl.BlockSpec((None, bt, R), lambda j, b, t: (b, t, 0)),
                  pl.BlockSpec((None, bt, ROPE_DIM), lambda j, b, t: (b, t, 0)),
                  pl.BlockSpec((R, hb * NOPE_DIM), lambda j, b, t: (0, j)),
                  pl.BlockSpec((R, hb * V_DIM), lambda j, b, t: (0, j))],
        out_specs=[pl.BlockSpec((None, hb, bt, DK), lambda j, b, t: (b, j, t, 0)),
                   pl.BlockSpec((None, hb, V_ROWS, bt), lambda j, b, t: (b, j, 0, t))],
        out_shape=[jax.ShapeDtypeStruct((B, H, S, DK), BF16), jax.ShapeDtypeStruct((B, H, V_ROWS, S), BF16)],
        compiler_params=_params(("parallel", "parallel", "arbitrary")),
        name="kv_expand",
    )(ckv, kpe, wuk, wuv)


def _wdq_kernel(x_ref, gx_ref, sh_ref, sc_ref, w_ref, g_ref, o_ref):
    h = (_rms(x_ref[...], gx_ref[...]) * (1.0 + sc_ref[...]) + sh_ref[...]).astype(BF16)
    o_ref[...] = _rms(_dot(h, w_ref[...]), g_ref[...]).astype(BF16)


def wdq_norm(x, gx, shift, scale, w, g):
    B, T, D = x.shape
    R = w.shape[1]
    bt = min(1024, T)
    vec = pl.BlockSpec((None, 1, D), lambda b, t: (b, 0, 0))
    return pl.pallas_call(
        _wdq_kernel,
        grid=(B, T // bt),
        in_specs=[pl.BlockSpec((None, bt, D), lambda b, t: (b, t, 0)),
                  pl.BlockSpec((1, D), lambda b, t: (0, 0)), vec, vec,
                  pl.BlockSpec((D, R), lambda b, t: (0, 0)),
                  pl.BlockSpec((1, R), lambda b, t: (0, 0))],
        out_specs=pl.BlockSpec((None, bt, R), lambda b, t: (b, t, 0)),
        out_shape=jax.ShapeDtypeStruct((B, T, R), BF16),
        compiler_params=_params(("parallel", "parallel")),
        name="wdq_norm",
    )(x, gx.reshape(1, D), shift.reshape(B, 1, D), scale.reshape(B, 1, D), w, g.reshape(1, R))


def _wuq_kernel(c_ref, w_ref, tab_ref, q_ref, *, scale):
    acc = _dot(c_ref[...], w_ref[...])
    tab = tab_ref[...]
    W = NOPE_DIM + 2 * ROPE_DIM
    for h in range(q_ref.shape[0]):
        q_ref[h, :, 0:NOPE_DIM] = (acc[:, h * W:h * W + NOPE_DIM] * scale).astype(BF16)
        pe = _rope_pair(acc[:, h * W + NOPE_DIM:(h + 1) * W], tab)
        q_ref[h, :, NOPE_DIM:NOPE_DIM + ROPE_DIM] = (pe * scale).astype(BF16)


def wuq_rope(cq, w, tab, H, head_major):
    B, T, R = cq.shape
    bt = min(512, T)
    nt = T // bt
    hb = min(8, H)
    W = NOPE_DIM + 2 * ROPE_DIM
    DK = NOPE_DIM + ROPE_DIM
    if head_major:
        out_spec = pl.BlockSpec((hb, bt, DK), lambda j, b, t: (j, b * nt + t, 0))
        out_shape = (H, B * T, DK)
    else:
        out_spec = pl.BlockSpec((None, hb, bt, DK), lambda j, b, t: (b, j, t, 0))
        out_shape = (B, H, T, DK)
    return pl.pallas_call(
        functools.partial(_wuq_kernel, scale=MLA_SCALE if head_major else MLA_SCALE * LOG2E),
        grid=(H // hb, B, nt),
        in_specs=[pl.BlockSpec((None, bt, R), lambda j, b, t: (b, t, 0)),
                  pl.BlockSpec((R, hb * W), lambda j, b, t: (0, j)),
                  pl.BlockSpec((bt, 2 * ROPE_DIM), lambda j, b, t: (t, 0))],
        out_specs=out_spec,
        out_shape=jax.ShapeDtypeStruct(out_shape, BF16),
        compiler_params=_params(("parallel", "parallel", "parallel")),
        name="wuq_rope",
    )(cq, w, tab)


def _visible(qpos, kpos):
    return _blk(kpos, CHUNK) <= _blk(qpos, CHUNK)


def _flash_kernel(q_ref, k_ref, vt_ref, o_ref, s_a, s_b, m_scr, acc_scr, *, tile):
    hb = q_ref.shape[0]
    ha = hb // 2
    first, second = range(0, ha), range(ha, hb)
    qi = pl.program_id(2)
    qpos = qi * tile + lax.broadcasted_iota(jnp.int32, (1, tile), 1)
    m_scr[...] = jnp.full(m_scr.shape, -jnp.inf, F32)
    acc_scr[...] = jnp.zeros(acc_scr.shape, F32)

    def scores(ki, heads, s_ref):
        start = pl.multiple_of(ki * tile, tile)
        for h in heads:
            s_ref[h - heads[0]] = _dot_nt(k_ref[h, pl.ds(start, tile), :], q_ref[h])

    def softmax_pv(ki, heads, s_ref, masked):
        start = pl.multiple_of(ki * tile, tile)
        for h in heads:
            s = s_ref[h - heads[0]]
            if masked:
                kpos = ki * tile + lax.broadcasted_iota(jnp.int32, (tile, 1), 0)
                s = jnp.where(_visible(qpos, kpos), s, -jnp.inf)
            m = m_scr[h]
            m_new = jnp.maximum(m, jnp.max(s, axis=0, keepdims=True))
            p = jnp.exp2(s - m_new).astype(BF16)
            acc_scr[h] = jnp.exp2(m - m_new) * acc_scr[h] + _dot(vt_ref[h, :, pl.ds(start, tile)], p)
            m_scr[h] = m_new

    def body(ki, carry):
        scores(ki, second, s_b)
        softmax_pv(ki, first, s_a, False)
        scores(ki + 1, first, s_a)
        softmax_pv(ki, second, s_b, False)
        return carry

    scores(0, first, s_a)
    lax.fori_loop(0, qi, body, 0)
    scores(qi, second, s_b)
    softmax_pv(qi, first, s_a, True)
    softmax_pv(qi, second, s_b, True)
    for h in range(hb):
        acc = acc_scr[h]
        inv_l = 1.0 / acc[V_DIM:V_DIM + 1, :]
        o_ref[:, h * V_DIM:(h + 1) * V_DIM] = (acc[0:V_DIM, :] * inv_l).T.astype(o_ref.dtype)


def flash_prompt(q, k, vt):
    B, H, T, DK = q.shape
    tile = min(256, T)
    assert tile % CHUNK == 0 and T % tile == 0 and H % 2 == 0
    hb = next(n for n in (8, 4, 2) if H % n == 0)
    return pl.pallas_call(
        functools.partial(_flash_kernel, tile=tile),
        grid=(B, H // hb, T // tile),
        in_specs=[pl.BlockSpec((None, hb, tile, DK), lambda b, j, i: (b, j, i, 0)),
                  pl.BlockSpec((None, hb, T, DK), lambda b, j, i: (b, j, 0, 0)),
                  pl.BlockSpec((None, hb, V_ROWS, T), lambda b, j, i: (b, j, 0, 0))],
        out_specs=pl.BlockSpec((None, tile, hb * V_DIM), lambda b, j, i: (b, i, j)),
        out_shape=jax.ShapeDtypeStruct((B, T, H * V_DIM), BF16),
        scratch_shapes=[pltpu.VMEM((hb // 2, tile, tile), F32), pltpu.VMEM((hb // 2, tile, tile), F32),
                        pltpu.VMEM((hb, 1, tile), F32), pltpu.VMEM((hb, V_ROWS, tile), F32)],
        compiler_params=_params(("parallel", "parallel", "arbitrary")),
        name="flash_prompt",
    )(q, k, vt)


def _q_absorb_kernel(q_ref, wuk_ref, o_ref):
    o_ref[...] = _dot_nt(q_ref[:, 0:NOPE_DIM], wuk_ref[...]).astype(BF16)


def _attn_latent_kernel(ql_ref, q_ref, qpos_ref, c_ref, pe_ref, o_ref):
    H, T, R = ql_ref.shape
    S = c_ref.shape[0]
    c = c_ref[...].astype(BF16)
    s = (_dot_nt(ql_ref[...].reshape(H * T, R), c)
         + _dot_nt(q_ref[:, :, NOPE_DIM:NOPE_DIM + ROPE_DIM].reshape(H * T, ROPE_DIM), pe_ref[...].astype(BF16)))
    kpos = lax.broadcasted_iota(jnp.int32, (1, S), 1)
    s = jnp.where(_visible(qpos_ref[...], kpos), s, -jnp.inf)
    p = jnp.exp(s - jnp.max(s, axis=-1, keepdims=True))
    l = jnp.sum(p, axis=-1, keepdims=True)
    o_ref[...] = (_dot(p.astype(BF16), c) / l).astype(BF16).reshape(H, T, R)


def _o_expand_kernel(o_ref, wuv_ref, out_ref):
    out_ref[...] = _dot(o_ref[...], wuv_ref[...]).astype(BF16)


def attn_latent(q, ckv, kpe, wuk, wuv, B, T, q0):
    H, M, DK = q.shape
    S, R = ckv.shape[1:]
    q_lat = pl.pallas_call(
        _q_absorb_kernel,
        grid=(H,),
        in_specs=[pl.BlockSpec((None, M, DK), lambda h: (h, 0, 0)),
                  pl.BlockSpec((R, NOPE_DIM), lambda h: (0, h))],
        out_specs=pl.BlockSpec((None, M, R), lambda h: (h, 0, 0)),
        out_shape=jax.ShapeDtypeStruct((H, M, R), BF16),
        compiler_params=_params(("parallel",)),
        name="q_absorb",
    )(q, wuk)
    qpos = jnp.tile(q0 + jnp.arange(T, dtype=jnp.int32), H)[:, None]
    o_lat = pl.pallas_call(
        _attn_latent_kernel,
        grid=(B,),
        in_specs=[pl.BlockSpec((H, T, R), lambda b: (0, b, 0)),
                  pl.BlockSpec((H, T, DK), lambda b: (0, b, 0)),
                  pl.BlockSpec((H * T, 1), lambda b: (0, 0)),
                  pl.BlockSpec((None, S, R), lambda b: (b, 0, 0)),
                  pl.BlockSpec((None, S, ROPE_DIM), lambda b: (b, 0, 0))],
        out_specs=pl.BlockSpec((H, T, R), lambda b: (0, b, 0)),
        out_shape=jax.ShapeDtypeStruct((H, M, R), BF16),
        compiler_params=_params(("parallel",)),
        name="attn_latent",
    )(q_lat, q, qpos, ckv, kpe)
    return pl.pallas_call(
        _o_expand_kernel,
        grid=(H,),
        in_specs=[pl.BlockSpec((None, M, R), lambda h: (h, 0, 0)),
                  pl.BlockSpec((R, V_DIM), lambda h: (0, h))],
        out_specs=pl.BlockSpec((M, V_DIM), lambda h: (0, h)),
        out_shape=jax.ShapeDtypeStruct((M, H * V_DIM), BF16),
        compiler_params=_params(("parallel",)),
        name="o_expand",
    )(o_lat, wuv)


def _pad_cols(w, n):
    return jnp.pad(w, ((0, 0), (0, n - w.shape[1])))


def _pad_rows(w, n):
    return jnp.pad(w, ((0, n - w.shape[0]), (0, 0)))


def _rotate_half_cols(w):
    half = ROPE_DIM // 2
    return jnp.concatenate([-w[..., half:], w[..., :half]], axis=-1)


def _rope_table(pos):
    half = ROPE_DIM // 2
    inv = ROPE_THETA ** (-jnp.arange(half, dtype=F32) / half)
    ang = pos.astype(F32)[:, None] * inv[None, :]
    cos, sin = jnp.cos(ang), jnp.sin(ang)
    return jnp.concatenate([cos, cos, sin, sin], axis=-1)


def _block_diag_states(s):
    B, H, N, _ = s.shape
    s = s.reshape(B, H // 2, 2, N, N)
    z = jnp.zeros_like(s[:, :, 0])
    top = jnp.concatenate([s[:, :, 0], z], axis=-1)
    bot = jnp.concatenate([z, s[:, :, 1]], axis=-1)
    return jnp.concatenate([top, bot], axis=-2)


def _diag_states(s):
    B, HP = s.shape[:2]
    N = RW_HEAD
    return jnp.stack([s[:, :, :N, :N], s[:, :, N:, N:]], axis=2).reshape(B, 2 * HP, N, N)


def _prepare(W):
    D = W['rw_wr'].shape[1]
    P = {}
    bf = lambda a: a.astype(BF16)
    P['mlp_w1'], P['mlp_w2'] = bf(W['mlp_w1']), bf(W['mlp_w2'])
    for n in ('rw_wr', 'rw_wk', 'rw_wv', 'rw_wo', 'rw_g1', 'rw_g2', 'mla_wdq', 'mla_wo'):
        P[n] = [bf(W[n][l]) for l in range(W[n].shape[0])]
    NA = W['rw_wr'].shape[0]
    P['rw_w1'] = [bf(_pad_cols(W['rw_w1'][l], LORA_PAD)) for l in range(NA)]
    P['rw_w2'] = [bf(_pad_rows(W['rw_w2'][l], LORA_PAD)) for l in range(NA)]
    P['rw_a1'] = [bf(_pad_cols(W['rw_a1'][l], LORA_PAD)) for l in range(NA)]
    P['rw_a2'] = [bf(_pad_rows(W['rw_a2'][l], LORA_PAD)) for l in range(NA)]
    P['rw_v1'] = [bf(_pad_cols(W['rw_v1'][l], LORA_PAD)) for l in range(NA - 1)]
    P['rw_v2'] = [bf(_pad_rows(W['rw_v2'][l], LORA_PAD)) for l in range(NA - 1)]
    zeros = jnp.zeros((D,), F32)
    P['proj_vecs'], P['recur_vecs'] = [], []
    for l in range(NA):
        v0 = W['rw_v0'][l - 1] if l > 0 else zeros
        P['proj_vecs'].append(jnp.stack([W['rw_w0'][l], W['rw_a0'][l], v0, W['rw_kk'][l], W['rw_ka'][l],
                                         zeros, zeros, zeros]))
        P['recur_vecs'].append(jnp.stack([W['rw_rk'][l], W['rw_lnx_g'][l], W['rw_lnx_b'][l]] + [zeros] * 5))
    HL = min(512, D)
    head = jnp.arange(HL) // RW_HEAD
    P['ones_bd'] = (head[:, None] == head[None, :]).astype(BF16)
    R = W['kv_lat_g'].shape[0]
    wd = W['kv_wd']
    P['kv_wd'] = bf(jnp.concatenate([wd, _rotate_half_cols(wd[:, R:])], axis=1))
    H = W['kv_wuk'].shape[1]
    P['kv_wuk'] = bf(W['kv_wuk'].reshape(R, H * NOPE_DIM))
    P['kv_wuv'] = bf(W['kv_wuv'].reshape(R, H * V_DIM))
    NB, Q = W['mla_wuq'].shape[:2]
    wuq = W['mla_wuq'].reshape(NB, Q, H, NOPE_DIM + ROPE_DIM)
    pe = wuq[..., NOPE_DIM:]
    P['mla_wuq'] = bf(jnp.concatenate([wuq, _rotate_half_cols(pe)], axis=-1).reshape(NB, Q, -1))
    return P


def _trunk(x, mod, kv_mod, pos0, h_prev, s0, past_ckv, past_kpe, W, P):
    B, T, D = x.shape
    M = B * T
    depth = W['ada_w'].shape[0]
    NA = W['rw_wr'].shape[0]
    H = W['kv_wuk'].shape[1]
    tab = _rope_table(pos0 + jnp.arange(T))
    xf = x.reshape(M, D)
    shifts, states = [], []
    v_first = None
    keys = vals = ckv = kpe = None
    for l in range(depth):
        m = mod[l]
        if l < NA:
            has_v = l > 0
            pre = rwkv_pre(xf.reshape(B, T, D), W['norm_mix_g'][l], m[:, 0], m[:, 1], h_prev[l], W['rw_mu'][l],
                           P['rw_w1'][l], P['rw_a1'][l], P['rw_g1'][l], P['rw_v1'][l - 1] if has_v else None)
            shifts.append(pre[-1].reshape(B, D))
            acts = [a.reshape(M, a.shape[-1]) for a in pre[:-1]]
            ops = rwkv_proj(acts, v_first, P['rw_wr'][l], P['rw_wk'][l], P['rw_wv'][l], P['rw_w2'][l],
                            P['rw_a2'][l], P['rw_g2'][l], P['rw_v2'][l - 1] if has_v else None,
                            P['proj_vecs'][l], P['ones_bd'])
            if l == 0:
                v_first = ops[3]
            z, s_fin = rwkv_recur(ops, _block_diag_states(s0[l]), P['recur_vecs'][l], B, T)
            states.append(_diag_states(s_fin))
            xf = matmul_res(z, P['rw_wo'][l], xf, m[:, 2], B, T)
        else:
            j = l - NA
            cq = wdq_norm(xf.reshape(B, T, D), W['norm_mix_g'][l], m[:, 0], m[:, 1], P['mla_wdq'][j], W['mla_q_g'][j])
            if past_ckv is None:
                q = wuq_rope(cq, P['mla_wuq'][j], tab, H, False)
                o = flash_prompt(q, keys, vals).reshape(M, H * V_DIM)
            else:
                q = wuq_rope(cq, P['mla_wuq'][j], tab, H, True)
                o = attn_latent(q, keys, vals, P['kv_wuk'], P['kv_wuv'], B, T, pos0)
            xf = matmul_res(o, P['mla_wo'][j], xf, m[:, 2], B, T)
        xf = mlp_res(xf, W['norm_mlp_g'][l], m[:, 3], m[:, 4], m[:, 5], P['mlp_w1'], P['mlp_w2'], l,
                     W['final_g'], l == depth - 1, B, T)
        if l == NA - 1:
            ckv, kpe = latent(xf.reshape(B, T, D), W['kv_norm_g'], kv_mod[:, 0], kv_mod[:, 1],
                              P['kv_wd'], W['kv_lat_g'], tab)
            if past_ckv is None:
                keys, vals = kv_expand(ckv, kpe, P['kv_wuk'], P['kv_wuv'], H)
            else:
                keys = jnp.concatenate([past_ckv, ckv], axis=1)
                vals = jnp.concatenate([past_kpe, kpe], axis=1)
    return xf.reshape(B, T, D), ckv, kpe, jnp.stack(states), jnp.stack(shifts)


def kernel(x_prompt, x_sample, cache_ckv, cache_kpe, state_wkv, state_shift, c_prompt, c_sample, ada_w, ada_b, norm_mix_g, norm_mlp_g, mlp_w1, mlp_w2, rw_mu, rw_w0, rw_w1, rw_w2, rw_a0, rw_a1, rw_a2, rw_v0, rw_v1, rw_v2, rw_g1, rw_g2, rw_wr, rw_wk, rw_wv, rw_wo, rw_kk, rw_ka, rw_rk, rw_lnx_g, rw_lnx_b, kv_ada_w, kv_ada_b, kv_norm_g, kv_wd, kv_lat_g, kv_wuk, kv_wuv, mla_wdq, mla_q_g, mla_wuq, mla_wo, final_g):
    W = dict(ada_w=ada_w, ada_b=ada_b, norm_mix_g=norm_mix_g, norm_mlp_g=norm_mlp_g,
             mlp_w1=mlp_w1, mlp_w2=mlp_w2, rw_mu=rw_mu, rw_w0=rw_w0, rw_w1=rw_w1, rw_w2=rw_w2,
             rw_a0=rw_a0, rw_a1=rw_a1, rw_a2=rw_a2, rw_v0=rw_v0, rw_v1=rw_v1, rw_v2=rw_v2,
             rw_g1=rw_g1, rw_g2=rw_g2, rw_wr=rw_wr, rw_wk=rw_wk, rw_wv=rw_wv, rw_wo=rw_wo,
             rw_kk=rw_kk, rw_ka=rw_ka, rw_rk=rw_rk, rw_lnx_g=rw_lnx_g, rw_lnx_b=rw_lnx_b,
             kv_ada_w=kv_ada_w, kv_ada_b=kv_ada_b, kv_norm_g=kv_norm_g, kv_wd=kv_wd,
             kv_lat_g=kv_lat_g, kv_wuk=kv_wuk, kv_wuv=kv_wuv, mla_wdq=mla_wdq, mla_q_g=mla_q_g,
             mla_wuq=mla_wuq, mla_wo=mla_wo, final_g=final_g)
    P = _prepare(W)
    Bp, Tp, D = x_prompt.shape
    Bs = x_sample.shape[0]
    depth = ada_w.shape[0]
    NA = rw_wr.shape[0]
    c_all = jnp.concatenate([c_prompt, c_sample], axis=0)
    mod = ada_linear(c_all, ada_w, ada_b).reshape(depth, Bp + Bs, N_MOD, D)
    kv_mod = ada_linear(c_all, kv_ada_w[None], kv_ada_b[None]).reshape(Bp + Bs, 2, D)
    h0 = jnp.zeros((NA, Bp, D), F32)
    s0 = jnp.zeros((NA, Bp, D // RW_HEAD, RW_HEAD, RW_HEAD), F32)
    out_p = _trunk(x_prompt, mod[:, :Bp], kv_mod[:Bp], 0, h0, s0, None, None, W, P)
    out_s = _trunk(x_sample, mod[:, Bp:], kv_mod[Bp:], cache_ckv.shape[1], state_shift, state_wkv,
                   cache_ckv, cache_kpe, W, P)
    return (out_p[0], out_s[0]) + out_p[1:] + out_s[1:]
```

```python
import functools

import jax
import jax.numpy as jnp
from jax import lax
from jax.experimental import pallas as pl
from jax.experimental.pallas import tpu as pltpu

F32, BF16 = jnp.float32, jnp.bfloat16

RW_HEAD = 64
CHUNK = 64
GN_EPS = 64e-5
NOPE_DIM = 128
ROPE_DIM = 64
V_DIM = 128
ROPE_THETA = 10000.0
MLA_SCALE = (NOPE_DIM + ROPE_DIM) ** -0.5
LOG2E = 1.4426950408889634
V_ROWS = V_DIM + 16
NORM_EPS = 1e-6
N_MOD = 6

LANES = 128
VMEM_LIMIT = 56 * 1024 * 1024
LORA_PAD = 128


def _params(sem, vmem=VMEM_LIMIT):
    return pltpu.CompilerParams(dimension_semantics=sem, vmem_limit_bytes=vmem)


def _dot(a, b):
    return jnp.dot(a, b, preferred_element_type=F32)


def _dot_nt(a, b):
    return lax.dot_general(a, b, (((1,), (1,)), ((), ())), preferred_element_type=F32)


def _dot_tn(a, b):
    return lax.dot_general(a, b, (((0,), (0,)), ((), ())), preferred_element_type=F32)


def _split2(x):
    hi = x.astype(BF16)
    return hi, (x - hi.astype(F32)).astype(BF16)


def _split3(x):
    hi = x.astype(BF16)
    r1 = x - hi.astype(F32)
    mid = r1.astype(BF16)
    return hi, mid, (r1 - mid.astype(F32)).astype(BF16)


def _sigmoid(x):
    return 1.0 / (1.0 + jnp.exp(-x))


def _blk(i, n):
    assert n & (n - 1) == 0
    return i >> (n.bit_length() - 1)


def _off(i, n):
    assert n & (n - 1) == 0
    return i & (n - 1)


def _lower_left(ri, cj, s):
    return (_blk(ri, 2 * s) == _blk(cj, 2 * s)) & (_off(ri, 2 * s) >= s) & (_off(cj, 2 * s) < s)


def _rms(x, g):
    return x * lax.rsqrt(jnp.mean(x * x, axis=-1, keepdims=True) + NORM_EPS) * g


def _ada_kernel(c_ref, w_ref, b_ref, o_ref):
    c = c_ref[...]
    cs = c * _sigmoid(c)
    o_ref[...] = _dot(cs.astype(BF16), w_ref[...].astype(BF16)) + b_ref[...]


def ada_linear(c, w, b):
    L, K, N = w.shape
    M = c.shape[0]
    bn = min(512, N)
    return pl.pallas_call(
        _ada_kernel,
        grid=(L, N // bn),
        in_specs=[pl.BlockSpec((M, K), lambda l, j: (0, 0)),
                  pl.BlockSpec((None, K, bn), lambda l, j: (l, 0, j)),
                  pl.BlockSpec((None, 1, bn), lambda l, j: (l, 0, j))],
        out_specs=pl.BlockSpec((None, M, bn), lambda l, j: (l, 0, j)),
        out_shape=jax.ShapeDtypeStruct((L, M, N), F32),
        compiler_params=_params(("parallel", "parallel")),
        name="ada_linear",
    )(c, w, b.reshape(L, 1, N))


def _rwkv_pre_kernel(*refs, has_v):
    if has_v:
        (x_ref, g_ref, sh_ref, sc_ref, hp_ref, mu_ref, w1_ref, a1_ref, g1_ref, v1_ref,
         xr_ref, xk_ref, xv_ref, tw_ref, av_ref, gg_ref, vv_ref, hl_ref, prev) = refs
    else:
        (x_ref, g_ref, sh_ref, sc_ref, hp_ref, mu_ref, w1_ref, a1_ref, g1_ref,
         xr_ref, xk_ref, xv_ref, tw_ref, av_ref, gg_ref, hl_ref, prev) = refs
    bt = x_ref.shape[0]
    h = _rms(x_ref[...], g_ref[...]) * (1.0 + sc_ref[...]) + sh_ref[...]

    @pl.when(pl.program_id(1) == 0)
    def _():
        prev[...] = hp_ref[...]

    row = lax.broadcasted_iota(jnp.int32, h.shape, 0)
    xx = jnp.where(row == 0, prev[...], pltpu.roll(h, 1, 0)) - h
    last = h[bt - 1:bt, :]
    prev[...] = last
    hl_ref[...] = last
    mu = mu_ref[...]

    def mix(i):
        return (h + xx * mu[i:i + 1, :]).astype(BF16)

    xr_ref[...] = mix(0)
    tw_ref[...] = jnp.tanh(_dot(mix(1), w1_ref[...])).astype(BF16)
    xk_ref[...] = mix(2)
    xv = mix(3)
    xv_ref[...] = xv
    if has_v:
        vv_ref[...] = _dot(xv, v1_ref[...]).astype(BF16)
    av_ref[...] = _dot(mix(4), a1_ref[...]).astype(BF16)
    gg_ref[...] = _sigmoid(_dot(mix(5), g1_ref[...])).astype(BF16)


def rwkv_pre(x, g, shift, scale, h_prev, mu, w1, a1, g1, v1):
    B, T, D = x.shape
    bt = min(256, T)
    has_v = v1 is not None
    row = lambda n: pl.BlockSpec((None, bt, n), lambda b, t: (b, t, 0))
    vec = pl.BlockSpec((None, 1, D), lambda b, t: (b, 0, 0))
    full = lambda a: pl.BlockSpec(a.shape, lambda b, t: (0, 0))
    lora = [w1, a1, g1] + ([v1] if has_v else [])
    outs = [(D, BF16)] * 3 + [(w1.shape[1], BF16), (a1.shape[1], BF16), (g1.shape[1], BF16)]
    if has_v:
        outs.append((v1.shape[1], BF16))
    res = pl.pallas_call(
        functools.partial(_rwkv_pre_kernel, has_v=has_v),
        grid=(B, T // bt),
        in_specs=[row(D), pl.BlockSpec((1, D), lambda b, t: (0, 0)), vec, vec, vec, full(mu)]
                 + [full(a) for a in lora],
        out_specs=[row(n) for n, _ in outs] + [vec],
        out_shape=[jax.ShapeDtypeStruct((B, T, n), dt) for n, dt in outs]
                  + [jax.ShapeDtypeStruct((B, 1, D), F32)],
        scratch_shapes=[pltpu.VMEM((1, D), F32)],
        compiler_params=_params(("parallel", "arbitrary")),
        name="rwkv_pre",
    )(x, g.reshape(1, D), shift.reshape(B, 1, D), scale.reshape(B, 1, D), h_prev.reshape(B, 1, D), mu, *lora)
    return res


def _rwkv_proj_kernel(*refs, has_v):
    if has_v:
        (xr_ref, xk_ref, xv_ref, tw_ref, av_ref, gg_ref, vv_ref, vf_ref,
         wr_ref, wk_ref, wv_ref, w2_ref, a2_ref, g2_ref, v2_ref, vec_ref, ones_ref,
         r_ref, ld_ref, k_ref, v_ref, kk_ref, b_ref, g_ref) = refs
    else:
        (xr_ref, xk_ref, xv_ref, tw_ref, av_ref, gg_ref,
         wr_ref, wk_ref, wv_ref, w2_ref, a2_ref, g2_ref, vec_ref, ones_ref,
         r_ref, ld_ref, k_ref, v_ref, kk_ref, b_ref, g_ref) = refs
    vec = vec_ref[...]
    w0, a0, v0, k_k, k_a = (vec[i:i + 1, :] for i in range(5))
    r_ref[...] = _dot(xr_ref[...], wr_ref[...])
    kraw = _dot(xk_ref[...], wk_ref[...])
    v = _dot(xv_ref[...], wv_ref[...])
    nz = -(w0 + _dot(tw_ref[...], w2_ref[...]))
    softplus = jnp.maximum(nz, 0.0) + jnp.log(1.0 + jnp.exp(-jnp.abs(nz)))
    ld_ref[...] = -jnp.exp(-softplus - 0.5)
    a = _sigmoid(a0 + _dot(av_ref[...], a2_ref[...]))
    if has_v:
        v = v + (vf_ref[...] - v) * _sigmoid(v0 + _dot(vv_ref[...], v2_ref[...]))
    v_ref[...] = v
    g_ref[...] = _dot(gg_ref[...], g2_ref[...])
    kk = kraw * k_k
    ss = _dot((kk * kk).astype(BF16), ones_ref[...])
    kk = kk / jnp.maximum(jnp.sqrt(ss), 1e-12)
    kk_ref[...] = kk
    b_ref[...] = kk * a
    k_ref[...] = kraw * (1.0 + (a - 1.0) * k_a)


def rwkv_proj(pre, v_first, wr, wk, wv, w2, a2, g2, v2, vecs, ones_bd):
    has_v = v2 is not None
    xr = pre[0]
    M, D = xr.shape
    HL = ones_bd.shape[0]
    bm = min(512, M)
    row = lambda a: pl.BlockSpec((bm, a.shape[1]), lambda i, j: (i, 0))
    tile = pl.BlockSpec((bm, HL), lambda i, j: (i, j))
    col = lambda a: pl.BlockSpec((a.shape[0], HL), lambda i, j: (0, j))
    acts = list(pre[:6]) + ([pre[6]] if has_v else [])
    weights = [wr, wk, wv, w2, a2, g2] + ([v2] if has_v else [])
    return pl.pallas_call(
        functools.partial(_rwkv_proj_kernel, has_v=has_v),
        grid=(M // bm, D // HL),
        in_specs=[row(a) for a in acts] + ([tile] if has_v else []) + [col(w) for w in weights]
                 + [col(vecs), pl.BlockSpec((HL, HL), lambda i, j: (0, 0))],
        out_specs=[tile] * 7,
        out_shape=[jax.ShapeDtypeStruct((M, D), F32)] * 7,
        compiler_params=_params(("parallel", "arbitrary")),
        name="rwkv_proj",
    )(*acts, *([v_first] if has_v else []), *weights, vecs, ones_bd)


def _recur_kernel(r_ref, ld_ref, k_ref, v_ref, kk_ref, b_ref, g_ref, s0_ref, vec_ref,
                  z_ref, sout_ref, s_scr, *, C, NC, HP):
    ci = pl.program_id(2)
    C2 = 2 * C
    RB = NC * C

    @pl.when(ci == 0)
    def _():
        s_scr[...] = s0_ref[...]

    ti = lax.broadcasted_iota(jnp.int32, (RB, RB), 0)
    tj = lax.broadcasted_iota(jnp.int32, (RB, RB), 1)
    tri = jnp.where((_blk(ti, C) == _blk(tj, C)) & (ti >= tj), 1.0, 0.0).astype(BF16)
    ld = ld_ref[...]
    cl = sum(_dot(tri, part) for part in _split3(ld))
    p_in = jnp.exp(cl)
    p_inv = jnp.exp(-cl)
    r, k, v, b = r_ref[...], k_ref[...], v_ref[...], b_ref[...]
    a_t = -(jnp.exp(cl - ld) * kk_ref[...])
    r_t = p_in * r
    b_t = p_inv * b
    k_t = p_inv * k
    rows = [slice(c * C, (c + 1) * C) for c in range(NC)]
    cl_end = [cl[(c + 1) * C - 1:(c + 1) * C, :] for c in range(NC)]
    p_rem = [jnp.exp(cl_end[c] - cl[rows[c], :]) for c in range(NC)]
    p_end = [jnp.exp(cl_end[c]) for c in range(NC)]
    b_h = [p_rem[c] * b[rows[c], :] for c in range(NC)]
    k_h = [p_rem[c] * k[rows[c], :] for c in range(NC)]
    vec = vec_ref[...]
    r_k, lnx_g, lnx_b = (vec[i:i + 1, :] for i in range(3))
    rk = r * k * r_k
    g = g_ref[...]

    lane = lax.broadcasted_iota(jnp.int32, (1, LANES), 1)
    first = lane < RW_HEAD

    def stack(x):
        return jnp.concatenate([jnp.where(first, x, 0.0), jnp.where(first, 0.0, x)], axis=0).astype(BF16)

    def fold(x):
        return x[0:C, :] + x[C:C2, :]

    ri = lax.broadcasted_iota(jnp.int32, (C2, C2), 0)
    cj = lax.broadcasted_iota(jnp.int32, (C2, C2), 1)
    same = _blk(ri, C) == _blk(cj, C)
    strict = same & (_off(ri, C) > _off(cj, C))
    incl = same & (_off(ri, C) >= _off(cj, C))
    eye = jnp.where(ri == cj, 1.0, 0.0)
    same_head = (_blk(lax.broadcasted_iota(jnp.int32, (LANES, LANES), 0), RW_HEAD)
                 == _blk(lax.broadcasted_iota(jnp.int32, (LANES, LANES), 1), RW_HEAD))
    ones_bd = jnp.where(same_head, 1.0, 0.0).astype(BF16)

    lanes = [slice(p * LANES, (p + 1) * LANES) for p in range(HP)]
    probs = [(c, p) for c in range(NC) for p in range(HP)]
    a_b = {cp: a_t[rows[cp[0]], lanes[cp[1]]].astype(BF16) for cp in probs}
    r_b = {cp: r_t[rows[cp[0]], lanes[cp[1]]].astype(BF16) for cp in probs}
    v_st = {cp: stack(v[rows[cp[0]], lanes[cp[1]]]) for cp in probs}
    low, rb_f, ak_f, rk_f = {}, {}, {}, {}
    if C2 == LANES:
        t_f64 = lax.broadcasted_iota(jnp.int32, (C, LANES), 0)
        s_f64 = _off(lax.broadcasted_iota(jnp.int32, (C, LANES), 1), C)
        strict_f, incl_f = t_f64 > s_f64, t_f64 >= s_f64
        for c, p in probs:
            q_st = jnp.concatenate([stack(a_t[rows[c], lanes[p]]), stack(r_t[rows[c], lanes[p]])], axis=0)
            w_st = jnp.concatenate([b_t[rows[c], lanes[p]], k_t[rows[c], lanes[p]]], axis=0).astype(BF16)
            gm = _dot_nt(q_st, w_st)
            gr = pltpu.roll(gm, C, 1)
            a0, a1, r0, r1 = (gm[i * C:(i + 1) * C, :] for i in range(4))
            a0r, a1r, r0r, r1r = (gr[i * C:(i + 1) * C, :] for i in range(4))
            low[c, p] = jnp.concatenate([jnp.where(first & strict_f, a0, 0.0),
                                         jnp.where(strict_f & ~first, a1r, 0.0)], axis=0)
            ak_f[c, p] = jnp.where(strict_f, jnp.where(first, a0r, a1), 0.0).astype(BF16)
            rb_f[c, p] = jnp.where(incl_f, jnp.where(first, r0, r1r), 0.0).astype(BF16)
            rk_f[c, p] = jnp.where(incl_f, jnp.where(first, r0r, r1), 0.0).astype(BF16)
    else:
        for c, p in probs:
            q_st = jnp.concatenate([stack(a_t[rows[c], lanes[p]]), stack(r_t[rows[c], lanes[p]])], axis=0)
            b_c = b_t[rows[c], lanes[p]].astype(BF16)
            k_c = k_t[rows[c], lanes[p]].astype(BF16)
            gb = _dot_nt(q_st, jnp.concatenate([b_c, b_c], axis=0))
            gk = _dot_nt(q_st, jnp.concatenate([k_c, k_c], axis=0))
            low[c, p] = jnp.where(strict, gb[0:C2, :], 0.0)
            rb_f[c, p] = fold(jnp.where(incl, gb[C2:, :], 0.0)).astype(BF16)
            ak_f[c, p] = fold(jnp.where(strict, gk[0:C2, :], 0.0)).astype(BF16)
            rk_f[c, p] = fold(jnp.where(incl, gk[C2:, :], 0.0)).astype(BF16)
    akv = {cp: _dot(ak_f[cp], v_st[cp]) for cp in probs}
    rkv = {cp: _dot(rk_f[cp], v_st[cp]) for cp in probs}

    first_level = _lower_left(ri, cj, 1)
    t_inv = {cp: eye + jnp.where(first_level, low[cp], 0.0) for cp in probs}
    s = 2
    while s < C:
        sel = _lower_left(ri, cj, s)
        tb = {cp: t_inv[cp].astype(BF16) for cp in probs}
        mid = {cp: _dot(tb[cp], jnp.where(sel, low[cp], 0.0).astype(BF16)).astype(BF16) for cp in probs}
        t_inv = {cp: t_inv[cp] + _dot(mid[cp], tb[cp]) for cp in probs}
        s *= 2
    t_f = {cp: fold(t_inv[cp]).astype(BF16) for cp in probs}

    state = [s_scr[p] for p in range(HP)]
    y = {}
    for c in range(NC):
        state_b = [st.astype(BF16) for st in state]
        x = [_dot_nt(a_b[c, p], state_b[p]) + akv[c, p] for p in range(HP)]
        y0 = [_dot_nt(r_b[c, p], state_b[p]) + rkv[c, p] for p in range(HP)]
        u = [_dot(t_f[c, p], stack(x[p])) for p in range(HP)]
        for p in range(HP):
            y[c, p] = y0[p] + _dot(rb_f[c, p], stack(u[p]))
        upd = []
        for p in range(HP):
            uv = jnp.concatenate([u[p], v[rows[c], lanes[p]]], axis=0).astype(BF16)
            bk = jnp.concatenate([b_h[c][:, lanes[p]], k_h[c][:, lanes[p]]], axis=0).astype(BF16)
            upd.append(_dot_tn(uv, bk))
        state = [state[p] * p_end[c][:, lanes[p]] + jnp.where(same_head, upd[p], 0.0) for p in range(HP)]
    for p in range(HP):
        s_scr[p] = state[p]

    def seg(t, slices):
        parts = _split2(t) if slices == 2 else (t.astype(BF16),)
        return sum(_dot(part, ones_bd) for part in parts)

    y_all = [jnp.concatenate([y[c, p] for c in range(NC)], axis=0) for p in range(HP)]
    mean = [seg(y_all[p], 2) * (1.0 / RW_HEAD) for p in range(HP)]
    bonus = [seg(rk[:, lanes[p]], 1) * v[:, lanes[p]] for p in range(HP)]
    d = [y_all[p] - mean[p] for p in range(HP)]
    var = [seg(d[p] * d[p], 1) * (1.0 / RW_HEAD) for p in range(HP)]
    for p in range(HP):
        yn = d[p] * lax.rsqrt(var[p] + GN_EPS) * lnx_g[:, lanes[p]] + lnx_b[:, lanes[p]]
        z_ref[:, lanes[p]] = ((yn + bonus[p]) * g[:, lanes[p]]).astype(z_ref.dtype)

    @pl.when(ci == pl.num_programs(2) - 1)
    def _():
        sout_ref[...] = s_scr[...]


def rwkv_recur(ops, s0_bd, vecs, B, T):
    D = ops[0].shape[1]
    HL = min(1024, D)
    HP = HL // LANES
    C = min(CHUNK, T)
    NC = 2 if T % (2 * C) == 0 else 1
    ops3 = [o.reshape(B, T, D) for o in ops]
    tile = pl.BlockSpec((None, NC * C, HL), lambda b, j, c: (b, c, j))
    st = pl.BlockSpec((None, HP, LANES, LANES), lambda b, j, c: (b, j, 0, 0))
    z, s_fin = pl.pallas_call(
        functools.partial(_recur_kernel, C=C, NC=NC, HP=HP),
        grid=(B, D // HL, T // (NC * C)),
        in_specs=[tile] * 7 + [st, pl.BlockSpec((vecs.shape[0], HL), lambda b, j, c: (0, j))],
        out_specs=[tile, st],
        out_shape=[jax.ShapeDtypeStruct((B, T, D), BF16), jax.ShapeDtypeStruct(s0_bd.shape, F32)],
        scratch_shapes=[pltpu.VMEM((HP, LANES, LANES), F32)],
        compiler_params=_params(("parallel", "parallel", "arbitrary")),
        name="rwkv_recur",
    )(*ops3, s0_bd, vecs)
    return z.reshape(B * T, D), s_fin


def _gate_operand(gate, B, T, bm, bn):
    N = gate.shape[1]
    col = (lambda j: j) if bn < N else (lambda j: 0)
    if T % bm == 0:
        return gate.reshape(B, 1, N), pl.BlockSpec((None, 1, bn), lambda i, j: ((i * bm) // T, 0, col(j)))
    rows = jnp.broadcast_to(gate[:, None, :], (B, T, N)).reshape(B * T, N)
    return rows, pl.BlockSpec((bm, bn), lambda i, j: (i, col(j)))


def _matmul_res_kernel(a_ref, w_ref, x_ref, gate_ref, o_ref):
    o_ref[...] = x_ref[...] + gate_ref[...] * _dot(a_ref[...], w_ref[...])


def matmul_res(a, w, x, gate, B, T):
    M, K = a.shape
    N = w.shape[1]
    bm, bn = min(1024, M), min(512, N)
    gate_arr, gate_spec = _gate_operand(gate, B, T, bm, bn)
    return pl.pallas_call(
        _matmul_res_kernel,
        grid=(M // bm, N // bn),
        in_specs=[pl.BlockSpec((bm, K), lambda i, j: (i, 0)),
                  pl.BlockSpec((K, bn), lambda i, j: (0, j)),
                  pl.BlockSpec((bm, bn), lambda i, j: (i, j)), gate_spec],
        out_specs=pl.BlockSpec((bm, bn), lambda i, j: (i, j)),
        out_shape=jax.ShapeDtypeStruct((M, N), F32),
        compiler_params=_params(("parallel", "arbitrary")),
        name="matmul_res",
    )(a, w, x, gate_arr)


def _mlp_kernel(x_ref, g_ref, sh_ref, sc_ref, gate_ref, w1_ref, w2_ref, fg_ref, o_ref, h_scr, acc, *, final):
    f = pl.program_id(1)
    bm = x_ref.shape[0]

    def rows_of(ref, rs):
        return ref[...] if ref.shape[0] == 1 else ref[rs, :]

    def mlp(h):
        return _dot(jnp.square(jnp.maximum(_dot(h, w1_ref[...]), 0.0)).astype(BF16), w2_ref[...])

    @pl.when(f == 0)
    def _():
        parts = 4 if bm % 64 == 0 else 1
        for c in range(parts):
            rs = slice(c * bm // parts, (c + 1) * bm // parts)
            h = (_rms(x_ref[rs, :], g_ref[...]) * (1.0 + rows_of(sc_ref, rs)) + rows_of(sh_ref, rs)).astype(BF16)
            h_scr[rs, :] = h
            acc[rs, :] = mlp(h)

    @pl.when(f > 0)
    def _():
        acc[...] += mlp(h_scr[...])

    @pl.when(f == pl.num_programs(1) - 1)
    def _():
        y = x_ref[...] + gate_ref[...] * acc[...]
        o_ref[...] = _rms(y, fg_ref[...]) if final else y


def mlp_res(x, g, shift, scale, gate, w1, w2, layer, final_g, final, B, T):
    M, D = x.shape
    F = w1.shape[2]
    bm = 512 if T % 512 == 0 else min(256, M)
    bf = min(1024, F)
    (sh_arr, vec_spec), (sc_arr, _), (gate_arr, _) = (_gate_operand(v, B, T, bm, D) for v in (shift, scale, gate))
    return pl.pallas_call(
        functools.partial(_mlp_kernel, final=final),
        grid=(M // bm, F // bf),
        in_specs=[pl.BlockSpec((bm, D), lambda i, f: (i, 0)),
                  pl.BlockSpec((1, D), lambda i, f: (0, 0)), vec_spec, vec_spec, vec_spec,
                  pl.BlockSpec((None, D, bf), lambda i, f: (layer, 0, f)),
                  pl.BlockSpec((None, bf, D), lambda i, f: (layer, f, 0)),
                  pl.BlockSpec((1, D), lambda i, f: (0, 0))],
        out_specs=pl.BlockSpec((bm, D), lambda i, f: (i, 0)),
        out_shape=jax.ShapeDtypeStruct((M, D), F32),
        scratch_shapes=[pltpu.VMEM((bm, D), BF16), pltpu.VMEM((bm, D), F32)],
        compiler_params=_params(("parallel", "arbitrary")),
        name="mlp_res",
    )(x, g.reshape(1, D), sh_arr, sc_arr, gate_arr, w1, w2, final_g.reshape(1, D))


def _rope_pair(t, tab):
    prod = t * tab
    return (prod + pltpu.roll(prod, ROPE_DIM, 1))[:, :ROPE_DIM]


def _latent_kernel(x_ref, gx_ref, sh_ref, sc_ref, w_ref, g_ref, tab_ref, ckv_ref, kpe_ref):
    h = (_rms(x_ref[...], gx_ref[...]) * (1.0 + sc_ref[...]) + sh_ref[...]).astype(BF16)
    acc = _dot(h, w_ref[...])
    R = g_ref.shape[1]
    ckv_ref[...] = _rms(acc[:, :R], g_ref[...])
    kpe_ref[...] = _rope_pair(acc[:, R:R + 2 * ROPE_DIM], tab_ref[...])


def latent(x, gx, shift, scale, w, g, tab):
    B, T, D = x.shape
    R = g.shape[0]
    bt = min(512, T)
    vec = pl.BlockSpec((None, 1, D), lambda b, t: (b, 0, 0))
    return pl.pallas_call(
        _latent_kernel,
        grid=(B, T // bt),
        in_specs=[pl.BlockSpec((None, bt, D), lambda b, t: (b, t, 0)),
                  pl.BlockSpec((1, D), lambda b, t: (0, 0)), vec, vec,
                  pl.BlockSpec(w.shape, lambda b, t: (0, 0)),
                  pl.BlockSpec((1, R), lambda b, t: (0, 0)),
                  pl.BlockSpec((bt, 2 * ROPE_DIM), lambda b, t: (t, 0))],
        out_specs=[pl.BlockSpec((None, bt, R), lambda b, t: (b, t, 0)),
                   pl.BlockSpec((None, bt, ROPE_DIM), lambda b, t: (b, t, 0))],
        out_shape=[jax.ShapeDtypeStruct((B, T, R), F32), jax.ShapeDtypeStruct((B, T, ROPE_DIM), F32)],
        compiler_params=_params(("parallel", "parallel")),
        name="latent",
    )(x, gx.reshape(1, D), shift.reshape(B, 1, D), scale.reshape(B, 1, D), w, g.reshape(1, R), tab)


def _kv_expand_kernel(c_ref, pe_ref, wuk_ref, wuv_ref, k_ref, vt_ref):
    c = c_ref[...].astype(BF16)
    kn = _dot(c, wuk_ref[...])
    vv = _dot(c, wuv_ref[...])
    pe = pe_ref[...].astype(BF16)
    for h in range(k_ref.shape[0]):
        k_ref[h, :, 0:NOPE_DIM] = kn[:, h * NOPE_DIM:(h + 1) * NOPE_DIM].astype(BF16)
        k_ref[h, :, NOPE_DIM:NOPE_DIM + ROPE_DIM] = pe
        vt_ref[h, 0:V_DIM, :] = vv[:, h * V_DIM:(h + 1) * V_DIM].T.astype(BF16)
        vt_ref[h, V_DIM:V_ROWS, :] = jnp.ones((V_ROWS - V_DIM, c.shape[0]), BF16)


def kv_expand(ckv, kpe, wuk, wuv, H):
    B, S, R = ckv.shape
    bt = min(512, S)
    hb = min(8, H)
    DK = NOPE_DIM + ROPE_DIM
    return pl.pallas_call(
        _kv_expand_kernel,
        grid=(H // hb, B, S // bt),
        in_specs=[pl.BlockSpec((None, bt, R), lambda j, b, t: (b, t, 0)),
                  pl.BlockSpec((None, bt, ROPE_DIM), lambda j, b, t: (b, t, 0)),
                  pl.BlockSpec((R, hb * NOPE_DIM), lambda j, b, t: (0, j)),
                  pl.BlockSpec((R, hb * V_DIM), lambda j, b, t: (0, j))],
        out_specs=[pl.BlockSpec((None, hb, bt, DK), lambda j, b, t: (b, j, t, 0)),
                   pl.BlockSpec((None, hb, V_ROWS, bt), lambda j, b, t: (b, j, 0, t))],
        out_shape=[jax.ShapeDtypeStruct((B, H, S, DK), BF16), jax.ShapeDtypeStruct((B, H, V_ROWS, S), BF16)],
        compiler_params=_params(("parallel", "parallel", "arbitrary")),
        name="kv_expand",
    )(ckv, kpe, wuk, wuv)


def _wdq_kernel(x_ref, gx_ref, sh_ref, sc_ref, w_ref, g_ref, o_ref):
    h = (_rms(x_ref[...], gx_ref[...]) * (1.0 + sc_ref[...]) + sh_ref[...]).astype(BF16)
    o_ref[...] = _rms(_dot(h, w_ref[...]), g_ref[...]).astype(BF16)


def wdq_norm(x, gx, shift, scale, w, g):
    B, T, D = x.shape
    R = w.shape[1]
    bt = min(1024, T)
    vec = pl.BlockSpec((None, 1, D), lambda b, t: (b, 0, 0))
    return pl.pallas_call(
        _wdq_kernel,
        grid=(B, T // bt),
        in_specs=[pl.BlockSpec((None, bt, D), lambda b, t: (b, t, 0)),
                  pl.BlockSpec((1, D), lambda b, t: (0, 0)), vec, vec,
                  pl.BlockSpec((D, R), lambda b, t: (0, 0)),
                  pl.BlockSpec((1, R), lambda b, t: (0, 0))],
        out_specs=pl.BlockSpec((None, bt, R), lambda b, t: (b, t, 0)),
        out_shape=jax.ShapeDtypeStruct((B, T, R), BF16),
        compiler_params=_params(("parallel", "parallel")),
        name="wdq_norm",
    )(x, gx.reshape(1, D), shift.reshape(B, 1, D), scale.reshape(B, 1, D), w, g.reshape(1, R))


def _wuq_kernel(c_ref, w_ref, tab_ref, q_ref, *, scale):
    acc = _dot(c_ref[...], w_ref[...])
    tab = tab_ref[...]
    W = NOPE_DIM + 2 * ROPE_DIM
    for h in range(q_ref.shape[0]):
        q_ref[h, :, 0:NOPE_DIM] = (acc[:, h * W:h * W + NOPE_DIM] * scale).astype(BF16)
        pe = _rope_pair(acc[:, h * W + NOPE_DIM:(h + 1) * W], tab)
        q_ref[h, :, NOPE_DIM:NOPE_DIM + ROPE_DIM] = (pe * scale).astype(BF16)
        if q_ref.shape[2] > NOPE_DIM + ROPE_DIM:
            pad = q_ref.shape[2] - NOPE_DIM - ROPE_DIM
            q_ref[h, :, NOPE_DIM + ROPE_DIM:] = jnp.zeros((q_ref.shape[1], pad), BF16)


def wuq_rope(cq, w, tab, H, head_major):
    B, T, R = cq.shape
    bt = min(512, T)
    nt = T // bt
    hb = min(8, H)
    W = NOPE_DIM + 2 * ROPE_DIM
    DK = NOPE_DIM + ROPE_DIM
    if head_major:
        out_spec = pl.BlockSpec((hb, bt, DK), lambda j, b, t: (j, b * nt + t, 0))
        out_shape = (H, B * T, DK)
    else:
        out_spec = pl.BlockSpec((None, hb, bt, 2 * LANES), lambda j, b, t: (b, j, t, 0))
        out_shape = (B, H, T, 2 * LANES)
    return pl.pallas_call(
        functools.partial(_wuq_kernel, scale=MLA_SCALE if head_major else MLA_SCALE * LOG2E),
        grid=(H // hb, B, nt),
        in_specs=[pl.BlockSpec((None, bt, R), lambda j, b, t: (b, t, 0)),
                  pl.BlockSpec((R, hb * W), lambda j, b, t: (0, j)),
                  pl.BlockSpec((bt, 2 * ROPE_DIM), lambda j, b, t: (t, 0))],
        out_specs=out_spec,
        out_shape=jax.ShapeDtypeStruct(out_shape, BF16),
        compiler_params=_params(("parallel", "parallel", "parallel")),
        name="wuq_rope",
    )(cq, w, tab)


def _visible(qpos, kpos):
    return _blk(kpos, CHUNK) <= _blk(qpos, CHUNK)


def _flash_kernel(q_ref, k_ref, vt_ref, o_ref, qt_scr, s_a, s_b, m_scr, acc_scr, *, tile):
    hb = q_ref.shape[0]
    ha = hb // 2
    first, second = range(0, ha), range(ha, hb)
    qi = pl.program_id(2)
    qpos = qi * tile + lax.broadcasted_iota(jnp.int32, (1, tile), 1)
    m_scr[...] = jnp.full(m_scr.shape, -jnp.inf, F32)
    acc_scr[...] = jnp.zeros(acc_scr.shape, F32)
    DK = NOPE_DIM + ROPE_DIM
    for h in range(hb):
        qf = q_ref[h].astype(F32)
        qt_scr[h, 0:LANES, :] = qf[:, 0:LANES].T.astype(BF16)
        qt_scr[h, LANES:DK, :] = qf[:, LANES:2 * LANES].T[0:DK - LANES, :].astype(BF16)

    def scores(ki, heads, s_ref):
        start = pl.multiple_of(ki * tile, tile)
        for h in heads:
            s_ref[h - heads[0]] = _dot(k_ref[h, pl.ds(start, tile), :], qt_scr[h])

    def softmax_pv(ki, heads, s_ref, masked):
        start = pl.multiple_of(ki * tile, tile)
        for h in heads:
            s = s_ref[h - heads[0]]
            if masked:
                kpos = ki * tile + lax.broadcasted_iota(jnp.int32, (tile, 1), 0)
                s = jnp.where(_visible(qpos, kpos), s, -jnp.inf)
            m = m_scr[h]
            m_new = jnp.maximum(m, jnp.max(s, axis=0, keepdims=True))
            p = jnp.exp2(s - m_new).astype(BF16)
            acc_scr[h] = jnp.exp2(m - m_new) * acc_scr[h] + _dot(vt_ref[h, :, pl.ds(start, tile)], p)
            m_scr[h] = m_new

    def body(ki, carry):
        scores(ki, second, s_b)
        softmax_pv(ki, first, s_a, False)
        scores(ki + 1, first, s_a)
        softmax_pv(ki, second, s_b, False)
        return carry

    scores(0, first, s_a)
    lax.fori_loop(0, qi, body, 0)
    scores(qi, second, s_b)
    softmax_pv(qi, first, s_a, True)
    softmax_pv(qi, second, s_b, True)
    for h in range(hb):
        acc = acc_scr[h]
        inv_l = 1.0 / acc[V_DIM:V_DIM + 1, :]
        o_ref[:, h * V_DIM:(h + 1) * V_DIM] = (acc[0:V_DIM, :] * inv_l).T.astype(o_ref.dtype)


def flash_prompt(q, k, vt):
    B, H, T, QW = q.shape
    DK = k.shape[3]
    tile = min(256, T)
    assert tile % CHUNK == 0 and T % tile == 0 and H % 2 == 0 and QW == 2 * LANES
    hb = next(n for n in (8, 4, 2) if H % n == 0)
    return pl.pallas_call(
        functools.partial(_flash_kernel, tile=tile),
        grid=(B, H // hb, T // tile),
        in_specs=[pl.BlockSpec((None, hb, tile, QW), lambda b, j, i: (b, j, i, 0)),
                  pl.BlockSpec((None, hb, T, DK), lambda b, j, i: (b, j, 0, 0)),
                  pl.BlockSpec((None, hb, V_ROWS, T), lambda b, j, i: (b, j, 0, 0))],
        out_specs=pl.BlockSpec((None, tile, hb * V_DIM), lambda b, j, i: (b, i, j)),
        out_shape=jax.ShapeDtypeStruct((B, T, H * V_DIM), BF16),
        scratch_shapes=[pltpu.VMEM((hb, DK, tile), BF16),
                        pltpu.VMEM((hb // 2, tile, tile), F32), pltpu.VMEM((hb // 2, tile, tile), F32),
                        pltpu.VMEM((hb, 1, tile), F32), pltpu.VMEM((hb, V_ROWS, tile), F32)],
        compiler_params=_params(("parallel", "parallel", "arbitrary")),
        name="flash_prompt",
    )(q, k, vt)


def _q_absorb_kernel(q_ref, wuk_ref, o_ref):
    o_ref[...] = _dot_nt(q_ref[:, 0:NOPE_DIM], wuk_ref[...]).astype(BF16)


def _attn_latent_kernel(ql_ref, q_ref, qpos_ref, c_ref, pe_ref, o_ref):
    H, T, R = ql_ref.shape
    S = c_ref.shape[0]
    c = c_ref[...].astype(BF16)
    s = (_dot_nt(ql_ref[...].reshape(H * T, R), c)
         + _dot_nt(q_ref[:, :, NOPE_DIM:NOPE_DIM + ROPE_DIM].reshape(H * T, ROPE_DIM), pe_ref[...].astype(BF16)))
    kpos = lax.broadcasted_iota(jnp.int32, (1, S), 1)
    s = jnp.where(_visible(qpos_ref[...], kpos), s, -jnp.inf)
    p = jnp.exp(s - jnp.max(s, axis=-1, keepdims=True))
    l = jnp.sum(p, axis=-1, keepdims=True)
    o_ref[...] = (_dot(p.astype(BF16), c) / l).astype(BF16).reshape(H, T, R)


def _o_expand_kernel(o_ref, wuv_ref, out_ref):
    out_ref[...] = _dot(o_ref[...], wuv_ref[...]).astype(BF16)


def attn_latent(q, ckv, kpe, wuk, wuv, B, T, q0):
    H, M, DK = q.shape
    S, R = ckv.shape[1:]
    q_lat = pl.pallas_call(
        _q_absorb_kernel,
        grid=(H,),
        in_specs=[pl.BlockSpec((None, M, DK), lambda h: (h, 0, 0)),
                  pl.BlockSpec((R, NOPE_DIM), lambda h: (0, h))],
        out_specs=pl.BlockSpec((None, M, R), lambda h: (h, 0, 0)),
        out_shape=jax.ShapeDtypeStruct((H, M, R), BF16),
        compiler_params=_params(("parallel",)),
        name="q_absorb",
    )(q, wuk)
    qpos = jnp.tile(q0 + jnp.arange(T, dtype=jnp.int32), H)[:, None]
    o_lat = pl.pallas_call(
        _attn_latent_kernel,
        grid=(B,),
        in_specs=[pl.BlockSpec((H, T, R), lambda b: (0, b, 0)),
                  pl.BlockSpec((H, T, DK), lambda b: (0, b, 0)),
                  pl.BlockSpec((H * T, 1), lambda b: (0, 0)),
                  pl.BlockSpec((None, S, R), lambda b: (b, 0, 0)),
                  pl.BlockSpec((None, S, ROPE_DIM), lambda b: (b, 0, 0))],
        out_specs=pl.BlockSpec((H, T, R), lambda b: (0, b, 0)),
        out_shape=jax.ShapeDtypeStruct((H, M, R), BF16),
        compiler_params=_params(("parallel",)),
        name="attn_latent",
    )(q_lat, q, qpos, ckv, kpe)
    return pl.pallas_call(
        _o_expand_kernel,
        grid=(H,),
        in_specs=[pl.BlockSpec((None, M, R), lambda h: (h, 0, 0)),
                  pl.BlockSpec((R, V_DIM), lambda h: (0, h))],
        out_specs=pl.BlockSpec((M, V_DIM), lambda h: (0, h)),
        out_shape=jax.ShapeDtypeStruct((M, H * V_DIM), BF16),
        compiler_params=_params(("parallel",)),
        name="o_expand",
    )(o_lat, wuv)


def _pad_cols(w, n):
    return jnp.pad(w, ((0, 0), (0, n - w.shape[1])))


def _pad_rows(w, n):
    return jnp.pad(w, ((0, n - w.shape[0]), (0, 0)))


def _rotate_half_cols(w):
    half = ROPE_DIM // 2
    return jnp.concatenate([-w[..., half:], w[..., :half]], axis=-1)


def _rope_table(pos):
    half = ROPE_DIM // 2
    inv = ROPE_THETA ** (-jnp.arange(half, dtype=F32) / half)
    ang = pos.astype(F32)[:, None] * inv[None, :]
    cos, sin = jnp.cos(ang), jnp.sin(ang)
    return jnp.concatenate([cos, cos, sin, sin], axis=-1)


def _block_diag_states(s):
    B, H, N, _ = s.shape
    s = s.reshape(B, H // 2, 2, N, N)
    z = jnp.zeros_like(s[:, :, 0])
    top = jnp.concatenate([s[:, :, 0], z], axis=-1)
    bot = jnp.concatenate([z, s[:, :, 1]], axis=-1)
    return jnp.concatenate([top, bot], axis=-2)


def _diag_states(s):
    B, HP = s.shape[:2]
    N = RW_HEAD
    return jnp.stack([s[:, :, :N, :N], s[:, :, N:, N:]], axis=2).reshape(B, 2 * HP, N, N)


def _prepare(W):
    D = W['rw_wr'].shape[1]
    P = {}
    bf = lambda a: a.astype(BF16)
    P['mlp_w1'], P['mlp_w2'] = bf(W['mlp_w1']), bf(W['mlp_w2'])
    for n in ('rw_wr', 'rw_wk', 'rw_wv', 'rw_wo', 'rw_g1', 'rw_g2', 'mla_wdq', 'mla_wo'):
        P[n] = [bf(W[n][l]) for l in range(W[n].shape[0])]
    NA = W['rw_wr'].shape[0]
    P['rw_w1'] = [bf(_pad_cols(W['rw_w1'][l], LORA_PAD)) for l in range(NA)]
    P['rw_w2'] = [bf(_pad_rows(W['rw_w2'][l], LORA_PAD)) for l in range(NA)]
    P['rw_a1'] = [bf(_pad_cols(W['rw_a1'][l], LORA_PAD)) for l in range(NA)]
    P['rw_a2'] = [bf(_pad_rows(W['rw_a2'][l], LORA_PAD)) for l in range(NA)]
    P['rw_v1'] = [bf(_pad_cols(W['rw_v1'][l], LORA_PAD)) for l in range(NA - 1)]
    P['rw_v2'] = [bf(_pad_rows(W['rw_v2'][l], LORA_PAD)) for l in range(NA - 1)]
    zeros = jnp.zeros((D,), F32)
    P['proj_vecs'], P['recur_vecs'] = [], []
    for l in range(NA):
        v0 = W['rw_v0'][l - 1] if l > 0 else zeros
        P['proj_vecs'].append(jnp.stack([W['rw_w0'][l], W['rw_a0'][l], v0, W['rw_kk'][l], W['rw_ka'][l],
                                         zeros, zeros, zeros]))
        P['recur_vecs'].append(jnp.stack([W['rw_rk'][l], W['rw_lnx_g'][l], W['rw_lnx_b'][l]] + [zeros] * 5))
    HL = min(512, D)
    head = jnp.arange(HL) // RW_HEAD
    P['ones_bd'] = (head[:, None] == head[None, :]).astype(BF16)
    R = W['kv_lat_g'].shape[0]
    wd = W['kv_wd']
    P['kv_wd'] = bf(jnp.concatenate([wd, _rotate_half_cols(wd[:, R:])], axis=1))
    H = W['kv_wuk'].shape[1]
    P['kv_wuk'] = bf(W['kv_wuk'].reshape(R, H * NOPE_DIM))
    P['kv_wuv'] = bf(W['kv_wuv'].reshape(R, H * V_DIM))
    NB, Q = W['mla_wuq'].shape[:2]
    wuq = W['mla_wuq'].reshape(NB, Q, H, NOPE_DIM + ROPE_DIM)
    pe = wuq[..., NOPE_DIM:]
    P['mla_wuq'] = bf(jnp.concatenate([wuq, _rotate_half_cols(pe)], axis=-1).reshape(NB, Q, -1))
    return P


def _trunk(x, mod, kv_mod, pos0, h_prev, s0, past_ckv, past_kpe, W, P):
    B, T, D = x.shape
    M = B * T
    depth = W['ada_w'].shape[0]
    NA = W['rw_wr'].shape[0]
    H = W['kv_wuk'].shape[1]
    tab = _rope_table(pos0 + jnp.arange(T))
    xf = x.reshape(M, D)
    shifts, states = [], []
    v_first = None
    keys = vals = ckv = kpe = None
    for l in range(depth):
        m = mod[l]
        if l < NA:
            has_v = l > 0
            pre = rwkv_pre(xf.reshape(B, T, D), W['norm_mix_g'][l], m[:, 0], m[:, 1], h_prev[l], W['rw_mu'][l],
                           P['rw_w1'][l], P['rw_a1'][l], P['rw_g1'][l], P['rw_v1'][l - 1] if has_v else None)
            shifts.append(pre[-1].reshape(B, D))
            acts = [a.reshape(M, a.shape[-1]) for a in pre[:-1]]
            ops = rwkv_proj(acts, v_first, P['rw_wr'][l], P['rw_wk'][l], P['rw_wv'][l], P['rw_w2'][l],
                            P['rw_a2'][l], P['rw_g2'][l], P['rw_v2'][l - 1] if has_v else None,
                            P['proj_vecs'][l], P['ones_bd'])
            if l == 0:
                v_first = ops[3]
            z, s_fin = rwkv_recur(ops, _block_diag_states(s0[l]), P['recur_vecs'][l], B, T)
            states.append(_diag_states(s_fin))
            xf = matmul_res(z, P['rw_wo'][l], xf, m[:, 2], B, T)
        else:
            j = l - NA
            cq = wdq_norm(xf.reshape(B, T, D), W['norm_mix_g'][l], m[:, 0], m[:, 1], P['mla_wdq'][j], W['mla_q_g'][j])
            if past_ckv is None:
                q = wuq_rope(cq, P['mla_wuq'][j], tab, H, False)
                o = flash_prompt(q, keys, vals).reshape(M, H * V_DIM)
            else:
                q = wuq_rope(cq, P['mla_wuq'][j], tab, H, True)
                o = attn_latent(q, keys, vals, P['kv_wuk'], P['kv_wuv'], B, T, pos0)
            xf = matmul_res(o, P['mla_wo'][j], xf, m[:, 2], B, T)
        xf = mlp_res(xf, W['norm_mlp_g'][l], m[:, 3], m[:, 4], m[:, 5], P['mlp_w1'], P['mlp_w2'], l,
                     W['final_g'], l == depth - 1, B, T)
        if l == NA - 1:
            ckv, kpe = latent(xf.reshape(B, T, D), W['kv_norm_g'], kv_mod[:, 0], kv_mod[:, 1],
                              P['kv_wd'], W['kv_lat_g'], tab)
            if past_ckv is None:
                keys, vals = kv_expand(ckv, kpe, P['kv_wuk'], P['kv_wuv'], H)
            else:
                keys = jnp.concatenate([past_ckv, ckv], axis=1)
                vals = jnp.concatenate([past_kpe, kpe], axis=1)
    return xf.reshape(B, T, D), ckv, kpe, jnp.stack(states), jnp.stack(shifts)


def kernel(x_prompt, x_sample, cache_ckv, cache_kpe, state_wkv, state_shift, c_prompt, c_sample, ada_w, ada_b, norm_mix_g, norm_mlp_g, mlp_w1, mlp_w2, rw_mu, rw_w0, rw_w1, rw_w2, rw_a0, rw_a1, rw_a2, rw_v0, rw_v1, rw_v2, rw_g1, rw_g2, rw_wr, rw_wk, rw_wv, rw_wo, rw_kk, rw_ka, rw_rk, rw_lnx_g, rw_lnx_b, kv_ada_w, kv_ada_b, kv_norm_g, kv_wd, kv_lat_g, kv_wuk, kv_wuv, mla_wdq, mla_q_g, mla_wuq, mla_wo, final_g):
    W = dict(ada_w=ada_w, ada_b=ada_b, norm_mix_g=norm_mix_g, norm_mlp_g=norm_mlp_g,
             mlp_w1=mlp_w1, mlp_w2=mlp_w2, rw_mu=rw_mu, rw_w0=rw_w0, rw_w1=rw_w1, rw_w2=rw_w2,
             rw_a0=rw_a0, rw_a1=rw_a1, rw_a2=rw_a2, rw_v0=rw_v0, rw_v1=rw_v1, rw_v2=rw_v2,
             rw_g1=rw_g1, rw_g2=rw_g2, rw_wr=rw_wr, rw_wk=rw_wk, rw_wv=rw_wv, rw_wo=rw_wo,
             rw_kk=rw_kk, rw_ka=rw_ka, rw_rk=rw_rk, rw_lnx_g=rw_lnx_g, rw_lnx_b=rw_lnx_b,
             kv_ada_w=kv_ada_w, kv_ada_b=kv_ada_b, kv_norm_g=kv_norm_g, kv_wd=kv_wd,
             kv_lat_g=kv_lat_g, kv_wuk=kv_wuk, kv_wuv=kv_wuv, mla_wdq=mla_wdq, mla_q_g=mla_q_g,
             mla_wuq=mla_wuq, mla_wo=mla_wo, final_g=final_g)
    P = _prepare(W)
    Bp, Tp, D = x_prompt.shape
    Bs = x_sample.shape[0]
    depth = ada_w.shape[0]
    NA = rw_wr.shape[0]
    c_all = jnp.concatenate([c_prompt, c_sample], axis=0)
    mod = ada_linear(c_all, ada_w, ada_b).reshape(depth, Bp + Bs, N_MOD, D)
    kv_mod = ada_linear(c_all, kv_ada_w[None], kv_ada_b[None]).reshape(Bp + Bs, 2, D)
    h0 = jnp.zeros((NA, Bp, D), F32)
    s0 = jnp.zeros((NA, Bp, D // RW_HEAD, RW_HEAD, RW_HEAD), F32)
    out_p = _trunk(x_prompt, mod[:, :Bp], kv_mod[:Bp], 0, h0, s0, None, None, W, P)
    out_s = _trunk(x_sample, mod[:, Bp:], kv_mod[Bp:], cache_ckv.shape[1], state_shift, state_wkv,
                   cache_ckv, cache_kpe, W, P)
    return (out_p[0], out_s[0]) + out_p[1:] + out_s[1:]
```

```python
import functools

import jax
import jax.numpy as jnp
from jax import lax
from jax.experimental import pallas as pl
from jax.experimental.pallas import tpu as pltpu

F32, BF16 = jnp.float32, jnp.bfloat16

RW_HEAD = 64
CHUNK = 64
GN_EPS = 64e-5
NOPE_DIM = 128
ROPE_DIM = 64
V_DIM = 128
ROPE_THETA = 10000.0
MLA_SCALE = (NOPE_DIM + ROPE_DIM) ** -0.5
LOG2E = 1.4426950408889634
V_ROWS = V_DIM + 16
NORM_EPS = 1e-6
N_MOD = 6

LANES = 128
VMEM_LIMIT = 56 * 1024 * 1024
LORA_PAD = 128


def _params(sem, vmem=VMEM_LIMIT):
    return pltpu.CompilerParams(dimension_semantics=sem, vmem_limit_bytes=vmem)


def _dot(a, b):
    return jnp.dot(a, b, preferred_element_type=F32)


def _dot_nt(a, b):
    return lax.dot_general(a, b, (((1,), (1,)), ((), ())), preferred_element_type=F32)


def _dot_tn(a, b):
    return lax.dot_general(a, b, (((0,), (0,)), ((), ())), preferred_element_type=F32)


def _split2(x):
    hi = x.astype(BF16)
    return hi, (x - hi.astype(F32)).astype(BF16)


def _split3(x):
    hi = x.astype(BF16)
    r1 = x - hi.astype(F32)
    mid = r1.astype(BF16)
    return hi, mid, (r1 - mid.astype(F32)).astype(BF16)


def _sigmoid(x):
    return 1.0 / (1.0 + jnp.exp(-x))


def _blk(i, n):
    assert n & (n - 1) == 0
    return i >> (n.bit_length() - 1)


def _off(i, n):
    assert n & (n - 1) == 0
    return i & (n - 1)


def _lower_left(ri, cj, s):
    return (_blk(ri, 2 * s) == _blk(cj, 2 * s)) & (_off(ri, 2 * s) >= s) & (_off(cj, 2 * s) < s)


def _rms(x, g):
    return x * lax.rsqrt(jnp.mean(x * x, axis=-1, keepdims=True) + NORM_EPS) * g


def _ada_kernel(c_ref, w_ref, b_ref, o_ref):
    c = c_ref[...]
    cs = c * _sigmoid(c)
    o_ref[...] = _dot(cs.astype(BF16), w_ref[...].astype(BF16)) + b_ref[...]


def ada_linear(c, w, b):
    L, K, N = w.shape
    M = c.shape[0]
    bn = min(512, N)
    return pl.pallas_call(
        _ada_kernel,
        grid=(L, N // bn),
        in_specs=[pl.BlockSpec((M, K), lambda l, j: (0, 0)),
                  pl.BlockSpec((None, K, bn), lambda l, j: (l, 0, j)),
                  pl.BlockSpec((None, 1, bn), lambda l, j: (l, 0, j))],
        out_specs=pl.BlockSpec((None, M, bn), lambda l, j: (l, 0, j)),
        out_shape=jax.ShapeDtypeStruct((L, M, N), F32),
        compiler_params=_params(("parallel", "parallel")),
        name="ada_linear",
    )(c, w, b.reshape(L, 1, N))


def _rwkv_pre_kernel(*refs, has_v):
    if has_v:
        (x_ref, g_ref, sh_ref, sc_ref, hp_ref, mu_ref, w1_ref, a1_ref, g1_ref, v1_ref,
         xr_ref, xk_ref, xv_ref, tw_ref, av_ref, gg_ref, vv_ref, hl_ref, prev) = refs
    else:
        (x_ref, g_ref, sh_ref, sc_ref, hp_ref, mu_ref, w1_ref, a1_ref, g1_ref,
         xr_ref, xk_ref, xv_ref, tw_ref, av_ref, gg_ref, hl_ref, prev) = refs
    bt = x_ref.shape[0]
    h = _rms(x_ref[...], g_ref[...]) * (1.0 + sc_ref[...]) + sh_ref[...]

    @pl.when(pl.program_id(1) == 0)
    def _():
        prev[...] = hp_ref[...]

    row = lax.broadcasted_iota(jnp.int32, h.shape, 0)
    xx = jnp.where(row == 0, prev[...], pltpu.roll(h, 1, 0)) - h
    last = h[bt - 1:bt, :]
    prev[...] = last
    hl_ref[...] = last
    mu = mu_ref[...]

    def mix(i):
        return (h + xx * mu[i:i + 1, :]).astype(BF16)

    xr_ref[...] = mix(0)
    tw_ref[...] = jnp.tanh(_dot(mix(1), w1_ref[...])).astype(BF16)
    xk_ref[...] = mix(2)
    xv = mix(3)
    xv_ref[...] = xv
    if has_v:
        vv_ref[...] = _dot(xv, v1_ref[...]).astype(BF16)
    av_ref[...] = _dot(mix(4), a1_ref[...]).astype(BF16)
    gg_ref[...] = _sigmoid(_dot(mix(5), g1_ref[...])).astype(BF16)


def rwkv_pre(x, g, shift, scale, h_prev, mu, w1, a1, g1, v1):
    B, T, D = x.shape
    bt = min(256, T)
    has_v = v1 is not None
    row = lambda n: pl.BlockSpec((None, bt, n), lambda b, t: (b, t, 0))
    vec = pl.BlockSpec((None, 1, D), lambda b, t: (b, 0, 0))
    full = lambda a: pl.BlockSpec(a.shape, lambda b, t: (0, 0))
    lora = [w1, a1, g1] + ([v1] if has_v else [])
    outs = [(D, BF16)] * 3 + [(w1.shape[1], BF16), (a1.shape[1], BF16), (g1.shape[1], BF16)]
    if has_v:
        outs.append((v1.shape[1], BF16))
    res = pl.pallas_call(
        functools.partial(_rwkv_pre_kernel, has_v=has_v),
        grid=(B, T // bt),
        in_specs=[row(D), pl.BlockSpec((1, D), lambda b, t: (0, 0)), vec, vec, vec, full(mu)]
                 + [full(a) for a in lora],
        out_specs=[row(n) for n, _ in outs] + [vec],
        out_shape=[jax.ShapeDtypeStruct((B, T, n), dt) for n, dt in outs]
                  + [jax.ShapeDtypeStruct((B, 1, D), F32)],
        scratch_shapes=[pltpu.VMEM((1, D), F32)],
        compiler_params=_params(("parallel", "arbitrary")),
        name="rwkv_pre",
    )(x, g.reshape(1, D), shift.reshape(B, 1, D), scale.reshape(B, 1, D), h_prev.reshape(B, 1, D), mu, *lora)
    return res


def _rwkv_proj_kernel(*refs, has_v):
    if has_v:
        (xr_ref, xk_ref, xv_ref, tw_ref, av_ref, gg_ref, vv_ref, vf_ref,
         wr_ref, wk_ref, wv_ref, w2_ref, a2_ref, g2_ref, v2_ref, vec_ref, ones_ref,
         r_ref, ld_ref, k_ref, v_ref, kk_ref, b_ref, g_ref) = refs
    else:
        (xr_ref, xk_ref, xv_ref, tw_ref, av_ref, gg_ref,
         wr_ref, wk_ref, wv_ref, w2_ref, a2_ref, g2_ref, vec_ref, ones_ref,
         r_ref, ld_ref, k_ref, v_ref, kk_ref, b_ref, g_ref) = refs
    vec = vec_ref[...]
    w0, a0, v0, k_k, k_a = (vec[i:i + 1, :] for i in range(5))
    r_ref[...] = _dot(xr_ref[...], wr_ref[...])
    kraw = _dot(xk_ref[...], wk_ref[...])
    v = _dot(xv_ref[...], wv_ref[...])
    nz = -(w0 + _dot(tw_ref[...], w2_ref[...]))
    softplus = jnp.maximum(nz, 0.0) + jnp.log(1.0 + jnp.exp(-jnp.abs(nz)))
    ld_ref[...] = -jnp.exp(-softplus - 0.5)
    a = _sigmoid(a0 + _dot(av_ref[...], a2_ref[...]))
    if has_v:
        v = v + (vf_ref[...] - v) * _sigmoid(v0 + _dot(vv_ref[...], v2_ref[...]))
    v_ref[...] = v
    g_ref[...] = _dot(gg_ref[...], g2_ref[...])
    kk = kraw * k_k
    ss = _dot((kk * kk).astype(BF16), ones_ref[...])
    kk = kk / jnp.maximum(jnp.sqrt(ss), 1e-12)
    kk_ref[...] = kk
    b_ref[...] = kk * a
    k_ref[...] = kraw * (1.0 + (a - 1.0) * k_a)


def rwkv_proj(pre, v_first, wr, wk, wv, w2, a2, g2, v2, vecs, ones_bd):
    has_v = v2 is not None
    xr = pre[0]
    M, D = xr.shape
    HL = ones_bd.shape[0]
    bm = min(512, M)
    row = lambda a: pl.BlockSpec((bm, a.shape[1]), lambda i, j: (i, 0))
    tile = pl.BlockSpec((bm, HL), lambda i, j: (i, j))
    col = lambda a: pl.BlockSpec((a.shape[0], HL), lambda i, j: (0, j))
    acts = list(pre[:6]) + ([pre[6]] if has_v else [])
    weights = [wr, wk, wv, w2, a2, g2] + ([v2] if has_v else [])
    return pl.pallas_call(
        functools.partial(_rwkv_proj_kernel, has_v=has_v),
        grid=(M // bm, D // HL),
        in_specs=[row(a) for a in acts] + ([tile] if has_v else []) + [col(w) for w in weights]
                 + [col(vecs), pl.BlockSpec((HL, HL), lambda i, j: (0, 0))],
        out_specs=[tile] * 7,
        out_shape=[jax.ShapeDtypeStruct((M, D), F32)] * 7,
        compiler_params=_params(("parallel", "arbitrary")),
        name="rwkv_proj",
    )(*acts, *([v_first] if has_v else []), *weights, vecs, ones_bd)


def _recur_kernel(r_ref, ld_ref, k_ref, v_ref, kk_ref, b_ref, g_ref, s0_ref, vec_ref,
                  z_ref, sout_ref, s_scr, *, C, NC, HP):
    ci = pl.program_id(2)
    C2 = 2 * C
    RB = NC * C

    @pl.when(ci == 0)
    def _():
        s_scr[...] = s0_ref[...]

    ti = lax.broadcasted_iota(jnp.int32, (RB, RB), 0)
    tj = lax.broadcasted_iota(jnp.int32, (RB, RB), 1)
    tri = jnp.where((_blk(ti, C) == _blk(tj, C)) & (ti >= tj), 1.0, 0.0).astype(BF16)
    ld = ld_ref[...]
    cl = sum(_dot(tri, part) for part in _split3(ld))
    p_in = jnp.exp(cl)
    p_inv = jnp.exp(-cl)
    r, k, v, b = r_ref[...], k_ref[...], v_ref[...], b_ref[...]
    a_t = -(jnp.exp(cl - ld) * kk_ref[...])
    r_t = p_in * r
    b_t = p_inv * b
    k_t = p_inv * k
    rows = [slice(c * C, (c + 1) * C) for c in range(NC)]
    cl_end = [cl[(c + 1) * C - 1:(c + 1) * C, :] for c in range(NC)]
    p_rem = [jnp.exp(cl_end[c] - cl[rows[c], :]) for c in range(NC)]
    p_end = [jnp.exp(cl_end[c]) for c in range(NC)]
    b_h = [p_rem[c] * b[rows[c], :] for c in range(NC)]
    k_h = [p_rem[c] * k[rows[c], :] for c in range(NC)]
    vec = vec_ref[...]
    r_k, lnx_g, lnx_b = (vec[i:i + 1, :] for i in range(3))
    rk = r * k * r_k
    g = g_ref[...]

    lane = lax.broadcasted_iota(jnp.int32, (1, LANES), 1)
    first = lane < RW_HEAD

    def stack(x):
        return jnp.concatenate([jnp.where(first, x, 0.0), jnp.where(first, 0.0, x)], axis=0).astype(BF16)

    def fold(x):
        return x[0:C, :] + x[C:C2, :]

    ri = lax.broadcasted_iota(jnp.int32, (C2, C2), 0)
    cj = lax.broadcasted_iota(jnp.int32, (C2, C2), 1)
    same = _blk(ri, C) == _blk(cj, C)
    strict = same & (_off(ri, C) > _off(cj, C))
    incl = same & (_off(ri, C) >= _off(cj, C))
    eye = jnp.where(ri == cj, 1.0, 0.0)
    same_head = (_blk(lax.broadcasted_iota(jnp.int32, (LANES, LANES), 0), RW_HEAD)
                 == _blk(lax.broadcasted_iota(jnp.int32, (LANES, LANES), 1), RW_HEAD))
    ones_bd = jnp.where(same_head, 1.0, 0.0).astype(BF16)

    lanes = [slice(p * LANES, (p + 1) * LANES) for p in range(HP)]
    probs = [(c, p) for c in range(NC) for p in range(HP)]
    a_b = {cp: a_t[rows[cp[0]], lanes[cp[1]]].astype(BF16) for cp in probs}
    r_b = {cp: r_t[rows[cp[0]], lanes[cp[1]]].astype(BF16) for cp in probs}
    v_st = {cp: stack(v[rows[cp[0]], lanes[cp[1]]]) for cp in probs}
    low, rb_f, ak_f, rk_f = {}, {}, {}, {}
    if C2 == LANES:
        t_f64 = lax.broadcasted_iota(jnp.int32, (C, LANES), 0)
        s_f64 = _off(lax.broadcasted_iota(jnp.int32, (C, LANES), 1), C)
        strict_f, incl_f = t_f64 > s_f64, t_f64 >= s_f64
        for c, p in probs:
            q_st = jnp.concatenate([stack(a_t[rows[c], lanes[p]]), stack(r_t[rows[c], lanes[p]])], axis=0)
            w_st = jnp.concatenate([b_t[rows[c], lanes[p]], k_t[rows[c], lanes[p]]], axis=0).astype(BF16)
            gm = _dot_nt(q_st, w_st)
            gr = pltpu.roll(gm, C, 1)
            a0, a1, r0, r1 = (gm[i * C:(i + 1) * C, :] for i in range(4))
            a0r, a1r, r0r, r1r = (gr[i * C:(i + 1) * C, :] for i in range(4))
            low[c, p] = jnp.concatenate([jnp.where(first & strict_f, a0, 0.0),
                                         jnp.where(strict_f & ~first, a1r, 0.0)], axis=0)
            ak_f[c, p] = jnp.where(strict_f, jnp.where(first, a0r, a1), 0.0).astype(BF16)
            rb_f[c, p] = jnp.where(incl_f, jnp.where(first, r0, r1r), 0.0).astype(BF16)
            rk_f[c, p] = jnp.where(incl_f, jnp.where(first, r0r, r1), 0.0).astype(BF16)
    else:
        for c, p in probs:
            q_st = jnp.concatenate([stack(a_t[rows[c], lanes[p]]), stack(r_t[rows[c], lanes[p]])], axis=0)
            b_c = b_t[rows[c], lanes[p]].astype(BF16)
            k_c = k_t[rows[c], lanes[p]].astype(BF16)
            gb = _dot_nt(q_st, jnp.concatenate([b_c, b_c], axis=0))
            gk = _dot_nt(q_st, jnp.concatenate([k_c, k_c], axis=0))
            low[c, p] = jnp.where(strict, gb[0:C2, :], 0.0)
            rb_f[c, p] = fold(jnp.where(incl, gb[C2:, :], 0.0)).astype(BF16)
            ak_f[c, p] = fold(jnp.where(strict, gk[0:C2, :], 0.0)).astype(BF16)
            rk_f[c, p] = fold(jnp.where(incl, gk[C2:, :], 0.0)).astype(BF16)
    akv = {cp: _dot(ak_f[cp], v_st[cp]) for cp in probs}
    rkv = {cp: _dot(rk_f[cp], v_st[cp]) for cp in probs}

    first_level = _lower_left(ri, cj, 1)
    t_inv = {cp: eye + jnp.where(first_level, low[cp], 0.0) for cp in probs}
    s = 2
    while s < C:
        sel = _lower_left(ri, cj, s)
        tb = {cp: t_inv[cp].astype(BF16) for cp in probs}
        mid = {cp: _dot(tb[cp], jnp.where(sel, low[cp], 0.0).astype(BF16)).astype(BF16) for cp in probs}
        t_inv = {cp: t_inv[cp] + _dot(mid[cp], tb[cp]) for cp in probs}
        s *= 2
    t_f = {cp: fold(t_inv[cp]).astype(BF16) for cp in probs}

    state = [s_scr[p] for p in range(HP)]
    y = {}
    for c in range(NC):
        state_b = [st.astype(BF16) for st in state]
        x = [_dot_nt(a_b[c, p], state_b[p]) + akv[c, p] for p in range(HP)]
        y0 = [_dot_nt(r_b[c, p], state_b[p]) + rkv[c, p] for p in range(HP)]
        u = [_dot(t_f[c, p], stack(x[p])) for p in range(HP)]
        for p in range(HP):
            y[c, p] = y0[p] + _dot(rb_f[c, p], stack(u[p]))
        upd = []
        for p in range(HP):
            uv = jnp.concatenate([u[p], v[rows[c], lanes[p]]], axis=0).astype(BF16)
            bk = jnp.concatenate([b_h[c][:, lanes[p]], k_h[c][:, lanes[p]]], axis=0).astype(BF16)
            upd.append(_dot_tn(uv, bk))
        state = [state[p] * p_end[c][:, lanes[p]] + jnp.where(same_head, upd[p], 0.0) for p in range(HP)]
    for p in range(HP):
        s_scr[p] = state[p]

    def seg(t, slices):
        parts = _split2(t) if slices == 2 else (t.astype(BF16),)
        return sum(_dot(part, ones_bd) for part in parts)

    y_all = [jnp.concatenate([y[c, p] for c in range(NC)], axis=0) for p in range(HP)]
    mean = [seg(y_all[p], 2) * (1.0 / RW_HEAD) for p in range(HP)]
    bonus = [seg(rk[:, lanes[p]], 1) * v[:, lanes[p]] for p in range(HP)]
    d = [y_all[p] - mean[p] for p in range(HP)]
    var = [seg(d[p] * d[p], 1) * (1.0 / RW_HEAD) for p in range(HP)]
    for p in range(HP):
        yn = d[p] * lax.rsqrt(var[p] + GN_EPS) * lnx_g[:, lanes[p]] + lnx_b[:, lanes[p]]
        z_ref[:, lanes[p]] = ((yn + bonus[p]) * g[:, lanes[p]]).astype(z_ref.dtype)

    @pl.when(ci == pl.num_programs(2) - 1)
    def _():
        sout_ref[...] = s_scr[...]


def rwkv_recur(ops, s0_bd, vecs, B, T):
    D = ops[0].shape[1]
    HL = min(1024, D)
    HP = HL // LANES
    C = min(CHUNK, T)
    NC = 2 if T % (2 * C) == 0 else 1
    ops3 = [o.reshape(B, T, D) for o in ops]
    tile = pl.BlockSpec((None, NC * C, HL), lambda b, j, c: (b, c, j))
    st = pl.BlockSpec((None, HP, LANES, LANES), lambda b, j, c: (b, j, 0, 0))
    z, s_fin = pl.pallas_call(
        functools.partial(_recur_kernel, C=C, NC=NC, HP=HP),
        grid=(B, D // HL, T // (NC * C)),
        in_specs=[tile] * 7 + [st, pl.BlockSpec((vecs.shape[0], HL), lambda b, j, c: (0, j))],
        out_specs=[tile, st],
        out_shape=[jax.ShapeDtypeStruct((B, T, D), BF16), jax.ShapeDtypeStruct(s0_bd.shape, F32)],
        scratch_shapes=[pltpu.VMEM((HP, LANES, LANES), F32)],
        compiler_params=_params(("parallel", "parallel", "arbitrary")),
        name="rwkv_recur",
    )(*ops3, s0_bd, vecs)
    return z.reshape(B * T, D), s_fin


def _rwkv_mix_kernel(*refs, C, NC, HP, nT, has_v):
    it = iter(refs)
    xr_ref, xk_ref, xv_ref, tw_ref, av_ref, gg_ref = (next(it) for _ in range(6))
    vv_ref, vf_ref = (next(it), next(it)) if has_v else (None, None)
    wr_ref, wk_ref, wv_ref, w2_ref, a2_ref, g2_ref = (next(it) for _ in range(6))
    v2_ref = next(it) if has_v else None
    pvec_ref, rvec_ref, s0_ref, z_ref, sout_ref = (next(it) for _ in range(5))
    vout_ref = None if has_v else next(it)
    ops_scr, s_scr = next(it), next(it)
    step = pl.program_id(1)
    C2, RB = 2 * C, NC * C

    @pl.when(step == 0)
    def _():
        ops_scr[...] = jnp.zeros(ops_scr.shape, F32)

    r, ld, k, v, kk, b, g = (ops_scr[i] for i in range(7))
    first_of_batch = _off(jnp.maximum(step - 1, 0), nT) == 0
    state = [jnp.where(first_of_batch, s0_ref[p], s_scr[p]) for p in range(HP)]

    lanes = [slice(p * LANES, (p + 1) * LANES) for p in range(HP)]
    same_head = (_blk(lax.broadcasted_iota(jnp.int32, (LANES, LANES), 0), RW_HEAD)
                 == _blk(lax.broadcasted_iota(jnp.int32, (LANES, LANES), 1), RW_HEAD))
    ones_bd = jnp.where(same_head, 1.0, 0.0).astype(BF16)
    pvec = pvec_ref[...]
    PW = 2 * LANES if HP % 2 == 0 else LANES
    ones_pw = jnp.where(_blk(lax.broadcasted_iota(jnp.int32, (PW, PW), 0), RW_HEAD)
                        == _blk(lax.broadcasted_iota(jnp.int32, (PW, PW), 1), RW_HEAD), 1.0, 0.0).astype(BF16)

    def project(q):
        sl = slice(q * PW, (q + 1) * PW)
        w0, a0, v0, k_k, k_a = (pvec[i:i + 1, sl] for i in range(5))
        ops_scr[0, :, sl] = _dot(xr_ref[...], wr_ref[:, sl])
        kraw = _dot(xk_ref[...], wk_ref[:, sl])
        vp = _dot(xv_ref[...], wv_ref[:, sl])
        nz = -(w0 + _dot(tw_ref[...], w2_ref[:, sl]))
        softplus = jnp.maximum(nz, 0.0) + jnp.log(1.0 + jnp.exp(-jnp.abs(nz)))
        ops_scr[1, :, sl] = -jnp.exp(-softplus - 0.5)
        a = _sigmoid(a0 + _dot(av_ref[...], a2_ref[:, sl]))
        if has_v:
            vp = vp + (vf_ref[:, sl] - vp) * _sigmoid(v0 + _dot(vv_ref[...], v2_ref[:, sl]))
        else:
            vout_ref[:, sl] = vp
        ops_scr[3, :, sl] = vp
        ops_scr[6, :, sl] = _dot(gg_ref[...], g2_ref[:, sl])
        kn = kraw * k_k
        kn = kn / jnp.maximum(jnp.sqrt(_dot((kn * kn).astype(BF16), ones_pw)), 1e-12)
        ops_scr[4, :, sl] = kn
        ops_scr[5, :, sl] = kn * a
        ops_scr[2, :, sl] = kraw * (1.0 + (a - 1.0) * k_a)

    n_units = HP * LANES // PW
    n_slots = 2 + (C.bit_length() - 2) + NC
    plan = iter([[u for u in range(n_units) if u * n_slots // n_units == i] for i in range(n_slots)])

    def fill():
        for u in next(plan):
            project(u)

    ti = lax.broadcasted_iota(jnp.int32, (RB, RB), 0)
    tj = lax.broadcasted_iota(jnp.int32, (RB, RB), 1)
    tri = jnp.where((_blk(ti, C) == _blk(tj, C)) & (ti >= tj), 1.0, 0.0).astype(BF16)
    cl = sum(_dot(tri, part) for part in _split3(ld))
    p_in = jnp.exp(cl)
    p_inv = jnp.exp(-cl)
    a_t = -(jnp.exp(cl - ld) * kk)
    r_t = p_in * r
    b_t = p_inv * b
    k_t = p_inv * k
    rows = [slice(c * C, (c + 1) * C) for c in range(NC)]
    cl_end = [cl[(c + 1) * C - 1:(c + 1) * C, :] for c in range(NC)]
    p_rem = [jnp.exp(cl_end[c] - cl[rows[c], :]) for c in range(NC)]
    p_end = [jnp.exp(cl_end[c]) for c in range(NC)]
    b_h = [p_rem[c] * b[rows[c], :] for c in range(NC)]
    k_h = [p_rem[c] * k[rows[c], :] for c in range(NC)]
    rvec = rvec_ref[...]
    r_k, lnx_g, lnx_b = (rvec[i:i + 1, :] for i in range(3))
    rk = r * k * r_k

    first = lax.broadcasted_iota(jnp.int32, (1, LANES), 1) < RW_HEAD

    def stack(x):
        return jnp.concatenate([jnp.where(first, x, 0.0), jnp.where(first, 0.0, x)], axis=0).astype(BF16)

    def fold(x):
        return x[0:C, :] + x[C:C2, :]

    ri = lax.broadcasted_iota(jnp.int32, (C2, C2), 0)
    cj = lax.broadcasted_iota(jnp.int32, (C2, C2), 1)
    same = _blk(ri, C) == _blk(cj, C)
    strict = same & (_off(ri, C) > _off(cj, C))
    incl = same & (_off(ri, C) >= _off(cj, C))
    eye = jnp.where(ri == cj, 1.0, 0.0)

    probs = [(c, p) for c in range(NC) for p in range(HP)]
    a_b = {cp: a_t[rows[cp[0]], lanes[cp[1]]].astype(BF16) for cp in probs}
    r_b = {cp: r_t[rows[cp[0]], lanes[cp[1]]].astype(BF16) for cp in probs}
    v_st = {cp: stack(v[rows[cp[0]], lanes[cp[1]]]) for cp in probs}
    low, rb_f, ak_f, rk_f = {}, {}, {}, {}
    if C2 == LANES:
        t_f64 = lax.broadcasted_iota(jnp.int32, (C, LANES), 0)
        s_f64 = _off(lax.broadcasted_iota(jnp.int32, (C, LANES), 1), C)
        strict_f, incl_f = t_f64 > s_f64, t_f64 >= s_f64
        for c, p in probs:
            q_st = jnp.concatenate([stack(a_t[rows[c], lanes[p]]), stack(r_t[rows[c], lanes[p]])], axis=0)
            w_st = jnp.concatenate([b_t[rows[c], lanes[p]], k_t[rows[c], lanes[p]]], axis=0).astype(BF16)
            gm = _dot_nt(q_st, w_st)
            gr = pltpu.roll(gm, C, 1)
            a0_, a1_, r0_, r1_ = (gm[i * C:(i + 1) * C, :] for i in range(4))
            a0r, a1r, r0r, r1r = (gr[i * C:(i + 1) * C, :] for i in range(4))
            low[c, p] = jnp.concatenate([jnp.where(first & strict_f, a0_, 0.0),
                                         jnp.where(strict_f & ~first, a1r, 0.0)], axis=0)
            ak_f[c, p] = jnp.where(strict_f, jnp.where(first, a0r, a1_), 0.0).astype(BF16)
            rb_f[c, p] = jnp.where(incl_f, jnp.where(first, r0_, r1r), 0.0).astype(BF16)
            rk_f[c, p] = jnp.where(incl_f, jnp.where(first, r0r, r1_), 0.0).astype(BF16)
    else:
        for c, p in probs:
            q_st = jnp.concatenate([stack(a_t[rows[c], lanes[p]]), stack(r_t[rows[c], lanes[p]])], axis=0)
            b_c = b_t[rows[c], lanes[p]].astype(BF16)
            k_c = k_t[rows[c], lanes[p]].astype(BF16)
            gb = _dot_nt(q_st, jnp.concatenate([b_c, b_c], axis=0))
            gk = _dot_nt(q_st, jnp.concatenate([k_c, k_c], axis=0))
            low[c, p] = jnp.where(strict, gb[0:C2, :], 0.0)
            rb_f[c, p] = fold(jnp.where(incl, gb[C2:, :], 0.0)).astype(BF16)
            ak_f[c, p] = fold(jnp.where(strict, gk[0:C2, :], 0.0)).astype(BF16)
            rk_f[c, p] = fold(jnp.where(incl, gk[C2:, :], 0.0)).astype(BF16)
    fill()
    akv = {cp: _dot(ak_f[cp], v_st[cp]) for cp in probs}
    rkv = {cp: _dot(rk_f[cp], v_st[cp]) for cp in probs}

    first_level = _lower_left(ri, cj, 1)
    t_inv = {cp: eye + jnp.where(first_level, low[cp], 0.0) for cp in probs}
    s = 2
    while s < C:
        sel = _lower_left(ri, cj, s)
        tb = {cp: t_inv[cp].astype(BF16) for cp in probs}
        mid = {cp: _dot(tb[cp], jnp.where(sel, low[cp], 0.0).astype(BF16)).astype(BF16) for cp in probs}
        t_inv = {cp: t_inv[cp] + _dot(mid[cp], tb[cp]) for cp in probs}
        fill()
        s *= 2
    t_f = {cp: fold(t_inv[cp]).astype(BF16) for cp in probs}

    y = {}
    for c in range(NC):
        state_b = [st.astype(BF16) for st in state]
        x = [_dot_nt(a_b[c, p], state_b[p]) + akv[c, p] for p in range(HP)]
        y0 = [_dot_nt(r_b[c, p], state_b[p]) + rkv[c, p] for p in range(HP)]
        u = [_dot(t_f[c, p], stack(x[p])) for p in range(HP)]
        for p in range(HP):
            y[c, p] = y0[p] + _dot(rb_f[c, p], stack(u[p]))
        upd = []
        for p in range(HP):
            uv = jnp.concatenate([u[p], v[rows[c], lanes[p]]], axis=0).astype(BF16)
            bk = jnp.concatenate([b_h[c][:, lanes[p]], k_h[c][:, lanes[p]]], axis=0).astype(BF16)
            upd.append(_dot_tn(uv, bk))
        state = [state[p] * p_end[c][:, lanes[p]] + jnp.where(same_head, upd[p], 0.0) for p in range(HP)]
        fill()
    for p in range(HP):
        s_scr[p] = state[p]
        sout_ref[p] = state[p]

    def seg(t, slices):
        parts = _split2(t) if slices == 2 else (t.astype(BF16),)
        return sum(_dot(part, ones_bd) for part in parts)

    y_all = [jnp.concatenate([y[c, p] for c in range(NC)], axis=0) for p in range(HP)]
    mean = [seg(y_all[p], 2) * (1.0 / RW_HEAD) for p in range(HP)]
    bonus = [seg(rk[:, lanes[p]], 1) * v[:, lanes[p]] for p in range(HP)]
    fill()
    d = [y_all[p] - mean[p] for p in range(HP)]
    var = [seg(d[p] * d[p], 1) * (1.0 / RW_HEAD) for p in range(HP)]
    for p in range(HP):
        yn = d[p] * lax.rsqrt(var[p] + GN_EPS) * lnx_g[:, lanes[p]] + lnx_b[:, lanes[p]]
        z_ref[:, lanes[p]] = ((yn + bonus[p]) * g[:, lanes[p]]).astype(z_ref.dtype)


def rwkv_mix(acts, v_first, wr, wk, wv, w2, a2, g2, v2, pvecs, rvecs, s0_bd, B, T):
    has_v = v2 is not None
    M, D = acts[0].shape
    HL = min(1024, D)
    HP = HL // LANES
    C = min(CHUNK, T)
    NC = 2 if T % (2 * C) == 0 else 1
    RB = NC * C
    nT = T // RB
    assert nT & (nT - 1) == 0
    S = M // RB
    proj_blk = lambda s: jnp.minimum(s, S - 1)
    rec_blk = lambda s: jnp.maximum(s - 1, 0)
    row = lambda a: pl.BlockSpec((RB, a.shape[1]), lambda j, s: (proj_blk(s), 0))
    col = lambda a: pl.BlockSpec((a.shape[0], HL), lambda j, s: (0, j))
    st = pl.BlockSpec((None, HP, LANES, LANES), lambda j, s: (rec_blk(s) // nT, j, 0, 0))
    weights = [wr, wk, wv, w2, a2, g2] + ([v2] if has_v else [])
    in_specs = ([row(a) for a in acts]
                + ([pl.BlockSpec((RB, HL), lambda j, s: (proj_blk(s), j))] if has_v else [])
                + [col(w) for w in weights] + [col(pvecs), col(rvecs), st])
    out_specs = [pl.BlockSpec((RB, HL), lambda j, s: (rec_blk(s), j)), st]
    out_shape = [jax.ShapeDtypeStruct((M, D), BF16), jax.ShapeDtypeStruct(s0_bd.shape, F32)]
    if not has_v:
        out_specs.append(pl.BlockSpec((RB, HL), lambda j, s: (proj_blk(s), j)))
        out_shape.append(jax.ShapeDtypeStruct((M, D), F32))
    res = pl.pallas_call(
        functools.partial(_rwkv_mix_kernel, C=C, NC=NC, HP=HP, nT=nT, has_v=has_v),
        grid=(D // HL, S + 1),
        in_specs=in_specs, out_specs=out_specs, out_shape=out_shape,
        scratch_shapes=[pltpu.VMEM((7, RB, HL), F32), pltpu.VMEM((HP, LANES, LANES), F32)],
        compiler_params=_params(("parallel", "arbitrary")),
        name="rwkv_mix",
    )(*acts, *([v_first] if has_v else []), *weights, pvecs, rvecs, s0_bd)
    return res


def _gate_operand(gate, B, T, bm, bn):
    N = gate.shape[1]
    col = (lambda j: j) if bn < N else (lambda j: 0)
    if T % bm == 0:
        return gate.reshape(B, 1, N), pl.BlockSpec((None, 1, bn), lambda i, j: ((i * bm) // T, 0, col(j)))
    rows = jnp.broadcast_to(gate[:, None, :], (B, T, N)).reshape(B * T, N)
    return rows, pl.BlockSpec((bm, bn), lambda i, j: (i, col(j)))


def _matmul_res_kernel(a_ref, w_ref, x_ref, gate_ref, o_ref):
    o_ref[...] = x_ref[...] + gate_ref[...] * _dot(a_ref[...], w_ref[...])


def matmul_res(a, w, x, gate, B, T):
    M, K = a.shape
    N = w.shape[1]
    bm, bn = min(1024, M), min(512, N)
    gate_arr, gate_spec = _gate_operand(gate, B, T, bm, bn)
    return pl.pallas_call(
        _matmul_res_kernel,
        grid=(M // bm, N // bn),
        in_specs=[pl.BlockSpec((bm, K), lambda i, j: (i, 0)),
                  pl.BlockSpec((K, bn), lambda i, j: (0, j)),
                  pl.BlockSpec((bm, bn), lambda i, j: (i, j)), gate_spec],
        out_specs=pl.BlockSpec((bm, bn), lambda i, j: (i, j)),
        out_shape=jax.ShapeDtypeStruct((M, N), F32),
        compiler_params=_params(("parallel", "arbitrary")),
        name="matmul_res",
    )(a, w, x, gate_arr)


def _mlp_kernel(x_ref, g_ref, sh_ref, sc_ref, gate_ref, w1_ref, w2_ref, fg_ref, o_ref, h_scr, acc, *, final):
    f = pl.program_id(1)
    bm = x_ref.shape[0]

    def rows_of(ref, rs):
        return ref[...] if ref.shape[0] == 1 else ref[rs, :]

    def mlp(h):
        return _dot(jnp.square(jnp.maximum(_dot(h, w1_ref[...]), 0.0)).astype(BF16), w2_ref[...])

    @pl.when(f == 0)
    def _():
        parts = 4 if bm % 64 == 0 else 1
        for c in range(parts):
            rs = slice(c * bm // parts, (c + 1) * bm // parts)
            h = (_rms(x_ref[rs, :], g_ref[...]) * (1.0 + rows_of(sc_ref, rs)) + rows_of(sh_ref, rs)).astype(BF16)
            h_scr[rs, :] = h
            acc[rs, :] = mlp(h)

    @pl.when(f > 0)
    def _():
        acc[...] += mlp(h_scr[...])

    @pl.when(f == pl.num_programs(1) - 1)
    def _():
        y = x_ref[...] + gate_ref[...] * acc[...]
        o_ref[...] = _rms(y, fg_ref[...]) if final else y


def mlp_res(x, g, shift, scale, gate, w1, w2, layer, final_g, final, B, T):
    M, D = x.shape
    F = w1.shape[2]
    bm = 512 if T % 512 == 0 else min(256, M)
    bf = min(1024, F)
    (sh_arr, vec_spec), (sc_arr, _), (gate_arr, _) = (_gate_operand(v, B, T, bm, D) for v in (shift, scale, gate))
    return pl.pallas_call(
        functools.partial(_mlp_kernel, final=final),
        grid=(M // bm, F // bf),
        in_specs=[pl.BlockSpec((bm, D), lambda i, f: (i, 0)),
                  pl.BlockSpec((1, D), lambda i, f: (0, 0)), vec_spec, vec_spec, vec_spec,
                  pl.BlockSpec((None, D, bf), lambda i, f: (layer, 0, f)),
                  pl.BlockSpec((None, bf, D), lambda i, f: (layer, f, 0)),
                  pl.BlockSpec((1, D), lambda i, f: (0, 0))],
        out_specs=pl.BlockSpec((bm, D), lambda i, f: (i, 0)),
        out_shape=jax.ShapeDtypeStruct((M, D), F32),
        scratch_shapes=[pltpu.VMEM((bm, D), BF16), pltpu.VMEM((bm, D), F32)],
        compiler_params=_params(("parallel", "arbitrary")),
        name="mlp_res",
    )(x, g.reshape(1, D), sh_arr, sc_arr, gate_arr, w1, w2, final_g.reshape(1, D))


def _rope_pair(t, tab):
    prod = t * tab
    return (prod + pltpu.roll(prod, ROPE_DIM, 1))[:, :ROPE_DIM]


def _latent_kernel(x_ref, gx_ref, sh_ref, sc_ref, w_ref, g_ref, tab_ref, ckv_ref, kpe_ref):
    h = (_rms(x_ref[...], gx_ref[...]) * (1.0 + sc_ref[...]) + sh_ref[...]).astype(BF16)
    acc = _dot(h, w_ref[...])
    R = g_ref.shape[1]
    ckv_ref[...] = _rms(acc[:, :R], g_ref[...])
    kpe_ref[...] = _rope_pair(acc[:, R:R + 2 * ROPE_DIM], tab_ref[...])


def latent(x, gx, shift, scale, w, g, tab):
    B, T, D = x.shape
    R = g.shape[0]
    bt = min(512, T)
    vec = pl.BlockSpec((None, 1, D), lambda b, t: (b, 0, 0))
    return pl.pallas_call(
        _latent_kernel,
        grid=(B, T // bt),
        in_specs=[pl.BlockSpec((None, bt, D), lambda b, t: (b, t, 0)),
                  pl.BlockSpec((1, D), lambda b, t: (0, 0)), vec, vec,
                  pl.BlockSpec(w.shape, lambda b, t: (0, 0)),
                  pl.BlockSpec((1, R), lambda b, t: (0, 0)),
                  pl.BlockSpec((bt, 2 * ROPE_DIM), lambda b, t: (t, 0))],
        out_specs=[pl.BlockSpec((None, bt, R), lambda b, t: (b, t, 0)),
                   pl.BlockSpec((None, bt, ROPE_DIM), lambda b, t: (b, t, 0))],
        out_shape=[jax.ShapeDtypeStruct((B, T, R), F32), jax.ShapeDtypeStruct((B, T, ROPE_DIM), F32)],
        compiler_params=_params(("parallel", "parallel")),
        name="latent",
    )(x, gx.reshape(1, D), shift.reshape(B, 1, D), scale.reshape(B, 1, D), w, g.reshape(1, R), tab)


def _kv_expand_kernel(c_ref, pe_ref, wuk_ref, wuv_ref, k_ref, vt_ref):
    c = c_ref[...].astype(BF16)
    kn = _dot(c, wuk_ref[...])
    vv = _dot(c, wuv_ref[...])
    pe = pe_ref[...].astype(BF16)
    for h in range(k_ref.shape[0]):
        k_ref[h, :, 0:NOPE_DIM] = kn[:, h * NOPE_DIM:(h + 1) * NOPE_DIM].astype(BF16)
        k_ref[h, :, NOPE_DIM:NOPE_DIM + ROPE_DIM] = pe
        vt_ref[h, 0:V_DIM, :] = vv[:, h * V_DIM:(h + 1) * V_DIM].T.astype(BF16)
        vt_ref[h, V_DIM:V_ROWS, :] = jnp.ones((V_ROWS - V_DIM, c.shape[0]), BF16)


def kv_expand(ckv, kpe, wuk, wuv, H):
    B, S, R = ckv.shape
    bt = min(512, S)
    hb = min(8, H)
    DK = NOPE_DIM + ROPE_DIM
    return pl.pallas_call(
        _kv_expand_kernel,
        grid=(H // hb, B, S // bt),
        in_specs=[pl.BlockSpec((None, bt, R), lambda j, b, t: (b, t, 0)),
                  pl.BlockSpec((None, bt, ROPE_DIM), lambda j, b, t: (b, t, 0)),
                  pl.BlockSpec((R, hb * NOPE_DIM), lambda j, b, t: (0, j)),
                  pl.BlockSpec((R, hb * V_DIM), lambda j, b, t: (0, j))],
        out_specs=[pl.BlockSpec((None, hb, bt, DK), lambda j, b, t: (b, j, t, 0)),
                   pl.BlockSpec((None, hb, V_ROWS, bt), lambda j, b, t: (b, j, 0, t))],
        out_shape=[jax.ShapeDtypeStruct((B, H, S, DK), BF16), jax.ShapeDtypeStruct((B, H, V_ROWS, S), BF16)],
        compiler_params=_params(("parallel", "parallel", "arbitrary")),
        name="kv_expand",
    )(ckv, kpe, wuk, wuv)


def _wdq_kernel(x_ref, gx_ref, sh_ref, sc_ref, w_ref, g_ref, o_ref):
    h = (_rms(x_ref[...], gx_ref[...]) * (1.0 + sc_ref[...]) + sh_ref[...]).astype(BF16)
    o_ref[...] = _rms(_dot(h, w_ref[...]), g_ref[...]).astype(BF16)


def wdq_norm(x, gx, shift, scale, w, g):
    B, T, D = x.shape
    R = w.shape[1]
    bt = min(1024, T)
    vec = pl.BlockSpec((None, 1, D), lambda b, t: (b, 0, 0))
    return pl.pallas_call(
        _wdq_kernel,
        grid=(B, T // bt),
        in_specs=[pl.BlockSpec((None, bt, D), lambda b, t: (b, t, 0)),
                  pl.BlockSpec((1, D), lambda b, t: (0, 0)), vec, vec,
                  pl.BlockSpec((D, R), lambda b, t: (0, 0)),
                  pl.BlockSpec((1, R), lambda b, t: (0, 0))],
        out_specs=pl.BlockSpec((None, bt, R), lambda b, t: (b, t, 0)),
        out_shape=jax.ShapeDtypeStruct((B, T, R), BF16),
        compiler_params=_params(("parallel", "parallel")),
        name="wdq_norm",
    )(x, gx.reshape(1, D), shift.reshape(B, 1, D), scale.reshape(B, 1, D), w, g.reshape(1, R))


def _wuq_kernel(c_ref, w_ref, tab_ref, q_ref, *, scale):
    acc = _dot(c_ref[...], w_ref[...])
    tab = tab_ref[...]
    W = NOPE_DIM + 2 * ROPE_DIM
    for h in range(q_ref.shape[0]):
        q_ref[h, :, 0:NOPE_DIM] = (acc[:, h * W:h * W + NOPE_DIM] * scale).astype(BF16)
        pe = _rope_pair(acc[:, h * W + NOPE_DIM:(h + 1) * W], tab)
        q_ref[h, :, NOPE_DIM:NOPE_DIM + ROPE_DIM] = (pe * scale).astype(BF16)
        if q_ref.shape[2] > NOPE_DIM + ROPE_DIM:
            pad = q_ref.shape[2] - NOPE_DIM - ROPE_DIM
            q_ref[h, :, NOPE_DIM + ROPE_DIM:] = jnp.zeros((q_ref.shape[1], pad), BF16)


def wuq_rope(cq, w, tab, H, head_major):
    B, T, R = cq.shape
    bt = min(512, T)
    nt = T // bt
    hb = min(8, H)
    W = NOPE_DIM + 2 * ROPE_DIM
    DK = NOPE_DIM + ROPE_DIM
    if head_major:
        out_spec = pl.BlockSpec((hb, bt, DK), lambda j, b, t: (j, b * nt + t, 0))
        out_shape = (H, B * T, DK)
    else:
        out_spec = pl.BlockSpec((None, hb, bt, 2 * LANES), lambda j, b, t: (b, j, t, 0))
        out_shape = (B, H, T, 2 * LANES)
    return pl.pallas_call(
        functools.partial(_wuq_kernel, scale=MLA_SCALE if head_major else MLA_SCALE * LOG2E),
        grid=(H // hb, B, nt),
        in_specs=[pl.BlockSpec((None, bt, R), lambda j, b, t: (b, t, 0)),
                  pl.BlockSpec((R, hb * W), lambda j, b, t: (0, j)),
                  pl.BlockSpec((bt, 2 * ROPE_DIM), lambda j, b, t: (t, 0))],
        out_specs=out_spec,
        out_shape=jax.ShapeDtypeStruct(out_shape, BF16),
        compiler_params=_params(("parallel", "parallel", "parallel")),
        name="wuq_rope",
    )(cq, w, tab)


def _visible(qpos, kpos):
    return _blk(kpos, CHUNK) <= _blk(qpos, CHUNK)


def _flash_kernel(q_ref, k_ref, vt_ref, o_ref, qt_scr, s_a, s_b, m_scr, acc_scr, *, tile):
    hb = q_ref.shape[0]
    ha = hb // 2
    first, second = range(0, ha), range(ha, hb)
    qi = pl.program_id(2)
    qpos = qi * tile + lax.broadcasted_iota(jnp.int32, (1, tile), 1)
    m_scr[...] = jnp.full(m_scr.shape, -jnp.inf, F32)
    acc_scr[...] = jnp.zeros(acc_scr.shape, F32)
    DK = NOPE_DIM + ROPE_DIM
    for h in range(hb):
        qf = q_ref[h].astype(F32)
        qt_scr[h, 0:LANES, :] = qf[:, 0:LANES].T.astype(BF16)
        qt_scr[h, LANES:DK, :] = qf[:, LANES:2 * LANES].T[0:DK - LANES, :].astype(BF16)

    def scores(ki, heads, s_ref):
        start = pl.multiple_of(ki * tile, tile)
        for h in heads:
            s_ref[h - heads[0]] = _dot(k_ref[h, pl.ds(start, tile), :], qt_scr[h])

    def softmax_pv(ki, heads, s_ref, masked):
        start = pl.multiple_of(ki * tile, tile)
        for h in heads:
            s = s_ref[h - heads[0]]
            if masked:
                kpos = ki * tile + lax.broadcasted_iota(jnp.int32, (tile, 1), 0)
                s = jnp.where(_visible(qpos, kpos), s, -jnp.inf)
            m = m_scr[h]
            m_new = jnp.maximum(m, jnp.max(s, axis=0, keepdims=True))
            p = jnp.exp2(s - m_new).astype(BF16)
            acc_scr[h] = jnp.exp2(m - m_new) * acc_scr[h] + _dot(vt_ref[h, :, pl.ds(start, tile)], p)
            m_scr[h] = m_new

    def body(ki, carry):
        scores(ki, second, s_b)
        softmax_pv(ki, first, s_a, False)
        scores(ki + 1, first, s_a)
        softmax_pv(ki, second, s_b, False)
        return carry

    scores(0, first, s_a)
    lax.fori_loop(0, qi, body, 0)
    scores(qi, second, s_b)
    softmax_pv(qi, first, s_a, True)
    softmax_pv(qi, second, s_b, True)
    for h in range(hb):
        acc = acc_scr[h]
        inv_l = 1.0 / acc[V_DIM:V_DIM + 1, :]
        o_ref[:, h * V_DIM:(h + 1) * V_DIM] = (acc[0:V_DIM, :] * inv_l).T.astype(o_ref.dtype)


def flash_prompt(q, k, vt):
    B, H, T, QW = q.shape
    DK = k.shape[3]
    tile = min(256, T)
    assert tile % CHUNK == 0 and T % tile == 0 and H % 2 == 0 and QW == 2 * LANES
    hb = next(n for n in (8, 4, 2) if H % n == 0)
    return pl.pallas_call(
        functools.partial(_flash_kernel, tile=tile),
        grid=(B, H // hb, T // tile),
        in_specs=[pl.BlockSpec((None, hb, tile, QW), lambda b, j, i: (b, j, i, 0)),
                  pl.BlockSpec((None, hb, T, DK), lambda b, j, i: (b, j, 0, 0)),
                  pl.BlockSpec((None, hb, V_ROWS, T), lambda b, j, i: (b, j, 0, 0))],
        out_specs=pl.BlockSpec((None, tile, hb * V_DIM), lambda b, j, i: (b, i, j)),
        out_shape=jax.ShapeDtypeStruct((B, T, H * V_DIM), BF16),
        scratch_shapes=[pltpu.VMEM((hb, DK, tile), BF16),
                        pltpu.VMEM((hb // 2, tile, tile), F32), pltpu.VMEM((hb // 2, tile, tile), F32),
                        pltpu.VMEM((hb, 1, tile), F32), pltpu.VMEM((hb, V_ROWS, tile), F32)],
        compiler_params=_params(("parallel", "parallel", "arbitrary")),
        name="flash_prompt",
    )(q, k, vt)


def _q_absorb_kernel(q_ref, wuk_ref, o_ref):
    o_ref[...] = _dot_nt(q_ref[:, 0:NOPE_DIM], wuk_ref[...]).astype(BF16)


def _attn_latent_kernel(ql_ref, q_ref, qpos_ref, c_ref, pe_ref, o_ref):
    H, T, R = ql_ref.shape
    S = c_ref.shape[0]
    c = c_ref[...].astype(BF16)
    s = (_dot_nt(ql_ref[...].reshape(H * T, R), c)
         + _dot_nt(q_ref[:, :, NOPE_DIM:NOPE_DIM + ROPE_DIM].reshape(H * T, ROPE_DIM), pe_ref[...].astype(BF16)))
    kpos = lax.broadcasted_iota(jnp.int32, (1, S), 1)
    s = jnp.where(_visible(qpos_ref[...], kpos), s, -jnp.inf)
    p = jnp.exp(s - jnp.max(s, axis=-1, keepdims=True))
    l = jnp.sum(p, axis=-1, keepdims=True)
    o_ref[...] = (_dot(p.astype(BF16), c) / l).astype(BF16).reshape(H, T, R)


def _o_expand_kernel(o_ref, wuv_ref, out_ref):
    out_ref[...] = _dot(o_ref[...], wuv_ref[...]).astype(BF16)


def attn_latent(q, ckv, kpe, wuk, wuv, B, T, q0):
    H, M, DK = q.shape
    S, R = ckv.shape[1:]
    q_lat = pl.pallas_call(
        _q_absorb_kernel,
        grid=(H,),
        in_specs=[pl.BlockSpec((None, M, DK), lambda h: (h, 0, 0)),
                  pl.BlockSpec((R, NOPE_DIM), lambda h: (0, h))],
        out_specs=pl.BlockSpec((None, M, R), lambda h: (h, 0, 0)),
        out_shape=jax.ShapeDtypeStruct((H, M, R), BF16),
        compiler_params=_params(("parallel",)),
        name="q_absorb",
    )(q, wuk)
    qpos = jnp.tile(q0 + jnp.arange(T, dtype=jnp.int32), H)[:, None]
    o_lat = pl.pallas_call(
        _attn_latent_kernel,
        grid=(B,),
        in_specs=[pl.BlockSpec((H, T, R), lambda b: (0, b, 0)),
                  pl.BlockSpec((H, T, DK), lambda b: (0, b, 0)),
                  pl.BlockSpec((H * T, 1), lambda b: (0, 0)),
                  pl.BlockSpec((None, S, R), lambda b: (b, 0, 0)),
                  pl.BlockSpec((None, S, ROPE_DIM), lambda b: (b, 0, 0))],
        out_specs=pl.BlockSpec((H, T, R), lambda b: (0, b, 0)),
        out_shape=jax.ShapeDtypeStruct((H, M, R), BF16),
        compiler_params=_params(("parallel",)),
        name="attn_latent",
    )(q_lat, q, qpos, ckv, kpe)
    return pl.pallas_call(
        _o_expand_kernel,
        grid=(H,),
        in_specs=[pl.BlockSpec((None, M, R), lambda h: (h, 0, 0)),
                  pl.BlockSpec((R, V_DIM), lambda h: (0, h))],
        out_specs=pl.BlockSpec((M, V_DIM), lambda h: (0, h)),
        out_shape=jax.ShapeDtypeStruct((M, H * V_DIM), BF16),
        compiler_params=_params(("parallel",)),
        name="o_expand",
    )(o_lat, wuv)


def _pad_cols(w, n):
    return jnp.pad(w, ((0, 0), (0, n - w.shape[1])))


def _pad_rows(w, n):
    return jnp.pad(w, ((0, n - w.shape[0]), (0, 0)))


def _rotate_half_cols(w):
    half = ROPE_DIM // 2
    return jnp.concatenate([-w[..., half:], w[..., :half]], axis=-1)


def _rope_table(pos):
    half = ROPE_DIM // 2
    inv = ROPE_THETA ** (-jnp.arange(half, dtype=F32) / half)
    ang = pos.astype(F32)[:, None] * inv[None, :]
    cos, sin = jnp.cos(ang), jnp.sin(ang)
    return jnp.concatenate([cos, cos, sin, sin], axis=-1)


def _block_diag_states(s):
    B, H, N, _ = s.shape
    s = s.reshape(B, H // 2, 2, N, N)
    z = jnp.zeros_like(s[:, :, 0])
    top = jnp.concatenate([s[:, :, 0], z], axis=-1)
    bot = jnp.concatenate([z, s[:, :, 1]], axis=-1)
    return jnp.concatenate([top, bot], axis=-2)


def _diag_states(s):
    B, HP = s.shape[:2]
    N = RW_HEAD
    return jnp.stack([s[:, :, :N, :N], s[:, :, N:, N:]], axis=2).reshape(B, 2 * HP, N, N)


def _prepare(W):
    D = W['rw_wr'].shape[1]
    P = {}
    bf = lambda a: a.astype(BF16)
    P['mlp_w1'], P['mlp_w2'] = bf(W['mlp_w1']), bf(W['mlp_w2'])
    for n in ('rw_wr', 'rw_wk', 'rw_wv', 'rw_wo', 'rw_g1', 'rw_g2', 'mla_wdq', 'mla_wo'):
        P[n] = [bf(W[n][l]) for l in range(W[n].shape[0])]
    NA = W['rw_wr'].shape[0]
    P['rw_w1'] = [bf(_pad_cols(W['rw_w1'][l], LORA_PAD)) for l in range(NA)]
    P['rw_w2'] = [bf(_pad_rows(W['rw_w2'][l], LORA_PAD)) for l in range(NA)]
    P['rw_a1'] = [bf(_pad_cols(W['rw_a1'][l], LORA_PAD)) for l in range(NA)]
    P['rw_a2'] = [bf(_pad_rows(W['rw_a2'][l], LORA_PAD)) for l in range(NA)]
    P['rw_v1'] = [bf(_pad_cols(W['rw_v1'][l], LORA_PAD)) for l in range(NA - 1)]
    P['rw_v2'] = [bf(_pad_rows(W['rw_v2'][l], LORA_PAD)) for l in range(NA - 1)]
    zeros = jnp.zeros((D,), F32)
    P['proj_vecs'], P['recur_vecs'] = [], []
    for l in range(NA):
        v0 = W['rw_v0'][l - 1] if l > 0 else zeros
        P['proj_vecs'].append(jnp.stack([W['rw_w0'][l], W['rw_a0'][l], v0, W['rw_kk'][l], W['rw_ka'][l],
                                         zeros, zeros, zeros]))
        P['recur_vecs'].append(jnp.stack([W['rw_rk'][l], W['rw_lnx_g'][l], W['rw_lnx_b'][l]] + [zeros] * 5))
    HL = min(512, D)
    head = jnp.arange(HL) // RW_HEAD
    P['ones_bd'] = (head[:, None] == head[None, :]).astype(BF16)
    R = W['kv_lat_g'].shape[0]
    wd = W['kv_wd']
    P['kv_wd'] = bf(jnp.concatenate([wd, _rotate_half_cols(wd[:, R:])], axis=1))
    H = W['kv_wuk'].shape[1]
    P['kv_wuk'] = bf(W['kv_wuk'].reshape(R, H * NOPE_DIM))
    P['kv_wuv'] = bf(W['kv_wuv'].reshape(R, H * V_DIM))
    NB, Q = W['mla_wuq'].shape[:2]
    wuq = W['mla_wuq'].reshape(NB, Q, H, NOPE_DIM + ROPE_DIM)
    pe = wuq[..., NOPE_DIM:]
    P['mla_wuq'] = bf(jnp.concatenate([wuq, _rotate_half_cols(pe)], axis=-1).reshape(NB, Q, -1))
    return P


def _trunk(x, mod, kv_mod, pos0, h_prev, s0, past_ckv, past_kpe, W, P):
    B, T, D = x.shape
    M = B * T
    depth = W['ada_w'].shape[0]
    NA = W['rw_wr'].shape[0]
    H = W['kv_wuk'].shape[1]
    tab = _rope_table(pos0 + jnp.arange(T))
    xf = x.reshape(M, D)
    shifts, states = [], []
    v_first = None
    keys = vals = ckv = kpe = None
    for l in range(depth):
        m = mod[l]
        if l < NA:
            has_v = l > 0
            pre = rwkv_pre(xf.reshape(B, T, D), W['norm_mix_g'][l], m[:, 0], m[:, 1], h_prev[l], W['rw_mu'][l],
                           P['rw_w1'][l], P['rw_a1'][l], P['rw_g1'][l], P['rw_v1'][l - 1] if has_v else None)
            shifts.append(pre[-1].reshape(B, D))
            acts = [a.reshape(M, a.shape[-1]) for a in pre[:-1]]
            res = rwkv_mix(acts, v_first, P['rw_wr'][l], P['rw_wk'][l], P['rw_wv'][l], P['rw_w2'][l],
                           P['rw_a2'][l], P['rw_g2'][l], P['rw_v2'][l - 1] if has_v else None,
                           P['proj_vecs'][l], P['recur_vecs'][l], _block_diag_states(s0[l]), B, T)
            z, s_fin = res[0], res[1]
            if not has_v:
                v_first = res[2]
            states.append(_diag_states(s_fin))
            xf = matmul_res(z, P['rw_wo'][l], xf, m[:, 2], B, T)
        else:
            j = l - NA
            cq = wdq_norm(xf.reshape(B, T, D), W['norm_mix_g'][l], m[:, 0], m[:, 1], P['mla_wdq'][j], W['mla_q_g'][j])
            if past_ckv is None:
                q = wuq_rope(cq, P['mla_wuq'][j], tab, H, False)
                o = flash_prompt(q, keys, vals).reshape(M, H * V_DIM)
            else:
                q = wuq_rope(cq, P['mla_wuq'][j], tab, H, True)
                o = attn_latent(q, keys, vals, P['kv_wuk'], P['kv_wuv'], B, T, pos0)
            xf = matmul_res(o, P['mla_wo'][j], xf, m[:, 2], B, T)
        xf = mlp_res(xf, W['norm_mlp_g'][l], m[:, 3], m[:, 4], m[:, 5], P['mlp_w1'], P['mlp_w2'], l,
                     W['final_g'], l == depth - 1, B, T)
        if l == NA - 1:
            ckv, kpe = latent(xf.reshape(B, T, D), W['kv_norm_g'], kv_mod[:, 0], kv_mod[:, 1],
                              P['kv_wd'], W['kv_lat_g'], tab)
            if past_ckv is None:
                keys, vals = kv_expand(ckv, kpe, P['kv_wuk'], P['kv_wuv'], H)
            else:
                keys = jnp.concatenate([past_ckv, ckv], axis=1)
                vals = jnp.concatenate([past_kpe, kpe], axis=1)
    return xf.reshape(B, T, D), ckv, kpe, jnp.stack(states), jnp.stack(shifts)


def kernel(x_prompt, x_sample, cache_ckv, cache_kpe, state_wkv, state_shift, c_prompt, c_sample, ada_w, ada_b, norm_mix_g, norm_mlp_g, mlp_w1, mlp_w2, rw_mu, rw_w0, rw_w1, rw_w2, rw_a0, rw_a1, rw_a2, rw_v0, rw_v1, rw_v2, rw_g1, rw_g2, rw_wr, rw_wk, rw_wv, rw_wo, rw_kk, rw_ka, rw_rk, rw_lnx_g, rw_lnx_b, kv_ada_w, kv_ada_b, kv_norm_g, kv_wd, kv_lat_g, kv_wuk, kv_wuv, mla_wdq, mla_q_g, mla_wuq, mla_wo, final_g):
    W = dict(ada_w=ada_w, ada_b=ada_b, norm_mix_g=norm_mix_g, norm_mlp_g=norm_mlp_g,
             mlp_w1=mlp_w1, mlp_w2=mlp_w2, rw_mu=rw_mu, rw_w0=rw_w0, rw_w1=rw_w1, rw_w2=rw_w2,
             rw_a0=rw_a0, rw_a1=rw_a1, rw_a2=rw_a2, rw_v0=rw_v0, rw_v1=rw_v1, rw_v2=rw_v2,
             rw_g1=rw_g1, rw_g2=rw_g2, rw_wr=rw_wr, rw_wk=rw_wk, rw_wv=rw_wv, rw_wo=rw_wo,
             rw_kk=rw_kk, rw_ka=rw_ka, rw_rk=rw_rk, rw_lnx_g=rw_lnx_g, rw_lnx_b=rw_lnx_b,
             kv_ada_w=kv_ada_w, kv_ada_b=kv_ada_b, kv_norm_g=kv_norm_g, kv_wd=kv_wd,
             kv_lat_g=kv_lat_g, kv_wuk=kv_wuk, kv_wuv=kv_wuv, mla_wdq=mla_wdq, mla_q_g=mla_q_g,
             mla_wuq=mla_wuq, mla_wo=mla_wo, final_g=final_g)
    P = _prepare(W)
    Bp, Tp, D = x_prompt.shape
    Bs = x_sample.shape[0]
    depth = ada_w.shape[0]
    NA = rw_wr.shape[0]
    c_all = jnp.concatenate([c_prompt, c_sample], axis=0)
    mod = ada_linear(c_all, ada_w, ada_b).reshape(depth, Bp + Bs, N_MOD, D)
    kv_mod = ada_linear(c_all, kv_ada_w[None], kv_ada_b[None]).reshape(Bp + Bs, 2, D)
    h0 = jnp.zeros((NA, Bp, D), F32)
    s0 = jnp.zeros((NA, Bp, D // RW_HEAD, RW_HEAD, RW_HEAD), F32)
    out_p = _trunk(x_prompt, mod[:, :Bp], kv_mod[:Bp], 0, h0, s0, None, None, W, P)
    out_s = _trunk(x_sample, mod[:, Bp:], kv_mod[Bp:], cache_ckv.shape[1], state_shift, state_wkv,
                   cache_ckv, cache_kpe, W, P)
    return (out_p[0], out_s[0]) + out_p[1:] + out_s[1:]
```

```python
import functools

import jax
import jax.numpy as jnp
from jax import lax
from jax.experimental import pallas as pl
from jax.experimental.pallas import tpu as pltpu

F32, BF16 = jnp.float32, jnp.bfloat16

RW_HEAD = 64
CHUNK = 64
GN_EPS = 64e-5
NOPE_DIM = 128
ROPE_DIM = 64
V_DIM = 128
ROPE_THETA = 10000.0
MLA_SCALE = (NOPE_DIM + ROPE_DIM) ** -0.5
LOG2E = 1.4426950408889634
V_ROWS = V_DIM + 16
NORM_EPS = 1e-6
N_MOD = 6

LANES = 128
VMEM_LIMIT = 56 * 1024 * 1024
LORA_PAD = 128


def _params(sem, vmem=VMEM_LIMIT):
    return pltpu.CompilerParams(dimension_semantics=sem, vmem_limit_bytes=vmem)


def _dot(a, b):
    return jnp.dot(a, b, preferred_element_type=F32)


def _dot_nt(a, b):
    return lax.dot_general(a, b, (((1,), (1,)), ((), ())), preferred_element_type=F32)


def _dot_tn(a, b):
    return lax.dot_general(a, b, (((0,), (0,)), ((), ())), preferred_element_type=F32)


def _split2(x):
    hi = x.astype(BF16)
    return hi, (x - hi.astype(F32)).astype(BF16)


def _split3(x):
    hi = x.astype(BF16)
    r1 = x - hi.astype(F32)
    mid = r1.astype(BF16)
    return hi, mid, (r1 - mid.astype(F32)).astype(BF16)


def _sigmoid(x):
    return 1.0 / (1.0 + jnp.exp(-x))


def _blk(i, n):
    assert n & (n - 1) == 0
    return i >> (n.bit_length() - 1)


def _off(i, n):
    assert n & (n - 1) == 0
    return i & (n - 1)


def _lower_left(ri, cj, s):
    return (_blk(ri, 2 * s) == _blk(cj, 2 * s)) & (_off(ri, 2 * s) >= s) & (_off(cj, 2 * s) < s)


def _rms(x, g):
    return x * lax.rsqrt(jnp.mean(x * x, axis=-1, keepdims=True) + NORM_EPS) * g


def _ada_kernel(c_ref, w_ref, b_ref, o_ref):
    c = c_ref[...]
    cs = c * _sigmoid(c)
    o_ref[...] = _dot(cs.astype(BF16), w_ref[...].astype(BF16)) + b_ref[...]


def ada_linear(c, w, b):
    L, K, N = w.shape
    M = c.shape[0]
    bn = min(512, N)
    return pl.pallas_call(
        _ada_kernel,
        grid=(L, N // bn),
        in_specs=[pl.BlockSpec((M, K), lambda l, j: (0, 0)),
                  pl.BlockSpec((None, K, bn), lambda l, j: (l, 0, j)),
                  pl.BlockSpec((None, 1, bn), lambda l, j: (l, 0, j))],
        out_specs=pl.BlockSpec((None, M, bn), lambda l, j: (l, 0, j)),
        out_shape=jax.ShapeDtypeStruct((L, M, N), F32),
        compiler_params=_params(("parallel", "parallel")),
        name="ada_linear",
    )(c, w, b.reshape(L, 1, N))


def _rwkv_pre_kernel(*refs, has_v):
    if has_v:
        (x_ref, g_ref, sh_ref, sc_ref, hp_ref, mu_ref, w1_ref, a1_ref, g1_ref, v1_ref,
         xr_ref, xk_ref, xv_ref, tw_ref, av_ref, gg_ref, vv_ref, hl_ref, prev) = refs
    else:
        (x_ref, g_ref, sh_ref, sc_ref, hp_ref, mu_ref, w1_ref, a1_ref, g1_ref,
         xr_ref, xk_ref, xv_ref, tw_ref, av_ref, gg_ref, hl_ref, prev) = refs
    bt = x_ref.shape[0]
    h = _rms(x_ref[...], g_ref[...]) * (1.0 + sc_ref[...]) + sh_ref[...]

    @pl.when(pl.program_id(1) == 0)
    def _():
        prev[...] = hp_ref[...]

    row = lax.broadcasted_iota(jnp.int32, h.shape, 0)
    xx = jnp.where(row == 0, prev[...], pltpu.roll(h, 1, 0)) - h
    last = h[bt - 1:bt, :]
    prev[...] = last
    hl_ref[...] = last
    mu = mu_ref[...]

    def mix(i):
        return (h + xx * mu[i:i + 1, :]).astype(BF16)

    xr_ref[...] = mix(0)
    tw_ref[...] = jnp.tanh(_dot(mix(1), w1_ref[...])).astype(BF16)
    xk_ref[...] = mix(2)
    xv = mix(3)
    xv_ref[...] = xv
    if has_v:
        vv_ref[...] = _dot(xv, v1_ref[...]).astype(BF16)
    av_ref[...] = _dot(mix(4), a1_ref[...]).astype(BF16)
    gg_ref[...] = _sigmoid(_dot(mix(5), g1_ref[...])).astype(BF16)


def rwkv_pre(x, g, shift, scale, h_prev, mu, w1, a1, g1, v1):
    B, T, D = x.shape
    bt = min(256, T)
    has_v = v1 is not None
    row = lambda n: pl.BlockSpec((None, bt, n), lambda b, t: (b, t, 0))
    vec = pl.BlockSpec((None, 1, D), lambda b, t: (b, 0, 0))
    full = lambda a: pl.BlockSpec(a.shape, lambda b, t: (0, 0))
    lora = [w1, a1, g1] + ([v1] if has_v else [])
    outs = [(D, BF16)] * 3 + [(w1.shape[1], BF16), (a1.shape[1], BF16), (g1.shape[1], BF16)]
    if has_v:
        outs.append((v1.shape[1], BF16))
    res = pl.pallas_call(
        functools.partial(_rwkv_pre_kernel, has_v=has_v),
        grid=(B, T // bt),
        in_specs=[row(D), pl.BlockSpec((1, D), lambda b, t: (0, 0)), vec, vec, vec, full(mu)]
                 + [full(a) for a in lora],
        out_specs=[row(n) for n, _ in outs] + [vec],
        out_shape=[jax.ShapeDtypeStruct((B, T, n), dt) for n, dt in outs]
                  + [jax.ShapeDtypeStruct((B, 1, D), F32)],
        scratch_shapes=[pltpu.VMEM((1, D), F32)],
        compiler_params=_params(("parallel", "arbitrary")),
        name="rwkv_pre",
    )(x, g.reshape(1, D), shift.reshape(B, 1, D), scale.reshape(B, 1, D), h_prev.reshape(B, 1, D), mu, *lora)
    return res


def _rwkv_proj_kernel(*refs, has_v):
    if has_v:
        (xr_ref, xk_ref, xv_ref, tw_ref, av_ref, gg_ref, vv_ref, vf_ref,
         wr_ref, wk_ref, wv_ref, w2_ref, a2_ref, g2_ref, v2_ref, vec_ref, ones_ref,
         r_ref, ld_ref, k_ref, v_ref, kk_ref, b_ref, g_ref) = refs
    else:
        (xr_ref, xk_ref, xv_ref, tw_ref, av_ref, gg_ref,
         wr_ref, wk_ref, wv_ref, w2_ref, a2_ref, g2_ref, vec_ref, ones_ref,
         r_ref, ld_ref, k_ref, v_ref, kk_ref, b_ref, g_ref) = refs
    vec = vec_ref[...]
    w0, a0, v0, k_k, k_a = (vec[i:i + 1, :] for i in range(5))
    r_ref[...] = _dot(xr_ref[...], wr_ref[...])
    kraw = _dot(xk_ref[...], wk_ref[...])
    v = _dot(xv_ref[...], wv_ref[...])
    nz = -(w0 + _dot(tw_ref[...], w2_ref[...]))
    softplus = jnp.maximum(nz, 0.0) + jnp.log(1.0 + jnp.exp(-jnp.abs(nz)))
    ld_ref[...] = -jnp.exp(-softplus - 0.5)
    a = _sigmoid(a0 + _dot(av_ref[...], a2_ref[...]))
    if has_v:
        v = v + (vf_ref[...] - v) * _sigmoid(v0 + _dot(vv_ref[...], v2_ref[...]))
    v_ref[...] = v
    g_ref[...] = _dot(gg_ref[...], g2_ref[...])
    kk = kraw * k_k
    ss = _dot((kk * kk).astype(BF16), ones_ref[...])
    kk = kk / jnp.maximum(jnp.sqrt(ss), 1e-12)
    kk_ref[...] = kk
    b_ref[...] = kk * a
    k_ref[...] = kraw * (1.0 + (a - 1.0) * k_a)


def rwkv_proj(pre, v_first, wr, wk, wv, w2, a2, g2, v2, vecs, ones_bd):
    has_v = v2 is not None
    xr = pre[0]
    M, D = xr.shape
    HL = ones_bd.shape[0]
    bm = min(512, M)
    row = lambda a: pl.BlockSpec((bm, a.shape[1]), lambda i, j: (i, 0))
    tile = pl.BlockSpec((bm, HL), lambda i, j: (i, j))
    col = lambda a: pl.BlockSpec((a.shape[0], HL), lambda i, j: (0, j))
    acts = list(pre[:6]) + ([pre[6]] if has_v else [])
    weights = [wr, wk, wv, w2, a2, g2] + ([v2] if has_v else [])
    return pl.pallas_call(
        functools.partial(_rwkv_proj_kernel, has_v=has_v),
        grid=(M // bm, D // HL),
        in_specs=[row(a) for a in acts] + ([tile] if has_v else []) + [col(w) for w in weights]
                 + [col(vecs), pl.BlockSpec((HL, HL), lambda i, j: (0, 0))],
        out_specs=[tile] * 7,
        out_shape=[jax.ShapeDtypeStruct((M, D), F32)] * 7,
        compiler_params=_params(("parallel", "arbitrary")),
        name="rwkv_proj",
    )(*acts, *([v_first] if has_v else []), *weights, vecs, ones_bd)


def _recur_kernel(r_ref, ld_ref, k_ref, v_ref, kk_ref, b_ref, g_ref, s0_ref, vec_ref,
                  z_ref, sout_ref, s_scr, *, C, NC, HP):
    ci = pl.program_id(2)
    C2 = 2 * C
    RB = NC * C

    @pl.when(ci == 0)
    def _():
        s_scr[...] = s0_ref[...]

    ti = lax.broadcasted_iota(jnp.int32, (RB, RB), 0)
    tj = lax.broadcasted_iota(jnp.int32, (RB, RB), 1)
    tri = jnp.where((_blk(ti, C) == _blk(tj, C)) & (ti >= tj), 1.0, 0.0).astype(BF16)
    ld = ld_ref[...]
    cl = sum(_dot(tri, part) for part in _split3(ld))
    p_in = jnp.exp(cl)
    p_inv = jnp.exp(-cl)
    r, k, v, b = r_ref[...], k_ref[...], v_ref[...], b_ref[...]
    a_t = -(jnp.exp(cl - ld) * kk_ref[...])
    r_t = p_in * r
    b_t = p_inv * b
    k_t = p_inv * k
    rows = [slice(c * C, (c + 1) * C) for c in range(NC)]
    cl_end = [cl[(c + 1) * C - 1:(c + 1) * C, :] for c in range(NC)]
    p_rem = [jnp.exp(cl_end[c] - cl[rows[c], :]) for c in range(NC)]
    p_end = [jnp.exp(cl_end[c]) for c in range(NC)]
    b_h = [p_rem[c] * b[rows[c], :] for c in range(NC)]
    k_h = [p_rem[c] * k[rows[c], :] for c in range(NC)]
    vec = vec_ref[...]
    r_k, lnx_g, lnx_b = (vec[i:i + 1, :] for i in range(3))
    rk = r * k * r_k
    g = g_ref[...]

    lane = lax.broadcasted_iota(jnp.int32, (1, LANES), 1)
    first = lane < RW_HEAD

    def stack(x):
        return jnp.concatenate([jnp.where(first, x, 0.0), jnp.where(first, 0.0, x)], axis=0).astype(BF16)

    def fold(x):
        return x[0:C, :] + x[C:C2, :]

    ri = lax.broadcasted_iota(jnp.int32, (C2, C2), 0)
    cj = lax.broadcasted_iota(jnp.int32, (C2, C2), 1)
    same = _blk(ri, C) == _blk(cj, C)
    strict = same & (_off(ri, C) > _off(cj, C))
    incl = same & (_off(ri, C) >= _off(cj, C))
    eye = jnp.where(ri == cj, 1.0, 0.0)
    same_head = (_blk(lax.broadcasted_iota(jnp.int32, (LANES, LANES), 0), RW_HEAD)
                 == _blk(lax.broadcasted_iota(jnp.int32, (LANES, LANES), 1), RW_HEAD))
    ones_bd = jnp.where(same_head, 1.0, 0.0).astype(BF16)

    lanes = [slice(p * LANES, (p + 1) * LANES) for p in range(HP)]
    probs = [(c, p) for c in range(NC) for p in range(HP)]
    a_b = {cp: a_t[rows[cp[0]], lanes[cp[1]]].astype(BF16) for cp in probs}
    r_b = {cp: r_t[rows[cp[0]], lanes[cp[1]]].astype(BF16) for cp in probs}
    v_st = {cp: stack(v[rows[cp[0]], lanes[cp[1]]]) for cp in probs}
    low, rb_f, ak_f, rk_f = {}, {}, {}, {}
    if C2 == LANES:
        t_f64 = lax.broadcasted_iota(jnp.int32, (C, LANES), 0)
        s_f64 = _off(lax.broadcasted_iota(jnp.int32, (C, LANES), 1), C)
        strict_f, incl_f = t_f64 > s_f64, t_f64 >= s_f64
        for c, p in probs:
            q_st = jnp.concatenate([stack(a_t[rows[c], lanes[p]]), stack(r_t[rows[c], lanes[p]])], axis=0)
            w_st = jnp.concatenate([b_t[rows[c], lanes[p]], k_t[rows[c], lanes[p]]], axis=0).astype(BF16)
            gm = _dot_nt(q_st, w_st)
            gr = pltpu.roll(gm, C, 1)
            a0, a1, r0, r1 = (gm[i * C:(i + 1) * C, :] for i in range(4))
            a0r, a1r, r0r, r1r = (gr[i * C:(i + 1) * C, :] for i in range(4))
            low[c, p] = jnp.concatenate([jnp.where(first & strict_f, a0, 0.0),
                                         jnp.where(strict_f & ~first, a1r, 0.0)], axis=0)
            ak_f[c, p] = jnp.where(strict_f, jnp.where(first, a0r, a1), 0.0).astype(BF16)
            rb_f[c, p] = jnp.where(incl_f, jnp.where(first, r0, r1r), 0.0).astype(BF16)
            rk_f[c, p] = jnp.where(incl_f, jnp.where(first, r0r, r1), 0.0).astype(BF16)
    else:
        for c, p in probs:
            q_st = jnp.concatenate([stack(a_t[rows[c], lanes[p]]), stack(r_t[rows[c], lanes[p]])], axis=0)
            b_c = b_t[rows[c], lanes[p]].astype(BF16)
            k_c = k_t[rows[c], lanes[p]].astype(BF16)
            gb = _dot_nt(q_st, jnp.concatenate([b_c, b_c], axis=0))
            gk = _dot_nt(q_st, jnp.concatenate([k_c, k_c], axis=0))
            low[c, p] = jnp.where(strict, gb[0:C2, :], 0.0)
            rb_f[c, p] = fold(jnp.where(incl, gb[C2:, :], 0.0)).astype(BF16)
            ak_f[c, p] = fold(jnp.where(strict, gk[0:C2, :], 0.0)).astype(BF16)
            rk_f[c, p] = fold(jnp.where(incl, gk[C2:, :], 0.0)).astype(BF16)
    akv = {cp: _dot(ak_f[cp], v_st[cp]) for cp in probs}
    rkv = {cp: _dot(rk_f[cp], v_st[cp]) for cp in probs}

    first_level = _lower_left(ri, cj, 1)
    t_inv = {cp: eye + jnp.where(first_level, low[cp], 0.0) for cp in probs}
    s = 2
    while s < C:
        sel = _lower_left(ri, cj, s)
        tb = {cp: t_inv[cp].astype(BF16) for cp in probs}
        mid = {cp: _dot(tb[cp], jnp.where(sel, low[cp], 0.0).astype(BF16)).astype(BF16) for cp in probs}
        t_inv = {cp: t_inv[cp] + _dot(mid[cp], tb[cp]) for cp in probs}
        s *= 2
    t_f = {cp: fold(t_inv[cp]).astype(BF16) for cp in probs}

    state = [s_scr[p] for p in range(HP)]
    y = {}
    for c in range(NC):
        state_b = [st.astype(BF16) for st in state]
        x = [_dot_nt(a_b[c, p], state_b[p]) + akv[c, p] for p in range(HP)]
        y0 = [_dot_nt(r_b[c, p], state_b[p]) + rkv[c, p] for p in range(HP)]
        u = [_dot(t_f[c, p], stack(x[p])) for p in range(HP)]
        for p in range(HP):
            y[c, p] = y0[p] + _dot(rb_f[c, p], stack(u[p]))
        upd = []
        for p in range(HP):
            uv = jnp.concatenate([u[p], v[rows[c], lanes[p]]], axis=0).astype(BF16)
            bk = jnp.concatenate([b_h[c][:, lanes[p]], k_h[c][:, lanes[p]]], axis=0).astype(BF16)
            upd.append(_dot_tn(uv, bk))
        state = [state[p] * p_end[c][:, lanes[p]] + jnp.where(same_head, upd[p], 0.0) for p in range(HP)]
    for p in range(HP):
        s_scr[p] = state[p]

    def seg(t, slices):
        parts = _split2(t) if slices == 2 else (t.astype(BF16),)
        return sum(_dot(part, ones_bd) for part in parts)

    y_all = [jnp.concatenate([y[c, p] for c in range(NC)], axis=0) for p in range(HP)]
    mean = [seg(y_all[p], 2) * (1.0 / RW_HEAD) for p in range(HP)]
    bonus = [seg(rk[:, lanes[p]], 1) * v[:, lanes[p]] for p in range(HP)]
    d = [y_all[p] - mean[p] for p in range(HP)]
    var = [seg(d[p] * d[p], 1) * (1.0 / RW_HEAD) for p in range(HP)]
    for p in range(HP):
        yn = d[p] * lax.rsqrt(var[p] + GN_EPS) * lnx_g[:, lanes[p]] + lnx_b[:, lanes[p]]
        z_ref[:, lanes[p]] = ((yn + bonus[p]) * g[:, lanes[p]]).astype(z_ref.dtype)

    @pl.when(ci == pl.num_programs(2) - 1)
    def _():
        sout_ref[...] = s_scr[...]


def rwkv_recur(ops, s0_bd, vecs, B, T):
    D = ops[0].shape[1]
    HL = min(1024, D)
    HP = HL // LANES
    C = min(CHUNK, T)
    NC = 2 if T % (2 * C) == 0 else 1
    ops3 = [o.reshape(B, T, D) for o in ops]
    tile = pl.BlockSpec((None, NC * C, HL), lambda b, j, c: (b, c, j))
    st = pl.BlockSpec((None, HP, LANES, LANES), lambda b, j, c: (b, j, 0, 0))
    z, s_fin = pl.pallas_call(
        functools.partial(_recur_kernel, C=C, NC=NC, HP=HP),
        grid=(B, D // HL, T // (NC * C)),
        in_specs=[tile] * 7 + [st, pl.BlockSpec((vecs.shape[0], HL), lambda b, j, c: (0, j))],
        out_specs=[tile, st],
        out_shape=[jax.ShapeDtypeStruct((B, T, D), BF16), jax.ShapeDtypeStruct(s0_bd.shape, F32)],
        scratch_shapes=[pltpu.VMEM((HP, LANES, LANES), F32)],
        compiler_params=_params(("parallel", "parallel", "arbitrary")),
        name="rwkv_recur",
    )(*ops3, s0_bd, vecs)
    return z.reshape(B * T, D), s_fin


def _rwkv_mix_kernel(*refs, C, NC, HP, nT, has_v):
    it = iter(refs)
    xr_ref, xk_ref, xv_ref, tw_ref, av_ref, gg_ref = (next(it) for _ in range(6))
    vv_ref, vf_ref = (next(it), next(it)) if has_v else (None, None)
    wr_ref, wk_ref, wv_ref, w2_ref, a2_ref, g2_ref = (next(it) for _ in range(6))
    v2_ref = next(it) if has_v else None
    pvec_ref, rvec_ref, s0_ref, z_ref, sout_ref = (next(it) for _ in range(5))
    vout_ref = None if has_v else next(it)
    ops_scr, s_scr = next(it), next(it)
    step = pl.program_id(1)
    C2, RB = 2 * C, NC * C

    @pl.when(step == 0)
    def _():
        ops_scr[...] = jnp.zeros(ops_scr.shape, F32)

    r, ld, k, v, kk, b, g = (ops_scr[i] for i in range(7))
    first_of_batch = _off(jnp.maximum(step - 1, 0), nT) == 0
    state = [jnp.where(first_of_batch, s0_ref[p], s_scr[p]) for p in range(HP)]

    lanes = [slice(p * LANES, (p + 1) * LANES) for p in range(HP)]
    same_head = (_blk(lax.broadcasted_iota(jnp.int32, (LANES, LANES), 0), RW_HEAD)
                 == _blk(lax.broadcasted_iota(jnp.int32, (LANES, LANES), 1), RW_HEAD))
    ones_bd = jnp.where(same_head, 1.0, 0.0).astype(BF16)
    pvec = pvec_ref[...]
    PW = 2 * LANES if HP % 2 == 0 else LANES
    ones_pw = jnp.where(_blk(lax.broadcasted_iota(jnp.int32, (PW, PW), 0), RW_HEAD)
                        == _blk(lax.broadcasted_iota(jnp.int32, (PW, PW), 1), RW_HEAD), 1.0, 0.0).astype(BF16)

    def project(q):
        sl = slice(q * PW, (q + 1) * PW)
        w0, a0, v0, k_k, k_a = (pvec[i:i + 1, sl] for i in range(5))
        ops_scr[0, :, sl] = _dot(xr_ref[...], wr_ref[:, sl])
        kraw = _dot(xk_ref[...], wk_ref[:, sl])
        vp = _dot(xv_ref[...], wv_ref[:, sl])
        nz = -(w0 + _dot(tw_ref[...], w2_ref[:, sl]))
        softplus = jnp.maximum(nz, 0.0) + jnp.log(1.0 + jnp.exp(-jnp.abs(nz)))
        ops_scr[1, :, sl] = -jnp.exp(-softplus - 0.5)
        a = _sigmoid(a0 + _dot(av_ref[...], a2_ref[:, sl]))
        if has_v:
            vp = vp + (vf_ref[:, sl] - vp) * _sigmoid(v0 + _dot(vv_ref[...], v2_ref[:, sl]))
        else:
            vout_ref[:, sl] = vp
        ops_scr[3, :, sl] = vp
        ops_scr[6, :, sl] = _dot(gg_ref[...], g2_ref[:, sl])
        kn = kraw * k_k
        kn = kn / jnp.maximum(jnp.sqrt(_dot((kn * kn).astype(BF16), ones_pw)), 1e-12)
        ops_scr[4, :, sl] = kn
        ops_scr[5, :, sl] = kn * a
        ops_scr[2, :, sl] = kraw * (1.0 + (a - 1.0) * k_a)

    n_units = HP * LANES // PW
    n_slots = 2 + (C.bit_length() - 2) + NC
    plan = iter([[u for u in range(n_units) if u * n_slots // n_units == i] for i in range(n_slots)])

    def fill():
        for u in next(plan):
            project(u)

    ti = lax.broadcasted_iota(jnp.int32, (RB, RB), 0)
    tj = lax.broadcasted_iota(jnp.int32, (RB, RB), 1)
    tri = jnp.where((_blk(ti, C) == _blk(tj, C)) & (ti >= tj), 1.0, 0.0).astype(BF16)
    cl = sum(_dot(tri, part) for part in _split3(ld))
    p_in = jnp.exp(cl)
    p_inv = jnp.exp(-cl)
    a_t = -(jnp.exp(cl - ld) * kk)
    r_t = p_in * r
    b_t = p_inv * b
    k_t = p_inv * k
    rows = [slice(c * C, (c + 1) * C) for c in range(NC)]
    cl_end = [cl[(c + 1) * C - 1:(c + 1) * C, :] for c in range(NC)]
    p_rem = [jnp.exp(cl_end[c] - cl[rows[c], :]) for c in range(NC)]
    p_end = [jnp.exp(cl_end[c]) for c in range(NC)]
    b_h = [p_rem[c] * b[rows[c], :] for c in range(NC)]
    k_h = [p_rem[c] * k[rows[c], :] for c in range(NC)]
    rvec = rvec_ref[...]
    r_k, lnx_g, lnx_b = (rvec[i:i + 1, :] for i in range(3))
    rk = r * k * r_k

    first = lax.broadcasted_iota(jnp.int32, (1, LANES), 1) < RW_HEAD

    def stack(x):
        return jnp.concatenate([jnp.where(first, x, 0.0), jnp.where(first, 0.0, x)], axis=0).astype(BF16)

    def fold(x):
        return x[0:C, :] + x[C:C2, :]

    ri = lax.broadcasted_iota(jnp.int32, (C2, C2), 0)
    cj = lax.broadcasted_iota(jnp.int32, (C2, C2), 1)
    same = _blk(ri, C) == _blk(cj, C)
    strict = same & (_off(ri, C) > _off(cj, C))
    incl = same & (_off(ri, C) >= _off(cj, C))
    eye = jnp.where(ri == cj, 1.0, 0.0)

    probs = [(c, p) for c in range(NC) for p in range(HP)]
    a_b = {cp: a_t[rows[cp[0]], lanes[cp[1]]].astype(BF16) for cp in probs}
    r_b = {cp: r_t[rows[cp[0]], lanes[cp[1]]].astype(BF16) for cp in probs}
    v_st = {cp: stack(v[rows[cp[0]], lanes[cp[1]]]) for cp in probs}
    low, rb_f, ak_f, rk_f = {}, {}, {}, {}
    if C2 == LANES:
        t_f64 = lax.broadcasted_iota(jnp.int32, (C, LANES), 0)
        s_f64 = _off(lax.broadcasted_iota(jnp.int32, (C, LANES), 1), C)
        strict_f, incl_f = t_f64 > s_f64, t_f64 >= s_f64
        for c, p in probs:
            q_st = jnp.concatenate([stack(a_t[rows[c], lanes[p]]), stack(r_t[rows[c], lanes[p]])], axis=0)
            w_st = jnp.concatenate([b_t[rows[c], lanes[p]], k_t[rows[c], lanes[p]]], axis=0).astype(BF16)
            gm = _dot_nt(q_st, w_st)
            gr = pltpu.roll(gm, C, 1)
            a0_, a1_, r0_, r1_ = (gm[i * C:(i + 1) * C, :] for i in range(4))
            a0r, a1r, r0r, r1r = (gr[i * C:(i + 1) * C, :] for i in range(4))
            low[c, p] = jnp.concatenate([jnp.where(first & strict_f, a0_, 0.0),
                                         jnp.where(strict_f & ~first, a1r, 0.0)], axis=0)
            ak_f[c, p] = jnp.where(strict_f, jnp.where(first, a0r, a1_), 0.0).astype(BF16)
            rb_f[c, p] = jnp.where(incl_f, jnp.where(first, r0_, r1r), 0.0).astype(BF16)
            rk_f[c, p] = jnp.where(incl_f, jnp.where(first, r0r, r1_), 0.0).astype(BF16)
    else:
        for c, p in probs:
            q_st = jnp.concatenate([stack(a_t[rows[c], lanes[p]]), stack(r_t[rows[c], lanes[p]])], axis=0)
            b_c = b_t[rows[c], lanes[p]].astype(BF16)
            k_c = k_t[rows[c], lanes[p]].astype(BF16)
            gb = _dot_nt(q_st, jnp.concatenate([b_c, b_c], axis=0))
            gk = _dot_nt(q_st, jnp.concatenate([k_c, k_c], axis=0))
            low[c, p] = jnp.where(strict, gb[0:C2, :], 0.0)
            rb_f[c, p] = fold(jnp.where(incl, gb[C2:, :], 0.0)).astype(BF16)
            ak_f[c, p] = fold(jnp.where(strict, gk[0:C2, :], 0.0)).astype(BF16)
            rk_f[c, p] = fold(jnp.where(incl, gk[C2:, :], 0.0)).astype(BF16)
    fill()
    akv = {cp: _dot(ak_f[cp], v_st[cp]) for cp in probs}
    rkv = {cp: _dot(rk_f[cp], v_st[cp]) for cp in probs}

    first_level = _lower_left(ri, cj, 1)
    t_inv = {cp: eye + jnp.where(first_level, low[cp], 0.0) for cp in probs}
    s = 2
    while s < C:
        sel = _lower_left(ri, cj, s)
        tb = {cp: t_inv[cp].astype(BF16) for cp in probs}
        mid = {cp: _dot(tb[cp], jnp.where(sel, low[cp], 0.0).astype(BF16)).astype(BF16) for cp in probs}
        t_inv = {cp: t_inv[cp] + _dot(mid[cp], tb[cp]) for cp in probs}
        fill()
        s *= 2
    t_f = {cp: fold(t_inv[cp]).astype(BF16) for cp in probs}

    y = {}
    for c in range(NC):
        state_b = [st.astype(BF16) for st in state]
        x = [_dot_nt(a_b[c, p], state_b[p]) + akv[c, p] for p in range(HP)]
        y0 = [_dot_nt(r_b[c, p], state_b[p]) + rkv[c, p] for p in range(HP)]
        u = [_dot(t_f[c, p], stack(x[p])) for p in range(HP)]
        for p in range(HP):
            y[c, p] = y0[p] + _dot(rb_f[c, p], stack(u[p]))
        upd = []
        for p in range(HP):
            uv = jnp.concatenate([u[p], v[rows[c], lanes[p]]], axis=0).astype(BF16)
            bk = jnp.concatenate([b_h[c][:, lanes[p]], k_h[c][:, lanes[p]]], axis=0).astype(BF16)
            upd.append(_dot_tn(uv, bk))
        state = [state[p] * p_end[c][:, lanes[p]] + jnp.where(same_head, upd[p], 0.0) for p in range(HP)]
        fill()
    for p in range(HP):
        s_scr[p] = state[p]
        sout_ref[p] = state[p]

    def seg(t, slices):
        parts = _split2(t) if slices == 2 else (t.astype(BF16),)
        return sum(_dot(part, ones_bd) for part in parts)

    y_all = [jnp.concatenate([y[c, p] for c in range(NC)], axis=0) for p in range(HP)]
    mean = [seg(y_all[p], 2) * (1.0 / RW_HEAD) for p in range(HP)]
    bonus = [seg(rk[:, lanes[p]], 1) * v[:, lanes[p]] for p in range(HP)]
    fill()
    d = [y_all[p] - mean[p] for p in range(HP)]
    var = [seg(d[p] * d[p], 1) * (1.0 / RW_HEAD) for p in range(HP)]
    for p in range(HP):
        yn = d[p] * lax.rsqrt(var[p] + GN_EPS) * lnx_g[:, lanes[p]] + lnx_b[:, lanes[p]]
        z_ref[:, lanes[p]] = ((yn + bonus[p]) * g[:, lanes[p]]).astype(z_ref.dtype)


def rwkv_mix(acts, v_first, wr, wk, wv, w2, a2, g2, v2, pvecs, rvecs, s0_bd, B, T):
    has_v = v2 is not None
    M, D = acts[0].shape
    HL = min(1024, D)
    HP = HL // LANES
    C = min(CHUNK, T)
    NC = 2 if T % (2 * C) == 0 else 1
    RB = NC * C
    nT = T // RB
    assert nT & (nT - 1) == 0
    S = M // RB
    proj_blk = lambda s: jnp.minimum(s, S - 1)
    rec_blk = lambda s: jnp.maximum(s - 1, 0)
    row = lambda a: pl.BlockSpec((RB, a.shape[1]), lambda j, s: (proj_blk(s), 0))
    col = lambda a: pl.BlockSpec((a.shape[0], HL), lambda j, s: (0, j))
    st = pl.BlockSpec((None, HP, LANES, LANES), lambda j, s: (rec_blk(s) // nT, j, 0, 0))
    weights = [wr, wk, wv, w2, a2, g2] + ([v2] if has_v else [])
    in_specs = ([row(a) for a in acts]
                + ([pl.BlockSpec((RB, HL), lambda j, s: (proj_blk(s), j))] if has_v else [])
                + [col(w) for w in weights] + [col(pvecs), col(rvecs), st])
    out_specs = [pl.BlockSpec((RB, HL), lambda j, s: (rec_blk(s), j)), st]
    out_shape = [jax.ShapeDtypeStruct((M, D), BF16), jax.ShapeDtypeStruct(s0_bd.shape, F32)]
    if not has_v:
        out_specs.append(pl.BlockSpec((RB, HL), lambda j, s: (proj_blk(s), j)))
        out_shape.append(jax.ShapeDtypeStruct((M, D), F32))
    res = pl.pallas_call(
        functools.partial(_rwkv_mix_kernel, C=C, NC=NC, HP=HP, nT=nT, has_v=has_v),
        grid=(D // HL, S + 1),
        in_specs=in_specs, out_specs=out_specs, out_shape=out_shape,
        scratch_shapes=[pltpu.VMEM((7, RB, HL), F32), pltpu.VMEM((HP, LANES, LANES), F32)],
        compiler_params=_params(("parallel", "arbitrary")),
        name="rwkv_mix",
    )(*acts, *([v_first] if has_v else []), *weights, pvecs, rvecs, s0_bd)
    return res


def _gate_operand(gate, B, T, bm, bn):
    N = gate.shape[1]
    col = (lambda j: j) if bn < N else (lambda j: 0)
    if T % bm == 0:
        return gate.reshape(B, 1, N), pl.BlockSpec((None, 1, bn), lambda i, j: ((i * bm) // T, 0, col(j)))
    rows = jnp.broadcast_to(gate[:, None, :], (B, T, N)).reshape(B * T, N)
    return rows, pl.BlockSpec((bm, bn), lambda i, j: (i, col(j)))


def _matmul_res_kernel(a_ref, w_ref, x_ref, gate_ref, o_ref):
    o_ref[...] = x_ref[...] + gate_ref[...] * _dot(a_ref[...], w_ref[...])


def matmul_res(a, w, x, gate, B, T):
    M, K = a.shape
    N = w.shape[1]
    bm, bn = min(1024, M), min(512, N)
    gate_arr, gate_spec = _gate_operand(gate, B, T, bm, bn)
    return pl.pallas_call(
        _matmul_res_kernel,
        grid=(M // bm, N // bn),
        in_specs=[pl.BlockSpec((bm, K), lambda i, j: (i, 0)),
                  pl.BlockSpec((K, bn), lambda i, j: (0, j)),
                  pl.BlockSpec((bm, bn), lambda i, j: (i, j)), gate_spec],
        out_specs=pl.BlockSpec((bm, bn), lambda i, j: (i, j)),
        out_shape=jax.ShapeDtypeStruct((M, N), F32),
        compiler_params=_params(("parallel", "arbitrary")),
        name="matmul_res",
    )(a, w, x, gate_arr)


def _mlp_kernel(x_ref, g_ref, sh_ref, sc_ref, gate_ref, w1_ref, w2_ref, fg_ref, o_ref, h_scr, acc, *, final):
    f = pl.program_id(1)
    bm = x_ref.shape[0]

    def rows_of(ref, rs):
        return ref[...] if ref.shape[0] == 1 else ref[rs, :]

    def mlp(h):
        return _dot(jnp.square(jnp.maximum(_dot(h, w1_ref[...]), 0.0)).astype(BF16), w2_ref[...])

    @pl.when(f == 0)
    def _():
        parts = 4 if bm % 64 == 0 else 1
        for c in range(parts):
            rs = slice(c * bm // parts, (c + 1) * bm // parts)
            h = (_rms(x_ref[rs, :], g_ref[...]) * (1.0 + rows_of(sc_ref, rs)) + rows_of(sh_ref, rs)).astype(BF16)
            h_scr[rs, :] = h
            acc[rs, :] = mlp(h)

    @pl.when(f > 0)
    def _():
        acc[...] += mlp(h_scr[...])

    @pl.when(f == pl.num_programs(1) - 1)
    def _():
        y = x_ref[...] + gate_ref[...] * acc[...]
        o_ref[...] = _rms(y, fg_ref[...]) if final else y


def mlp_res(x, g, shift, scale, gate, w1, w2, layer, final_g, final, B, T):
    M, D = x.shape
    F = w1.shape[2]
    bm = 512 if T % 512 == 0 else min(256, M)
    bf = min(1024, F)
    (sh_arr, vec_spec), (sc_arr, _), (gate_arr, _) = (_gate_operand(v, B, T, bm, D) for v in (shift, scale, gate))
    return pl.pallas_call(
        functools.partial(_mlp_kernel, final=final),
        grid=(M // bm, F // bf),
        in_specs=[pl.BlockSpec((bm, D), lambda i, f: (i, 0)),
                  pl.BlockSpec((1, D), lambda i, f: (0, 0)), vec_spec, vec_spec, vec_spec,
                  pl.BlockSpec((None, D, bf), lambda i, f: (layer, 0, f)),
                  pl.BlockSpec((None, bf, D), lambda i, f: (layer, f, 0)),
                  pl.BlockSpec((1, D), lambda i, f: (0, 0))],
        out_specs=pl.BlockSpec((bm, D), lambda i, f: (i, 0)),
        out_shape=jax.ShapeDtypeStruct((M, D), F32),
        scratch_shapes=[pltpu.VMEM((bm, D), BF16), pltpu.VMEM((bm, D), F32)],
        compiler_params=_params(("parallel", "arbitrary")),
        name="mlp_res",
    )(x, g.reshape(1, D), sh_arr, sc_arr, gate_arr, w1, w2, final_g.reshape(1, D))


def _rope_pair(t, tab):
    prod = t * tab
    return (prod + pltpu.roll(prod, ROPE_DIM, 1))[:, :ROPE_DIM]


def _latent_kernel(x_ref, gx_ref, sh_ref, sc_ref, w_ref, g_ref, tab_ref, ckv_ref, kpe_ref):
    h = (_rms(x_ref[...], gx_ref[...]) * (1.0 + sc_ref[...]) + sh_ref[...]).astype(BF16)
    acc = _dot(h, w_ref[...])
    R = g_ref.shape[1]
    ckv_ref[...] = _rms(acc[:, :R], g_ref[...])
    kpe_ref[...] = _rope_pair(acc[:, R:R + 2 * ROPE_DIM], tab_ref[...])


def latent(x, gx, shift, scale, w, g, tab):
    B, T, D = x.shape
    R = g.shape[0]
    bt = min(512, T)
    vec = pl.BlockSpec((None, 1, D), lambda b, t: (b, 0, 0))
    return pl.pallas_call(
        _latent_kernel,
        grid=(B, T // bt),
        in_specs=[pl.BlockSpec((None, bt, D), lambda b, t: (b, t, 0)),
                  pl.BlockSpec((1, D), lambda b, t: (0, 0)), vec, vec,
                  pl.BlockSpec(w.shape, lambda b, t: (0, 0)),
                  pl.BlockSpec((1, R), lambda b, t: (0, 0)),
                  pl.BlockSpec((bt, 2 * ROPE_DIM), lambda b, t: (t, 0))],
        out_specs=[pl.BlockSpec((None, bt, R), lambda b, t: (b, t, 0)),
                   pl.BlockSpec((None, bt, ROPE_DIM), lambda b, t: (b, t, 0))],
        out_shape=[jax.ShapeDtypeStruct((B, T, R), F32), jax.ShapeDtypeStruct((B, T, ROPE_DIM), F32)],
        compiler_params=_params(("parallel", "parallel")),
        name="latent",
    )(x, gx.reshape(1, D), shift.reshape(B, 1, D), scale.reshape(B, 1, D), w, g.reshape(1, R), tab)


def _kv_expand_kernel(c_ref, pe_ref, wuk_ref, wuv_ref, k_ref, vt_ref):
    c = c_ref[...].astype(BF16)
    kn = _dot(c, wuk_ref[...])
    vv = _dot(c, wuv_ref[...])
    pe = pe_ref[...].astype(BF16)
    for h in range(k_ref.shape[0]):
        k_ref[h, :, 0:NOPE_DIM] = kn[:, h * NOPE_DIM:(h + 1) * NOPE_DIM].astype(BF16)
        k_ref[h, :, NOPE_DIM:NOPE_DIM + ROPE_DIM] = pe
        vt_ref[h, 0:V_DIM, :] = vv[:, h * V_DIM:(h + 1) * V_DIM].T.astype(BF16)
        vt_ref[h, V_DIM:V_ROWS, :] = jnp.ones((V_ROWS - V_DIM, c.shape[0]), BF16)


def kv_expand(ckv, kpe, wuk, wuv, H):
    B, S, R = ckv.shape
    bt = min(512, S)
    hb = min(8, H)
    DK = NOPE_DIM + ROPE_DIM
    return pl.pallas_call(
        _kv_expand_kernel,
        grid=(H // hb, B, S // bt),
        in_specs=[pl.BlockSpec((None, bt, R), lambda j, b, t: (b, t, 0)),
                  pl.BlockSpec((None, bt, ROPE_DIM), lambda j, b, t: (b, t, 0)),
                  pl.BlockSpec((R, hb * NOPE_DIM), lambda j, b, t: (0, j)),
                  pl.BlockSpec((R, hb * V_DIM), lambda j, b, t: (0, j))],
        out_specs=[pl.BlockSpec((None, hb, bt, DK), lambda j, b, t: (b, j, t, 0)),
                   pl.BlockSpec((None, hb, V_ROWS, bt), lambda j, b, t: (b, j, 0, t))],
        out_shape=[jax.ShapeDtypeStruct((B, H, S, DK), BF16), jax.ShapeDtypeStruct((B, H, V_ROWS, S), BF16)],
        compiler_params=_params(("parallel", "parallel", "arbitrary")),
        name="kv_expand",
    )(ckv, kpe, wuk, wuv)


def _wdq_kernel(x_ref, gx_ref, sh_ref, sc_ref, w_ref, g_ref, o_ref):
    h = (_rms(x_ref[...], gx_ref[...]) * (1.0 + sc_ref[...]) + sh_ref[...]).astype(BF16)
    o_ref[...] = _rms(_dot(h, w_ref[...]), g_ref[...]).astype(BF16)


def wdq_norm(x, gx, shift, scale, w, g):
    B, T, D = x.shape
    R = w.shape[1]
    bt = min(1024, T)
    vec = pl.BlockSpec((None, 1, D), lambda b, t: (b, 0, 0))
    return pl.pallas_call(
        _wdq_kernel,
        grid=(B, T // bt),
        in_specs=[pl.BlockSpec((None, bt, D), lambda b, t: (b, t, 0)),
                  pl.BlockSpec((1, D), lambda b, t: (0, 0)), vec, vec,
                  pl.BlockSpec((D, R), lambda b, t: (0, 0)),
                  pl.BlockSpec((1, R), lambda b, t: (0, 0))],
        out_specs=pl.BlockSpec((None, bt, R), lambda b, t: (b, t, 0)),
        out_shape=jax.ShapeDtypeStruct((B, T, R), BF16),
        compiler_params=_params(("parallel", "parallel")),
        name="wdq_norm",
    )(x, gx.reshape(1, D), shift.reshape(B, 1, D), scale.reshape(B, 1, D), w, g.reshape(1, R))


def _wuq_kernel(c_ref, w_ref, tab_ref, q_ref, *, scale):
    acc = _dot(c_ref[...], w_ref[...])
    tab = tab_ref[...]
    W = NOPE_DIM + 2 * ROPE_DIM
    for h in range(q_ref.shape[0]):
        q_ref[h, :, 0:NOPE_DIM] = (acc[:, h * W:h * W + NOPE_DIM] * scale).astype(BF16)
        pe = _rope_pair(acc[:, h * W + NOPE_DIM:(h + 1) * W], tab)
        q_ref[h, :, NOPE_DIM:NOPE_DIM + ROPE_DIM] = (pe * scale).astype(BF16)
        if q_ref.shape[2] > NOPE_DIM + ROPE_DIM:
            pad = q_ref.shape[2] - NOPE_DIM - ROPE_DIM
            q_ref[h, :, NOPE_DIM + ROPE_DIM:] = jnp.zeros((q_ref.shape[1], pad), BF16)


def wuq_rope(cq, w, tab, H, head_major):
    B, T, R = cq.shape
    bt = min(512, T)
    nt = T // bt
    hb = min(8, H)
    W = NOPE_DIM + 2 * ROPE_DIM
    DK = NOPE_DIM + ROPE_DIM
    if head_major:
        out_spec = pl.BlockSpec((hb, bt, DK), lambda j, b, t: (j, b * nt + t, 0))
        out_shape = (H, B * T, DK)
    else:
        out_spec = pl.BlockSpec((None, hb, bt, 2 * LANES), lambda j, b, t: (b, j, t, 0))
        out_shape = (B, H, T, 2 * LANES)
    return pl.pallas_call(
        functools.partial(_wuq_kernel, scale=MLA_SCALE if head_major else MLA_SCALE * LOG2E),
        grid=(H // hb, B, nt),
        in_specs=[pl.BlockSpec((None, bt, R), lambda j, b, t: (b, t, 0)),
                  pl.BlockSpec((R, hb * W), lambda j, b, t: (0, j)),
                  pl.BlockSpec((bt, 2 * ROPE_DIM), lambda j, b, t: (t, 0))],
        out_specs=out_spec,
        out_shape=jax.ShapeDtypeStruct(out_shape, BF16),
        compiler_params=_params(("parallel", "parallel", "parallel")),
        name="wuq_rope",
    )(cq, w, tab)


def _visible(qpos, kpos):
    return _blk(kpos, CHUNK) <= _blk(qpos, CHUNK)


def _flash_kernel(q_ref, k_ref, vt_ref, o_ref, qt_scr, s_a, s_b, m_scr, acc_scr, *, tile):
    hb = q_ref.shape[0]
    ha = hb // 2
    first, second = range(0, ha), range(ha, hb)
    qi = pl.program_id(2)
    qpos = qi * tile + lax.broadcasted_iota(jnp.int32, (1, tile), 1)
    m_scr[...] = jnp.full(m_scr.shape, -jnp.inf, F32)
    acc_scr[...] = jnp.zeros(acc_scr.shape, F32)
    DK = NOPE_DIM + ROPE_DIM
    for h in range(hb):
        qf = q_ref[h].astype(F32)
        qt_scr[h, 0:LANES, :] = qf[:, 0:LANES].T.astype(BF16)
        qt_scr[h, LANES:DK, :] = qf[:, LANES:2 * LANES].T[0:DK - LANES, :].astype(BF16)

    def scores(ki, heads, s_ref):
        start = pl.multiple_of(ki * tile, tile)
        for h in heads:
            s_ref[h - heads[0]] = _dot(k_ref[h, pl.ds(start, tile), :], qt_scr[h])

    def softmax_pv(ki, heads, s_ref, masked):
        start = pl.multiple_of(ki * tile, tile)
        for h in heads:
            s = s_ref[h - heads[0]]
            if masked:
                kpos = ki * tile + lax.broadcasted_iota(jnp.int32, (tile, 1), 0)
                s = jnp.where(_visible(qpos, kpos), s, -jnp.inf)
            m = m_scr[h]
            m_new = jnp.maximum(m, jnp.max(s, axis=0, keepdims=True))
            p = jnp.exp2(s - m_new).astype(BF16)
            acc_scr[h] = jnp.exp2(m - m_new) * acc_scr[h] + _dot(vt_ref[h, :, pl.ds(start, tile)], p)
            m_scr[h] = m_new

    def one_tile(ki):
        scores(ki, second, s_b)
        softmax_pv(ki, first, s_a, False)
        scores(ki + 1, first, s_a)
        softmax_pv(ki, second, s_b, False)

    def two_tiles(kk, carry):
        one_tile(2 * kk)
        one_tile(2 * kk + 1)
        return carry

    scores(0, first, s_a)
    lax.fori_loop(0, qi // 2, two_tiles, 0)

    @pl.when(qi % 2 == 1)
    def _():
        one_tile(qi - 1)

    scores(qi, second, s_b)
    softmax_pv(qi, first, s_a, True)
    softmax_pv(qi, second, s_b, True)
    for h in range(hb):
        acc = acc_scr[h]
        inv_l = 1.0 / acc[V_DIM:V_DIM + 1, :]
        o_ref[:, h * V_DIM:(h + 1) * V_DIM] = (acc[0:V_DIM, :] * inv_l).T.astype(o_ref.dtype)


def flash_prompt(q, k, vt):
    B, H, T, QW = q.shape
    DK = k.shape[3]
    tile = min(256, T)
    assert tile % CHUNK == 0 and T % tile == 0 and H % 2 == 0 and QW == 2 * LANES
    hb = next(n for n in (8, 4, 2) if H % n == 0)
    return pl.pallas_call(
        functools.partial(_flash_kernel, tile=tile),
        grid=(B, H // hb, T // tile),
        in_specs=[pl.BlockSpec((None, hb, tile, QW), lambda b, j, i: (b, j, i, 0)),
                  pl.BlockSpec((None, hb, T, DK), lambda b, j, i: (b, j, 0, 0)),
                  pl.BlockSpec((None, hb, V_ROWS, T), lambda b, j, i: (b, j, 0, 0))],
        out_specs=pl.BlockSpec((None, tile, hb * V_DIM), lambda b, j, i: (b, i, j)),
        out_shape=jax.ShapeDtypeStruct((B, T, H * V_DIM), BF16),
        scratch_shapes=[pltpu.VMEM((hb, DK, tile), BF16),
                        pltpu.VMEM((hb // 2, tile, tile), F32), pltpu.VMEM((hb // 2, tile, tile), F32),
                        pltpu.VMEM((hb, 1, tile), F32), pltpu.VMEM((hb, V_ROWS, tile), F32)],
        compiler_params=_params(("parallel", "parallel", "arbitrary")),
        name="flash_prompt",
    )(q, k, vt)


def _q_absorb_kernel(q_ref, wuk_ref, o_ref):
    o_ref[...] = _dot_nt(q_ref[:, 0:NOPE_DIM], wuk_ref[...]).astype(BF16)


def _attn_latent_kernel(ql_ref, q_ref, qpos_ref, c_ref, pe_ref, o_ref):
    H, T, R = ql_ref.shape
    S = c_ref.shape[0]
    c = c_ref[...].astype(BF16)
    s = (_dot_nt(ql_ref[...].reshape(H * T, R), c)
         + _dot_nt(q_ref[:, :, NOPE_DIM:NOPE_DIM + ROPE_DIM].reshape(H * T, ROPE_DIM), pe_ref[...].astype(BF16)))
    kpos = lax.broadcasted_iota(jnp.int32, (1, S), 1)
    s = jnp.where(_visible(qpos_ref[...], kpos), s, -jnp.inf)
    p = jnp.exp(s - jnp.max(s, axis=-1, keepdims=True))
    l = jnp.sum(p, axis=-1, keepdims=True)
    o_ref[...] = (_dot(p.astype(BF16), c) / l).astype(BF16).reshape(H, T, R)


def _o_expand_kernel(o_ref, wuv_ref, out_ref):
    out_ref[...] = _dot(o_ref[...], wuv_ref[...]).astype(BF16)


def attn_latent(q, ckv, kpe, wuk, wuv, B, T, q0):
    H, M, DK = q.shape
    S, R = ckv.shape[1:]
    q_lat = pl.pallas_call(
        _q_absorb_kernel,
        grid=(H,),
        in_specs=[pl.BlockSpec((None, M, DK), lambda h: (h, 0, 0)),
                  pl.BlockSpec((R, NOPE_DIM), lambda h: (0, h))],
        out_specs=pl.BlockSpec((None, M, R), lambda h: (h, 0, 0)),
        out_shape=jax.ShapeDtypeStruct((H, M, R), BF16),
        compiler_params=_params(("parallel",)),
        name="q_absorb",
    )(q, wuk)
    qpos = jnp.tile(q0 + jnp.arange(T, dtype=jnp.int32), H)[:, None]
    o_lat = pl.pallas_call(
        _attn_latent_kernel,
        grid=(B,),
        in_specs=[pl.BlockSpec((H, T, R), lambda b: (0, b, 0)),
                  pl.BlockSpec((H, T, DK), lambda b: (0, b, 0)),
                  pl.BlockSpec((H * T, 1), lambda b: (0, 0)),
                  pl.BlockSpec((None, S, R), lambda b: (b, 0, 0)),
                  pl.BlockSpec((None, S, ROPE_DIM), lambda b: (b, 0, 0))],
        out_specs=pl.BlockSpec((H, T, R), lambda b: (0, b, 0)),
        out_shape=jax.ShapeDtypeStruct((H, M, R), BF16),
        compiler_params=_params(("parallel",)),
        name="attn_latent",
    )(q_lat, q, qpos, ckv, kpe)
    return pl.pallas_call(
        _o_expand_kernel,
        grid=(H,),
        in_specs=[pl.BlockSpec((None, M, R), lambda h: (h, 0, 0)),
                  pl.BlockSpec((R, V_DIM), lambda h: (0, h))],
        out_specs=pl.BlockSpec((M, V_DIM), lambda h: (0, h)),
        out_shape=jax.ShapeDtypeStruct((M, H * V_DIM), BF16),
        compiler_params=_params(("parallel",)),
        name="o_expand",
    )(o_lat, wuv)


def _pad_cols(w, n):
    return jnp.pad(w, ((0, 0), (0, n - w.shape[1])))


def _pad_rows(w, n):
    return jnp.pad(w, ((0, n - w.shape[0]), (0, 0)))


def _rotate_half_cols(w):
    half = ROPE_DIM // 2
    return jnp.concatenate([-w[..., half:], w[..., :half]], axis=-1)


def _rope_table(pos):
    half = ROPE_DIM // 2
    inv = ROPE_THETA ** (-jnp.arange(half, dtype=F32) / half)
    ang = pos.astype(F32)[:, None] * inv[None, :]
    cos, sin = jnp.cos(ang), jnp.sin(ang)
    return jnp.concatenate([cos, cos, sin, sin], axis=-1)


def _block_diag_states(s):
    B, H, N, _ = s.shape
    s = s.reshape(B, H // 2, 2, N, N)
    z = jnp.zeros_like(s[:, :, 0])
    top = jnp.concatenate([s[:, :, 0], z], axis=-1)
    bot = jnp.concatenate([z, s[:, :, 1]], axis=-1)
    return jnp.concatenate([top, bot], axis=-2)


def _diag_states(s):
    B, HP = s.shape[:2]
    N = RW_HEAD
    return jnp.stack([s[:, :, :N, :N], s[:, :, N:, N:]], axis=2).reshape(B, 2 * HP, N, N)


def _prepare(W):
    D = W['rw_wr'].shape[1]
    P = {}
    bf = lambda a: a.astype(BF16)
    P['mlp_w1'], P['mlp_w2'] = bf(W['mlp_w1']), bf(W['mlp_w2'])
    for n in ('rw_wr', 'rw_wk', 'rw_wv', 'rw_wo', 'rw_g1', 'rw_g2', 'mla_wdq', 'mla_wo'):
        P[n] = [bf(W[n][l]) for l in range(W[n].shape[0])]
    NA = W['rw_wr'].shape[0]
    P['rw_w1'] = [bf(_pad_cols(W['rw_w1'][l], LORA_PAD)) for l in range(NA)]
    P['rw_w2'] = [bf(_pad_rows(W['rw_w2'][l], LORA_PAD)) for l in range(NA)]
    P['rw_a1'] = [bf(_pad_cols(W['rw_a1'][l], LORA_PAD)) for l in range(NA)]
    P['rw_a2'] = [bf(_pad_rows(W['rw_a2'][l], LORA_PAD)) for l in range(NA)]
    P['rw_v1'] = [bf(_pad_cols(W['rw_v1'][l], LORA_PAD)) for l in range(NA - 1)]
    P['rw_v2'] = [bf(_pad_rows(W['rw_v2'][l], LORA_PAD)) for l in range(NA - 1)]
    zeros = jnp.zeros((D,), F32)
    P['proj_vecs'], P['recur_vecs'] = [], []
    for l in range(NA):
        v0 = W['rw_v0'][l - 1] if l > 0 else zeros
        P['proj_vecs'].append(jnp.stack([W['rw_w0'][l], W['rw_a0'][l], v0, W['rw_kk'][l], W['rw_ka'][l],
                                         zeros, zeros, zeros]))
        P['recur_vecs'].append(jnp.stack([W['rw_rk'][l], W['rw_lnx_g'][l], W['rw_lnx_b'][l]] + [zeros] * 5))
    HL = min(512, D)
    head = jnp.arange(HL) // RW_HEAD
    P['ones_bd'] = (head[:, None] == head[None, :]).astype(BF16)
    R = W['kv_lat_g'].shape[0]
    wd = W['kv_wd']
    P['kv_wd'] = bf(jnp.concatenate([wd, _rotate_half_cols(wd[:, R:])], axis=1))
    H = W['kv_wuk'].shape[1]
    P['kv_wuk'] = bf(W['kv_wuk'].reshape(R, H * NOPE_DIM))
    P['kv_wuv'] = bf(W['kv_wuv'].reshape(R, H * V_DIM))
    NB, Q = W['mla_wuq'].shape[:2]
    wuq = W['mla_wuq'].reshape(NB, Q, H, NOPE_DIM + ROPE_DIM)
    pe = wuq[..., NOPE_DIM:]
    P['mla_wuq'] = bf(jnp.concatenate([wuq, _rotate_half_cols(pe)], axis=-1).reshape(NB, Q, -1))
    return P


def _trunk(x, mod, kv_mod, pos0, h_prev, s0, past_ckv, past_kpe, W, P):
    B, T, D = x.shape
    M = B * T
    depth = W['ada_w'].shape[0]
    NA = W['rw_wr'].shape[0]
    H = W['kv_wuk'].shape[1]
    tab = _rope_table(pos0 + jnp.arange(T))
    xf = x.reshape(M, D)
    shifts, states = [], []
    v_first = None
    keys = vals = ckv = kpe = None
    for l in range(depth):
        m = mod[l]
        if l < NA:
            has_v = l > 0
            pre = rwkv_pre(xf.reshape(B, T, D), W['norm_mix_g'][l], m[:, 0], m[:, 1], h_prev[l], W['rw_mu'][l],
                           P['rw_w1'][l], P['rw_a1'][l], P['rw_g1'][l], P['rw_v1'][l - 1] if has_v else None)
            shifts.append(pre[-1].reshape(B, D))
            acts = [a.reshape(M, a.shape[-1]) for a in pre[:-1]]
            res = rwkv_mix(acts, v_first, P['rw_wr'][l], P['rw_wk'][l], P['rw_wv'][l], P['rw_w2'][l],
                           P['rw_a2'][l], P['rw_g2'][l], P['rw_v2'][l - 1] if has_v else None,
                           P['proj_vecs'][l], P['recur_vecs'][l], _block_diag_states(s0[l]), B, T)
            z, s_fin = res[0], res[1]
            if not has_v:
                v_first = res[2]
            states.append(_diag_states(s_fin))
            xf = matmul_res(z, P['rw_wo'][l], xf, m[:, 2], B, T)
        else:
            j = l - NA
            cq = wdq_norm(xf.reshape(B, T, D), W['norm_mix_g'][l], m[:, 0], m[:, 1], P['mla_wdq'][j], W['mla_q_g'][j])
            if past_ckv is None:
                q = wuq_rope(cq, P['mla_wuq'][j], tab, H, False)
                o = flash_prompt(q, keys, vals).reshape(M, H * V_DIM)
            else:
                q = wuq_rope(cq.reshape(1, M, -1), P['mla_wuq'][j], jnp.tile(tab, (B, 1)), H, True)
                o = attn_latent(q, keys, vals, P['kv_wuk'], P['kv_wuv'], B, T, pos0)
            xf = matmul_res(o, P['mla_wo'][j], xf, m[:, 2], B, T)
        xf = mlp_res(xf, W['norm_mlp_g'][l], m[:, 3], m[:, 4], m[:, 5], P['mlp_w1'], P['mlp_w2'], l,
                     W['final_g'], l == depth - 1, B, T)
        if l == NA - 1:
            ckv, kpe = latent(xf.reshape(B, T, D), W['kv_norm_g'], kv_mod[:, 0], kv_mod[:, 1],
                              P['kv_wd'], W['kv_lat_g'], tab)
            if past_ckv is None:
                keys, vals = kv_expand(ckv, kpe, P['kv_wuk'], P['kv_wuv'], H)
            else:
                keys = jnp.concatenate([past_ckv, ckv], axis=1)
                vals = jnp.concatenate([past_kpe, kpe], axis=1)
    return xf.reshape(B, T, D), ckv, kpe, jnp.stack(states), jnp.stack(shifts)


def kernel(x_prompt, x_sample, cache_ckv, cache_kpe, state_wkv, state_shift, c_prompt, c_sample, ada_w, ada_b, norm_mix_g, norm_mlp_g, mlp_w1, mlp_w2, rw_mu, rw_w0, rw_w1, rw_w2, rw_a0, rw_a1, rw_a2, rw_v0, rw_v1, rw_v2, rw_g1, rw_g2, rw_wr, rw_wk, rw_wv, rw_wo, rw_kk, rw_ka, rw_rk, rw_lnx_g, rw_lnx_b, kv_ada_w, kv_ada_b, kv_norm_g, kv_wd, kv_lat_g, kv_wuk, kv_wuv, mla_wdq, mla_q_g, mla_wuq, mla_wo, final_g):
    W = dict(ada_w=ada_w, ada_b=ada_b, norm_mix_g=norm_mix_g, norm_mlp_g=norm_mlp_g,
             mlp_w1=mlp_w1, mlp_w2=mlp_w2, rw_mu=rw_mu, rw_w0=rw_w0, rw_w1=rw_w1, rw_w2=rw_w2,
             rw_a0=rw_a0, rw_a1=rw_a1, rw_a2=rw_a2, rw_v0=rw_v0, rw_v1=rw_v1, rw_v2=rw_v2,
             rw_g1=rw_g1, rw_g2=rw_g2, rw_wr=rw_wr, rw_wk=rw_wk, rw_wv=rw_wv, rw_wo=rw_wo,
             rw_kk=rw_kk, rw_ka=rw_ka, rw_rk=rw_rk, rw_lnx_g=rw_lnx_g, rw_lnx_b=rw_lnx_b,
             kv_ada_w=kv_ada_w, kv_ada_b=kv_ada_b, kv_norm_g=kv_norm_g, kv_wd=kv_wd,
             kv_lat_g=kv_lat_g, kv_wuk=kv_wuk, kv_wuv=kv_wuv, mla_wdq=mla_wdq, mla_q_g=mla_q_g,
             mla_wuq=mla_wuq, mla_wo=mla_wo, final_g=final_g)
    P = _prepare(W)
    Bp, Tp, D = x_prompt.shape
    Bs = x_sample.shape[0]
    depth = ada_w.shape[0]
    NA = rw_wr.shape[0]
    c_all = jnp.concatenate([c_prompt, c_sample], axis=0)
    mod = ada_linear(c_all, ada_w, ada_b).reshape(depth, Bp + Bs, N_MOD, D)
    kv_mod = ada_linear(c_all, kv_ada_w[None], kv_ada_b[None]).reshape(Bp + Bs, 2, D)
    h0 = jnp.zeros((NA, Bp, D), F32)
    s0 = jnp.zeros((NA, Bp, D // RW_HEAD, RW_HEAD, RW_HEAD), F32)
    out_p = _trunk(x_prompt, mod[:, :Bp], kv_mod[:Bp], 0, h0, s0, None, None, W, P)
    out_s = _trunk(x_sample, mod[:, Bp:], kv_mod[Bp:], cache_ckv.shape[1], state_shift, state_wkv,
                   cache_ckv, cache_kpe, W, P)
    return (out_p[0], out_s[0]) + out_p[1:] + out_s[1:]
```

```python
import functools

import jax
import jax.numpy as jnp
from jax import lax
from jax.experimental import pallas as pl
from jax.experimental.pallas import tpu as pltpu

F32, BF16 = jnp.float32, jnp.bfloat16

RW_HEAD = 64
CHUNK = 64
GN_EPS = 64e-5
NOPE_DIM = 128
ROPE_DIM = 64
V_DIM = 128
ROPE_THETA = 10000.0
MLA_SCALE = (NOPE_DIM + ROPE_DIM) ** -0.5
LOG2E = 1.4426950408889634
V_ROWS = V_DIM + 16
NORM_EPS = 1e-6
N_MOD = 6

LANES = 128
VMEM_LIMIT = 56 * 1024 * 1024
LORA_PAD = 128


def _params(sem, vmem=VMEM_LIMIT):
    return pltpu.CompilerParams(dimension_semantics=sem, vmem_limit_bytes=vmem)


def _dot(a, b):
    return jnp.dot(a, b, preferred_element_type=F32)


def _dot_nt(a, b):
    return lax.dot_general(a, b, (((1,), (1,)), ((), ())), preferred_element_type=F32)


def _dot_tn(a, b):
    return lax.dot_general(a, b, (((0,), (0,)), ((), ())), preferred_element_type=F32)


def _split2(x):
    hi = x.astype(BF16)
    return hi, (x - hi.astype(F32)).astype(BF16)


def _split3(x):
    hi = x.astype(BF16)
    r1 = x - hi.astype(F32)
    mid = r1.astype(BF16)
    return hi, mid, (r1 - mid.astype(F32)).astype(BF16)


def _sigmoid(x):
    return 1.0 / (1.0 + jnp.exp(-x))


def _blk(i, n):
    assert n & (n - 1) == 0
    return i >> (n.bit_length() - 1)


def _off(i, n):
    assert n & (n - 1) == 0
    return i & (n - 1)


def _lower_left(ri, cj, s):
    return (_blk(ri, 2 * s) == _blk(cj, 2 * s)) & (_off(ri, 2 * s) >= s) & (_off(cj, 2 * s) < s)


def _rms(x, g):
    return x * lax.rsqrt(jnp.mean(x * x, axis=-1, keepdims=True) + NORM_EPS) * g


def _ada_kernel(c_ref, w_ref, b_ref, o_ref):
    c = c_ref[...]
    cs = c * _sigmoid(c)
    o_ref[...] = _dot(cs.astype(BF16), w_ref[...].astype(BF16)) + b_ref[...]


def ada_linear(c, w, b):
    L, K, N = w.shape
    M = c.shape[0]
    bn = min(512, N)
    return pl.pallas_call(
        _ada_kernel,
        grid=(L, N // bn),
        in_specs=[pl.BlockSpec((M, K), lambda l, j: (0, 0)),
                  pl.BlockSpec((None, K, bn), lambda l, j: (l, 0, j)),
                  pl.BlockSpec((None, 1, bn), lambda l, j: (l, 0, j))],
        out_specs=pl.BlockSpec((None, M, bn), lambda l, j: (l, 0, j)),
        out_shape=jax.ShapeDtypeStruct((L, M, N), F32),
        compiler_params=_params(("parallel", "parallel")),
        name="ada_linear",
    )(c, w, b.reshape(L, 1, N))


def _rwkv_pre_kernel(*refs, has_v):
    if has_v:
        (x_ref, g_ref, sh_ref, sc_ref, hp_ref, mu_ref, w1_ref, a1_ref, g1_ref, v1_ref,
         xr_ref, xk_ref, xv_ref, tw_ref, av_ref, gg_ref, vv_ref, hl_ref, prev) = refs
    else:
        (x_ref, g_ref, sh_ref, sc_ref, hp_ref, mu_ref, w1_ref, a1_ref, g1_ref,
         xr_ref, xk_ref, xv_ref, tw_ref, av_ref, gg_ref, hl_ref, prev) = refs
    bt = x_ref.shape[0]
    h = _rms(x_ref[...], g_ref[...]) * (1.0 + sc_ref[...]) + sh_ref[...]

    @pl.when(pl.program_id(1) == 0)
    def _():
        prev[...] = hp_ref[...]

    row = lax.broadcasted_iota(jnp.int32, h.shape, 0)
    xx = jnp.where(row == 0, prev[...], pltpu.roll(h, 1, 0)) - h
    last = h[bt - 1:bt, :]
    prev[...] = last
    hl_ref[...] = last
    mu = mu_ref[...]

    def mix(i):
        return (h + xx * mu[i:i + 1, :]).astype(BF16)

    xr_ref[...] = mix(0)
    tw_ref[...] = jnp.tanh(_dot(mix(1), w1_ref[...])).astype(BF16)
    xk_ref[...] = mix(2)
    xv = mix(3)
    xv_ref[...] = xv
    if has_v:
        vv_ref[...] = _dot(xv, v1_ref[...]).astype(BF16)
    av_ref[...] = _dot(mix(4), a1_ref[...]).astype(BF16)
    gg_ref[...] = _sigmoid(_dot(mix(5), g1_ref[...])).astype(BF16)


def rwkv_pre(x, g, shift, scale, h_prev, mu, w1, a1, g1, v1):
    B, T, D = x.shape
    bt = min(256, T)
    has_v = v1 is not None
    row = lambda n: pl.BlockSpec((None, bt, n), lambda b, t: (b, t, 0))
    vec = pl.BlockSpec((None, 1, D), lambda b, t: (b, 0, 0))
    full = lambda a: pl.BlockSpec(a.shape, lambda b, t: (0, 0))
    lora = [w1, a1, g1] + ([v1] if has_v else [])
    outs = [(D, BF16)] * 3 + [(w1.shape[1], BF16), (a1.shape[1], BF16), (g1.shape[1], BF16)]
    if has_v:
        outs.append((v1.shape[1], BF16))
    res = pl.pallas_call(
        functools.partial(_rwkv_pre_kernel, has_v=has_v),
        grid=(B, T // bt),
        in_specs=[row(D), pl.BlockSpec((1, D), lambda b, t: (0, 0)), vec, vec, vec, full(mu)]
                 + [full(a) for a in lora],
        out_specs=[row(n) for n, _ in outs] + [vec],
        out_shape=[jax.ShapeDtypeStruct((B, T, n), dt) for n, dt in outs]
                  + [jax.ShapeDtypeStruct((B, 1, D), F32)],
        scratch_shapes=[pltpu.VMEM((1, D), F32)],
        compiler_params=_params(("parallel", "arbitrary")),
        name="rwkv_pre",
    )(x, g.reshape(1, D), shift.reshape(B, 1, D), scale.reshape(B, 1, D), h_prev.reshape(B, 1, D), mu, *lora)
    return res


def _rwkv_mix_kernel(*refs, C, NC, HP, nT, has_v):
    it = iter(refs)
    xr_ref, xk_ref, xv_ref, tw_ref, av_ref, gg_ref = (next(it) for _ in range(6))
    vv_ref, vf_ref = (next(it), next(it)) if has_v else (None, None)
    wr_ref, wk_ref, wv_ref, w2_ref, a2_ref, g2_ref = (next(it) for _ in range(6))
    v2_ref = next(it) if has_v else None
    pvec_ref, rvec_ref, s0_ref, z_ref, sout_ref = (next(it) for _ in range(5))
    vout_ref = None if has_v else next(it)
    ops_scr, s_scr = next(it), next(it)
    step = pl.program_id(1)
    C2, RB = 2 * C, NC * C

    @pl.when(step == 0)
    def _():
        ops_scr[...] = jnp.zeros(ops_scr.shape, F32)

    r, ld, k, v, kk, b, g = (ops_scr[i] for i in range(7))
    first_of_batch = _off(jnp.maximum(step - 1, 0), nT) == 0
    state = [jnp.where(first_of_batch, s0_ref[p], s_scr[p]) for p in range(HP)]

    lanes = [slice(p * LANES, (p + 1) * LANES) for p in range(HP)]
    same_head = (_blk(lax.broadcasted_iota(jnp.int32, (LANES, LANES), 0), RW_HEAD)
                 == _blk(lax.broadcasted_iota(jnp.int32, (LANES, LANES), 1), RW_HEAD))
    ones_bd = jnp.where(same_head, 1.0, 0.0).astype(BF16)
    pvec = pvec_ref[...]
    PW = 2 * LANES if HP % 2 == 0 else LANES
    ones_pw = jnp.where(_blk(lax.broadcasted_iota(jnp.int32, (PW, PW), 0), RW_HEAD)
                        == _blk(lax.broadcasted_iota(jnp.int32, (PW, PW), 1), RW_HEAD), 1.0, 0.0).astype(BF16)

    def project(q):
        sl = slice(q * PW, (q + 1) * PW)
        w0, a0, v0, k_k, k_a = (pvec[i:i + 1, sl] for i in range(5))
        ops_scr[0, :, sl] = _dot(xr_ref[...], wr_ref[:, sl])
        kraw = _dot(xk_ref[...], wk_ref[:, sl])
        vp = _dot(xv_ref[...], wv_ref[:, sl])
        nz = -(w0 + _dot(tw_ref[...], w2_ref[:, sl]))
        softplus = jnp.maximum(nz, 0.0) + jnp.log(1.0 + jnp.exp(-jnp.abs(nz)))
        ops_scr[1, :, sl] = -jnp.exp(-softplus - 0.5)
        a = _sigmoid(a0 + _dot(av_ref[...], a2_ref[:, sl]))
        if has_v:
            vp = vp + (vf_ref[:, sl] - vp) * _sigmoid(v0 + _dot(vv_ref[...], v2_ref[:, sl]))
        else:
            vout_ref[:, sl] = vp
        ops_scr[3, :, sl] = vp
        ops_scr[6, :, sl] = _dot(gg_ref[...], g2_ref[:, sl])
        kn = kraw * k_k
        kn = kn / jnp.maximum(jnp.sqrt(_dot((kn * kn).astype(BF16), ones_pw)), 1e-12)
        ops_scr[4, :, sl] = kn
        ops_scr[5, :, sl] = kn * a
        ops_scr[2, :, sl] = kraw * (1.0 + (a - 1.0) * k_a)

    n_units = HP * LANES // PW
    n_slots = 2 + (C.bit_length() - 2) + NC
    plan = iter([[u for u in range(n_units) if u * n_slots // n_units == i] for i in range(n_slots)])

    def fill():
        for u in next(plan):
            project(u)

    ti = lax.broadcasted_iota(jnp.int32, (RB, RB), 0)
    tj = lax.broadcasted_iota(jnp.int32, (RB, RB), 1)
    tri = jnp.where((_blk(ti, C) == _blk(tj, C)) & (ti >= tj), 1.0, 0.0).astype(BF16)
    cl = _dot(jnp.concatenate([tri] * 3, axis=1), jnp.concatenate(_split3(ld), axis=0))
    p_in = jnp.exp(cl)
    p_inv = jnp.exp(-cl)
    a_t = -(jnp.exp(cl - ld) * kk)
    r_t = p_in * r
    b_t = p_inv * b
    k_t = p_inv * k
    rows = [slice(c * C, (c + 1) * C) for c in range(NC)]
    cl_end = [cl[(c + 1) * C - 1:(c + 1) * C, :] for c in range(NC)]
    p_rem = [jnp.exp(cl_end[c] - cl[rows[c], :]) for c in range(NC)]
    p_end = [jnp.exp(cl_end[c]) for c in range(NC)]
    b_h = [p_rem[c] * b[rows[c], :] for c in range(NC)]
    k_h = [p_rem[c] * k[rows[c], :] for c in range(NC)]
    rvec = rvec_ref[...]
    r_k, lnx_g, lnx_b = (rvec[i:i + 1, :] for i in range(3))
    rk = r * k * r_k

    first = lax.broadcasted_iota(jnp.int32, (1, LANES), 1) < RW_HEAD

    def stack(x):
        return jnp.concatenate([jnp.where(first, x, 0.0), jnp.where(first, 0.0, x)], axis=0).astype(BF16)

    def fold(x):
        return x[0:C, :] + x[C:C2, :]

    ri = lax.broadcasted_iota(jnp.int32, (C2, C2), 0)
    cj = lax.broadcasted_iota(jnp.int32, (C2, C2), 1)
    same = _blk(ri, C) == _blk(cj, C)
    strict = same & (_off(ri, C) > _off(cj, C))
    incl = same & (_off(ri, C) >= _off(cj, C))
    eye = jnp.where(ri == cj, 1.0, 0.0)

    probs = [(c, p) for c in range(NC) for p in range(HP)]
    a_b = {cp: a_t[rows[cp[0]], lanes[cp[1]]].astype(BF16) for cp in probs}
    r_b = {cp: r_t[rows[cp[0]], lanes[cp[1]]].astype(BF16) for cp in probs}
    v_st = {cp: stack(v[rows[cp[0]], lanes[cp[1]]]) for cp in probs}
    low, rb_f, ak_f, rk_f = {}, {}, {}, {}
    if C2 == LANES:
        t_f64 = lax.broadcasted_iota(jnp.int32, (C, LANES), 0)
        s_f64 = _off(lax.broadcasted_iota(jnp.int32, (C, LANES), 1), C)
        strict_f, incl_f = t_f64 > s_f64, t_f64 >= s_f64
        for c, p in probs:
            q_st = jnp.concatenate([stack(a_t[rows[c], lanes[p]]), stack(r_t[rows[c], lanes[p]])], axis=0)
            w_st = jnp.concatenate([b_t[rows[c], lanes[p]], k_t[rows[c], lanes[p]]], axis=0).astype(BF16)
            gm = _dot_nt(q_st, w_st)
            gr = pltpu.roll(gm, C, 1)
            a0_, a1_, r0_, r1_ = (gm[i * C:(i + 1) * C, :] for i in range(4))
            a0r, a1r, r0r, r1r = (gr[i * C:(i + 1) * C, :] for i in range(4))
            low[c, p] = jnp.concatenate([jnp.where(first & strict_f, a0_, 0.0),
                                         jnp.where(strict_f & ~first, a1r, 0.0)], axis=0)
            ak_f[c, p] = jnp.where(strict_f, jnp.where(first, a0r, a1_), 0.0).astype(BF16)
            rb_f[c, p] = jnp.where(incl_f, jnp.where(first, r0_, r1r), 0.0).astype(BF16)
            rk_f[c, p] = jnp.where(incl_f, jnp.where(first, r0r, r1_), 0.0).astype(BF16)
    else:
        for c, p in probs:
            q_st = jnp.concatenate([stack(a_t[rows[c], lanes[p]]), stack(r_t[rows[c], lanes[p]])], axis=0)
            b_c = b_t[rows[c], lanes[p]].astype(BF16)
            k_c = k_t[rows[c], lanes[p]].astype(BF16)
            gb = _dot_nt(q_st, jnp.concatenate([b_c, b_c], axis=0))
            gk = _dot_nt(q_st, jnp.concatenate([k_c, k_c], axis=0))
            low[c, p] = jnp.where(strict, gb[0:C2, :], 0.0)
            rb_f[c, p] = fold(jnp.where(incl, gb[C2:, :], 0.0)).astype(BF16)
            ak_f[c, p] = fold(jnp.where(strict, gk[0:C2, :], 0.0)).astype(BF16)
            rk_f[c, p] = fold(jnp.where(incl, gk[C2:, :], 0.0)).astype(BF16)
    fill()
    akv = {cp: _dot(ak_f[cp], v_st[cp]) for cp in probs}

    first_level = _lower_left(ri, cj, 1)
    t_inv = {cp: eye + jnp.where(first_level, low[cp], 0.0) for cp in probs}
    s = 2
    while s < C:
        sel = _lower_left(ri, cj, s)
        tb = {cp: t_inv[cp].astype(BF16) for cp in probs}
        ls = {cp: jnp.where(sel, low[cp], 0.0).astype(BF16) for cp in probs}
        if s % 8 == 0:
            groups = range(C2 // (2 * s))

            def lower_rows(m):
                return jnp.concatenate([m[g * 2 * s + s:(g + 1) * 2 * s, :] for g in groups], axis=0)

            mid = {cp: _dot(lower_rows(t_inv[cp]).astype(BF16), ls[cp]).astype(BF16) for cp in probs}
            upd = {cp: _dot(mid[cp], tb[cp]) for cp in probs}
            t_inv = {cp: jnp.concatenate(
                [piece for g in groups for piece in (
                    t_inv[cp][g * 2 * s:g * 2 * s + s, :],
                    t_inv[cp][g * 2 * s + s:(g + 1) * 2 * s, :] + upd[cp][g * s:(g + 1) * s, :])], axis=0)
                for cp in probs}
        else:
            mid = {cp: _dot(tb[cp], ls[cp]).astype(BF16) for cp in probs}
            t_inv = {cp: t_inv[cp] + _dot(mid[cp], tb[cp]) for cp in probs}
        fill()
        s *= 2
    t_f = {cp: fold(t_inv[cp]).astype(BF16) for cp in probs}

    y = {}
    for c in range(NC):
        state_b = [st.astype(BF16) for st in state]
        x = [_dot_nt(a_b[c, p], state_b[p]) + akv[c, p] for p in range(HP)]
        y0 = [_dot_nt(r_b[c, p], state_b[p]) for p in range(HP)]
        u = [_dot(t_f[c, p], stack(x[p])) for p in range(HP)]
        for p in range(HP):
            if C2 == LANES:
                y[c, p] = y0[p] + _dot(jnp.concatenate([rb_f[c, p], rk_f[c, p]], axis=1),
                                       jnp.concatenate([stack(u[p]), v_st[c, p]], axis=0))
            else:
                y[c, p] = y0[p] + _dot(rb_f[c, p], stack(u[p])) + _dot(rk_f[c, p], v_st[c, p])
        upd = []
        for p in range(HP):
            uv = jnp.concatenate([u[p], v[rows[c], lanes[p]]], axis=0).astype(BF16)
            bk = jnp.concatenate([b_h[c][:, lanes[p]], k_h[c][:, lanes[p]]], axis=0).astype(BF16)
            upd.append(_dot_tn(uv, bk))
        state = [state[p] * p_end[c][:, lanes[p]] + jnp.where(same_head, upd[p], 0.0) for p in range(HP)]
        fill()
    for p in range(HP):
        s_scr[p] = state[p]
        sout_ref[p] = state[p]

    def seg(t, slices):
        if slices == 1:
            return _dot(t.astype(BF16), ones_bd)
        return _dot(jnp.concatenate(_split2(t), axis=1), jnp.concatenate([ones_bd, ones_bd], axis=0))

    y_all = [jnp.concatenate([y[c, p] for c in range(NC)], axis=0) for p in range(HP)]
    mean = [seg(y_all[p], 2) * (1.0 / RW_HEAD) for p in range(HP)]
    bonus = [seg(rk[:, lanes[p]], 1) * v[:, lanes[p]] for p in range(HP)]
    fill()
    d = [y_all[p] - mean[p] for p in range(HP)]
    var = [seg(d[p] * d[p], 1) * (1.0 / RW_HEAD) for p in range(HP)]
    for p in range(HP):
        yn = d[p] * lax.rsqrt(var[p] + GN_EPS) * lnx_g[:, lanes[p]] + lnx_b[:, lanes[p]]
        z_ref[:, lanes[p]] = ((yn + bonus[p]) * g[:, lanes[p]]).astype(z_ref.dtype)


def rwkv_mix(acts, v_first, wr, wk, wv, w2, a2, g2, v2, pvecs, rvecs, s0_bd, B, T):
    has_v = v2 is not None
    M, D = acts[0].shape
    HL = min(1024, D)
    HP = HL // LANES
    C = min(CHUNK, T)
    NC = 2 if T % (2 * C) == 0 else 1
    RB = NC * C
    nT = T // RB
    assert nT & (nT - 1) == 0
    S = M // RB
    proj_blk = lambda s: jnp.minimum(s, S - 1)
    rec_blk = lambda s: jnp.maximum(s - 1, 0)
    row = lambda a: pl.BlockSpec((RB, a.shape[1]), lambda j, s: (proj_blk(s), 0))
    col = lambda a: pl.BlockSpec((a.shape[0], HL), lambda j, s: (0, j))
    st = pl.BlockSpec((None, HP, LANES, LANES), lambda j, s: (rec_blk(s) // nT, j, 0, 0))
    weights = [wr, wk, wv, w2, a2, g2] + ([v2] if has_v else [])
    in_specs = ([row(a) for a in acts]
                + ([pl.BlockSpec((RB, HL), lambda j, s: (proj_blk(s), j))] if has_v else [])
                + [col(w) for w in weights] + [col(pvecs), col(rvecs), st])
    out_specs = [pl.BlockSpec((RB, HL), lambda j, s: (rec_blk(s), j)), st]
    out_shape = [jax.ShapeDtypeStruct((M, D), BF16), jax.ShapeDtypeStruct(s0_bd.shape, F32)]
    if not has_v:
        out_specs.append(pl.BlockSpec((RB, HL), lambda j, s: (proj_blk(s), j)))
        out_shape.append(jax.ShapeDtypeStruct((M, D), F32))
    res = pl.pallas_call(
        functools.partial(_rwkv_mix_kernel, C=C, NC=NC, HP=HP, nT=nT, has_v=has_v),
        grid=(D // HL, S + 1),
        in_specs=in_specs, out_specs=out_specs, out_shape=out_shape,
        scratch_shapes=[pltpu.VMEM((7, RB, HL), F32), pltpu.VMEM((HP, LANES, LANES), F32)],
        compiler_params=_params(("parallel", "arbitrary")),
        name="rwkv_mix",
    )(*acts, *([v_first] if has_v else []), *weights, pvecs, rvecs, s0_bd)
    return res


def _gate_operand(gate, B, T, bm, bn):
    N = gate.shape[1]
    col = (lambda j: j) if bn < N else (lambda j: 0)
    if T % bm == 0:
        return gate.reshape(B, 1, N), pl.BlockSpec((None, 1, bn), lambda i, j: ((i * bm) // T, 0, col(j)))
    rows = jnp.broadcast_to(gate[:, None, :], (B, T, N)).reshape(B * T, N)
    return rows, pl.BlockSpec((bm, bn), lambda i, j: (i, col(j)))


def _matmul_res_kernel(a_ref, w_ref, x_ref, gate_ref, o_ref):
    o_ref[...] = x_ref[...] + gate_ref[...] * _dot(a_ref[...], w_ref[...])


def matmul_res(a, w, x, gate, B, T):
    M, K = a.shape
    N = w.shape[1]
    bm, bn = min(1024, M), min(512, N)
    gate_arr, gate_spec = _gate_operand(gate, B, T, bm, bn)
    return pl.pallas_call(
        _matmul_res_kernel,
        grid=(M // bm, N // bn),
        in_specs=[pl.BlockSpec((bm, K), lambda i, j: (i, 0)),
                  pl.BlockSpec((K, bn), lambda i, j: (0, j)),
                  pl.BlockSpec((bm, bn), lambda i, j: (i, j)), gate_spec],
        out_specs=pl.BlockSpec((bm, bn), lambda i, j: (i, j)),
        out_shape=jax.ShapeDtypeStruct((M, N), F32),
        compiler_params=_params(("parallel", "arbitrary")),
        name="matmul_res",
    )(a, w, x, gate_arr)


def _mlp_kernel(x_ref, g_ref, sh_ref, sc_ref, gate_ref, w1_ref, w2_ref, fg_ref, o_ref, h_scr, acc, *, final):
    f = pl.program_id(1)
    bm = x_ref.shape[0]

    def rows_of(ref, rs):
        return ref[...] if ref.shape[0] == 1 else ref[rs, :]

    def mlp(h):
        return _dot(jnp.square(jnp.maximum(_dot(h, w1_ref[...]), 0.0)).astype(BF16), w2_ref[...])

    @pl.when(f == 0)
    def _():
        parts = 4 if bm % 64 == 0 else 1
        for c in range(parts):
            rs = slice(c * bm // parts, (c + 1) * bm // parts)
            h = (_rms(x_ref[rs, :], g_ref[...]) * (1.0 + rows_of(sc_ref, rs)) + rows_of(sh_ref, rs)).astype(BF16)
            h_scr[rs, :] = h
            acc[rs, :] = mlp(h)

    @pl.when(f > 0)
    def _():
        acc[...] += mlp(h_scr[...])

    @pl.when(f == pl.num_programs(1) - 1)
    def _():
        y = x_ref[...] + gate_ref[...] * acc[...]
        o_ref[...] = _rms(y, fg_ref[...]) if final else y


def mlp_res(x, g, shift, scale, gate, w1, w2, layer, final_g, final, B, T):
    M, D = x.shape
    F = w1.shape[2]
    bm = 512 if T % 512 == 0 else min(256, M)
    bf = min(1024, F)
    (sh_arr, vec_spec), (sc_arr, _), (gate_arr, _) = (_gate_operand(v, B, T, bm, D) for v in (shift, scale, gate))
    return pl.pallas_call(
        functools.partial(_mlp_kernel, final=final),
        grid=(M // bm, F // bf),
        in_specs=[pl.BlockSpec((bm, D), lambda i, f: (i, 0)),
                  pl.BlockSpec((1, D), lambda i, f: (0, 0)), vec_spec, vec_spec, vec_spec,
                  pl.BlockSpec((None, D, bf), lambda i, f: (layer, 0, f)),
                  pl.BlockSpec((None, bf, D), lambda i, f: (layer, f, 0)),
                  pl.BlockSpec((1, D), lambda i, f: (0, 0))],
        out_specs=pl.BlockSpec((bm, D), lambda i, f: (i, 0)),
        out_shape=jax.ShapeDtypeStruct((M, D), F32),
        scratch_shapes=[pltpu.VMEM((bm, D), BF16), pltpu.VMEM((bm, D), F32)],
        compiler_params=_params(("parallel", "arbitrary")),
        name="mlp_res",
    )(x, g.reshape(1, D), sh_arr, sc_arr, gate_arr, w1, w2, final_g.reshape(1, D))


def _rope_pair(t, tab):
    prod = t * tab
    return (prod + pltpu.roll(prod, ROPE_DIM, 1))[:, :ROPE_DIM]


def _latent_kernel(x_ref, gx_ref, sh_ref, sc_ref, w_ref, g_ref, tab_ref, ckv_ref, kpe_ref):
    h = (_rms(x_ref[...], gx_ref[...]) * (1.0 + sc_ref[...]) + sh_ref[...]).astype(BF16)
    acc = _dot(h, w_ref[...])
    R = g_ref.shape[1]
    ckv_ref[...] = _rms(acc[:, :R], g_ref[...])
    kpe_ref[...] = _rope_pair(acc[:, R:R + 2 * ROPE_DIM], tab_ref[...])


def latent(x, gx, shift, scale, w, g, tab):
    B, T, D = x.shape
    R = g.shape[0]
    bt = min(512, T)
    vec = pl.BlockSpec((None, 1, D), lambda b, t: (b, 0, 0))
    return pl.pallas_call(
        _latent_kernel,
        grid=(B, T // bt),
        in_specs=[pl.BlockSpec((None, bt, D), lambda b, t: (b, t, 0)),
                  pl.BlockSpec((1, D), lambda b, t: (0, 0)), vec, vec,
                  pl.BlockSpec(w.shape, lambda b, t: (0, 0)),
                  pl.BlockSpec((1, R), lambda b, t: (0, 0)),
                  pl.BlockSpec((bt, 2 * ROPE_DIM), lambda b, t: (t, 0))],
        out_specs=[pl.BlockSpec((None, bt, R), lambda b, t: (b, t, 0)),
                   pl.BlockSpec((None, bt, ROPE_DIM), lambda b, t: (b, t, 0))],
        out_shape=[jax.ShapeDtypeStruct((B, T, R), F32), jax.ShapeDtypeStruct((B, T, ROPE_DIM), F32)],
        compiler_params=_params(("parallel", "parallel")),
        name="latent",
    )(x, gx.reshape(1, D), shift.reshape(B, 1, D), scale.reshape(B, 1, D), w, g.reshape(1, R), tab)


def _kv_expand_kernel(c_ref, pe_ref, wuk_ref, wuv_ref, k_ref, vt_ref):
    c = c_ref[...].astype(BF16)
    kn = _dot(c, wuk_ref[...])
    vv = _dot(c, wuv_ref[...])
    pe = pe_ref[...].astype(BF16)
    for h in range(k_ref.shape[0]):
        k_ref[h, :, 0:NOPE_DIM] = kn[:, h * NOPE_DIM:(h + 1) * NOPE_DIM].astype(BF16)
        k_ref[h, :, NOPE_DIM:NOPE_DIM + ROPE_DIM] = pe
        vt_ref[h, 0:V_DIM, :] = vv[:, h * V_DIM:(h + 1) * V_DIM].T.astype(BF16)
        vt_ref[h, V_DIM:V_ROWS, :] = jnp.ones((V_ROWS - V_DIM, c.shape[0]), BF16)


def kv_expand(ckv, kpe, wuk, wuv, H):
    B, S, R = ckv.shape
    bt = min(512, S)
    hb = min(8, H)
    DK = NOPE_DIM + ROPE_DIM
    return pl.pallas_call(
        _kv_expand_kernel,
        grid=(H // hb, B, S // bt),
        in_specs=[pl.BlockSpec((None, bt, R), lambda j, b, t: (b, t, 0)),
                  pl.BlockSpec((None, bt, ROPE_DIM), lambda j, b, t: (b, t, 0)),
                  pl.BlockSpec((R, hb * NOPE_DIM), lambda j, b, t: (0, j)),
                  pl.BlockSpec((R, hb * V_DIM), lambda j, b, t: (0, j))],
        out_specs=[pl.BlockSpec((None, hb, bt, DK), lambda j, b, t: (b, j, t, 0)),
                   pl.BlockSpec((None, hb, V_ROWS, bt), lambda j, b, t: (b, j, 0, t))],
        out_shape=[jax.ShapeDtypeStruct((B, H, S, DK), BF16), jax.ShapeDtypeStruct((B, H, V_ROWS, S), BF16)],
        compiler_params=_params(("parallel", "parallel", "arbitrary")),
        name="kv_expand",
    )(ckv, kpe, wuk, wuv)


def _wdq_kernel(x_ref, gx_ref, sh_ref, sc_ref, w_ref, g_ref, o_ref):
    h = (_rms(x_ref[...], gx_ref[...]) * (1.0 + sc_ref[...]) + sh_ref[...]).astype(BF16)
    o_ref[...] = _rms(_dot(h, w_ref[...]), g_ref[...]).astype(BF16)


def wdq_norm(x, gx, shift, scale, w, g):
    B, T, D = x.shape
    R = w.shape[1]
    bt = min(1024, T)
    vec = pl.BlockSpec((None, 1, D), lambda b, t: (b, 0, 0))
    return pl.pallas_call(
        _wdq_kernel,
        grid=(B, T // bt),
        in_specs=[pl.BlockSpec((None, bt, D), lambda b, t: (b, t, 0)),
                  pl.BlockSpec((1, D), lambda b, t: (0, 0)), vec, vec,
                  pl.BlockSpec((D, R), lambda b, t: (0, 0)),
                  pl.BlockSpec((1, R), lambda b, t: (0, 0))],
        out_specs=pl.BlockSpec((None, bt, R), lambda b, t: (b, t, 0)),
        out_shape=jax.ShapeDtypeStruct((B, T, R), BF16),
        compiler_params=_params(("parallel", "parallel")),
        name="wdq_norm",
    )(x, gx.reshape(1, D), shift.reshape(B, 1, D), scale.reshape(B, 1, D), w, g.reshape(1, R))


def _wuq_kernel(c_ref, w_ref, tab_ref, q_ref, *, scale):
    acc = _dot(c_ref[...], w_ref[...])
    tab = tab_ref[...]
    W = NOPE_DIM + 2 * ROPE_DIM
    for h in range(q_ref.shape[0]):
        q_ref[h, :, 0:NOPE_DIM] = (acc[:, h * W:h * W + NOPE_DIM] * scale).astype(BF16)
        pe = _rope_pair(acc[:, h * W + NOPE_DIM:(h + 1) * W], tab)
        q_ref[h, :, NOPE_DIM:NOPE_DIM + ROPE_DIM] = (pe * scale).astype(BF16)
        if q_ref.shape[2] > NOPE_DIM + ROPE_DIM:
            pad = q_ref.shape[2] - NOPE_DIM - ROPE_DIM
            q_ref[h, :, NOPE_DIM + ROPE_DIM:] = jnp.zeros((q_ref.shape[1], pad), BF16)


def wuq_rope(cq, w, tab, H, head_major):
    B, T, R = cq.shape
    bt = min(512, T)
    nt = T // bt
    hb = min(8, H)
    W = NOPE_DIM + 2 * ROPE_DIM
    DK = NOPE_DIM + ROPE_DIM
    if head_major:
        out_spec = pl.BlockSpec((hb, bt, DK), lambda j, b, t: (j, b * nt + t, 0))
        out_shape = (H, B * T, DK)
    else:
        out_spec = pl.BlockSpec((None, hb, bt, 2 * LANES), lambda j, b, t: (b, j, t, 0))
        out_shape = (B, H, T, 2 * LANES)
    return pl.pallas_call(
        functools.partial(_wuq_kernel, scale=MLA_SCALE if head_major else MLA_SCALE * LOG2E),
        grid=(H // hb, B, nt),
        in_specs=[pl.BlockSpec((None, bt, R), lambda j, b, t: (b, t, 0)),
                  pl.BlockSpec((R, hb * W), lambda j, b, t: (0, j)),
                  pl.BlockSpec((bt, 2 * ROPE_DIM), lambda j, b, t: (t, 0))],
        out_specs=out_spec,
        out_shape=jax.ShapeDtypeStruct(out_shape, BF16),
        compiler_params=_params(("parallel", "parallel", "parallel")),
        name="wuq_rope",
    )(cq, w, tab)


def _visible(qpos, kpos):
    return _blk(kpos, CHUNK) <= _blk(qpos, CHUNK)


def _flash_kernel(q_ref, k_ref, vt_ref, o_ref, qt_scr, s_a, s_b, m_scr, acc_scr, *, tile):
    hb = q_ref.shape[0]
    ha = hb // 2
    first, second = range(0, ha), range(ha, hb)
    qi = pl.program_id(2)
    qpos = qi * tile + lax.broadcasted_iota(jnp.int32, (1, tile), 1)
    m_scr[...] = jnp.full(m_scr.shape, -jnp.inf, F32)
    acc_scr[...] = jnp.zeros(acc_scr.shape, F32)
    DK = NOPE_DIM + ROPE_DIM
    for h in range(hb):
        qf = q_ref[h].astype(F32)
        qt_scr[h, 0:LANES, :] = qf[:, 0:LANES].T.astype(BF16)
        qt_scr[h, LANES:DK, :] = qf[:, LANES:2 * LANES].T[0:DK - LANES, :].astype(BF16)

    def scores(ki, heads, s_ref):
        start = pl.multiple_of(ki * tile, tile)
        for h in heads:
            s_ref[h - heads[0]] = _dot(k_ref[h, pl.ds(start, tile), :], qt_scr[h])

    def softmax_pv(ki, heads, s_ref, masked):
        start = pl.multiple_of(ki * tile, tile)
        for h in heads:
            s = s_ref[h - heads[0]]
            if masked:
                kpos = ki * tile + lax.broadcasted_iota(jnp.int32, (tile, 1), 0)
                s = jnp.where(_visible(qpos, kpos), s, -jnp.inf)
            m = m_scr[h]
            m_new = jnp.maximum(m, jnp.max(s, axis=0, keepdims=True))
            p = jnp.exp2(s - m_new).astype(BF16)
            acc_scr[h] = jnp.exp2(m - m_new) * acc_scr[h] + _dot(vt_ref[h, :, pl.ds(start, tile)], p)
            m_scr[h] = m_new

    def one_tile(ki):
        scores(ki, second, s_b)
        softmax_pv(ki, first, s_a, False)
        scores(ki + 1, first, s_a)
        softmax_pv(ki, second, s_b, False)

    def two_tiles(kk, carry):
        one_tile(2 * kk)
        one_tile(2 * kk + 1)
        return carry

    scores(0, first, s_a)
    lax.fori_loop(0, qi // 2, two_tiles, 0)

    @pl.when(qi % 2 == 1)
    def _():
        one_tile(qi - 1)

    scores(qi, second, s_b)
    softmax_pv(qi, first, s_a, True)
    softmax_pv(qi, second, s_b, True)
    for h in range(hb):
        acc = acc_scr[h]
        inv_l = 1.0 / acc[V_DIM:V_DIM + 1, :]
        o_ref[:, h * V_DIM:(h + 1) * V_DIM] = (acc[0:V_DIM, :] * inv_l).T.astype(o_ref.dtype)


def flash_prompt(q, k, vt):
    B, H, T, QW = q.shape
    DK = k.shape[3]
    tile = min(256, T)
    assert tile % CHUNK == 0 and T % tile == 0 and H % 2 == 0 and QW == 2 * LANES
    hb = next(n for n in (8, 4, 2) if H % n == 0)
    return pl.pallas_call(
        functools.partial(_flash_kernel, tile=tile),
        grid=(B, H // hb, T // tile),
        in_specs=[pl.BlockSpec((None, hb, tile, QW), lambda b, j, i: (b, j, i, 0)),
                  pl.BlockSpec((None, hb, T, DK), lambda b, j, i: (b, j, 0, 0)),
                  pl.BlockSpec((None, hb, V_ROWS, T), lambda b, j, i: (b, j, 0, 0))],
        out_specs=pl.BlockSpec((None, tile, hb * V_DIM), lambda b, j, i: (b, i, j)),
        out_shape=jax.ShapeDtypeStruct((B, T, H * V_DIM), BF16),
        scratch_shapes=[pltpu.VMEM((hb, DK, tile), BF16),
                        pltpu.VMEM((hb // 2, tile, tile), F32), pltpu.VMEM((hb // 2, tile, tile), F32),
                        pltpu.VMEM((hb, 1, tile), F32), pltpu.VMEM((hb, V_ROWS, tile), F32)],
        compiler_params=_params(("parallel", "parallel", "arbitrary")),
        name="flash_prompt",
    )(q, k, vt)


def _q_absorb_kernel(q_ref, wuk_ref, o_ref):
    o_ref[...] = _dot_nt(q_ref[:, 0:NOPE_DIM], wuk_ref[...]).astype(BF16)


def _attn_latent_kernel(ql_ref, q_ref, qpos_ref, c_ref, pe_ref, o_ref):
    H, T, R = ql_ref.shape
    S = c_ref.shape[0]
    c = c_ref[...].astype(BF16)
    s = (_dot_nt(ql_ref[...].reshape(H * T, R), c)
         + _dot_nt(q_ref[:, :, NOPE_DIM:NOPE_DIM + ROPE_DIM].reshape(H * T, ROPE_DIM), pe_ref[...].astype(BF16)))
    kpos = lax.broadcasted_iota(jnp.int32, (1, S), 1)
    s = jnp.where(_visible(qpos_ref[...], kpos), s, -jnp.inf)
    p = jnp.exp(s - jnp.max(s, axis=-1, keepdims=True))
    l = jnp.sum(p, axis=-1, keepdims=True)
    o_ref[...] = (_dot(p.astype(BF16), c) / l).astype(BF16).reshape(H, T, R)


def _o_expand_kernel(o_ref, wuv_ref, out_ref):
    out_ref[...] = _dot(o_ref[...], wuv_ref[...]).astype(BF16)


def attn_latent(q, ckv, kpe, wuk, wuv, B, T, q0):
    H, M, DK = q.shape
    S, R = ckv.shape[1:]
    q_lat = pl.pallas_call(
        _q_absorb_kernel,
        grid=(H,),
        in_specs=[pl.BlockSpec((None, M, DK), lambda h: (h, 0, 0)),
                  pl.BlockSpec((R, NOPE_DIM), lambda h: (0, h))],
        out_specs=pl.BlockSpec((None, M, R), lambda h: (h, 0, 0)),
        out_shape=jax.ShapeDtypeStruct((H, M, R), BF16),
        compiler_params=_params(("parallel",)),
        name="q_absorb",
    )(q, wuk)
    qpos = jnp.tile(q0 + jnp.arange(T, dtype=jnp.int32), H)[:, None]
    o_lat = pl.pallas_call(
        _attn_latent_kernel,
        grid=(B,),
        in_specs=[pl.BlockSpec((H, T, R), lambda b: (0, b, 0)),
                  pl.BlockSpec((H, T, DK), lambda b: (0, b, 0)),
                  pl.BlockSpec((H * T, 1), lambda b: (0, 0)),
                  pl.BlockSpec((None, S, R), lambda b: (b, 0, 0)),
                  pl.BlockSpec((None, S, ROPE_DIM), lambda b: (b, 0, 0))],
        out_specs=pl.BlockSpec((H, T, R), lambda b: (0, b, 0)),
        out_shape=jax.ShapeDtypeStruct((H, M, R), BF16),
        compiler_params=_params(("parallel",)),
        name="attn_latent",
    )(q_lat, q, qpos, ckv, kpe)
    return pl.pallas_call(
        _o_expand_kernel,
        grid=(H,),
        in_specs=[pl.BlockSpec((None, M, R), lambda h: (h, 0, 0)),
                  pl.BlockSpec((R, V_DIM), lambda h: (0, h))],
        out_specs=pl.BlockSpec((M, V_DIM), lambda h: (0, h)),
        out_shape=jax.ShapeDtypeStruct((M, H * V_DIM), BF16),
        compiler_params=_params(("parallel",)),
        name="o_expand",
    )(o_lat, wuv)


def _pad_cols(w, n):
    return jnp.pad(w, ((0, 0), (0, n - w.shape[1])))


def _pad_rows(w, n):
    return jnp.pad(w, ((0, n - w.shape[0]), (0, 0)))


def _rotate_half_cols(w):
    half = ROPE_DIM // 2
    return jnp.concatenate([-w[..., half:], w[..., :half]], axis=-1)


def _rope_table(pos):
    half = ROPE_DIM // 2
    inv = ROPE_THETA ** (-jnp.arange(half, dtype=F32) / half)
    ang = pos.astype(F32)[:, None] * inv[None, :]
    cos, sin = jnp.cos(ang), jnp.sin(ang)
    return jnp.concatenate([cos, cos, sin, sin], axis=-1)


def _block_diag_states(s):
    B, H, N, _ = s.shape
    s = s.reshape(B, H // 2, 2, N, N)
    z = jnp.zeros_like(s[:, :, 0])
    top = jnp.concatenate([s[:, :, 0], z], axis=-1)
    bot = jnp.concatenate([z, s[:, :, 1]], axis=-1)
    return jnp.concatenate([top, bot], axis=-2)


def _diag_states(s):
    B, HP = s.shape[:2]
    N = RW_HEAD
    return jnp.stack([s[:, :, :N, :N], s[:, :, N:, N:]], axis=2).reshape(B, 2 * HP, N, N)


def _prepare(W):
    D = W['rw_wr'].shape[1]
    P = {}
    bf = lambda a: a.astype(BF16)
    P['mlp_w1'], P['mlp_w2'] = bf(W['mlp_w1']), bf(W['mlp_w2'])
    for n in ('rw_wr', 'rw_wk', 'rw_wv', 'rw_wo', 'rw_g1', 'rw_g2', 'mla_wdq', 'mla_wo'):
        P[n] = [bf(W[n][l]) for l in range(W[n].shape[0])]
    NA = W['rw_wr'].shape[0]
    P['rw_w1'] = [bf(_pad_cols(W['rw_w1'][l], LORA_PAD)) for l in range(NA)]
    P['rw_w2'] = [bf(_pad_rows(W['rw_w2'][l], LORA_PAD)) for l in range(NA)]
    P['rw_a1'] = [bf(_pad_cols(W['rw_a1'][l], LORA_PAD)) for l in range(NA)]
    P['rw_a2'] = [bf(_pad_rows(W['rw_a2'][l], LORA_PAD)) for l in range(NA)]
    P['rw_v1'] = [bf(_pad_cols(W['rw_v1'][l], LORA_PAD)) for l in range(NA - 1)]
    P['rw_v2'] = [bf(_pad_rows(W['rw_v2'][l], LORA_PAD)) for l in range(NA - 1)]
    zeros = jnp.zeros((D,), F32)
    P['proj_vecs'], P['recur_vecs'] = [], []
    for l in range(NA):
        v0 = W['rw_v0'][l - 1] if l > 0 else zeros
        P['proj_vecs'].append(jnp.stack([W['rw_w0'][l], W['rw_a0'][l], v0, W['rw_kk'][l], W['rw_ka'][l],
                                         zeros, zeros, zeros]))
        P['recur_vecs'].append(jnp.stack([W['rw_rk'][l], W['rw_lnx_g'][l], W['rw_lnx_b'][l]] + [zeros] * 5))
    R = W['kv_lat_g'].shape[0]
    wd = W['kv_wd']
    P['kv_wd'] = bf(jnp.concatenate([wd, _rotate_half_cols(wd[:, R:])], axis=1))
    H = W['kv_wuk'].shape[1]
    P['kv_wuk'] = bf(W['kv_wuk'].reshape(R, H * NOPE_DIM))
    P['kv_wuv'] = bf(W['kv_wuv'].reshape(R, H * V_DIM))
    NB, Q = W['mla_wuq'].shape[:2]
    wuq = W['mla_wuq'].reshape(NB, Q, H, NOPE_DIM + ROPE_DIM)
    pe = wuq[..., NOPE_DIM:]
    P['mla_wuq'] = bf(jnp.concatenate([wuq, _rotate_half_cols(pe)], axis=-1).reshape(NB, Q, -1))
    return P


def _trunk(x, mod, kv_mod, pos0, h_prev, s0, past_ckv, past_kpe, W, P):
    B, T, D = x.shape
    M = B * T
    depth = W['ada_w'].shape[0]
    NA = W['rw_wr'].shape[0]
    H = W['kv_wuk'].shape[1]
    tab = _rope_table(pos0 + jnp.arange(T))
    xf = x.reshape(M, D)
    shifts, states = [], []
    v_first = None
    keys = vals = ckv = kpe = None
    for l in range(depth):
        m = mod[l]
        if l < NA:
            has_v = l > 0
            pre = rwkv_pre(xf.reshape(B, T, D), W['norm_mix_g'][l], m[:, 0], m[:, 1], h_prev[l], W['rw_mu'][l],
                           P['rw_w1'][l], P['rw_a1'][l], P['rw_g1'][l], P['rw_v1'][l - 1] if has_v else None)
            shifts.append(pre[-1].reshape(B, D))
            acts = [a.reshape(M, a.shape[-1]) for a in pre[:-1]]
            res = rwkv_mix(acts, v_first, P['rw_wr'][l], P['rw_wk'][l], P['rw_wv'][l], P['rw_w2'][l],
                           P['rw_a2'][l], P['rw_g2'][l], P['rw_v2'][l - 1] if has_v else None,
                           P['proj_vecs'][l], P['recur_vecs'][l], _block_diag_states(s0[l]), B, T)
            z, s_fin = res[0], res[1]
            if not has_v:
                v_first = res[2]
            states.append(_diag_states(s_fin))
            xf = matmul_res(z, P['rw_wo'][l], xf, m[:, 2], B, T)
        else:
            j = l - NA
            cq = wdq_norm(xf.reshape(B, T, D), W['norm_mix_g'][l], m[:, 0], m[:, 1], P['mla_wdq'][j], W['mla_q_g'][j])
            if past_ckv is None:
                q = wuq_rope(cq, P['mla_wuq'][j], tab, H, False)
                o = flash_prompt(q, keys, vals).reshape(M, H * V_DIM)
            else:
                q = wuq_rope(cq.reshape(1, M, -1), P['mla_wuq'][j], jnp.tile(tab, (B, 1)), H, True)
                o = attn_latent(q, keys, vals, P['kv_wuk'], P['kv_wuv'], B, T, pos0)
            xf = matmul_res(o, P['mla_wo'][j], xf, m[:, 2], B, T)
        xf = mlp_res(xf, W['norm_mlp_g'][l], m[:, 3], m[:, 4], m[:, 5], P['mlp_w1'], P['mlp_w2'], l,
                     W['final_g'], l == depth - 1, B, T)
        if l == NA - 1:
            ckv, kpe = latent(xf.reshape(B, T, D), W['kv_norm_g'], kv_mod[:, 0], kv_mod[:, 1],
                              P['kv_wd'], W['kv_lat_g'], tab)
            if past_ckv is None:
                keys, vals = kv_expand(ckv, kpe, P['kv_wuk'], P['kv_wuv'], H)
            else:
                keys = jnp.concatenate([past_ckv, ckv], axis=1)
                vals = jnp.concatenate([past_kpe, kpe], axis=1)
    return xf.reshape(B, T, D), ckv, kpe, jnp.stack(states), jnp.stack(shifts)


def kernel(x_prompt, x_sample, cache_ckv, cache_kpe, state_wkv, state_shift, c_prompt, c_sample, ada_w, ada_b, norm_mix_g, norm_mlp_g, mlp_w1, mlp_w2, rw_mu, rw_w0, rw_w1, rw_w2, rw_a0, rw_a1, rw_a2, rw_v0, rw_v1, rw_v2, rw_g1, rw_g2, rw_wr, rw_wk, rw_wv, rw_wo, rw_kk, rw_ka, rw_rk, rw_lnx_g, rw_lnx_b, kv_ada_w, kv_ada_b, kv_norm_g, kv_wd, kv_lat_g, kv_wuk, kv_wuv, mla_wdq, mla_q_g, mla_wuq, mla_wo, final_g):
    W = dict(ada_w=ada_w, ada_b=ada_b, norm_mix_g=norm_mix_g, norm_mlp_g=norm_mlp_g,
             mlp_w1=mlp_w1, mlp_w2=mlp_w2, rw_mu=rw_mu, rw_w0=rw_w0, rw_w1=rw_w1, rw_w2=rw_w2,
             rw_a0=rw_a0, rw_a1=rw_a1, rw_a2=rw_a2, rw_v0=rw_v0, rw_v1=rw_v1, rw_v2=rw_v2,
             rw_g1=rw_g1, rw_g2=rw_g2, rw_wr=rw_wr, rw_wk=rw_wk, rw_wv=rw_wv, rw_wo=rw_wo,
             rw_kk=rw_kk, rw_ka=rw_ka, rw_rk=rw_rk, rw_lnx_g=rw_lnx_g, rw_lnx_b=rw_lnx_b,
             kv_ada_w=kv_ada_w, kv_ada_b=kv_ada_b, kv_norm_g=kv_norm_g, kv_wd=kv_wd,
             kv_lat_g=kv_lat_g, kv_wuk=kv_wuk, kv_wuv=kv_wuv, mla_wdq=mla_wdq, mla_q_g=mla_q_g,
             mla_wuq=mla_wuq, mla_wo=mla_wo, final_g=final_g)
    P = _prepare(W)
    Bp, Tp, D = x_prompt.shape
    Bs = x_sample.shape[0]
    depth = ada_w.shape[0]
    NA = rw_wr.shape[0]
    c_all = jnp.concatenate([c_prompt, c_sample], axis=0)
    mod = ada_linear(c_all, ada_w, ada_b).reshape(depth, Bp + Bs, N_MOD, D)
    kv_mod = ada_linear(c_all, kv_ada_w[None], kv_ada_b[None]).reshape(Bp + Bs, 2, D)
    h0 = jnp.zeros((NA, Bp, D), F32)
    s0 = jnp.zeros((NA, Bp, D // RW_HEAD, RW_HEAD, RW_HEAD), F32)
    out_p = _trunk(x_prompt, mod[:, :Bp], kv_mod[:Bp], 0, h0, s0, None, None, W, P)
    out_s = _trunk(x_sample, mod[:, Bp:], kv_mod[Bp:], cache_ckv.shape[1], state_shift, state_wkv,
                   cache_ckv, cache_kpe, W, P)
    return (out_p[0], out_s[0]) + out_p[1:] + out_s[1:]
```

```python
import functools

import jax
import jax.numpy as jnp
from jax import lax
from jax.experimental import pallas as pl
from jax.experimental.pallas import tpu as pltpu

F32, BF16 = jnp.float32, jnp.bfloat16

RW_HEAD = 64
CHUNK = 64
GN_EPS = 64e-5
NOPE_DIM = 128
ROPE_DIM = 64
V_DIM = 128
ROPE_THETA = 10000.0
MLA_SCALE = (NOPE_DIM + ROPE_DIM) ** -0.5
LOG2E = 1.4426950408889634
V_ROWS = V_DIM + 16
NORM_EPS = 1e-6
N_MOD = 6

LANES = 128
VMEM_LIMIT = 56 * 1024 * 1024
LORA_PAD = 128


def _params(sem, vmem=VMEM_LIMIT):
    return pltpu.CompilerParams(dimension_semantics=sem, vmem_limit_bytes=vmem)


def _dot(a, b):
    return jnp.dot(a, b, preferred_element_type=F32)


def _dot_nt(a, b):
    return lax.dot_general(a, b, (((1,), (1,)), ((), ())), preferred_element_type=F32)


def _dot_tn(a, b):
    return lax.dot_general(a, b, (((0,), (0,)), ((), ())), preferred_element_type=F32)


def _split2(x):
    hi = x.astype(BF16)
    return hi, (x - hi.astype(F32)).astype(BF16)


def _split3(x):
    hi = x.astype(BF16)
    r1 = x - hi.astype(F32)
    mid = r1.astype(BF16)
    return hi, mid, (r1 - mid.astype(F32)).astype(BF16)


def _sigmoid(x):
    return 1.0 / (1.0 + jnp.exp(-x))


def _blk(i, n):
    assert n & (n - 1) == 0
    return i >> (n.bit_length() - 1)


def _off(i, n):
    assert n & (n - 1) == 0
    return i & (n - 1)


def _lower_left(ri, cj, s):
    return (_blk(ri, 2 * s) == _blk(cj, 2 * s)) & (_off(ri, 2 * s) >= s) & (_off(cj, 2 * s) < s)


def _rms(x, g):
    return x * lax.rsqrt(jnp.mean(x * x, axis=-1, keepdims=True) + NORM_EPS) * g


def _ada_kernel(c_ref, w_ref, b_ref, o_ref):
    c = c_ref[...]
    cs = c * _sigmoid(c)
    o_ref[...] = _dot(cs.astype(BF16), w_ref[...].astype(BF16)) + b_ref[...]


def ada_linear(c, w, b):
    L, K, N = w.shape
    M = c.shape[0]
    bn = min(512, N)
    return pl.pallas_call(
        _ada_kernel,
        grid=(L, N // bn),
        in_specs=[pl.BlockSpec((M, K), lambda l, j: (0, 0)),
                  pl.BlockSpec((None, K, bn), lambda l, j: (l, 0, j)),
                  pl.BlockSpec((None, 1, bn), lambda l, j: (l, 0, j))],
        out_specs=pl.BlockSpec((None, M, bn), lambda l, j: (l, 0, j)),
        out_shape=jax.ShapeDtypeStruct((L, M, N), F32),
        compiler_params=_params(("parallel", "parallel")),
        name="ada_linear",
    )(c, w, b.reshape(L, 1, N))


def _rwkv_pre_kernel(*refs, has_v):
    if has_v:
        (x_ref, g_ref, sh_ref, sc_ref, hp_ref, mu_ref, w1_ref, a1_ref, g1_ref, v1_ref,
         xr_ref, xk_ref, xv_ref, tw_ref, av_ref, gg_ref, vv_ref, hl_ref, prev) = refs
    else:
        (x_ref, g_ref, sh_ref, sc_ref, hp_ref, mu_ref, w1_ref, a1_ref, g1_ref,
         xr_ref, xk_ref, xv_ref, tw_ref, av_ref, gg_ref, hl_ref, prev) = refs
    bt = x_ref.shape[0]
    h = _rms(x_ref[...], g_ref[...]) * (1.0 + sc_ref[...]) + sh_ref[...]

    @pl.when(pl.program_id(1) == 0)
    def _():
        prev[...] = hp_ref[...]

    row = lax.broadcasted_iota(jnp.int32, h.shape, 0)
    xx = jnp.where(row == 0, prev[...], pltpu.roll(h, 1, 0)) - h
    last = h[bt - 1:bt, :]
    prev[...] = last
    hl_ref[...] = last
    mu = mu_ref[...]

    def mix(i):
        return (h + xx * mu[i:i + 1, :]).astype(BF16)

    xr_ref[...] = mix(0)
    tw_ref[...] = jnp.tanh(_dot(mix(1), w1_ref[...])).astype(BF16)
    xk_ref[...] = mix(2)
    xv = mix(3)
    xv_ref[...] = xv
    if has_v:
        vv_ref[...] = _dot(xv, v1_ref[...]).astype(BF16)
    av_ref[...] = _dot(mix(4), a1_ref[...]).astype(BF16)
    gg_ref[...] = _sigmoid(_dot(mix(5), g1_ref[...])).astype(BF16)


def rwkv_pre(x, g, shift, scale, h_prev, mu, w1, a1, g1, v1):
    B, T, D = x.shape
    bt = min(256, T)
    has_v = v1 is not None
    row = lambda n: pl.BlockSpec((None, bt, n), lambda b, t: (b, t, 0))
    vec = pl.BlockSpec((None, 1, D), lambda b, t: (b, 0, 0))
    full = lambda a: pl.BlockSpec(a.shape, lambda b, t: (0, 0))
    lora = [w1, a1, g1] + ([v1] if has_v else [])
    outs = [(D, BF16)] * 3 + [(w1.shape[1], BF16), (a1.shape[1], BF16), (g1.shape[1], BF16)]
    if has_v:
        outs.append((v1.shape[1], BF16))
    res = pl.pallas_call(
        functools.partial(_rwkv_pre_kernel, has_v=has_v),
        grid=(B, T // bt),
        in_specs=[row(D), pl.BlockSpec((1, D), lambda b, t: (0, 0)), vec, vec, vec, full(mu)]
                 + [full(a) for a in lora],
        out_specs=[row(n) for n, _ in outs] + [vec],
        out_shape=[jax.ShapeDtypeStruct((B, T, n), dt) for n, dt in outs]
                  + [jax.ShapeDtypeStruct((B, 1, D), F32)],
        scratch_shapes=[pltpu.VMEM((1, D), F32)],
        compiler_params=_params(("parallel", "arbitrary")),
        name="rwkv_pre",
    )(x, g.reshape(1, D), shift.reshape(B, 1, D), scale.reshape(B, 1, D), h_prev.reshape(B, 1, D), mu, *lora)
    return res


def _rwkv_mix_kernel(*refs, C, NC, HP, nT, has_v):
    it = iter(refs)
    xr_ref, xk_ref, xv_ref, tw_ref, av_ref, gg_ref = (next(it) for _ in range(6))
    vv_ref, vf_ref = (next(it), next(it)) if has_v else (None, None)
    wr_ref, wk_ref, wv_ref, w2_ref, a2_ref, g2_ref = (next(it) for _ in range(6))
    v2_ref = next(it) if has_v else None
    pvec_ref, rvec_ref, s0_ref, z_ref, sout_ref = (next(it) for _ in range(5))
    vout_ref = None if has_v else next(it)
    ops_scr, s_scr = next(it), next(it)
    step = pl.program_id(1)
    C2, RB = 2 * C, NC * C

    @pl.when(step == 0)
    def _():
        ops_scr[...] = jnp.zeros(ops_scr.shape, F32)

    r, ld, k, v, kk, b, g = (ops_scr[i] for i in range(7))
    first_of_batch = _off(jnp.maximum(step - 1, 0), nT) == 0
    state = [jnp.where(first_of_batch, s0_ref[p], s_scr[p]) for p in range(HP)]

    lanes = [slice(p * LANES, (p + 1) * LANES) for p in range(HP)]
    same_head = (_blk(lax.broadcasted_iota(jnp.int32, (LANES, LANES), 0), RW_HEAD)
                 == _blk(lax.broadcasted_iota(jnp.int32, (LANES, LANES), 1), RW_HEAD))
    ones_bd = jnp.where(same_head, 1.0, 0.0).astype(BF16)
    pvec = pvec_ref[...]
    PW = 2 * LANES if HP % 2 == 0 else LANES
    ones_pw = jnp.where(_blk(lax.broadcasted_iota(jnp.int32, (PW, PW), 0), RW_HEAD)
                        == _blk(lax.broadcasted_iota(jnp.int32, (PW, PW), 1), RW_HEAD), 1.0, 0.0).astype(BF16)

    def project(q):
        sl = slice(q * PW, (q + 1) * PW)
        w0, a0, v0, k_k, k_a = (pvec[i:i + 1, sl] for i in range(5))
        ops_scr[0, :, sl] = _dot(xr_ref[...], wr_ref[:, sl])
        kraw = _dot(xk_ref[...], wk_ref[:, sl])
        vp = _dot(xv_ref[...], wv_ref[:, sl])
        nz = -(w0 + _dot(tw_ref[...], w2_ref[:, sl]))
        softplus = jnp.maximum(nz, 0.0) + jnp.log(1.0 + jnp.exp(-jnp.abs(nz)))
        ops_scr[1, :, sl] = -jnp.exp(-softplus - 0.5)
        a = _sigmoid(a0 + _dot(av_ref[...], a2_ref[:, sl]))
        if has_v:
            vp = vp + (vf_ref[:, sl] - vp) * _sigmoid(v0 + _dot(vv_ref[...], v2_ref[:, sl]))
        else:
            vout_ref[:, sl] = vp
        ops_scr[3, :, sl] = vp
        ops_scr[6, :, sl] = _dot(gg_ref[...], g2_ref[:, sl])
        kn = kraw * k_k
        kn = kn / jnp.maximum(jnp.sqrt(_dot((kn * kn).astype(BF16), ones_pw)), 1e-12)
        ops_scr[4, :, sl] = kn
        ops_scr[5, :, sl] = kn * a
        ops_scr[2, :, sl] = kraw * (1.0 + (a - 1.0) * k_a)

    n_units = HP * LANES // PW
    n_slots = 2 + (C.bit_length() - 2) + NC
    plan = iter([[u for u in range(n_units) if u * n_slots // n_units == i] for i in range(n_slots)])

    def fill():
        for u in next(plan):
            project(u)

    ti = lax.broadcasted_iota(jnp.int32, (RB, RB), 0)
    tj = lax.broadcasted_iota(jnp.int32, (RB, RB), 1)
    tri = jnp.where((_blk(ti, C) == _blk(tj, C)) & (ti >= tj), 1.0, 0.0).astype(BF16)
    cl = _dot(jnp.concatenate([tri] * 3, axis=1), jnp.concatenate(_split3(ld), axis=0))
    p_in = jnp.exp(cl)
    p_inv = jnp.exp(-cl)
    a_t = -(jnp.exp(cl - ld) * kk)
    r_t = p_in * r
    b_t = p_inv * b
    k_t = p_inv * k
    rows = [slice(c * C, (c + 1) * C) for c in range(NC)]
    cl_end = [cl[(c + 1) * C - 1:(c + 1) * C, :] for c in range(NC)]
    p_rem = [jnp.exp(cl_end[c] - cl[rows[c], :]) for c in range(NC)]
    p_end = [jnp.exp(cl_end[c]) for c in range(NC)]
    b_h = [p_rem[c] * b[rows[c], :] for c in range(NC)]
    k_h = [p_rem[c] * k[rows[c], :] for c in range(NC)]
    rvec = rvec_ref[...]
    r_k, lnx_g, lnx_b = (rvec[i:i + 1, :] for i in range(3))
    rk = r * k * r_k

    first = lax.broadcasted_iota(jnp.int32, (1, LANES), 1) < RW_HEAD

    def stack(x):
        return jnp.concatenate([jnp.where(first, x, 0.0), jnp.where(first, 0.0, x)], axis=0).astype(BF16)

    def fold(x):
        return x[0:C, :] + x[C:C2, :]

    ri = lax.broadcasted_iota(jnp.int32, (C2, C2), 0)
    cj = lax.broadcasted_iota(jnp.int32, (C2, C2), 1)
    same = _blk(ri, C) == _blk(cj, C)
    strict = same & (_off(ri, C) > _off(cj, C))
    incl = same & (_off(ri, C) >= _off(cj, C))
    eye = jnp.where(ri == cj, 1.0, 0.0)

    probs = [(c, p) for c in range(NC) for p in range(HP)]
    a_b = {cp: a_t[rows[cp[0]], lanes[cp[1]]].astype(BF16) for cp in probs}
    r_b = {cp: r_t[rows[cp[0]], lanes[cp[1]]].astype(BF16) for cp in probs}
    v_st = {cp: stack(v[rows[cp[0]], lanes[cp[1]]]) for cp in probs}
    low, rb_f, ak_f, rk_f = {}, {}, {}, {}
    if C2 == LANES:
        t_f64 = lax.broadcasted_iota(jnp.int32, (C, LANES), 0)
        s_f64 = _off(lax.broadcasted_iota(jnp.int32, (C, LANES), 1), C)
        strict_f, incl_f = t_f64 > s_f64, t_f64 >= s_f64
        for c, p in probs:
            q_st = jnp.concatenate([stack(a_t[rows[c], lanes[p]]), stack(r_t[rows[c], lanes[p]])], axis=0)
            w_st = jnp.concatenate([b_t[rows[c], lanes[p]], k_t[rows[c], lanes[p]]], axis=0).astype(BF16)
            gm = _dot_nt(q_st, w_st)
            gr = pltpu.roll(gm, C, 1)
            a0_, a1_, r0_, r1_ = (gm[i * C:(i + 1) * C, :] for i in range(4))
            a0r, a1r, r0r, r1r = (gr[i * C:(i + 1) * C, :] for i in range(4))
            low[c, p] = jnp.concatenate([jnp.where(first & strict_f, a0_, 0.0),
                                         jnp.where(strict_f & ~first, a1r, 0.0)], axis=0)
            ak_f[c, p] = jnp.where(strict_f, jnp.where(first, a0r, a1_), 0.0).astype(BF16)
            rb_f[c, p] = jnp.where(incl_f, jnp.where(first, r0_, r1r), 0.0).astype(BF16)
            rk_f[c, p] = jnp.where(incl_f, jnp.where(first, r0r, r1_), 0.0).astype(BF16)
    else:
        for c, p in probs:
            q_st = jnp.concatenate([stack(a_t[rows[c], lanes[p]]), stack(r_t[rows[c], lanes[p]])], axis=0)
            b_c = b_t[rows[c], lanes[p]].astype(BF16)
            k_c = k_t[rows[c], lanes[p]].astype(BF16)
            gb = _dot_nt(q_st, jnp.concatenate([b_c, b_c], axis=0))
            gk = _dot_nt(q_st, jnp.concatenate([k_c, k_c], axis=0))
            low[c, p] = jnp.where(strict, gb[0:C2, :], 0.0)
            rb_f[c, p] = fold(jnp.where(incl, gb[C2:, :], 0.0)).astype(BF16)
            ak_f[c, p] = fold(jnp.where(strict, gk[0:C2, :], 0.0)).astype(BF16)
            rk_f[c, p] = fold(jnp.where(incl, gk[C2:, :], 0.0)).astype(BF16)
    fill()
    akv = {cp: _dot(ak_f[cp], v_st[cp]) for cp in probs}

    first_level = _lower_left(ri, cj, 1)
    t_inv = {cp: eye + jnp.where(first_level, low[cp], 0.0) for cp in probs}
    s = 2
    while s < C:
        sel = _lower_left(ri, cj, s)
        tb = {cp: t_inv[cp].astype(BF16) for cp in probs}
        ls = {cp: jnp.where(sel, low[cp], 0.0).astype(BF16) for cp in probs}
        if s % 8 == 0:
            groups = range(C2 // (2 * s))

            def lower_rows(m):
                return jnp.concatenate([m[g * 2 * s + s:(g + 1) * 2 * s, :] for g in groups], axis=0)

            mid = {cp: _dot(lower_rows(t_inv[cp]).astype(BF16), ls[cp]).astype(BF16) for cp in probs}
            upd = {cp: _dot(mid[cp], tb[cp]) for cp in probs}
            t_inv = {cp: jnp.concatenate(
                [piece for g in groups for piece in (
                    t_inv[cp][g * 2 * s:g * 2 * s + s, :],
                    t_inv[cp][g * 2 * s + s:(g + 1) * 2 * s, :] + upd[cp][g * s:(g + 1) * s, :])], axis=0)
                for cp in probs}
        else:
            mid = {cp: _dot(tb[cp], ls[cp]).astype(BF16) for cp in probs}
            t_inv = {cp: t_inv[cp] + _dot(mid[cp], tb[cp]) for cp in probs}
        fill()
        s *= 2
    t_f = {cp: fold(t_inv[cp]).astype(BF16) for cp in probs}

    y = {}
    for c in range(NC):
        state_b = [st.astype(BF16) for st in state]
        x = [_dot_nt(a_b[c, p], state_b[p]) + akv[c, p] for p in range(HP)]
        y0 = [_dot_nt(r_b[c, p], state_b[p]) for p in range(HP)]
        u = [_dot(t_f[c, p], stack(x[p])) for p in range(HP)]
        for p in range(HP):
            if C2 == LANES:
                y[c, p] = y0[p] + _dot(jnp.concatenate([rb_f[c, p], rk_f[c, p]], axis=1),
                                       jnp.concatenate([stack(u[p]), v_st[c, p]], axis=0))
            else:
                y[c, p] = y0[p] + _dot(rb_f[c, p], stack(u[p])) + _dot(rk_f[c, p], v_st[c, p])
        upd = []
        for p in range(HP):
            uv = jnp.concatenate([u[p], v[rows[c], lanes[p]]], axis=0).astype(BF16)
            bk = jnp.concatenate([b_h[c][:, lanes[p]], k_h[c][:, lanes[p]]], axis=0).astype(BF16)
            upd.append(_dot_tn(uv, bk))
        state = [state[p] * p_end[c][:, lanes[p]] + jnp.where(same_head, upd[p], 0.0) for p in range(HP)]
        fill()
    for p in range(HP):
        s_scr[p] = state[p]
        sout_ref[p] = state[p]

    def seg(t, slices):
        if slices == 1:
            return _dot(t.astype(BF16), ones_bd)
        return _dot(jnp.concatenate(_split2(t), axis=1), jnp.concatenate([ones_bd, ones_bd], axis=0))

    y_all = [jnp.concatenate([y[c, p] for c in range(NC)], axis=0) for p in range(HP)]
    mean = [seg(y_all[p], 2) * (1.0 / RW_HEAD) for p in range(HP)]
    bonus = [seg(rk[:, lanes[p]], 1) * v[:, lanes[p]] for p in range(HP)]
    fill()
    d = [y_all[p] - mean[p] for p in range(HP)]
    var = [seg(d[p] * d[p], 1) * (1.0 / RW_HEAD) for p in range(HP)]
    for p in range(HP):
        yn = d[p] * lax.rsqrt(var[p] + GN_EPS) * lnx_g[:, lanes[p]] + lnx_b[:, lanes[p]]
        z_ref[:, lanes[p]] = ((yn + bonus[p]) * g[:, lanes[p]]).astype(z_ref.dtype)


def rwkv_mix(acts, v_first, wr, wk, wv, w2, a2, g2, v2, pvecs, rvecs, s0_bd, B, T):
    has_v = v2 is not None
    M, D = acts[0].shape
    HL = min(1024, D)
    HP = HL // LANES
    C = min(CHUNK, T)
    NC = 2 if T % (2 * C) == 0 else 1
    RB = NC * C
    nT = T // RB
    assert nT & (nT - 1) == 0
    S = M // RB
    proj_blk = lambda s: jnp.minimum(s, S - 1)
    rec_blk = lambda s: jnp.maximum(s - 1, 0)
    row = lambda a: pl.BlockSpec((RB, a.shape[1]), lambda j, s: (proj_blk(s), 0))
    col = lambda a: pl.BlockSpec((a.shape[0], HL), lambda j, s: (0, j))
    st = pl.BlockSpec((None, HP, LANES, LANES), lambda j, s: (rec_blk(s) // nT, j, 0, 0))
    weights = [wr, wk, wv, w2, a2, g2] + ([v2] if has_v else [])
    in_specs = ([row(a) for a in acts]
                + ([pl.BlockSpec((RB, HL), lambda j, s: (proj_blk(s), j))] if has_v else [])
                + [col(w) for w in weights] + [col(pvecs), col(rvecs), st])
    out_specs = [pl.BlockSpec((RB, HL), lambda j, s: (rec_blk(s), j)), st]
    out_shape = [jax.ShapeDtypeStruct((M, D), BF16), jax.ShapeDtypeStruct(s0_bd.shape, F32)]
    if not has_v:
        out_specs.append(pl.BlockSpec((RB, HL), lambda j, s: (proj_blk(s), j)))
        out_shape.append(jax.ShapeDtypeStruct((M, D), F32))
    res = pl.pallas_call(
        functools.partial(_rwkv_mix_kernel, C=C, NC=NC, HP=HP, nT=nT, has_v=has_v),
        grid=(D // HL, S + 1),
        in_specs=in_specs, out_specs=out_specs, out_shape=out_shape,
        scratch_shapes=[pltpu.VMEM((7, RB, HL), F32), pltpu.VMEM((HP, LANES, LANES), F32)],
        compiler_params=_params(("parallel", "arbitrary")),
        name="rwkv_mix",
    )(*acts, *([v_first] if has_v else []), *weights, pvecs, rvecs, s0_bd)
    return res


def _gate_operand(gate, B, T, bm, bn):
    N = gate.shape[1]
    col = (lambda j: j) if bn < N else (lambda j: 0)
    if T % bm == 0:
        return gate.reshape(B, 1, N), pl.BlockSpec((None, 1, bn), lambda i, j: ((i * bm) // T, 0, col(j)))
    rows = jnp.broadcast_to(gate[:, None, :], (B, T, N)).reshape(B * T, N)
    return rows, pl.BlockSpec((bm, bn), lambda i, j: (i, col(j)))


def _matmul_res_kernel(a_ref, w_ref, x_ref, gate_ref, o_ref):
    o_ref[...] = x_ref[...] + gate_ref[...] * _dot(a_ref[...], w_ref[...])


def matmul_res(a, w, x, gate, B, T):
    M, K = a.shape
    N = w.shape[1]
    bm, bn = min(1024, M), min(512, N)
    gate_arr, gate_spec = _gate_operand(gate, B, T, bm, bn)
    return pl.pallas_call(
        _matmul_res_kernel,
        grid=(M // bm, N // bn),
        in_specs=[pl.BlockSpec((bm, K), lambda i, j: (i, 0)),
                  pl.BlockSpec((K, bn), lambda i, j: (0, j)),
                  pl.BlockSpec((bm, bn), lambda i, j: (i, j)), gate_spec],
        out_specs=pl.BlockSpec((bm, bn), lambda i, j: (i, j)),
        out_shape=jax.ShapeDtypeStruct((M, N), F32),
        compiler_params=_params(("parallel", "arbitrary")),
        name="matmul_res",
    )(a, w, x, gate_arr)


def _mlp_kernel(x_ref, g_ref, sh_ref, sc_ref, gate_ref, w1_ref, w2_ref, fg_ref, o_ref, h_scr, acc, *, final):
    f = pl.program_id(1)
    bm = x_ref.shape[0]

    def rows_of(ref, rs):
        return ref[...] if ref.shape[0] == 1 else ref[rs, :]

    def mlp(h):
        return _dot(jnp.square(jnp.maximum(_dot(h, w1_ref[...]), 0.0)).astype(BF16), w2_ref[...])

    @pl.when(f == 0)
    def _():
        parts = 4 if bm % 64 == 0 else 1
        for c in range(parts):
            rs = slice(c * bm // parts, (c + 1) * bm // parts)
            h = (_rms(x_ref[rs, :], g_ref[...]) * (1.0 + rows_of(sc_ref, rs)) + rows_of(sh_ref, rs)).astype(BF16)
            h_scr[rs, :] = h
            acc[rs, :] = mlp(h)

    @pl.when(f > 0)
    def _():
        acc[...] += mlp(h_scr[...])

    @pl.when(f == pl.num_programs(1) - 1)
    def _():
        y = x_ref[...] + gate_ref[...] * acc[...]
        o_ref[...] = _rms(y, fg_ref[...]) if final else y


def mlp_res(x, g, shift, scale, gate, w1, w2, layer, final_g, final, B, T):
    M, D = x.shape
    F = w1.shape[2]
    bm = 512 if T % 512 == 0 else min(256, M)
    bf = min(1024, F)
    (sh_arr, vec_spec), (sc_arr, _), (gate_arr, _) = (_gate_operand(v, B, T, bm, D) for v in (shift, scale, gate))
    return pl.pallas_call(
        functools.partial(_mlp_kernel, final=final),
        grid=(M // bm, F // bf),
        in_specs=[pl.BlockSpec((bm, D), lambda i, f: (i, 0)),
                  pl.BlockSpec((1, D), lambda i, f: (0, 0)), vec_spec, vec_spec, vec_spec,
                  pl.BlockSpec((None, D, bf), lambda i, f: (layer, 0, f)),
                  pl.BlockSpec((None, bf, D), lambda i, f: (layer, f, 0)),
                  pl.BlockSpec((1, D), lambda i, f: (0, 0))],
        out_specs=pl.BlockSpec((bm, D), lambda i, f: (i, 0)),
        out_shape=jax.ShapeDtypeStruct((M, D), F32),
        scratch_shapes=[pltpu.VMEM((bm, D), BF16), pltpu.VMEM((bm, D), F32)],
        compiler_params=_params(("parallel", "arbitrary")),
        name="mlp_res",
    )(x, g.reshape(1, D), sh_arr, sc_arr, gate_arr, w1, w2, final_g.reshape(1, D))


def _rope_pair(t, tab):
    prod = t * tab
    return (prod + pltpu.roll(prod, ROPE_DIM, 1))[:, :ROPE_DIM]


def _latent_kernel(x_ref, gx_ref, sh_ref, sc_ref, w_ref, g_ref, tab_ref, ckv_ref, kpe_ref):
    h = (_rms(x_ref[...], gx_ref[...]) * (1.0 + sc_ref[...]) + sh_ref[...]).astype(BF16)
    acc = _dot(h, w_ref[...])
    R = g_ref.shape[1]
    ckv_ref[...] = _rms(acc[:, :R], g_ref[...])
    kpe_ref[...] = _rope_pair(acc[:, R:R + 2 * ROPE_DIM], tab_ref[...])


def latent(x, gx, shift, scale, w, g, tab):
    B, T, D = x.shape
    R = g.shape[0]
    bt = min(512, T)
    vec = pl.BlockSpec((None, 1, D), lambda b, t: (b, 0, 0))
    return pl.pallas_call(
        _latent_kernel,
        grid=(B, T // bt),
        in_specs=[pl.BlockSpec((None, bt, D), lambda b, t: (b, t, 0)),
                  pl.BlockSpec((1, D), lambda b, t: (0, 0)), vec, vec,
                  pl.BlockSpec(w.shape, lambda b, t: (0, 0)),
                  pl.BlockSpec((1, R), lambda b, t: (0, 0)),
                  pl.BlockSpec((bt, 2 * ROPE_DIM), lambda b, t: (t, 0))],
        out_specs=[pl.BlockSpec((None, bt, R), lambda b, t: (b, t, 0)),
                   pl.BlockSpec((None, bt, ROPE_DIM), lambda b, t: (b, t, 0))],
        out_shape=[jax.ShapeDtypeStruct((B, T, R), F32), jax.ShapeDtypeStruct((B, T, ROPE_DIM), F32)],
        compiler_params=_params(("parallel", "parallel")),
        name="latent",
    )(x, gx.reshape(1, D), shift.reshape(B, 1, D), scale.reshape(B, 1, D), w, g.reshape(1, R), tab)


def _kv_expand_kernel(c_ref, pe_ref, wuk_ref, wuv_ref, k_ref, vt_ref):
    c = c_ref[...].astype(BF16)
    kn = _dot(c, wuk_ref[...])
    vv = _dot(c, wuv_ref[...])
    pe = pe_ref[...].astype(BF16)
    for h in range(k_ref.shape[0]):
        k_ref[h, :, 0:NOPE_DIM] = kn[:, h * NOPE_DIM:(h + 1) * NOPE_DIM].astype(BF16)
        k_ref[h, :, NOPE_DIM:NOPE_DIM + ROPE_DIM] = pe
        vt_ref[h, 0:V_DIM, :] = vv[:, h * V_DIM:(h + 1) * V_DIM].T.astype(BF16)
        vt_ref[h, V_DIM:V_ROWS, :] = jnp.ones((V_ROWS - V_DIM, c.shape[0]), BF16)


def kv_expand(ckv, kpe, wuk, wuv, H):
    B, S, R = ckv.shape
    bt = min(512, S)
    hb = min(8, H)
    DK = NOPE_DIM + ROPE_DIM
    return pl.pallas_call(
        _kv_expand_kernel,
        grid=(H // hb, B, S // bt),
        in_specs=[pl.BlockSpec((None, bt, R), lambda j, b, t: (b, t, 0)),
                  pl.BlockSpec((None, bt, ROPE_DIM), lambda j, b, t: (b, t, 0)),
                  pl.BlockSpec((R, hb * NOPE_DIM), lambda j, b, t: (0, j)),
                  pl.BlockSpec((R, hb * V_DIM), lambda j, b, t: (0, j))],
        out_specs=[pl.BlockSpec((None, hb, bt, DK), lambda j, b, t: (b, j, t, 0)),
                   pl.BlockSpec((None, hb, V_ROWS, bt), lambda j, b, t: (b, j, 0, t))],
        out_shape=[jax.ShapeDtypeStruct((B, H, S, DK), BF16), jax.ShapeDtypeStruct((B, H, V_ROWS, S), BF16)],
        compiler_params=_params(("parallel", "parallel", "arbitrary")),
        name="kv_expand",
    )(ckv, kpe, wuk, wuv)


def _wdq_kernel(x_ref, gx_ref, sh_ref, sc_ref, w_ref, g_ref, o_ref):
    h = (_rms(x_ref[...], gx_ref[...]) * (1.0 + sc_ref[...]) + sh_ref[...]).astype(BF16)
    o_ref[...] = _rms(_dot(h, w_ref[...]), g_ref[...]).astype(BF16)


def wdq_norm(x, gx, shift, scale, w, g):
    B, T, D = x.shape
    R = w.shape[1]
    bt = min(1024, T)
    vec = pl.BlockSpec((None, 1, D), lambda b, t: (b, 0, 0))
    return pl.pallas_call(
        _wdq_kernel,
        grid=(B, T // bt),
        in_specs=[pl.BlockSpec((None, bt, D), lambda b, t: (b, t, 0)),
                  pl.BlockSpec((1, D), lambda b, t: (0, 0)), vec, vec,
                  pl.BlockSpec((D, R), lambda b, t: (0, 0)),
                  pl.BlockSpec((1, R), lambda b, t: (0, 0))],
        out_specs=pl.BlockSpec((None, bt, R), lambda b, t: (b, t, 0)),
        out_shape=jax.ShapeDtypeStruct((B, T, R), BF16),
        compiler_params=_params(("parallel", "parallel")),
        name="wdq_norm",
    )(x, gx.reshape(1, D), shift.reshape(B, 1, D), scale.reshape(B, 1, D), w, g.reshape(1, R))


def _wuq_kernel(c_ref, w_ref, tab_ref, q_ref, *, scale):
    acc = _dot(c_ref[...], w_ref[...])
    tab = tab_ref[...]
    W = NOPE_DIM + 2 * ROPE_DIM
    for h in range(q_ref.shape[0]):
        q_ref[h, :, 0:NOPE_DIM] = (acc[:, h * W:h * W + NOPE_DIM] * scale).astype(BF16)
        pe = _rope_pair(acc[:, h * W + NOPE_DIM:(h + 1) * W], tab)
        q_ref[h, :, NOPE_DIM:NOPE_DIM + ROPE_DIM] = (pe * scale).astype(BF16)
        if q_ref.shape[2] > NOPE_DIM + ROPE_DIM:
            pad = q_ref.shape[2] - NOPE_DIM - ROPE_DIM
            q_ref[h, :, NOPE_DIM + ROPE_DIM:] = jnp.zeros((q_ref.shape[1], pad), BF16)


def wuq_rope(cq, w, tab, H, head_major):
    B, T, R = cq.shape
    bt = min(512, T)
    nt = T // bt
    hb = min(8, H)
    W = NOPE_DIM + 2 * ROPE_DIM
    DK = NOPE_DIM + ROPE_DIM
    if head_major:
        out_spec = pl.BlockSpec((hb, bt, DK), lambda j, b, t: (j, b * nt + t, 0))
        out_shape = (H, B * T, DK)
    else:
        out_spec = pl.BlockSpec((None, hb, bt, 2 * LANES), lambda j, b, t: (b, j, t, 0))
        out_shape = (B, H, T, 2 * LANES)
    return pl.pallas_call(
        functools.partial(_wuq_kernel, scale=MLA_SCALE if head_major else MLA_SCALE * LOG2E),
        grid=(H // hb, B, nt),
        in_specs=[pl.BlockSpec((None, bt, R), lambda j, b, t: (b, t, 0)),
                  pl.BlockSpec((R, hb * W), lambda j, b, t: (0, j)),
                  pl.BlockSpec((bt, 2 * ROPE_DIM), lambda j, b, t: (t, 0))],
        out_specs=out_spec,
        out_shape=jax.ShapeDtypeStruct(out_shape, BF16),
        compiler_params=_params(("parallel", "parallel", "parallel")),
        name="wuq_rope",
    )(cq, w, tab)


def _visible(qpos, kpos):
    return _blk(kpos, CHUNK) <= _blk(qpos, CHUNK)


def _flash_kernel(q_ref, k_ref, vt_ref, o_ref, *scratch, tile, q_tiles):
    for t in range(q_tiles):
        rows = pl.ds(t * tile, tile)
        _flash_query_tile(pl.program_id(2) * q_tiles + t, q_ref.at[:, rows, :], k_ref, vt_ref, o_ref.at[rows, :],
                          *scratch, tile=tile)


def _flash_query_tile(qi, q_ref, k_ref, vt_ref, o_ref, qt_scr, s_a, s_b, m_scr, acc_scr, *, tile):
    hb = q_ref.shape[0]
    ha = hb // 2
    first, second = range(0, ha), range(ha, hb)
    qpos = qi * tile + lax.broadcasted_iota(jnp.int32, (1, tile), 1)
    m_scr[...] = jnp.full(m_scr.shape, -jnp.inf, F32)
    acc_scr[...] = jnp.zeros(acc_scr.shape, F32)
    DK = NOPE_DIM + ROPE_DIM
    for h in range(hb):
        qf = q_ref[h].astype(F32)
        qt_scr[h, 0:LANES, :] = qf[:, 0:LANES].T.astype(BF16)
        qt_scr[h, LANES:DK, :] = qf[:, LANES:2 * LANES].T[0:DK - LANES, :].astype(BF16)

    def scores(ki, heads, s_ref):
        start = pl.multiple_of(ki * tile, tile)
        for h in heads:
            s_ref[h - heads[0]] = _dot(k_ref[h, pl.ds(start, tile), :], qt_scr[h])

    def softmax_pv(ki, heads, s_ref, masked):
        start = pl.multiple_of(ki * tile, tile)
        for h in heads:
            s = s_ref[h - heads[0]]
            if masked:
                kpos = ki * tile + lax.broadcasted_iota(jnp.int32, (tile, 1), 0)
                s = jnp.where(_visible(qpos, kpos), s, -jnp.inf)
            m = m_scr[h]
            m_new = jnp.maximum(m, jnp.max(s, axis=0, keepdims=True))
            p = jnp.exp2(s - m_new).astype(BF16)
            acc_scr[h] = jnp.exp2(m - m_new) * acc_scr[h] + _dot(vt_ref[h, :, pl.ds(start, tile)], p)
            m_scr[h] = m_new

    def one_tile(ki):
        scores(ki, second, s_b)
        softmax_pv(ki, first, s_a, False)
        scores(ki + 1, first, s_a)
        softmax_pv(ki, second, s_b, False)

    def two_tiles(kk, carry):
        one_tile(2 * kk)
        one_tile(2 * kk + 1)
        return carry

    scores(0, first, s_a)
    lax.fori_loop(0, qi // 2, two_tiles, 0)

    @pl.when(qi % 2 == 1)
    def _():
        one_tile(qi - 1)

    scores(qi, second, s_b)
    softmax_pv(qi, first, s_a, True)
    softmax_pv(qi, second, s_b, True)
    for h in range(hb):
        acc = acc_scr[h]
        inv_l = 1.0 / acc[V_DIM:V_DIM + 1, :]
        o_ref[:, h * V_DIM:(h + 1) * V_DIM] = (acc[0:V_DIM, :] * inv_l).T.astype(o_ref.dtype)


def flash_prompt(q, k, vt):
    B, H, T, QW = q.shape
    DK = k.shape[3]
    tile = min(256, T)
    assert tile % CHUNK == 0 and T % tile == 0 and H % 2 == 0 and QW == 2 * LANES
    hb = next(n for n in (8, 4, 2) if H % n == 0)
    q_tiles = 2 if (T // tile) % 2 == 0 else 1
    rows = q_tiles * tile
    return pl.pallas_call(
        functools.partial(_flash_kernel, tile=tile, q_tiles=q_tiles),
        grid=(B, H // hb, T // rows),
        in_specs=[pl.BlockSpec((None, hb, rows, QW), lambda b, j, i: (b, j, i, 0)),
                  pl.BlockSpec((None, hb, T, DK), lambda b, j, i: (b, j, 0, 0)),
                  pl.BlockSpec((None, hb, V_ROWS, T), lambda b, j, i: (b, j, 0, 0))],
        out_specs=pl.BlockSpec((None, rows, hb * V_DIM), lambda b, j, i: (b, i, j)),
        out_shape=jax.ShapeDtypeStruct((B, T, H * V_DIM), BF16),
        scratch_shapes=[pltpu.VMEM((hb, DK, tile), BF16),
                        pltpu.VMEM((hb // 2, tile, tile), F32), pltpu.VMEM((hb // 2, tile, tile), F32),
                        pltpu.VMEM((hb, 1, tile), F32), pltpu.VMEM((hb, V_ROWS, tile), F32)],
        compiler_params=_params(("parallel", "parallel", "arbitrary")),
        name="flash_prompt",
    )(q, k, vt)


def _q_absorb_kernel(q_ref, wuk_ref, o_ref):
    o_ref[...] = _dot_nt(q_ref[:, 0:NOPE_DIM], wuk_ref[...]).astype(BF16)


def _attn_latent_kernel(ql_ref, q_ref, qpos_ref, c_ref, pe_ref, o_ref):
    H, T, R = ql_ref.shape
    S = c_ref.shape[0]
    c = c_ref[...].astype(BF16)
    s = (_dot_nt(ql_ref[...].reshape(H * T, R), c)
         + _dot_nt(q_ref[:, :, NOPE_DIM:NOPE_DIM + ROPE_DIM].reshape(H * T, ROPE_DIM), pe_ref[...].astype(BF16)))
    kpos = lax.broadcasted_iota(jnp.int32, (1, S), 1)
    s = jnp.where(_visible(qpos_ref[...], kpos), s, -jnp.inf)
    p = jnp.exp(s - jnp.max(s, axis=-1, keepdims=True))
    l = jnp.sum(p, axis=-1, keepdims=True)
    o_ref[...] = (_dot(p.astype(BF16), c) / l).astype(BF16).reshape(H, T, R)


def _o_expand_kernel(o_ref, wuv_ref, out_ref):
    out_ref[...] = _dot(o_ref[...], wuv_ref[...]).astype(BF16)


def attn_latent(q, ckv, kpe, wuk, wuv, B, T, q0):
    H, M, DK = q.shape
    S, R = ckv.shape[1:]
    q_lat = pl.pallas_call(
        _q_absorb_kernel,
        grid=(H,),
        in_specs=[pl.BlockSpec((None, M, DK), lambda h: (h, 0, 0)),
                  pl.BlockSpec((R, NOPE_DIM), lambda h: (0, h))],
        out_specs=pl.BlockSpec((None, M, R), lambda h: (h, 0, 0)),
        out_shape=jax.ShapeDtypeStruct((H, M, R), BF16),
        compiler_params=_params(("parallel",)),
        name="q_absorb",
    )(q, wuk)
    qpos = jnp.tile(q0 + jnp.arange(T, dtype=jnp.int32), H)[:, None]
    o_lat = pl.pallas_call(
        _attn_latent_kernel,
        grid=(B,),
        in_specs=[pl.BlockSpec((H, T, R), lambda b: (0, b, 0)),
                  pl.BlockSpec((H, T, DK), lambda b: (0, b, 0)),
                  pl.BlockSpec((H * T, 1), lambda b: (0, 0)),
                  pl.BlockSpec((None, S, R), lambda b: (b, 0, 0)),
                  pl.BlockSpec((None, S, ROPE_DIM), lambda b: (b, 0, 0))],
        out_specs=pl.BlockSpec((H, T, R), lambda b: (0, b, 0)),
        out_shape=jax.ShapeDtypeStruct((H, M, R), BF16),
        compiler_params=_params(("parallel",)),
        name="attn_latent",
    )(q_lat, q, qpos, ckv, kpe)
    return pl.pallas_call(
        _o_expand_kernel,
        grid=(H,),
        in_specs=[pl.BlockSpec((None, M, R), lambda h: (h, 0, 0)),
                  pl.BlockSpec((R, V_DIM), lambda h: (0, h))],
        out_specs=pl.BlockSpec((M, V_DIM), lambda h: (0, h)),
        out_shape=jax.ShapeDtypeStruct((M, H * V_DIM), BF16),
        compiler_params=_params(("parallel",)),
        name="o_expand",
    )(o_lat, wuv)


def _pad_cols(w, n):
    return jnp.pad(w, ((0, 0), (0, n - w.shape[1])))


def _pad_rows(w, n):
    return jnp.pad(w, ((0, n - w.shape[0]), (0, 0)))


def _rotate_half_cols(w):
    half = ROPE_DIM // 2
    return jnp.concatenate([-w[..., half:], w[..., :half]], axis=-1)


def _rope_table(pos):
    half = ROPE_DIM // 2
    inv = ROPE_THETA ** (-jnp.arange(half, dtype=F32) / half)
    ang = pos.astype(F32)[:, None] * inv[None, :]
    cos, sin = jnp.cos(ang), jnp.sin(ang)
    return jnp.concatenate([cos, cos, sin, sin], axis=-1)


def _block_diag_states(s):
    B, H, N, _ = s.shape
    s = s.reshape(B, H // 2, 2, N, N)
    z = jnp.zeros_like(s[:, :, 0])
    top = jnp.concatenate([s[:, :, 0], z], axis=-1)
    bot = jnp.concatenate([z, s[:, :, 1]], axis=-1)
    return jnp.concatenate([top, bot], axis=-2)


def _diag_states(s):
    B, HP = s.shape[:2]
    N = RW_HEAD
    return jnp.stack([s[:, :, :N, :N], s[:, :, N:, N:]], axis=2).reshape(B, 2 * HP, N, N)


def _prepare(W):
    D = W['rw_wr'].shape[1]
    P = {}
    bf = lambda a: a.astype(BF16)
    P['mlp_w1'], P['mlp_w2'] = bf(W['mlp_w1']), bf(W['mlp_w2'])
    for n in ('rw_wr', 'rw_wk', 'rw_wv', 'rw_wo', 'rw_g1', 'rw_g2', 'mla_wdq', 'mla_wo'):
        P[n] = [bf(W[n][l]) for l in range(W[n].shape[0])]
    NA = W['rw_wr'].shape[0]
    P['rw_w1'] = [bf(_pad_cols(W['rw_w1'][l], LORA_PAD)) for l in range(NA)]
    P['rw_w2'] = [bf(_pad_rows(W['rw_w2'][l], LORA_PAD)) for l in range(NA)]
    P['rw_a1'] = [bf(_pad_cols(W['rw_a1'][l], LORA_PAD)) for l in range(NA)]
    P['rw_a2'] = [bf(_pad_rows(W['rw_a2'][l], LORA_PAD)) for l in range(NA)]
    P['rw_v1'] = [bf(_pad_cols(W['rw_v1'][l], LORA_PAD)) for l in range(NA - 1)]
    P['rw_v2'] = [bf(_pad_rows(W['rw_v2'][l], LORA_PAD)) for l in range(NA - 1)]
    zeros = jnp.zeros((D,), F32)
    P['proj_vecs'], P['recur_vecs'] = [], []
    for l in range(NA):
        v0 = W['rw_v0'][l - 1] if l > 0 else zeros
        P['proj_vecs'].append(jnp.stack([W['rw_w0'][l], W['rw_a0'][l], v0, W['rw_kk'][l], W['rw_ka'][l],
                                         zeros, zeros, zeros]))
        P['recur_vecs'].append(jnp.stack([W['rw_rk'][l], W['rw_lnx_g'][l], W['rw_lnx_b'][l]] + [zeros] * 5))
    R = W['kv_lat_g'].shape[0]
    wd = W['kv_wd']
    P['kv_wd'] = bf(jnp.concatenate([wd, _rotate_half_cols(wd[:, R:])], axis=1))
    H = W['kv_wuk'].shape[1]
    P['kv_wuk'] = bf(W['kv_wuk'].reshape(R, H * NOPE_DIM))
    P['kv_wuv'] = bf(W['kv_wuv'].reshape(R, H * V_DIM))
    NB, Q = W['mla_wuq'].shape[:2]
    wuq = W['mla_wuq'].reshape(NB, Q, H, NOPE_DIM + ROPE_DIM)
    pe = wuq[..., NOPE_DIM:]
    P['mla_wuq'] = bf(jnp.concatenate([wuq, _rotate_half_cols(pe)], axis=-1).reshape(NB, Q, -1))
    return P


def _trunk(x, mod, kv_mod, pos0, h_prev, s0, past_ckv, past_kpe, W, P):
    B, T, D = x.shape
    M = B * T
    depth = W['ada_w'].shape[0]
    NA = W['rw_wr'].shape[0]
    H = W['kv_wuk'].shape[1]
    tab = _rope_table(pos0 + jnp.arange(T))
    xf = x.reshape(M, D)
    shifts, states = [], []
    v_first = None
    keys = vals = ckv = kpe = None
    for l in range(depth):
        m = mod[l]
        if l < NA:
            has_v = l > 0
            pre = rwkv_pre(xf.reshape(B, T, D), W['norm_mix_g'][l], m[:, 0], m[:, 1], h_prev[l], W['rw_mu'][l],
                           P['rw_w1'][l], P['rw_a1'][l], P['rw_g1'][l], P['rw_v1'][l - 1] if has_v else None)
            shifts.append(pre[-1].reshape(B, D))
            acts = [a.reshape(M, a.shape[-1]) for a in pre[:-1]]
            res = rwkv_mix(acts, v_first, P['rw_wr'][l], P['rw_wk'][l], P['rw_wv'][l], P['rw_w2'][l],
                           P['rw_a2'][l], P['rw_g2'][l], P['rw_v2'][l - 1] if has_v else None,
                           P['proj_vecs'][l], P['recur_vecs'][l], _block_diag_states(s0[l]), B, T)
            z, s_fin = res[0], res[1]
            if not has_v:
                v_first = res[2]
            states.append(_diag_states(s_fin))
            xf = matmul_res(z, P['rw_wo'][l], xf, m[:, 2], B, T)
        else:
            j = l - NA
            cq = wdq_norm(xf.reshape(B, T, D), W['norm_mix_g'][l], m[:, 0], m[:, 1], P['mla_wdq'][j], W['mla_q_g'][j])
            if past_ckv is None:
                q = wuq_rope(cq, P['mla_wuq'][j], tab, H, False)
                o = flash_prompt(q, keys, vals).reshape(M, H * V_DIM)
            else:
                q = wuq_rope(cq.reshape(1, M, -1), P['mla_wuq'][j], jnp.tile(tab, (B, 1)), H, True)
                o = attn_latent(q, keys, vals, P['kv_wuk'], P['kv_wuv'], B, T, pos0)
            xf = matmul_res(o, P['mla_wo'][j], xf, m[:, 2], B, T)
        xf = mlp_res(xf, W['norm_mlp_g'][l], m[:, 3], m[:, 4], m[:, 5], P['mlp_w1'], P['mlp_w2'], l,
                     W['final_g'], l == depth - 1, B, T)
        if l == NA - 1:
            ckv, kpe = latent(xf.reshape(B, T, D), W['kv_norm_g'], kv_mod[:, 0], kv_mod[:, 1],
                              P['kv_wd'], W['kv_lat_g'], tab)
            if past_ckv is None:
                keys, vals = kv_expand(ckv, kpe, P['kv_wuk'], P['kv_wuv'], H)
            else:
                keys = jnp.concatenate([past_ckv, ckv], axis=1)
                vals = jnp.concatenate([past_kpe, kpe], axis=1)
    return xf.reshape(B, T, D), ckv, kpe, jnp.stack(states), jnp.stack(shifts)


def kernel(x_prompt, x_sample, cache_ckv, cache_kpe, state_wkv, state_shift, c_prompt, c_sample, ada_w, ada_b, norm_mix_g, norm_mlp_g, mlp_w1, mlp_w2, rw_mu, rw_w0, rw_w1, rw_w2, rw_a0, rw_a1, rw_a2, rw_v0, rw_v1, rw_v2, rw_g1, rw_g2, rw_wr, rw_wk, rw_wv, rw_wo, rw_kk, rw_ka, rw_rk, rw_lnx_g, rw_lnx_b, kv_ada_w, kv_ada_b, kv_norm_g, kv_wd, kv_lat_g, kv_wuk, kv_wuv, mla_wdq, mla_q_g, mla_wuq, mla_wo, final_g):
    W = dict(ada_w=ada_w, ada_b=ada_b, norm_mix_g=norm_mix_g, norm_mlp_g=norm_mlp_g,
             mlp_w1=mlp_w1, mlp_w2=mlp_w2, rw_mu=rw_mu, rw_w0=rw_w0, rw_w1=rw_w1, rw_w2=rw_w2,
             rw_a0=rw_a0, rw_a1=rw_a1, rw_a2=rw_a2, rw_v0=rw_v0, rw_v1=rw_v1, rw_v2=rw_v2,
             rw_g1=rw_g1, rw_g2=rw_g2, rw_wr=rw_wr, rw_wk=rw_wk, rw_wv=rw_wv, rw_wo=rw_wo,
             rw_kk=rw_kk, rw_ka=rw_ka, rw_rk=rw_rk, rw_lnx_g=rw_lnx_g, rw_lnx_b=rw_lnx_b,
             kv_ada_w=kv_ada_w, kv_ada_b=kv_ada_b, kv_norm_g=kv_norm_g, kv_wd=kv_wd,
             kv_lat_g=kv_lat_g, kv_wuk=kv_wuk, kv_wuv=kv_wuv, mla_wdq=mla_wdq, mla_q_g=mla_q_g,
             mla_wuq=mla_wuq, mla_wo=mla_wo, final_g=final_g)
    P = _prepare(W)
    Bp, Tp, D = x_prompt.shape
    Bs = x_sample.shape[0]
    depth = ada_w.shape[0]
    NA = rw_wr.shape[0]
    c_all = jnp.concatenate([c_prompt, c_sample], axis=0)
    mod = ada_linear(c_all, ada_w, ada_b).reshape(depth, Bp + Bs, N_MOD, D)
    kv_mod = ada_linear(c_all, kv_ada_w[None], kv_ada_b[None]).reshape(Bp + Bs, 2, D)
    h0 = jnp.zeros((NA, Bp, D), F32)
    s0 = jnp.zeros((NA, Bp, D // RW_HEAD, RW_HEAD, RW_HEAD), F32)
    out_p = _trunk(x_prompt, mod[:, :Bp], kv_mod[:Bp], 0, h0, s0, None, None, W, P)
    out_s = _trunk(x_sample, mod[:, Bp:], kv_mod[Bp:], cache_ckv.shape[1], state_shift, state_wkv,
                   cache_ckv, cache_kpe, W, P)
    return (out_p[0], out_s[0]) + out_p[1:] + out_s[1:]
```

```python
import functools

import jax
import jax.numpy as jnp
from jax import lax
from jax.experimental import pallas as pl
from jax.experimental.pallas import tpu as pltpu

F32, BF16 = jnp.float32, jnp.bfloat16

RW_HEAD = 64
CHUNK = 64
GN_EPS = 64e-5
NOPE_DIM = 128
ROPE_DIM = 64
V_DIM = 128
ROPE_THETA = 10000.0
MLA_SCALE = (NOPE_DIM + ROPE_DIM) ** -0.5
LOG2E = 1.4426950408889634
V_ROWS = V_DIM + 16
NORM_EPS = 1e-6
N_MOD = 6

LANES = 128
VMEM_LIMIT = 56 * 1024 * 1024
LORA_PAD = 128


def _params(sem, vmem=VMEM_LIMIT):
    return pltpu.CompilerParams(dimension_semantics=sem, vmem_limit_bytes=vmem)


def _dot(a, b):
    return jnp.dot(a, b, preferred_element_type=F32)


def _dot_nt(a, b):
    return lax.dot_general(a, b, (((1,), (1,)), ((), ())), preferred_element_type=F32)


def _dot_tn(a, b):
    return lax.dot_general(a, b, (((0,), (0,)), ((), ())), preferred_element_type=F32)


def _split2(x):
    hi = x.astype(BF16)
    return hi, (x - hi.astype(F32)).astype(BF16)


def _split3(x):
    hi = x.astype(BF16)
    r1 = x - hi.astype(F32)
    mid = r1.astype(BF16)
    return hi, mid, (r1 - mid.astype(F32)).astype(BF16)


def _sigmoid(x):
    return 1.0 / (1.0 + jnp.exp(-x))


def _blk(i, n):
    assert n & (n - 1) == 0
    return i >> (n.bit_length() - 1)


def _off(i, n):
    assert n & (n - 1) == 0
    return i & (n - 1)


def _lower_left(ri, cj, s):
    return (_blk(ri, 2 * s) == _blk(cj, 2 * s)) & (_off(ri, 2 * s) >= s) & (_off(cj, 2 * s) < s)


def _rms(x, g):
    return x * lax.rsqrt(jnp.mean(x * x, axis=-1, keepdims=True) + NORM_EPS) * g


def _ada_kernel(c_ref, w_ref, b_ref, o_ref):
    c = c_ref[...]
    cs = c * _sigmoid(c)
    o_ref[...] = _dot(cs.astype(BF16), w_ref[...].astype(BF16)) + b_ref[...]


def ada_linear(c, w, b):
    L, K, N = w.shape
    M = c.shape[0]
    bn = min(512, N)
    return pl.pallas_call(
        _ada_kernel,
        grid=(L, N // bn),
        in_specs=[pl.BlockSpec((M, K), lambda l, j: (0, 0)),
                  pl.BlockSpec((None, K, bn), lambda l, j: (l, 0, j)),
                  pl.BlockSpec((None, 1, bn), lambda l, j: (l, 0, j))],
        out_specs=pl.BlockSpec((None, M, bn), lambda l, j: (l, 0, j)),
        out_shape=jax.ShapeDtypeStruct((L, M, N), F32),
        compiler_params=_params(("parallel", "parallel")),
        name="ada_linear",
    )(c, w, b.reshape(L, 1, N))


def _rwkv_pre_kernel(*refs, has_v):
    if has_v:
        (x_ref, g_ref, sh_ref, sc_ref, hp_ref, mu_ref, w1_ref, a1_ref, g1_ref, v1_ref,
         xr_ref, xk_ref, xv_ref, tw_ref, av_ref, gg_ref, vv_ref, hl_ref, prev) = refs
    else:
        (x_ref, g_ref, sh_ref, sc_ref, hp_ref, mu_ref, w1_ref, a1_ref, g1_ref,
         xr_ref, xk_ref, xv_ref, tw_ref, av_ref, gg_ref, hl_ref, prev) = refs
    bt = x_ref.shape[0]
    h = _rms(x_ref[...], g_ref[...]) * (1.0 + sc_ref[...]) + sh_ref[...]

    @pl.when(pl.program_id(1) == 0)
    def _():
        prev[...] = hp_ref[...]

    row = lax.broadcasted_iota(jnp.int32, h.shape, 0)
    xx = jnp.where(row == 0, prev[...], pltpu.roll(h, 1, 0)) - h
    last = h[bt - 1:bt, :]
    prev[...] = last
    hl_ref[...] = last
    mu = mu_ref[...]

    def mix(i):
        return (h + xx * mu[i:i + 1, :]).astype(BF16)

    xr_ref[...] = mix(0)
    tw_ref[...] = jnp.tanh(_dot(mix(1), w1_ref[...])).astype(BF16)
    xk_ref[...] = mix(2)
    xv = mix(3)
    xv_ref[...] = xv
    if has_v:
        vv_ref[...] = _dot(xv, v1_ref[...]).astype(BF16)
    av_ref[...] = _dot(mix(4), a1_ref[...]).astype(BF16)
    gg_ref[...] = _sigmoid(_dot(mix(5), g1_ref[...])).astype(BF16)


def rwkv_pre(x, g, shift, scale, h_prev, mu, w1, a1, g1, v1):
    B, T, D = x.shape
    bt = min(512, T)
    has_v = v1 is not None
    row = lambda n: pl.BlockSpec((None, bt, n), lambda b, t: (b, t, 0))
    vec = pl.BlockSpec((None, 1, D), lambda b, t: (b, 0, 0))
    full = lambda a: pl.BlockSpec(a.shape, lambda b, t: (0, 0))
    lora = [w1, a1, g1] + ([v1] if has_v else [])
    outs = [(D, BF16)] * 3 + [(w1.shape[1], BF16), (a1.shape[1], BF16), (g1.shape[1], BF16)]
    if has_v:
        outs.append((v1.shape[1], BF16))
    res = pl.pallas_call(
        functools.partial(_rwkv_pre_kernel, has_v=has_v),
        grid=(B, T // bt),
        in_specs=[row(D), pl.BlockSpec((1, D), lambda b, t: (0, 0)), vec, vec, vec, full(mu)]
                 + [full(a) for a in lora],
        out_specs=[row(n) for n, _ in outs] + [vec],
        out_shape=[jax.ShapeDtypeStruct((B, T, n), dt) for n, dt in outs]
                  + [jax.ShapeDtypeStruct((B, 1, D), F32)],
        scratch_shapes=[pltpu.VMEM((1, D), F32)],
        compiler_params=_params(("parallel", "arbitrary")),
        name="rwkv_pre",
    )(x, g.reshape(1, D), shift.reshape(B, 1, D), scale.reshape(B, 1, D), h_prev.reshape(B, 1, D), mu, *lora)
    return res


def _rwkv_mix_kernel(*refs, C, NC, HP, nT, has_v):
    it = iter(refs)
    xr_ref, xk_ref, xv_ref, tw_ref, av_ref, gg_ref = (next(it) for _ in range(6))
    vv_ref, vf_ref = (next(it), next(it)) if has_v else (None, None)
    wr_ref, wk_ref, wv_ref, w2_ref, a2_ref, g2_ref = (next(it) for _ in range(6))
    v2_ref = next(it) if has_v else None
    pvec_ref, rvec_ref, s0_ref, z_ref, sout_ref = (next(it) for _ in range(5))
    vout_ref = None if has_v else next(it)
    ops_scr, s_scr = next(it), next(it)
    step = pl.program_id(1)
    C2, RB = 2 * C, NC * C

    @pl.when(step == 0)
    def _():
        ops_scr[...] = jnp.zeros(ops_scr.shape, F32)

    r, ld, k, v, kk, b, g = (ops_scr[i] for i in range(7))
    first_of_batch = _off(jnp.maximum(step - 1, 0), nT) == 0
    state = [jnp.where(first_of_batch, s0_ref[p], s_scr[p]) for p in range(HP)]

    lanes = [slice(p * LANES, (p + 1) * LANES) for p in range(HP)]
    same_head = (_blk(lax.broadcasted_iota(jnp.int32, (LANES, LANES), 0), RW_HEAD)
                 == _blk(lax.broadcasted_iota(jnp.int32, (LANES, LANES), 1), RW_HEAD))
    ones_bd = jnp.where(same_head, 1.0, 0.0).astype(BF16)
    pvec = pvec_ref[...]
    PW = 2 * LANES if HP % 2 == 0 else LANES
    ones_pw = jnp.where(_blk(lax.broadcasted_iota(jnp.int32, (PW, PW), 0), RW_HEAD)
                        == _blk(lax.broadcasted_iota(jnp.int32, (PW, PW), 1), RW_HEAD), 1.0, 0.0).astype(BF16)

    def project(q):
        sl = slice(q * PW, (q + 1) * PW)
        w0, a0, v0, k_k, k_a = (pvec[i:i + 1, sl] for i in range(5))
        ops_scr[0, :, sl] = _dot(xr_ref[...], wr_ref[:, sl])
        kraw = _dot(xk_ref[...], wk_ref[:, sl])
        vp = _dot(xv_ref[...], wv_ref[:, sl])
        nz = -(w0 + _dot(tw_ref[...], w2_ref[:, sl]))
        softplus = jnp.maximum(nz, 0.0) + jnp.log(1.0 + jnp.exp(-jnp.abs(nz)))
        ops_scr[1, :, sl] = -jnp.exp(-softplus - 0.5)
        a = _sigmoid(a0 + _dot(av_ref[...], a2_ref[:, sl]))
        if has_v:
            vp = vp + (vf_ref[:, sl] - vp) * _sigmoid(v0 + _dot(vv_ref[...], v2_ref[:, sl]))
        else:
            vout_ref[:, sl] = vp
        ops_scr[3, :, sl] = vp
        ops_scr[6, :, sl] = _dot(gg_ref[...], g2_ref[:, sl])
        kn = kraw * k_k
        kn = kn / jnp.maximum(jnp.sqrt(_dot((kn * kn).astype(BF16), ones_pw)), 1e-12)
        ops_scr[4, :, sl] = kn
        ops_scr[5, :, sl] = kn * a
        ops_scr[2, :, sl] = kraw * (1.0 + (a - 1.0) * k_a)

    n_units = HP * LANES // PW
    n_slots = 2 + (C.bit_length() - 2) + NC
    plan = iter([[u for u in range(n_units) if u * n_slots // n_units == i] for i in range(n_slots)])

    def fill():
        for u in next(plan):
            project(u)

    ti = lax.broadcasted_iota(jnp.int32, (RB, RB), 0)
    tj = lax.broadcasted_iota(jnp.int32, (RB, RB), 1)
    tri = jnp.where((_blk(ti, C) == _blk(tj, C)) & (ti >= tj), 1.0, 0.0).astype(BF16)
    cl = _dot(jnp.concatenate([tri] * 3, axis=1), jnp.concatenate(_split3(ld), axis=0))
    p_in = jnp.exp(cl)
    p_inv = jnp.exp(-cl)
    a_t = -(jnp.exp(cl - ld) * kk)
    r_t = p_in * r
    b_t = p_inv * b
    k_t = p_inv * k
    rows = [slice(c * C, (c + 1) * C) for c in range(NC)]
    cl_end = [cl[(c + 1) * C - 1:(c + 1) * C, :] for c in range(NC)]
    p_rem = [jnp.exp(cl_end[c] - cl[rows[c], :]) for c in range(NC)]
    p_end = [jnp.exp(cl_end[c]) for c in range(NC)]
    b_h = [p_rem[c] * b[rows[c], :] for c in range(NC)]
    k_h = [p_rem[c] * k[rows[c], :] for c in range(NC)]
    rvec = rvec_ref[...]
    r_k, lnx_g, lnx_b = (rvec[i:i + 1, :] for i in range(3))
    rk = r * k * r_k

    first = lax.broadcasted_iota(jnp.int32, (1, LANES), 1) < RW_HEAD

    def stack(x):
        return jnp.concatenate([jnp.where(first, x, 0.0), jnp.where(first, 0.0, x)], axis=0).astype(BF16)

    def fold(x):
        return x[0:C, :] + x[C:C2, :]

    ri = lax.broadcasted_iota(jnp.int32, (C2, C2), 0)
    cj = lax.broadcasted_iota(jnp.int32, (C2, C2), 1)
    same = _blk(ri, C) == _blk(cj, C)
    strict = same & (_off(ri, C) > _off(cj, C))
    incl = same & (_off(ri, C) >= _off(cj, C))
    eye = jnp.where(ri == cj, 1.0, 0.0)

    probs = [(c, p) for c in range(NC) for p in range(HP)]
    a_b = {cp: a_t[rows[cp[0]], lanes[cp[1]]].astype(BF16) for cp in probs}
    r_b = {cp: r_t[rows[cp[0]], lanes[cp[1]]].astype(BF16) for cp in probs}
    v_st = {cp: stack(v[rows[cp[0]], lanes[cp[1]]]) for cp in probs}
    low, rb_f, ak_f, rk_f = {}, {}, {}, {}
    if C2 == LANES:
        t_f64 = lax.broadcasted_iota(jnp.int32, (C, LANES), 0)
        s_f64 = _off(lax.broadcasted_iota(jnp.int32, (C, LANES), 1), C)
        strict_f, incl_f = t_f64 > s_f64, t_f64 >= s_f64
        for c, p in probs:
            q_st = jnp.concatenate([stack(a_t[rows[c], lanes[p]]), stack(r_t[rows[c], lanes[p]])], axis=0)
            w_st = jnp.concatenate([b_t[rows[c], lanes[p]], k_t[rows[c], lanes[p]]], axis=0).astype(BF16)
            gm = _dot_nt(q_st, w_st)
            gr = pltpu.roll(gm, C, 1)
            a0_, a1_, r0_, r1_ = (gm[i * C:(i + 1) * C, :] for i in range(4))
            a0r, a1r, r0r, r1r = (gr[i * C:(i + 1) * C, :] for i in range(4))
            low[c, p] = jnp.concatenate([jnp.where(first & strict_f, a0_, 0.0),
                                         jnp.where(strict_f & ~first, a1r, 0.0)], axis=0)
            ak_f[c, p] = jnp.where(strict_f, jnp.where(first, a0r, a1_), 0.0).astype(BF16)
            rb_f[c, p] = jnp.where(incl_f, jnp.where(first, r0_, r1r), 0.0).astype(BF16)
            rk_f[c, p] = jnp.where(incl_f, jnp.where(first, r0r, r1_), 0.0).astype(BF16)
    else:
        for c, p in probs:
            q_st = jnp.concatenate([stack(a_t[rows[c], lanes[p]]), stack(r_t[rows[c], lanes[p]])], axis=0)
            b_c = b_t[rows[c], lanes[p]].astype(BF16)
            k_c = k_t[rows[c], lanes[p]].astype(BF16)
            gb = _dot_nt(q_st, jnp.concatenate([b_c, b_c], axis=0))
            gk = _dot_nt(q_st, jnp.concatenate([k_c, k_c], axis=0))
            low[c, p] = jnp.where(strict, gb[0:C2, :], 0.0)
            rb_f[c, p] = fold(jnp.where(incl, gb[C2:, :], 0.0)).astype(BF16)
            ak_f[c, p] = fold(jnp.where(strict, gk[0:C2, :], 0.0)).astype(BF16)
            rk_f[c, p] = fold(jnp.where(incl, gk[C2:, :], 0.0)).astype(BF16)
    fill()
    akv = {cp: _dot(ak_f[cp], v_st[cp]) for cp in probs}

    first_level = _lower_left(ri, cj, 1)
    t_inv = {cp: eye + jnp.where(first_level, low[cp], 0.0) for cp in probs}
    s = 2
    while s < C:
        sel = _lower_left(ri, cj, s)
        tb = {cp: t_inv[cp].astype(BF16) for cp in probs}
        ls = {cp: jnp.where(sel, low[cp], 0.0).astype(BF16) for cp in probs}
        if s % 8 == 0:
            groups = range(C2 // (2 * s))

            def lower_rows(m):
                return jnp.concatenate([m[g * 2 * s + s:(g + 1) * 2 * s, :] for g in groups], axis=0)

            mid = {cp: _dot(lower_rows(t_inv[cp]).astype(BF16), ls[cp]).astype(BF16) for cp in probs}
            upd = {cp: _dot(mid[cp], tb[cp]) for cp in probs}
            t_inv = {cp: jnp.concatenate(
                [piece for g in groups for piece in (
                    t_inv[cp][g * 2 * s:g * 2 * s + s, :],
                    t_inv[cp][g * 2 * s + s:(g + 1) * 2 * s, :] + upd[cp][g * s:(g + 1) * s, :])], axis=0)
                for cp in probs}
        else:
            mid = {cp: _dot(tb[cp], ls[cp]).astype(BF16) for cp in probs}
            t_inv = {cp: t_inv[cp] + _dot(mid[cp], tb[cp]) for cp in probs}
        fill()
        s *= 2
    t_f = {cp: fold(t_inv[cp]).astype(BF16) for cp in probs}

    y = {}
    for c in range(NC):
        state_b = [st.astype(BF16) for st in state]
        x = [_dot_nt(a_b[c, p], state_b[p]) + akv[c, p] for p in range(HP)]
        y0 = [_dot_nt(r_b[c, p], state_b[p]) for p in range(HP)]
        u = [_dot(t_f[c, p], stack(x[p])) for p in range(HP)]
        for p in range(HP):
            if C2 == LANES:
                y[c, p] = y0[p] + _dot(jnp.concatenate([rb_f[c, p], rk_f[c, p]], axis=1),
                                       jnp.concatenate([stack(u[p]), v_st[c, p]], axis=0))
            else:
                y[c, p] = y0[p] + _dot(rb_f[c, p], stack(u[p])) + _dot(rk_f[c, p], v_st[c, p])
        upd = []
        for p in range(HP):
            uv = jnp.concatenate([u[p], v[rows[c], lanes[p]]], axis=0).astype(BF16)
            bk = jnp.concatenate([b_h[c][:, lanes[p]], k_h[c][:, lanes[p]]], axis=0).astype(BF16)
            upd.append(_dot_tn(uv, bk))
        state = [state[p] * p_end[c][:, lanes[p]] + jnp.where(same_head, upd[p], 0.0) for p in range(HP)]
        fill()
    for p in range(HP):
        s_scr[p] = state[p]
        sout_ref[p] = state[p]

    def seg(t, slices):
        if slices == 1:
            return _dot(t.astype(BF16), ones_bd)
        return _dot(jnp.concatenate(_split2(t), axis=1), jnp.concatenate([ones_bd, ones_bd], axis=0))

    y_all = [jnp.concatenate([y[c, p] for c in range(NC)], axis=0) for p in range(HP)]
    mean = [seg(y_all[p], 2) * (1.0 / RW_HEAD) for p in range(HP)]
    bonus = [seg(rk[:, lanes[p]], 1) * v[:, lanes[p]] for p in range(HP)]
    fill()
    d = [y_all[p] - mean[p] for p in range(HP)]
    var = [seg(d[p] * d[p], 1) * (1.0 / RW_HEAD) for p in range(HP)]
    for p in range(HP):
        yn = d[p] * lax.rsqrt(var[p] + GN_EPS) * lnx_g[:, lanes[p]] + lnx_b[:, lanes[p]]
        z_ref[:, lanes[p]] = ((yn + bonus[p]) * g[:, lanes[p]]).astype(z_ref.dtype)


def rwkv_mix(acts, v_first, wr, wk, wv, w2, a2, g2, v2, pvecs, rvecs, s0_bd, B, T):
    has_v = v2 is not None
    M, D = acts[0].shape
    HL = min(1024, D)
    HP = HL // LANES
    C = min(CHUNK, T)
    NC = 2 if T % (2 * C) == 0 else 1
    RB = NC * C
    nT = T // RB
    assert nT & (nT - 1) == 0
    S = M // RB
    proj_blk = lambda s: jnp.minimum(s, S - 1)
    rec_blk = lambda s: jnp.maximum(s - 1, 0)
    row = lambda a: pl.BlockSpec((RB, a.shape[1]), lambda j, s: (proj_blk(s), 0))
    col = lambda a: pl.BlockSpec((a.shape[0], HL), lambda j, s: (0, j))
    st = pl.BlockSpec((None, HP, LANES, LANES), lambda j, s: (rec_blk(s) // nT, j, 0, 0))
    weights = [wr, wk, wv, w2, a2, g2] + ([v2] if has_v else [])
    in_specs = ([row(a) for a in acts]
                + ([pl.BlockSpec((RB, HL), lambda j, s: (proj_blk(s), j))] if has_v else [])
                + [col(w) for w in weights] + [col(pvecs), col(rvecs), st])
    out_specs = [pl.BlockSpec((RB, HL), lambda j, s: (rec_blk(s), j)), st]
    out_shape = [jax.ShapeDtypeStruct((M, D), BF16), jax.ShapeDtypeStruct(s0_bd.shape, F32)]
    if not has_v:
        out_specs.append(pl.BlockSpec((RB, HL), lambda j, s: (proj_blk(s), j)))
        out_shape.append(jax.ShapeDtypeStruct((M, D), F32))
    res = pl.pallas_call(
        functools.partial(_rwkv_mix_kernel, C=C, NC=NC, HP=HP, nT=nT, has_v=has_v),
        grid=(D // HL, S + 1),
        in_specs=in_specs, out_specs=out_specs, out_shape=out_shape,
        scratch_shapes=[pltpu.VMEM((7, RB, HL), F32), pltpu.VMEM((HP, LANES, LANES), F32)],
        compiler_params=_params(("parallel", "arbitrary")),
        name="rwkv_mix",
    )(*acts, *([v_first] if has_v else []), *weights, pvecs, rvecs, s0_bd)
    return res


def _gate_operand(gate, B, T, bm, bn):
    N = gate.shape[1]
    col = (lambda j: j) if bn < N else (lambda j: 0)
    if T % bm == 0:
        return gate.reshape(B, 1, N), pl.BlockSpec((None, 1, bn), lambda i, j: ((i * bm) // T, 0, col(j)))
    rows = jnp.broadcast_to(gate[:, None, :], (B, T, N)).reshape(B * T, N)
    return rows, pl.BlockSpec((bm, bn), lambda i, j: (i, col(j)))


def _matmul_res_kernel(a_ref, w_ref, x_ref, gate_ref, o_ref):
    o_ref[...] = x_ref[...] + gate_ref[...] * _dot(a_ref[...], w_ref[...])


def matmul_res(a, w, x, gate, B, T):
    M, K = a.shape
    N = w.shape[1]
    bm, bn = min(1024, M), min(1024 if K <= 2048 else 512, N)
    gate_arr, gate_spec = _gate_operand(gate, B, T, bm, bn)
    return pl.pallas_call(
        _matmul_res_kernel,
        grid=(M // bm, N // bn),
        in_specs=[pl.BlockSpec((bm, K), lambda i, j: (i, 0)),
                  pl.BlockSpec((K, bn), lambda i, j: (0, j)),
                  pl.BlockSpec((bm, bn), lambda i, j: (i, j)), gate_spec],
        out_specs=pl.BlockSpec((bm, bn), lambda i, j: (i, j)),
        out_shape=jax.ShapeDtypeStruct((M, N), F32),
        compiler_params=_params(("parallel", "arbitrary")),
        name="matmul_res",
    )(a, w, x, gate_arr)


def _mlp_kernel(x_ref, g_ref, sh_ref, sc_ref, gate_ref, w1_ref, w2_ref, fg_ref, o_ref, h_scr, acc, *, final):
    f = pl.program_id(1)
    bm = x_ref.shape[0]

    def rows_of(ref, rs):
        return ref[...] if ref.shape[0] == 1 else ref[rs, :]

    def mlp(h):
        return _dot(jnp.square(jnp.maximum(_dot(h, w1_ref[...]), 0.0)).astype(BF16), w2_ref[...])

    @pl.when(f == 0)
    def _():
        parts = 4 if bm % 64 == 0 else 1
        for c in range(parts):
            rs = slice(c * bm // parts, (c + 1) * bm // parts)
            h = (_rms(x_ref[rs, :], g_ref[...]) * (1.0 + rows_of(sc_ref, rs)) + rows_of(sh_ref, rs)).astype(BF16)
            h_scr[rs, :] = h
            acc[rs, :] = mlp(h)

    @pl.when(f > 0)
    def _():
        acc[...] += mlp(h_scr[...])

    @pl.when(f == pl.num_programs(1) - 1)
    def _():
        y = x_ref[...] + gate_ref[...] * acc[...]
        o_ref[...] = _rms(y, fg_ref[...]) if final else y


def mlp_res(x, g, shift, scale, gate, w1, w2, layer, final_g, final, B, T):
    M, D = x.shape
    F = w1.shape[2]
    bm = 512 if T % 512 == 0 else min(256, M)
    bf = min(1024, F)
    (sh_arr, vec_spec), (sc_arr, _), (gate_arr, _) = (_gate_operand(v, B, T, bm, D) for v in (shift, scale, gate))
    return pl.pallas_call(
        functools.partial(_mlp_kernel, final=final),
        grid=(M // bm, F // bf),
        in_specs=[pl.BlockSpec((bm, D), lambda i, f: (i, 0)),
                  pl.BlockSpec((1, D), lambda i, f: (0, 0)), vec_spec, vec_spec, vec_spec,
                  pl.BlockSpec((None, D, bf), lambda i, f: (layer, 0, f)),
                  pl.BlockSpec((None, bf, D), lambda i, f: (layer, f, 0)),
                  pl.BlockSpec((1, D), lambda i, f: (0, 0))],
        out_specs=pl.BlockSpec((bm, D), lambda i, f: (i, 0)),
        out_shape=jax.ShapeDtypeStruct((M, D), F32),
        scratch_shapes=[pltpu.VMEM((bm, D), BF16), pltpu.VMEM((bm, D), F32)],
        compiler_params=_params(("parallel", "arbitrary")),
        name="mlp_res",
    )(x, g.reshape(1, D), sh_arr, sc_arr, gate_arr, w1, w2, final_g.reshape(1, D))


def _rope_pair(t, tab):
    prod = t * tab
    return (prod + pltpu.roll(prod, ROPE_DIM, 1))[:, :ROPE_DIM]


def _latent_kernel(x_ref, gx_ref, sh_ref, sc_ref, w_ref, g_ref, tab_ref, ckv_ref, kpe_ref):
    h = (_rms(x_ref[...], gx_ref[...]) * (1.0 + sc_ref[...]) + sh_ref[...]).astype(BF16)
    acc = _dot(h, w_ref[...])
    R = g_ref.shape[1]
    ckv_ref[...] = _rms(acc[:, :R], g_ref[...])
    kpe_ref[...] = _rope_pair(acc[:, R:R + 2 * ROPE_DIM], tab_ref[...])


def latent(x, gx, shift, scale, w, g, tab):
    B, T, D = x.shape
    R = g.shape[0]
    bt = min(512, T)
    vec = pl.BlockSpec((None, 1, D), lambda b, t: (b, 0, 0))
    return pl.pallas_call(
        _latent_kernel,
        grid=(B, T // bt),
        in_specs=[pl.BlockSpec((None, bt, D), lambda b, t: (b, t, 0)),
                  pl.BlockSpec((1, D), lambda b, t: (0, 0)), vec, vec,
                  pl.BlockSpec(w.shape, lambda b, t: (0, 0)),
                  pl.BlockSpec((1, R), lambda b, t: (0, 0)),
                  pl.BlockSpec((bt, 2 * ROPE_DIM), lambda b, t: (t, 0))],
        out_specs=[pl.BlockSpec((None, bt, R), lambda b, t: (b, t, 0)),
                   pl.BlockSpec((None, bt, ROPE_DIM), lambda b, t: (b, t, 0))],
        out_shape=[jax.ShapeDtypeStruct((B, T, R), F32), jax.ShapeDtypeStruct((B, T, ROPE_DIM), F32)],
        compiler_params=_params(("parallel", "parallel")),
        name="latent",
    )(x, gx.reshape(1, D), shift.reshape(B, 1, D), scale.reshape(B, 1, D), w, g.reshape(1, R), tab)


def _kv_expand_kernel(c_ref, pe_ref, wuk_ref, wuv_ref, k_ref, vt_ref):
    c = c_ref[...].astype(BF16)
    kn = _dot(c, wuk_ref[...])
    vv = _dot(c, wuv_ref[...])
    pe = pe_ref[...].astype(BF16)
    for h in range(k_ref.shape[0]):
        k_ref[h, :, 0:NOPE_DIM] = kn[:, h * NOPE_DIM:(h + 1) * NOPE_DIM].astype(BF16)
        k_ref[h, :, NOPE_DIM:NOPE_DIM + ROPE_DIM] = pe
        vt_ref[h, 0:V_DIM, :] = vv[:, h * V_DIM:(h + 1) * V_DIM].T.astype(BF16)
        vt_ref[h, V_DIM:V_ROWS, :] = jnp.ones((V_ROWS - V_DIM, c.shape[0]), BF16)


def kv_expand(ckv, kpe, wuk, wuv, H):
    B, S, R = ckv.shape
    bt = min(1024, S)
    hb = min(8, H)
    DK = NOPE_DIM + ROPE_DIM
    return pl.pallas_call(
        _kv_expand_kernel,
        grid=(H // hb, B, S // bt),
        in_specs=[pl.BlockSpec((None, bt, R), lambda j, b, t: (b, t, 0)),
                  pl.BlockSpec((None, bt, ROPE_DIM), lambda j, b, t: (b, t, 0)),
                  pl.BlockSpec((R, hb * NOPE_DIM), lambda j, b, t: (0, j)),
                  pl.BlockSpec((R, hb * V_DIM), lambda j, b, t: (0, j))],
        out_specs=[pl.BlockSpec((None, hb, bt, DK), lambda j, b, t: (b, j, t, 0)),
                   pl.BlockSpec((None, hb, V_ROWS, bt), lambda j, b, t: (b, j, 0, t))],
        out_shape=[jax.ShapeDtypeStruct((B, H, S, DK), BF16), jax.ShapeDtypeStruct((B, H, V_ROWS, S), BF16)],
        compiler_params=_params(("parallel", "parallel", "arbitrary")),
        name="kv_expand",
    )(ckv, kpe, wuk, wuv)


def _wdq_kernel(x_ref, gx_ref, sh_ref, sc_ref, w_ref, g_ref, o_ref):
    h = (_rms(x_ref[...], gx_ref[...]) * (1.0 + sc_ref[...]) + sh_ref[...]).astype(BF16)
    o_ref[...] = _rms(_dot(h, w_ref[...]), g_ref[...]).astype(BF16)


def wdq_norm(x, gx, shift, scale, w, g):
    B, T, D = x.shape
    R = w.shape[1]
    bt = min(1024, T)
    vec = pl.BlockSpec((None, 1, D), lambda b, t: (b, 0, 0))
    return pl.pallas_call(
        _wdq_kernel,
        grid=(B, T // bt),
        in_specs=[pl.BlockSpec((None, bt, D), lambda b, t: (b, t, 0)),
                  pl.BlockSpec((1, D), lambda b, t: (0, 0)), vec, vec,
                  pl.BlockSpec((D, R), lambda b, t: (0, 0)),
                  pl.BlockSpec((1, R), lambda b, t: (0, 0))],
        out_specs=pl.BlockSpec((None, bt, R), lambda b, t: (b, t, 0)),
        out_shape=jax.ShapeDtypeStruct((B, T, R), BF16),
        compiler_params=_params(("parallel", "parallel")),
        name="wdq_norm",
    )(x, gx.reshape(1, D), shift.reshape(B, 1, D), scale.reshape(B, 1, D), w, g.reshape(1, R))


def _wuq_kernel(c_ref, w_ref, tab_ref, q_ref, *, scale):
    acc = _dot(c_ref[...], w_ref[...])
    tab = tab_ref[...]
    W = NOPE_DIM + 2 * ROPE_DIM
    for h in range(q_ref.shape[0]):
        q_ref[h, :, 0:NOPE_DIM] = (acc[:, h * W:h * W + NOPE_DIM] * scale).astype(BF16)
        pe = _rope_pair(acc[:, h * W + NOPE_DIM:(h + 1) * W], tab)
        q_ref[h, :, NOPE_DIM:NOPE_DIM + ROPE_DIM] = (pe * scale).astype(BF16)
        if q_ref.shape[2] > NOPE_DIM + ROPE_DIM:
            pad = q_ref.shape[2] - NOPE_DIM - ROPE_DIM
            q_ref[h, :, NOPE_DIM + ROPE_DIM:] = jnp.zeros((q_ref.shape[1], pad), BF16)


def wuq_rope(cq, w, tab, H, head_major):
    B, T, R = cq.shape
    bt = min(1024, T)
    nt = T // bt
    hb = min(8, H)
    W = NOPE_DIM + 2 * ROPE_DIM
    DK = NOPE_DIM + ROPE_DIM
    if head_major:
        out_spec = pl.BlockSpec((hb, bt, DK), lambda j, b, t: (j, b * nt + t, 0))
        out_shape = (H, B * T, DK)
    else:
        out_spec = pl.BlockSpec((None, hb, bt, 2 * LANES), lambda j, b, t: (b, j, t, 0))
        out_shape = (B, H, T, 2 * LANES)
    return pl.pallas_call(
        functools.partial(_wuq_kernel, scale=MLA_SCALE if head_major else MLA_SCALE * LOG2E),
        grid=(H // hb, B, nt),
        in_specs=[pl.BlockSpec((None, bt, R), lambda j, b, t: (b, t, 0)),
                  pl.BlockSpec((R, hb * W), lambda j, b, t: (0, j)),
                  pl.BlockSpec((bt, 2 * ROPE_DIM), lambda j, b, t: (t, 0))],
        out_specs=out_spec,
        out_shape=jax.ShapeDtypeStruct(out_shape, BF16),
        compiler_params=_params(("parallel", "parallel", "parallel")),
        name="wuq_rope",
    )(cq, w, tab)


def _visible(qpos, kpos):
    return _blk(kpos, CHUNK) <= _blk(qpos, CHUNK)


def _flash_kernel(q_ref, k_ref, vt_ref, o_ref, *scratch, tile, q_tiles):
    for t in range(q_tiles):
        rows = pl.ds(t * tile, tile)
        _flash_query_tile(pl.program_id(2) * q_tiles + t, q_ref.at[:, rows, :], k_ref, vt_ref, o_ref.at[rows, :],
                          *scratch, tile=tile)


def _flash_query_tile(qi, q_ref, k_ref, vt_ref, o_ref, qt_scr, s_a, s_b, m_scr, acc_scr, *, tile):
    hb = q_ref.shape[0]
    ha = hb // 2
    first, second = range(0, ha), range(ha, hb)
    qpos = qi * tile + lax.broadcasted_iota(jnp.int32, (1, tile), 1)
    m_scr[...] = jnp.full(m_scr.shape, -jnp.inf, F32)
    acc_scr[...] = jnp.zeros(acc_scr.shape, F32)
    DK = NOPE_DIM + ROPE_DIM
    for h in range(hb):
        qf = q_ref[h].astype(F32)
        qt_scr[h, 0:LANES, :] = qf[:, 0:LANES].T.astype(BF16)
        qt_scr[h, LANES:DK, :] = qf[:, LANES:2 * LANES].T[0:DK - LANES, :].astype(BF16)

    def scores(ki, heads, s_ref):
        start = pl.multiple_of(ki * tile, tile)
        for h in heads:
            s_ref[h - heads[0]] = _dot(k_ref[h, pl.ds(start, tile), :], qt_scr[h])

    def softmax_pv(ki, heads, s_ref, masked):
        start = pl.multiple_of(ki * tile, tile)
        for h in heads:
            s = s_ref[h - heads[0]]
            if masked:
                kpos = ki * tile + lax.broadcasted_iota(jnp.int32, (tile, 1), 0)
                s = jnp.where(_visible(qpos, kpos), s, -jnp.inf)
            m = m_scr[h]
            m_new = jnp.maximum(m, jnp.max(s, axis=0, keepdims=True))
            p = jnp.exp2(s - m_new).astype(BF16)
            acc_scr[h] = jnp.exp2(m - m_new) * acc_scr[h] + _dot(vt_ref[h, :, pl.ds(start, tile)], p)
            m_scr[h] = m_new

    def one_tile(ki):
        scores(ki, second, s_b)
        softmax_pv(ki, first, s_a, False)
        scores(ki + 1, first, s_a)
        softmax_pv(ki, second, s_b, False)

    def two_tiles(kk, carry):
        one_tile(2 * kk)
        one_tile(2 * kk + 1)
        return carry

    scores(0, first, s_a)
    lax.fori_loop(0, qi // 2, two_tiles, 0)

    @pl.when(qi % 2 == 1)
    def _():
        one_tile(qi - 1)

    scores(qi, second, s_b)
    softmax_pv(qi, first, s_a, True)
    softmax_pv(qi, second, s_b, True)
    for h in range(hb):
        acc = acc_scr[h]
        inv_l = 1.0 / acc[V_DIM:V_DIM + 1, :]
        o_ref[:, h * V_DIM:(h + 1) * V_DIM] = (acc[0:V_DIM, :] * inv_l).T.astype(o_ref.dtype)


def flash_prompt(q, k, vt):
    B, H, T, QW = q.shape
    DK = k.shape[3]
    tile = min(256, T)
    assert tile % CHUNK == 0 and T % tile == 0 and H % 2 == 0 and QW == 2 * LANES
    hb = next(n for n in (8, 4, 2) if H % n == 0)
    q_tiles = next(n for n in (4, 2, 1) if (T // tile) % n == 0)
    rows = q_tiles * tile
    return pl.pallas_call(
        functools.partial(_flash_kernel, tile=tile, q_tiles=q_tiles),
        grid=(B, H // hb, T // rows),
        in_specs=[pl.BlockSpec((None, hb, rows, QW), lambda b, j, i: (b, j, i, 0)),
                  pl.BlockSpec((None, hb, T, DK), lambda b, j, i: (b, j, 0, 0)),
                  pl.BlockSpec((None, hb, V_ROWS, T), lambda b, j, i: (b, j, 0, 0))],
        out_specs=pl.BlockSpec((None, rows, hb * V_DIM), lambda b, j, i: (b, i, j)),
        out_shape=jax.ShapeDtypeStruct((B, T, H * V_DIM), BF16),
        scratch_shapes=[pltpu.VMEM((hb, DK, tile), BF16),
                        pltpu.VMEM((hb // 2, tile, tile), F32), pltpu.VMEM((hb // 2, tile, tile), F32),
                        pltpu.VMEM((hb, 1, tile), F32), pltpu.VMEM((hb, V_ROWS, tile), F32)],
        compiler_params=_params(("parallel", "parallel", "arbitrary")),
        name="flash_prompt",
    )(q, k, vt)


def _q_absorb_kernel(q_ref, wuk_ref, o_ref):
    o_ref[...] = _dot_nt(q_ref[:, 0:NOPE_DIM], wuk_ref[...]).astype(BF16)


def _attn_latent_kernel(ql_ref, q_ref, qpos_ref, c_ref, pe_ref, o_ref):
    H, T, R = ql_ref.shape
    S = c_ref.shape[0]
    c = c_ref[...].astype(BF16)
    s = (_dot_nt(ql_ref[...].reshape(H * T, R), c)
         + _dot_nt(q_ref[:, :, NOPE_DIM:NOPE_DIM + ROPE_DIM].reshape(H * T, ROPE_DIM), pe_ref[...].astype(BF16)))
    kpos = lax.broadcasted_iota(jnp.int32, (1, S), 1)
    s = jnp.where(_visible(qpos_ref[...], kpos), s, -jnp.inf)
    p = jnp.exp(s - jnp.max(s, axis=-1, keepdims=True))
    l = jnp.sum(p, axis=-1, keepdims=True)
    o_ref[...] = (_dot(p.astype(BF16), c) / l).astype(BF16).reshape(H, T, R)


def _o_expand_kernel(o_ref, wuv_ref, out_ref):
    out_ref[...] = _dot(o_ref[...], wuv_ref[...]).astype(BF16)


def attn_latent(q, ckv, kpe, wuk, wuv, B, T, q0):
    H, M, DK = q.shape
    S, R = ckv.shape[1:]
    q_lat = pl.pallas_call(
        _q_absorb_kernel,
        grid=(H,),
        in_specs=[pl.BlockSpec((None, M, DK), lambda h: (h, 0, 0)),
                  pl.BlockSpec((R, NOPE_DIM), lambda h: (0, h))],
        out_specs=pl.BlockSpec((None, M, R), lambda h: (h, 0, 0)),
        out_shape=jax.ShapeDtypeStruct((H, M, R), BF16),
        compiler_params=_params(("parallel",)),
        name="q_absorb",
    )(q, wuk)
    qpos = jnp.tile(q0 + jnp.arange(T, dtype=jnp.int32), H)[:, None]
    o_lat = pl.pallas_call(
        _attn_latent_kernel,
        grid=(B,),
        in_specs=[pl.BlockSpec((H, T, R), lambda b: (0, b, 0)),
                  pl.BlockSpec((H, T, DK), lambda b: (0, b, 0)),
                  pl.BlockSpec((H * T, 1), lambda b: (0, 0)),
                  pl.BlockSpec((None, S, R), lambda b: (b, 0, 0)),
                  pl.BlockSpec((None, S, ROPE_DIM), lambda b: (b, 0, 0))],
        out_specs=pl.BlockSpec((H, T, R), lambda b: (0, b, 0)),
        out_shape=jax.ShapeDtypeStruct((H, M, R), BF16),
        compiler_params=_params(("parallel",)),
        name="attn_latent",
    )(q_lat, q, qpos, ckv, kpe)
    return pl.pallas_call(
        _o_expand_kernel,
        grid=(H,),
        in_specs=[pl.BlockSpec((None, M, R), lambda h: (h, 0, 0)),
                  pl.BlockSpec((R, V_DIM), lambda h: (0, h))],
        out_specs=pl.BlockSpec((M, V_DIM), lambda h: (0, h)),
        out_shape=jax.ShapeDtypeStruct((M, H * V_DIM), BF16),
        compiler_params=_params(("parallel",)),
        name="o_expand",
    )(o_lat, wuv)


def _pad_cols(w, n):
    return jnp.pad(w, ((0, 0), (0, n - w.shape[1])))


def _pad_rows(w, n):
    return jnp.pad(w, ((0, n - w.shape[0]), (0, 0)))


def _rotate_half_cols(w):
    half = ROPE_DIM // 2
    return jnp.concatenate([-w[..., half:], w[..., :half]], axis=-1)


def _rope_table(pos):
    half = ROPE_DIM // 2
    inv = ROPE_THETA ** (-jnp.arange(half, dtype=F32) / half)
    ang = pos.astype(F32)[:, None] * inv[None, :]
    cos, sin = jnp.cos(ang), jnp.sin(ang)
    return jnp.concatenate([cos, cos, sin, sin], axis=-1)


def _block_diag_states(s):
    B, H, N, _ = s.shape
    s = s.reshape(B, H // 2, 2, N, N)
    z = jnp.zeros_like(s[:, :, 0])
    top = jnp.concatenate([s[:, :, 0], z], axis=-1)
    bot = jnp.concatenate([z, s[:, :, 1]], axis=-1)
    return jnp.concatenate([top, bot], axis=-2)


def _diag_states(s):
    B, HP = s.shape[:2]
    N = RW_HEAD
    return jnp.stack([s[:, :, :N, :N], s[:, :, N:, N:]], axis=2).reshape(B, 2 * HP, N, N)


def _prepare(W):
    D = W['rw_wr'].shape[1]
    P = {}
    bf = lambda a: a.astype(BF16)
    P['mlp_w1'], P['mlp_w2'] = bf(W['mlp_w1']), bf(W['mlp_w2'])
    for n in ('rw_wr', 'rw_wk', 'rw_wv', 'rw_wo', 'rw_g1', 'rw_g2', 'mla_wdq', 'mla_wo'):
        P[n] = [bf(W[n][l]) for l in range(W[n].shape[0])]
    NA = W['rw_wr'].shape[0]
    P['rw_w1'] = [bf(_pad_cols(W['rw_w1'][l], LORA_PAD)) for l in range(NA)]
    P['rw_w2'] = [bf(_pad_rows(W['rw_w2'][l], LORA_PAD)) for l in range(NA)]
    P['rw_a1'] = [bf(_pad_cols(W['rw_a1'][l], LORA_PAD)) for l in range(NA)]
    P['rw_a2'] = [bf(_pad_rows(W['rw_a2'][l], LORA_PAD)) for l in range(NA)]
    P['rw_v1'] = [bf(_pad_cols(W['rw_v1'][l], LORA_PAD)) for l in range(NA - 1)]
    P['rw_v2'] = [bf(_pad_rows(W['rw_v2'][l], LORA_PAD)) for l in range(NA - 1)]
    zeros = jnp.zeros((D,), F32)
    P['proj_vecs'], P['recur_vecs'] = [], []
    for l in range(NA):
        v0 = W['rw_v0'][l - 1] if l > 0 else zeros
        P['proj_vecs'].append(jnp.stack([W['rw_w0'][l], W['rw_a0'][l], v0, W['rw_kk'][l], W['rw_ka'][l],
                                         zeros, zeros, zeros]))
        P['recur_vecs'].append(jnp.stack([W['rw_rk'][l], W['rw_lnx_g'][l], W['rw_lnx_b'][l]] + [zeros] * 5))
    R = W['kv_lat_g'].shape[0]
    wd = W['kv_wd']
    P['kv_wd'] = bf(jnp.concatenate([wd, _rotate_half_cols(wd[:, R:])], axis=1))
    H = W['kv_wuk'].shape[1]
    P['kv_wuk'] = bf(W['kv_wuk'].reshape(R, H * NOPE_DIM))
    P['kv_wuv'] = bf(W['kv_wuv'].reshape(R, H * V_DIM))
    NB, Q = W['mla_wuq'].shape[:2]
    wuq = W['mla_wuq'].reshape(NB, Q, H, NOPE_DIM + ROPE_DIM)
    pe = wuq[..., NOPE_DIM:]
    P['mla_wuq'] = bf(jnp.concatenate([wuq, _rotate_half_cols(pe)], axis=-1).reshape(NB, Q, -1))
    return P


def _trunk(x, mod, kv_mod, pos0, h_prev, s0, past_ckv, past_kpe, W, P):
    B, T, D = x.shape
    M = B * T
    depth = W['ada_w'].shape[0]
    NA = W['rw_wr'].shape[0]
    H = W['kv_wuk'].shape[1]
    tab = _rope_table(pos0 + jnp.arange(T))
    xf = x.reshape(M, D)
    shifts, states = [], []
    v_first = None
    keys = vals = ckv = kpe = None
    for l in range(depth):
        m = mod[l]
        if l < NA:
            has_v = l > 0
            pre = rwkv_pre(xf.reshape(B, T, D), W['norm_mix_g'][l], m[:, 0], m[:, 1], h_prev[l], W['rw_mu'][l],
                           P['rw_w1'][l], P['rw_a1'][l], P['rw_g1'][l], P['rw_v1'][l - 1] if has_v else None)
            shifts.append(pre[-1].reshape(B, D))
            acts = [a.reshape(M, a.shape[-1]) for a in pre[:-1]]
            res = rwkv_mix(acts, v_first, P['rw_wr'][l], P['rw_wk'][l], P['rw_wv'][l], P['rw_w2'][l],
                           P['rw_a2'][l], P['rw_g2'][l], P['rw_v2'][l - 1] if has_v else None,
                           P['proj_vecs'][l], P['recur_vecs'][l], _block_diag_states(s0[l]), B, T)
            z, s_fin = res[0], res[1]
            if not has_v:
                v_first = res[2]
            states.append(_diag_states(s_fin))
            xf = matmul_res(z, P['rw_wo'][l], xf, m[:, 2], B, T)
        else:
            j = l - NA
            cq = wdq_norm(xf.reshape(B, T, D), W['norm_mix_g'][l], m[:, 0], m[:, 1], P['mla_wdq'][j], W['mla_q_g'][j])
            if past_ckv is None:
                q = wuq_rope(cq, P['mla_wuq'][j], tab, H, False)
                o = flash_prompt(q, keys, vals).reshape(M, H * V_DIM)
            else:
                q = wuq_rope(cq.reshape(1, M, -1), P['mla_wuq'][j], jnp.tile(tab, (B, 1)), H, True)
                o = attn_latent(q, keys, vals, P['kv_wuk'], P['kv_wuv'], B, T, pos0)
            xf = matmul_res(o, P['mla_wo'][j], xf, m[:, 2], B, T)
        xf = mlp_res(xf, W['norm_mlp_g'][l], m[:, 3], m[:, 4], m[:, 5], P['mlp_w1'], P['mlp_w2'], l,
                     W['final_g'], l == depth - 1, B, T)
        if l == NA - 1:
            ckv, kpe = latent(xf.reshape(B, T, D), W['kv_norm_g'], kv_mod[:, 0], kv_mod[:, 1],
                              P['kv_wd'], W['kv_lat_g'], tab)
            if past_ckv is None:
                keys, vals = kv_expand(ckv, kpe, P['kv_wuk'], P['kv_wuv'], H)
            else:
                keys = jnp.concatenate([past_ckv, ckv], axis=1)
                vals = jnp.concatenate([past_kpe, kpe], axis=1)
    return xf.reshape(B, T, D), ckv, kpe, jnp.stack(states), jnp.stack(shifts)


def kernel(x_prompt, x_sample, cache_ckv, cache_kpe, state_wkv, state_shift, c_prompt, c_sample, ada_w, ada_b, norm_mix_g, norm_mlp_g, mlp_w1, mlp_w2, rw_mu, rw_w0, rw_w1, rw_w2, rw_a0, rw_a1, rw_a2, rw_v0, rw_v1, rw_v2, rw_g1, rw_g2, rw_wr, rw_wk, rw_wv, rw_wo, rw_kk, rw_ka, rw_rk, rw_lnx_g, rw_lnx_b, kv_ada_w, kv_ada_b, kv_norm_g, kv_wd, kv_lat_g, kv_wuk, kv_wuv, mla_wdq, mla_q_g, mla_wuq, mla_wo, final_g):
    W = dict(ada_w=ada_w, ada_b=ada_b, norm_mix_g=norm_mix_g, norm_mlp_g=norm_mlp_g,
             mlp_w1=mlp_w1, mlp_w2=mlp_w2, rw_mu=rw_mu, rw_w0=rw_w0, rw_w1=rw_w1, rw_w2=rw_w2,
             rw_a0=rw_a0, rw_a1=rw_a1, rw_a2=rw_a2, rw_v0=rw_v0, rw_v1=rw_v1, rw_v2=rw_v2,
             rw_g1=rw_g1, rw_g2=rw_g2, rw_wr=rw_wr, rw_wk=rw_wk, rw_wv=rw_wv, rw_wo=rw_wo,
             rw_kk=rw_kk, rw_ka=rw_ka, rw_rk=rw_rk, rw_lnx_g=rw_lnx_g, rw_lnx_b=rw_lnx_b,
             kv_ada_w=kv_ada_w, kv_ada_b=kv_ada_b, kv_norm_g=kv_norm_g, kv_wd=kv_wd,
             kv_lat_g=kv_lat_g, kv_wuk=kv_wuk, kv_wuv=kv_wuv, mla_wdq=mla_wdq, mla_q_g=mla_q_g,
             mla_wuq=mla_wuq, mla_wo=mla_wo, final_g=final_g)
    P = _prepare(W)
    Bp, Tp, D = x_prompt.shape
    Bs = x_sample.shape[0]
    depth = ada_w.shape[0]
    NA = rw_wr.shape[0]
    c_all = jnp.concatenate([c_prompt, c_sample], axis=0)
    mod = ada_linear(c_all, ada_w, ada_b).reshape(depth, Bp + Bs, N_MOD, D)
    kv_mod = ada_linear(c_all, kv_ada_w[None], kv_ada_b[None]).reshape(Bp + Bs, 2, D)
    h0 = jnp.zeros((NA, Bp, D), F32)
    s0 = jnp.zeros((NA, Bp, D // RW_HEAD, RW_HEAD, RW_HEAD), F32)
    out_p = _trunk(x_prompt, mod[:, :Bp], kv_mod[:Bp], 0, h0, s0, None, None, W, P)
    out_s = _trunk(x_sample, mod[:, Bp:], kv_mod[Bp:], cache_ckv.shape[1], state_shift, state_wkv,
                   cache_ckv, cache_kpe, W, P)
    return (out_p[0], out_s[0]) + out_p[1:] + out_s[1:]
```

```python
import functools

import jax
import jax.numpy as jnp
from jax import lax
from jax.experimental import pallas as pl
from jax.experimental.pallas import tpu as pltpu

F32, BF16 = jnp.float32, jnp.bfloat16

RW_HEAD = 64
CHUNK = 64
GN_EPS = 64e-5
NOPE_DIM = 128
ROPE_DIM = 64
V_DIM = 128
ROPE_THETA = 10000.0
MLA_SCALE = (NOPE_DIM + ROPE_DIM) ** -0.5
LOG2E = 1.4426950408889634
V_ROWS = V_DIM + 16
NORM_EPS = 1e-6
N_MOD = 6

LANES = 128
VMEM_LIMIT = 56 * 1024 * 1024
LORA_PAD = 128


def _params(sem, vmem=VMEM_LIMIT):
    return pltpu.CompilerParams(dimension_semantics=sem, vmem_limit_bytes=vmem)


def _tiles(n, block):
    assert n % block == 0, (n, block)
    return n // block


def _dot(a, b):
    return jnp.dot(a, b, preferred_element_type=F32)


def _dot_nt(a, b):
    return lax.dot_general(a, b, (((1,), (1,)), ((), ())), preferred_element_type=F32)


def _dot_tn(a, b):
    return lax.dot_general(a, b, (((0,), (0,)), ((), ())), preferred_element_type=F32)


def _split2(x):
    hi = x.astype(BF16)
    return hi, (x - hi.astype(F32)).astype(BF16)


def _split3(x):
    hi = x.astype(BF16)
    r1 = x - hi.astype(F32)
    mid = r1.astype(BF16)
    return hi, mid, (r1 - mid.astype(F32)).astype(BF16)


def _sigmoid(x):
    return 1.0 / (1.0 + jnp.exp(-x))


def _blk(i, n):
    assert n & (n - 1) == 0
    return i >> (n.bit_length() - 1)


def _off(i, n):
    assert n & (n - 1) == 0
    return i & (n - 1)


def _lower_left(ri, cj, s):
    return (_blk(ri, 2 * s) == _blk(cj, 2 * s)) & (_off(ri, 2 * s) >= s) & (_off(cj, 2 * s) < s)


def _rms(x, g):
    return x * lax.rsqrt(jnp.mean(x * x, axis=-1, keepdims=True) + NORM_EPS) * g


def _ada_kernel(c_ref, w_ref, b_ref, o_ref):
    c = c_ref[...]
    cs = c * _sigmoid(c)
    o_ref[...] = _dot(cs.astype(BF16), w_ref[...].astype(BF16)) + b_ref[...]


def ada_linear(c, w, b):
    L, K, N = w.shape
    M = c.shape[0]
    bn = min(512, N)
    return pl.pallas_call(
        _ada_kernel,
        grid=(L, _tiles(N, bn)),
        in_specs=[pl.BlockSpec((M, K), lambda l, j: (0, 0)),
                  pl.BlockSpec((None, K, bn), lambda l, j: (l, 0, j)),
                  pl.BlockSpec((None, 1, bn), lambda l, j: (l, 0, j))],
        out_specs=pl.BlockSpec((None, M, bn), lambda l, j: (l, 0, j)),
        out_shape=jax.ShapeDtypeStruct((L, M, N), F32),
        compiler_params=_params(("parallel", "parallel")),
        name="ada_linear",
    )(c, w, b.reshape(L, 1, N))


def _rwkv_pre_kernel(*refs, has_v):
    if has_v:
        (x_ref, g_ref, sh_ref, sc_ref, hp_ref, mu_ref, w1_ref, a1_ref, g1_ref, v1_ref,
         xr_ref, xk_ref, xv_ref, tw_ref, av_ref, gg_ref, vv_ref, hl_ref, prev) = refs
    else:
        (x_ref, g_ref, sh_ref, sc_ref, hp_ref, mu_ref, w1_ref, a1_ref, g1_ref,
         xr_ref, xk_ref, xv_ref, tw_ref, av_ref, gg_ref, hl_ref, prev) = refs
    bt = x_ref.shape[0]
    h = _rms(x_ref[...], g_ref[...]) * (1.0 + sc_ref[...]) + sh_ref[...]

    @pl.when(pl.program_id(1) == 0)
    def _():
        prev[...] = hp_ref[...]

    row = lax.broadcasted_iota(jnp.int32, h.shape, 0)
    xx = jnp.where(row == 0, prev[...], pltpu.roll(h, 1, 0)) - h
    last = h[bt - 1:bt, :]
    prev[...] = last
    hl_ref[...] = last
    mu = mu_ref[...]

    def mix(i):
        return (h + xx * mu[i:i + 1, :]).astype(BF16)

    xr_ref[...] = mix(0)
    tw_ref[...] = jnp.tanh(_dot(mix(1), w1_ref[...])).astype(BF16)
    xk_ref[...] = mix(2)
    xv = mix(3)
    xv_ref[...] = xv
    if has_v:
        vv_ref[...] = _dot(xv, v1_ref[...]).astype(BF16)
    av_ref[...] = _dot(mix(4), a1_ref[...]).astype(BF16)
    gg_ref[...] = _sigmoid(_dot(mix(5), g1_ref[...])).astype(BF16)


def rwkv_pre(x, g, shift, scale, h_prev, mu, w1, a1, g1, v1):
    B, T, D = x.shape
    bt = min(512, T)
    has_v = v1 is not None
    row = lambda n: pl.BlockSpec((None, bt, n), lambda b, t: (b, t, 0))
    vec = pl.BlockSpec((None, 1, D), lambda b, t: (b, 0, 0))
    full = lambda a: pl.BlockSpec(a.shape, lambda b, t: (0, 0))
    lora = [w1, a1, g1] + ([v1] if has_v else [])
    outs = [(D, BF16)] * 3 + [(w1.shape[1], BF16), (a1.shape[1], BF16), (g1.shape[1], BF16)]
    if has_v:
        outs.append((v1.shape[1], BF16))
    res = pl.pallas_call(
        functools.partial(_rwkv_pre_kernel, has_v=has_v),
        grid=(B, _tiles(T, bt)),
        in_specs=[row(D), pl.BlockSpec((1, D), lambda b, t: (0, 0)), vec, vec, vec, full(mu)]
                 + [full(a) for a in lora],
        out_specs=[row(n) for n, _ in outs] + [vec],
        out_shape=[jax.ShapeDtypeStruct((B, T, n), dt) for n, dt in outs]
                  + [jax.ShapeDtypeStruct((B, 1, D), F32)],
        scratch_shapes=[pltpu.VMEM((1, D), F32)],
        compiler_params=_params(("parallel", "arbitrary")),
        name="rwkv_pre",
    )(x, g.reshape(1, D), shift.reshape(B, 1, D), scale.reshape(B, 1, D), h_prev.reshape(B, 1, D), mu, *lora)
    return res


def _rwkv_mix_kernel(*refs, C, NC, HP, nT, has_v):
    it = iter(refs)
    xr_ref, xk_ref, xv_ref, tw_ref, av_ref, gg_ref = (next(it) for _ in range(6))
    vv_ref, vf_ref = (next(it), next(it)) if has_v else (None, None)
    wr_ref, wk_ref, wv_ref, w2_ref, a2_ref, g2_ref = (next(it) for _ in range(6))
    v2_ref = next(it) if has_v else None
    pvec_ref, rvec_ref, s0_ref, z_ref, sout_ref = (next(it) for _ in range(5))
    vout_ref = None if has_v else next(it)
    ops_scr, s_scr = next(it), next(it)
    step = pl.program_id(1)
    C2, RB = 2 * C, NC * C

    @pl.when(step == 0)
    def _():
        ops_scr[...] = jnp.zeros(ops_scr.shape, F32)

    r, ld, k, v, kk, b, g = (ops_scr[i] for i in range(7))
    first_of_batch = _off(jnp.maximum(step - 1, 0), nT) == 0
    state = [jnp.where(first_of_batch, s0_ref[p], s_scr[p]) for p in range(HP)]

    lanes = [slice(p * LANES, (p + 1) * LANES) for p in range(HP)]
    same_head = (_blk(lax.broadcasted_iota(jnp.int32, (LANES, LANES), 0), RW_HEAD)
                 == _blk(lax.broadcasted_iota(jnp.int32, (LANES, LANES), 1), RW_HEAD))
    ones_bd = jnp.where(same_head, 1.0, 0.0).astype(BF16)
    pvec = pvec_ref[...]
    PW = 2 * LANES if HP % 2 == 0 else LANES
    ones_pw = jnp.where(_blk(lax.broadcasted_iota(jnp.int32, (PW, PW), 0), RW_HEAD)
                        == _blk(lax.broadcasted_iota(jnp.int32, (PW, PW), 1), RW_HEAD), 1.0, 0.0).astype(BF16)

    def project(q):
        sl = slice(q * PW, (q + 1) * PW)
        w0, a0, v0, k_k, k_a = (pvec[i:i + 1, sl] for i in range(5))
        ops_scr[0, :, sl] = _dot(xr_ref[...], wr_ref[:, sl])
        kraw = _dot(xk_ref[...], wk_ref[:, sl])
        vp = _dot(xv_ref[...], wv_ref[:, sl])
        nz = -(w0 + _dot(tw_ref[...], w2_ref[:, sl]))
        softplus = jnp.maximum(nz, 0.0) + jnp.log(1.0 + jnp.exp(-jnp.abs(nz)))
        ops_scr[1, :, sl] = -jnp.exp(-softplus - 0.5)
        a = _sigmoid(a0 + _dot(av_ref[...], a2_ref[:, sl]))
        if has_v:
            vp = vp + (vf_ref[:, sl] - vp) * _sigmoid(v0 + _dot(vv_ref[...], v2_ref[:, sl]))
        else:
            vout_ref[:, sl] = vp
        ops_scr[3, :, sl] = vp
        ops_scr[6, :, sl] = _dot(gg_ref[...], g2_ref[:, sl])
        kn = kraw * k_k
        kn = kn / jnp.maximum(jnp.sqrt(_dot((kn * kn).astype(BF16), ones_pw)), 1e-12)
        ops_scr[4, :, sl] = kn
        ops_scr[5, :, sl] = kn * a
        ops_scr[2, :, sl] = kraw * (1.0 + (a - 1.0) * k_a)

    n_units = HP * LANES // PW
    n_slots = 2 + (C.bit_length() - 2) + NC
    plan = iter([[u for u in range(n_units) if u * n_slots // n_units == i] for i in range(n_slots)])

    def fill():
        for u in next(plan):
            project(u)

    ti = lax.broadcasted_iota(jnp.int32, (RB, RB), 0)
    tj = lax.broadcasted_iota(jnp.int32, (RB, RB), 1)
    tri = jnp.where((_blk(ti, C) == _blk(tj, C)) & (ti >= tj), 1.0, 0.0).astype(BF16)
    cl = _dot(jnp.concatenate([tri] * 3, axis=1), jnp.concatenate(_split3(ld), axis=0))
    p_in = jnp.exp(cl)
    p_inv = jnp.exp(-cl)
    a_t = -(jnp.exp(cl - ld) * kk)
    r_t = p_in * r
    b_t = p_inv * b
    k_t = p_inv * k
    rows = [slice(c * C, (c + 1) * C) for c in range(NC)]
    cl_end = [cl[(c + 1) * C - 1:(c + 1) * C, :] for c in range(NC)]
    p_rem = [jnp.exp(cl_end[c] - cl[rows[c], :]) for c in range(NC)]
    p_end = [jnp.exp(cl_end[c]) for c in range(NC)]
    b_h = [p_rem[c] * b[rows[c], :] for c in range(NC)]
    k_h = [p_rem[c] * k[rows[c], :] for c in range(NC)]
    rvec = rvec_ref[...]
    r_k, lnx_g, lnx_b = (rvec[i:i + 1, :] for i in range(3))
    rk = r * k * r_k

    first = lax.broadcasted_iota(jnp.int32, (1, LANES), 1) < RW_HEAD

    def stack(x):
        return jnp.concatenate([jnp.where(first, x, 0.0), jnp.where(first, 0.0, x)], axis=0).astype(BF16)

    def fold(x):
        return x[0:C, :] + x[C:C2, :]

    ri = lax.broadcasted_iota(jnp.int32, (C2, C2), 0)
    cj = lax.broadcasted_iota(jnp.int32, (C2, C2), 1)
    same = _blk(ri, C) == _blk(cj, C)
    strict = same & (_off(ri, C) > _off(cj, C))
    incl = same & (_off(ri, C) >= _off(cj, C))
    eye = jnp.where(ri == cj, 1.0, 0.0)

    probs = [(c, p) for c in range(NC) for p in range(HP)]
    a_b = {cp: a_t[rows[cp[0]], lanes[cp[1]]].astype(BF16) for cp in probs}
    r_b = {cp: r_t[rows[cp[0]], lanes[cp[1]]].astype(BF16) for cp in probs}
    v_st = {cp: stack(v[rows[cp[0]], lanes[cp[1]]]) for cp in probs}
    low, rb_f, ak_f, rk_f = {}, {}, {}, {}
    if C2 == LANES:
        t_f64 = lax.broadcasted_iota(jnp.int32, (C, LANES), 0)
        s_f64 = _off(lax.broadcasted_iota(jnp.int32, (C, LANES), 1), C)
        strict_f, incl_f = t_f64 > s_f64, t_f64 >= s_f64
        for c, p in probs:
            q_st = jnp.concatenate([stack(a_t[rows[c], lanes[p]]), stack(r_t[rows[c], lanes[p]])], axis=0)
            w_st = jnp.concatenate([b_t[rows[c], lanes[p]], k_t[rows[c], lanes[p]]], axis=0).astype(BF16)
            gm = _dot_nt(q_st, w_st)
            gr = pltpu.roll(gm, C, 1)
            a0_, a1_, r0_, r1_ = (gm[i * C:(i + 1) * C, :] for i in range(4))
            a0r, a1r, r0r, r1r = (gr[i * C:(i + 1) * C, :] for i in range(4))
            low[c, p] = jnp.concatenate([jnp.where(first & strict_f, a0_, 0.0),
                                         jnp.where(strict_f & ~first, a1r, 0.0)], axis=0)
            ak_f[c, p] = jnp.where(strict_f, jnp.where(first, a0r, a1_), 0.0).astype(BF16)
            rb_f[c, p] = jnp.where(incl_f, jnp.where(first, r0_, r1r), 0.0).astype(BF16)
            rk_f[c, p] = jnp.where(incl_f, jnp.where(first, r0r, r1_), 0.0).astype(BF16)
    else:
        for c, p in probs:
            q_st = jnp.concatenate([stack(a_t[rows[c], lanes[p]]), stack(r_t[rows[c], lanes[p]])], axis=0)
            b_c = b_t[rows[c], lanes[p]].astype(BF16)
            k_c = k_t[rows[c], lanes[p]].astype(BF16)
            gb = _dot_nt(q_st, jnp.concatenate([b_c, b_c], axis=0))
            gk = _dot_nt(q_st, jnp.concatenate([k_c, k_c], axis=0))
            low[c, p] = jnp.where(strict, gb[0:C2, :], 0.0)
            rb_f[c, p] = fold(jnp.where(incl, gb[C2:, :], 0.0)).astype(BF16)
            ak_f[c, p] = fold(jnp.where(strict, gk[0:C2, :], 0.0)).astype(BF16)
            rk_f[c, p] = fold(jnp.where(incl, gk[C2:, :], 0.0)).astype(BF16)
    fill()
    akv = {cp: _dot(ak_f[cp], v_st[cp]) for cp in probs}

    first_level = _lower_left(ri, cj, 1)
    t_inv = {cp: eye + jnp.where(first_level, low[cp], 0.0) for cp in probs}
    s = 2
    while s < C:
        sel = _lower_left(ri, cj, s)
        tb = {cp: t_inv[cp].astype(BF16) for cp in probs}
        ls = {cp: jnp.where(sel, low[cp], 0.0).astype(BF16) for cp in probs}
        if s % 8 == 0:
            groups = range(C2 // (2 * s))

            def lower_rows(m):
                return jnp.concatenate([m[g * 2 * s + s:(g + 1) * 2 * s, :] for g in groups], axis=0)

            mid = {cp: _dot(lower_rows(t_inv[cp]).astype(BF16), ls[cp]).astype(BF16) for cp in probs}
            upd = {cp: _dot(mid[cp], tb[cp]) for cp in probs}
            t_inv = {cp: jnp.concatenate(
                [piece for g in groups for piece in (
                    t_inv[cp][g * 2 * s:g * 2 * s + s, :],
                    t_inv[cp][g * 2 * s + s:(g + 1) * 2 * s, :] + upd[cp][g * s:(g + 1) * s, :])], axis=0)
                for cp in probs}
        else:
            mid = {cp: _dot(tb[cp], ls[cp]).astype(BF16) for cp in probs}
            t_inv = {cp: t_inv[cp] + _dot(mid[cp], tb[cp]) for cp in probs}
        fill()
        s *= 2
    t_f = {cp: fold(t_inv[cp]).astype(BF16) for cp in probs}

    y = {}
    for c in range(NC):
        state_b = [st.astype(BF16) for st in state]
        x = [_dot_nt(a_b[c, p], state_b[p]) + akv[c, p] for p in range(HP)]
        y0 = [_dot_nt(r_b[c, p], state_b[p]) for p in range(HP)]
        u = [_dot(t_f[c, p], stack(x[p])) for p in range(HP)]
        for p in range(HP):
            if C2 == LANES:
                y[c, p] = y0[p] + _dot(jnp.concatenate([rb_f[c, p], rk_f[c, p]], axis=1),
                                       jnp.concatenate([stack(u[p]), v_st[c, p]], axis=0))
            else:
                y[c, p] = y0[p] + _dot(rb_f[c, p], stack(u[p])) + _dot(rk_f[c, p], v_st[c, p])
        upd = []
        for p in range(HP):
            uv = jnp.concatenate([u[p], v[rows[c], lanes[p]]], axis=0).astype(BF16)
            bk = jnp.concatenate([b_h[c][:, lanes[p]], k_h[c][:, lanes[p]]], axis=0).astype(BF16)
            upd.append(_dot_tn(uv, bk))
        state = [state[p] * p_end[c][:, lanes[p]] + jnp.where(same_head, upd[p], 0.0) for p in range(HP)]
        fill()
    for p in range(HP):
        s_scr[p] = state[p]
        sout_ref[p] = state[p]

    def seg(t, slices):
        if slices == 1:
            return _dot(t.astype(BF16), ones_bd)
        return _dot(jnp.concatenate(_split2(t), axis=1), jnp.concatenate([ones_bd, ones_bd], axis=0))

    y_all = [jnp.concatenate([y[c, p] for c in range(NC)], axis=0) for p in range(HP)]
    mean = [seg(y_all[p], 2) * (1.0 / RW_HEAD) for p in range(HP)]
    bonus = [seg(rk[:, lanes[p]], 1) * v[:, lanes[p]] for p in range(HP)]
    fill()
    d = [y_all[p] - mean[p] for p in range(HP)]
    var = [seg(d[p] * d[p], 1) * (1.0 / RW_HEAD) for p in range(HP)]
    for p in range(HP):
        yn = d[p] * lax.rsqrt(var[p] + GN_EPS) * lnx_g[:, lanes[p]] + lnx_b[:, lanes[p]]
        z_ref[:, lanes[p]] = ((yn + bonus[p]) * g[:, lanes[p]]).astype(z_ref.dtype)


def rwkv_mix(acts, v_first, wr, wk, wv, w2, a2, g2, v2, pvecs, rvecs, s0_bd, B, T):
    has_v = v2 is not None
    M, D = acts[0].shape
    HL = min(1024, D)
    HP = _tiles(HL, LANES)
    C = min(CHUNK, T)
    NC = 2 if T % (2 * C) == 0 else 1
    RB = NC * C
    nT = _tiles(T, RB)
    assert nT & (nT - 1) == 0
    S = _tiles(M, RB)
    proj_blk = lambda s: jnp.minimum(s, S - 1)
    rec_blk = lambda s: jnp.maximum(s - 1, 0)
    row = lambda a: pl.BlockSpec((RB, a.shape[1]), lambda j, s: (proj_blk(s), 0))
    col = lambda a: pl.BlockSpec((a.shape[0], HL), lambda j, s: (0, j))
    st = pl.BlockSpec((None, HP, LANES, LANES), lambda j, s: (rec_blk(s) // nT, j, 0, 0))
    weights = [wr, wk, wv, w2, a2, g2] + ([v2] if has_v else [])
    in_specs = ([row(a) for a in acts]
                + ([pl.BlockSpec((RB, HL), lambda j, s: (proj_blk(s), j))] if has_v else [])
                + [col(w) for w in weights] + [col(pvecs), col(rvecs), st])
    out_specs = [pl.BlockSpec((RB, HL), lambda j, s: (rec_blk(s), j)), st]
    out_shape = [jax.ShapeDtypeStruct((M, D), BF16), jax.ShapeDtypeStruct(s0_bd.shape, F32)]
    if not has_v:
        out_specs.append(pl.BlockSpec((RB, HL), lambda j, s: (proj_blk(s), j)))
        out_shape.append(jax.ShapeDtypeStruct((M, D), F32))
    res = pl.pallas_call(
        functools.partial(_rwkv_mix_kernel, C=C, NC=NC, HP=HP, nT=nT, has_v=has_v),
        grid=(_tiles(D, HL), S + 1),
        in_specs=in_specs, out_specs=out_specs, out_shape=out_shape,
        scratch_shapes=[pltpu.VMEM((7, RB, HL), F32), pltpu.VMEM((HP, LANES, LANES), F32)],
        compiler_params=_params(("parallel", "arbitrary")),
        name="rwkv_mix",
    )(*acts, *([v_first] if has_v else []), *weights, pvecs, rvecs, s0_bd)
    return res


def _gate_operand(gate, B, T, bm, bn):
    N = gate.shape[1]
    col = (lambda j: j) if bn < N else (lambda j: 0)
    if T % bm == 0:
        return gate.reshape(B, 1, N), pl.BlockSpec((None, 1, bn), lambda i, j: ((i * bm) // T, 0, col(j)))
    rows = jnp.broadcast_to(gate[:, None, :], (B, T, N)).reshape(B * T, N)
    return rows, pl.BlockSpec((bm, bn), lambda i, j: (i, col(j)))


def _matmul_res_kernel(a_ref, w_ref, x_ref, gate_ref, o_ref):
    o_ref[...] = x_ref[...] + gate_ref[...] * _dot(a_ref[...], w_ref[...])


def matmul_res(a, w, x, gate, B, T):
    M, K = a.shape
    N = w.shape[1]
    bm, bn = min(1024, M), min(1024 if K <= 2048 else 512, N)
    gate_arr, gate_spec = _gate_operand(gate, B, T, bm, bn)
    return pl.pallas_call(
        _matmul_res_kernel,
        grid=(_tiles(M, bm), _tiles(N, bn)),
        in_specs=[pl.BlockSpec((bm, K), lambda i, j: (i, 0)),
                  pl.BlockSpec((K, bn), lambda i, j: (0, j)),
                  pl.BlockSpec((bm, bn), lambda i, j: (i, j)), gate_spec],
        out_specs=pl.BlockSpec((bm, bn), lambda i, j: (i, j)),
        out_shape=jax.ShapeDtypeStruct((M, N), F32),
        compiler_params=_params(("parallel", "arbitrary")),
        name="matmul_res",
    )(a, w, x, gate_arr)


def _mlp_kernel(x_ref, g_ref, sh_ref, sc_ref, gate_ref, w1_ref, w2_ref, fg_ref, o_ref, h_scr, acc, *, final):
    f = pl.program_id(1)
    bm = x_ref.shape[0]

    def rows_of(ref, rs):
        return ref[...] if ref.shape[0] == 1 else ref[rs, :]

    def mlp(h):
        return _dot(jnp.square(jnp.maximum(_dot(h, w1_ref[...]), 0.0)).astype(BF16), w2_ref[...])

    @pl.when(f == 0)
    def _():
        parts = 4 if bm % 64 == 0 else 1
        for c in range(parts):
            rs = slice(c * bm // parts, (c + 1) * bm // parts)
            h = (_rms(x_ref[rs, :], g_ref[...]) * (1.0 + rows_of(sc_ref, rs)) + rows_of(sh_ref, rs)).astype(BF16)
            h_scr[rs, :] = h
            acc[rs, :] = mlp(h)

    @pl.when(f > 0)
    def _():
        acc[...] += mlp(h_scr[...])

    @pl.when(f == pl.num_programs(1) - 1)
    def _():
        y = x_ref[...] + gate_ref[...] * acc[...]
        o_ref[...] = _rms(y, fg_ref[...]) if final else y


def mlp_res(x, g, shift, scale, gate, w1, w2, layer, final_g, final, B, T):
    M, D = x.shape
    F = w1.shape[2]
    bm = 512 if T % 512 == 0 else min(256, M)
    bf = min(1024, F)
    (sh_arr, vec_spec), (sc_arr, _), (gate_arr, _) = (_gate_operand(v, B, T, bm, D) for v in (shift, scale, gate))
    return pl.pallas_call(
        functools.partial(_mlp_kernel, final=final),
        grid=(_tiles(M, bm), _tiles(F, bf)),
        in_specs=[pl.BlockSpec((bm, D), lambda i, f: (i, 0)),
                  pl.BlockSpec((1, D), lambda i, f: (0, 0)), vec_spec, vec_spec, vec_spec,
                  pl.BlockSpec((None, D, bf), lambda i, f: (layer, 0, f)),
                  pl.BlockSpec((None, bf, D), lambda i, f: (layer, f, 0)),
                  pl.BlockSpec((1, D), lambda i, f: (0, 0))],
        out_specs=pl.BlockSpec((bm, D), lambda i, f: (i, 0)),
        out_shape=jax.ShapeDtypeStruct((M, D), F32),
        scratch_shapes=[pltpu.VMEM((bm, D), BF16), pltpu.VMEM((bm, D), F32)],
        compiler_params=_params(("parallel", "arbitrary")),
        name="mlp_res",
    )(x, g.reshape(1, D), sh_arr, sc_arr, gate_arr, w1, w2, final_g.reshape(1, D))


def _rope_pair(t, tab):
    prod = t * tab
    return (prod + pltpu.roll(prod, ROPE_DIM, 1))[:, :ROPE_DIM]


def _latent_kernel(x_ref, gx_ref, sh_ref, sc_ref, w_ref, g_ref, tab_ref, ckv_ref, kpe_ref):
    h = (_rms(x_ref[...], gx_ref[...]) * (1.0 + sc_ref[...]) + sh_ref[...]).astype(BF16)
    acc = _dot(h, w_ref[...])
    R = g_ref.shape[1]
    ckv_ref[...] = _rms(acc[:, :R], g_ref[...])
    kpe_ref[...] = _rope_pair(acc[:, R:R + 2 * ROPE_DIM], tab_ref[...])


def latent(x, gx, shift, scale, w, g, tab):
    B, T, D = x.shape
    R = g.shape[0]
    bt = min(512, T)
    vec = pl.BlockSpec((None, 1, D), lambda b, t: (b, 0, 0))
    return pl.pallas_call(
        _latent_kernel,
        grid=(B, _tiles(T, bt)),
        in_specs=[pl.BlockSpec((None, bt, D), lambda b, t: (b, t, 0)),
                  pl.BlockSpec((1, D), lambda b, t: (0, 0)), vec, vec,
                  pl.BlockSpec(w.shape, lambda b, t: (0, 0)),
                  pl.BlockSpec((1, R), lambda b, t: (0, 0)),
                  pl.BlockSpec((bt, 2 * ROPE_DIM), lambda b, t: (t, 0))],
        out_specs=[pl.BlockSpec((None, bt, R), lambda b, t: (b, t, 0)),
                   pl.BlockSpec((None, bt, ROPE_DIM), lambda b, t: (b, t, 0))],
        out_shape=[jax.ShapeDtypeStruct((B, T, R), F32), jax.ShapeDtypeStruct((B, T, ROPE_DIM), F32)],
        compiler_params=_params(("parallel", "parallel")),
        name="latent",
    )(x, gx.reshape(1, D), shift.reshape(B, 1, D), scale.reshape(B, 1, D), w, g.reshape(1, R), tab)


def _kv_expand_kernel(c_ref, pe_ref, wuk_ref, wuv_ref, k_ref, vt_ref):
    c = c_ref[...].astype(BF16)
    kn = _dot(c, wuk_ref[...])
    vv = _dot(c, wuv_ref[...])
    pe = pe_ref[...].astype(BF16)
    for h in range(k_ref.shape[0]):
        k_ref[h, :, 0:NOPE_DIM] = kn[:, h * NOPE_DIM:(h + 1) * NOPE_DIM].astype(BF16)
        k_ref[h, :, NOPE_DIM:NOPE_DIM + ROPE_DIM] = pe
        vt_ref[h, 0:V_DIM, :] = vv[:, h * V_DIM:(h + 1) * V_DIM].T.astype(BF16)
        vt_ref[h, V_DIM:V_ROWS, :] = jnp.ones((V_ROWS - V_DIM, c.shape[0]), BF16)


def kv_expand(ckv, kpe, wuk, wuv, H):
    B, S, R = ckv.shape
    bt = min(1024, S)
    hb = min(8, H)
    DK = NOPE_DIM + ROPE_DIM
    return pl.pallas_call(
        _kv_expand_kernel,
        grid=(_tiles(H, hb), B, _tiles(S, bt)),
        in_specs=[pl.BlockSpec((None, bt, R), lambda j, b, t: (b, t, 0)),
                  pl.BlockSpec((None, bt, ROPE_DIM), lambda j, b, t: (b, t, 0)),
                  pl.BlockSpec((R, hb * NOPE_DIM), lambda j, b, t: (0, j)),
                  pl.BlockSpec((R, hb * V_DIM), lambda j, b, t: (0, j))],
        out_specs=[pl.BlockSpec((None, hb, bt, DK), lambda j, b, t: (b, j, t, 0)),
                   pl.BlockSpec((None, hb, V_ROWS, bt), lambda j, b, t: (b, j, 0, t))],
        out_shape=[jax.ShapeDtypeStruct((B, H, S, DK), BF16), jax.ShapeDtypeStruct((B, H, V_ROWS, S), BF16)],
        compiler_params=_params(("parallel", "parallel", "arbitrary")),
        name="kv_expand",
    )(ckv, kpe, wuk, wuv)


def _wdq_kernel(x_ref, gx_ref, sh_ref, sc_ref, w_ref, g_ref, o_ref):
    h = (_rms(x_ref[...], gx_ref[...]) * (1.0 + sc_ref[...]) + sh_ref[...]).astype(BF16)
    o_ref[...] = _rms(_dot(h, w_ref[...]), g_ref[...]).astype(BF16)


def wdq_norm(x, gx, shift, scale, w, g):
    B, T, D = x.shape
    R = w.shape[1]
    bt = min(1024, T)
    vec = pl.BlockSpec((None, 1, D), lambda b, t: (b, 0, 0))
    return pl.pallas_call(
        _wdq_kernel,
        grid=(B, _tiles(T, bt)),
        in_specs=[pl.BlockSpec((None, bt, D), lambda b, t: (b, t, 0)),
                  pl.BlockSpec((1, D), lambda b, t: (0, 0)), vec, vec,
                  pl.BlockSpec((D, R), lambda b, t: (0, 0)),
                  pl.BlockSpec((1, R), lambda b, t: (0, 0))],
        out_specs=pl.BlockSpec((None, bt, R), lambda b, t: (b, t, 0)),
        out_shape=jax.ShapeDtypeStruct((B, T, R), BF16),
        compiler_params=_params(("parallel", "parallel")),
        name="wdq_norm",
    )(x, gx.reshape(1, D), shift.reshape(B, 1, D), scale.reshape(B, 1, D), w, g.reshape(1, R))


def _wuq_kernel(c_ref, w_ref, tab_ref, q_ref, *, scale):
    acc = _dot(c_ref[...], w_ref[...])
    tab = tab_ref[...]
    W = NOPE_DIM + 2 * ROPE_DIM
    for h in range(q_ref.shape[0]):
        q_ref[h, :, 0:NOPE_DIM] = (acc[:, h * W:h * W + NOPE_DIM] * scale).astype(BF16)
        pe = _rope_pair(acc[:, h * W + NOPE_DIM:(h + 1) * W], tab)
        q_ref[h, :, NOPE_DIM:NOPE_DIM + ROPE_DIM] = (pe * scale).astype(BF16)
        if q_ref.shape[2] > NOPE_DIM + ROPE_DIM:
            pad = q_ref.shape[2] - NOPE_DIM - ROPE_DIM
            q_ref[h, :, NOPE_DIM + ROPE_DIM:] = jnp.zeros((q_ref.shape[1], pad), BF16)


def wuq_rope(cq, w, tab, H, head_major):
    B, T, R = cq.shape
    bt = min(1024, T)
    nt = _tiles(T, bt)
    hb = min(8, H)
    W = NOPE_DIM + 2 * ROPE_DIM
    DK = NOPE_DIM + ROPE_DIM
    if head_major:
        out_spec = pl.BlockSpec((hb, bt, DK), lambda j, b, t: (j, b * nt + t, 0))
        out_shape = (H, B * T, DK)
    else:
        out_spec = pl.BlockSpec((None, hb, bt, 2 * LANES), lambda j, b, t: (b, j, t, 0))
        out_shape = (B, H, T, 2 * LANES)
    return pl.pallas_call(
        functools.partial(_wuq_kernel, scale=MLA_SCALE if head_major else MLA_SCALE * LOG2E),
        grid=(_tiles(H, hb), B, nt),
        in_specs=[pl.BlockSpec((None, bt, R), lambda j, b, t: (b, t, 0)),
                  pl.BlockSpec((R, hb * W), lambda j, b, t: (0, j)),
                  pl.BlockSpec((bt, 2 * ROPE_DIM), lambda j, b, t: (t, 0))],
        out_specs=out_spec,
        out_shape=jax.ShapeDtypeStruct(out_shape, BF16),
        compiler_params=_params(("parallel", "parallel", "parallel")),
        name="wuq_rope",
    )(cq, w, tab)


def _visible(qpos, kpos):
    return _blk(kpos, CHUNK) <= _blk(qpos, CHUNK)


def _flash_kernel(q_ref, k_ref, vt_ref, o_ref, *scratch, tile, q_tiles):
    for t in range(q_tiles):
        rows = pl.ds(t * tile, tile)
        _flash_query_tile(pl.program_id(2) * q_tiles + t, q_ref.at[:, rows, :], k_ref, vt_ref, o_ref.at[rows, :],
                          *scratch, tile=tile)


def _flash_query_tile(qi, q_ref, k_ref, vt_ref, o_ref, qt_scr, s_a, s_b, m_scr, acc_scr, *, tile):
    hb = q_ref.shape[0]
    ha = hb // 2
    first, second = range(0, ha), range(ha, hb)
    qpos = qi * tile + lax.broadcasted_iota(jnp.int32, (1, tile), 1)
    m_scr[...] = jnp.full(m_scr.shape, -jnp.inf, F32)
    acc_scr[...] = jnp.zeros(acc_scr.shape, F32)
    DK = NOPE_DIM + ROPE_DIM
    for h in range(hb):
        qf = q_ref[h].astype(F32)
        qt_scr[h, 0:LANES, :] = qf[:, 0:LANES].T.astype(BF16)
        qt_scr[h, LANES:DK, :] = qf[:, LANES:2 * LANES].T[0:DK - LANES, :].astype(BF16)

    def scores(ki, heads, s_ref):
        start = pl.multiple_of(ki * tile, tile)
        for h in heads:
            s_ref[h - heads[0]] = _dot(k_ref[h, pl.ds(start, tile), :], qt_scr[h])

    def softmax_pv(ki, heads, s_ref, masked):
        start = pl.multiple_of(ki * tile, tile)
        for h in heads:
            s = s_ref[h - heads[0]]
            if masked:
                kpos = ki * tile + lax.broadcasted_iota(jnp.int32, (tile, 1), 0)
                s = jnp.where(_visible(qpos, kpos), s, -jnp.inf)
            m = m_scr[h]
            m_new = jnp.maximum(m, jnp.max(s, axis=0, keepdims=True))
            p = jnp.exp2(s - m_new).astype(BF16)
            acc_scr[h] = jnp.exp2(m - m_new) * acc_scr[h] + _dot(vt_ref[h, :, pl.ds(start, tile)], p)
            m_scr[h] = m_new

    def one_tile(ki):
        scores(ki, second, s_b)
        softmax_pv(ki, first, s_a, False)
        scores(ki + 1, first, s_a)
        softmax_pv(ki, second, s_b, False)

    def two_tiles(kk, carry):
        one_tile(2 * kk)
        one_tile(2 * kk + 1)
        return carry

    scores(0, first, s_a)
    lax.fori_loop(0, qi // 2, two_tiles, 0)

    @pl.when(qi % 2 == 1)
    def _():
        one_tile(qi - 1)

    scores(qi, second, s_b)
    softmax_pv(qi, first, s_a, True)
    softmax_pv(qi, second, s_b, True)
    for h in range(hb):
        acc = acc_scr[h]
        inv_l = 1.0 / acc[V_DIM:V_DIM + 1, :]
        o_ref[:, h * V_DIM:(h + 1) * V_DIM] = (acc[0:V_DIM, :] * inv_l).T.astype(o_ref.dtype)


def flash_prompt(q, k, vt):
    B, H, T, QW = q.shape
    DK = k.shape[3]
    tile = min(256, T)
    assert tile % CHUNK == 0 and T % tile == 0 and H % 2 == 0 and QW == 2 * LANES
    hb = next(n for n in (8, 4, 2) if H % n == 0)
    q_tiles = next(n for n in (4, 2, 1) if (T // tile) % n == 0)
    rows = q_tiles * tile
    return pl.pallas_call(
        functools.partial(_flash_kernel, tile=tile, q_tiles=q_tiles),
        grid=(B, _tiles(H, hb), _tiles(T, rows)),
        in_specs=[pl.BlockSpec((None, hb, rows, QW), lambda b, j, i: (b, j, i, 0)),
                  pl.BlockSpec((None, hb, T, DK), lambda b, j, i: (b, j, 0, 0)),
                  pl.BlockSpec((None, hb, V_ROWS, T), lambda b, j, i: (b, j, 0, 0))],
        out_specs=pl.BlockSpec((None, rows, hb * V_DIM), lambda b, j, i: (b, i, j)),
        out_shape=jax.ShapeDtypeStruct((B, T, H * V_DIM), BF16),
        scratch_shapes=[pltpu.VMEM((hb, DK, tile), BF16),
                        pltpu.VMEM((hb // 2, tile, tile), F32), pltpu.VMEM((hb // 2, tile, tile), F32),
                        pltpu.VMEM((hb, 1, tile), F32), pltpu.VMEM((hb, V_ROWS, tile), F32)],
        compiler_params=_params(("parallel", "parallel", "arbitrary")),
        name="flash_prompt",
    )(q, k, vt)


def _q_absorb_kernel(q_ref, wuk_ref, o_ref):
    o_ref[...] = _dot_nt(q_ref[:, 0:NOPE_DIM], wuk_ref[...]).astype(BF16)


def _attn_latent_kernel(ql_ref, q_ref, qpos_ref, c_ref, pe_ref, o_ref):
    H, T, R = ql_ref.shape
    S = c_ref.shape[0]
    c = c_ref[...].astype(BF16)
    s = (_dot_nt(ql_ref[...].reshape(H * T, R), c)
         + _dot_nt(q_ref[:, :, NOPE_DIM:NOPE_DIM + ROPE_DIM].reshape(H * T, ROPE_DIM), pe_ref[...].astype(BF16)))
    kpos = lax.broadcasted_iota(jnp.int32, (1, S), 1)
    s = jnp.where(_visible(qpos_ref[...], kpos), s, -jnp.inf)
    p = jnp.exp(s - jnp.max(s, axis=-1, keepdims=True))
    l = jnp.sum(p, axis=-1, keepdims=True)
    o_ref[...] = (_dot(p.astype(BF16), c) / l).astype(BF16).reshape(H, T, R)


def _o_expand_kernel(o_ref, wuv_ref, out_ref):
    out_ref[...] = _dot(o_ref[...], wuv_ref[...]).astype(BF16)


def attn_latent(q, ckv, kpe, wuk, wuv, B, T, q0):
    H, M, DK = q.shape
    S, R = ckv.shape[1:]
    q_lat = pl.pallas_call(
        _q_absorb_kernel,
        grid=(H,),
        in_specs=[pl.BlockSpec((None, M, DK), lambda h: (h, 0, 0)),
                  pl.BlockSpec((R, NOPE_DIM), lambda h: (0, h))],
        out_specs=pl.BlockSpec((None, M, R), lambda h: (h, 0, 0)),
        out_shape=jax.ShapeDtypeStruct((H, M, R), BF16),
        compiler_params=_params(("parallel",)),
        name="q_absorb",
    )(q, wuk)
    qpos = jnp.tile(q0 + jnp.arange(T, dtype=jnp.int32), H)[:, None]
    o_lat = pl.pallas_call(
        _attn_latent_kernel,
        grid=(B,),
        in_specs=[pl.BlockSpec((H, T, R), lambda b: (0, b, 0)),
                  pl.BlockSpec((H, T, DK), lambda b: (0, b, 0)),
                  pl.BlockSpec((H * T, 1), lambda b: (0, 0)),
                  pl.BlockSpec((None, S, R), lambda b: (b, 0, 0)),
                  pl.BlockSpec((None, S, ROPE_DIM), lambda b: (b, 0, 0))],
        out_specs=pl.BlockSpec((H, T, R), lambda b: (0, b, 0)),
        out_shape=jax.ShapeDtypeStruct((H, M, R), BF16),
        compiler_params=_params(("parallel",)),
        name="attn_latent",
    )(q_lat, q, qpos, ckv, kpe)
    return pl.pallas_call(
        _o_expand_kernel,
        grid=(H,),
        in_specs=[pl.BlockSpec((None, M, R), lambda h: (h, 0, 0)),
                  pl.BlockSpec((R, V_DIM), lambda h: (0, h))],
        out_specs=pl.BlockSpec((M, V_DIM), lambda h: (0, h)),
        out_shape=jax.ShapeDtypeStruct((M, H * V_DIM), BF16),
        compiler_params=_params(("parallel",)),
        name="o_expand",
    )(o_lat, wuv)


def _pad_cols(w, n):
    return jnp.pad(w, ((0, 0), (0, n - w.shape[1])))


def _pad_rows(w, n):
    return jnp.pad(w, ((0, n - w.shape[0]), (0, 0)))


def _rotate_half_cols(w):
    half = ROPE_DIM // 2
    return jnp.concatenate([-w[..., half:], w[..., :half]], axis=-1)


def _rope_table(pos):
    half = ROPE_DIM // 2
    inv = ROPE_THETA ** (-jnp.arange(half, dtype=F32) / half)
    ang = pos.astype(F32)[:, None] * inv[None, :]
    cos, sin = jnp.cos(ang), jnp.sin(ang)
    return jnp.concatenate([cos, cos, sin, sin], axis=-1)


def _block_diag_states(s):
    B, H, N, _ = s.shape
    s = s.reshape(B, H // 2, 2, N, N)
    z = jnp.zeros_like(s[:, :, 0])
    top = jnp.concatenate([s[:, :, 0], z], axis=-1)
    bot = jnp.concatenate([z, s[:, :, 1]], axis=-1)
    return jnp.concatenate([top, bot], axis=-2)


def _diag_states(s):
    B, HP = s.shape[:2]
    N = RW_HEAD
    return jnp.stack([s[:, :, :N, :N], s[:, :, N:, N:]], axis=2).reshape(B, 2 * HP, N, N)


def _prepare(W):
    D = W['rw_wr'].shape[1]
    P = {}
    bf = lambda a: a.astype(BF16)
    P['mlp_w1'], P['mlp_w2'] = bf(W['mlp_w1']), bf(W['mlp_w2'])
    for n in ('rw_wr', 'rw_wk', 'rw_wv', 'rw_wo', 'rw_g1', 'rw_g2', 'mla_wdq', 'mla_wo'):
        P[n] = [bf(W[n][l]) for l in range(W[n].shape[0])]
    NA = W['rw_wr'].shape[0]
    P['rw_w1'] = [bf(_pad_cols(W['rw_w1'][l], LORA_PAD)) for l in range(NA)]
    P['rw_w2'] = [bf(_pad_rows(W['rw_w2'][l], LORA_PAD)) for l in range(NA)]
    P['rw_a1'] = [bf(_pad_cols(W['rw_a1'][l], LORA_PAD)) for l in range(NA)]
    P['rw_a2'] = [bf(_pad_rows(W['rw_a2'][l], LORA_PAD)) for l in range(NA)]
    P['rw_v1'] = [bf(_pad_cols(W['rw_v1'][l], LORA_PAD)) for l in range(NA - 1)]
    P['rw_v2'] = [bf(_pad_rows(W['rw_v2'][l], LORA_PAD)) for l in range(NA - 1)]
    zeros = jnp.zeros((D,), F32)
    P['proj_vecs'], P['recur_vecs'] = [], []
    for l in range(NA):
        v0 = W['rw_v0'][l - 1] if l > 0 else zeros
        P['proj_vecs'].append(jnp.stack([W['rw_w0'][l], W['rw_a0'][l], v0, W['rw_kk'][l], W['rw_ka'][l],
                                         zeros, zeros, zeros]))
        P['recur_vecs'].append(jnp.stack([W['rw_rk'][l], W['rw_lnx_g'][l], W['rw_lnx_b'][l]] + [zeros] * 5))
    R = W['kv_lat_g'].shape[0]
    wd = W['kv_wd']
    P['kv_wd'] = bf(jnp.concatenate([wd, _rotate_half_cols(wd[:, R:])], axis=1))
    H = W['kv_wuk'].shape[1]
    P['kv_wuk'] = bf(W['kv_wuk'].reshape(R, H * NOPE_DIM))
    P['kv_wuv'] = bf(W['kv_wuv'].reshape(R, H * V_DIM))
    NB, Q = W['mla_wuq'].shape[:2]
    wuq = W['mla_wuq'].reshape(NB, Q, H, NOPE_DIM + ROPE_DIM)
    pe = wuq[..., NOPE_DIM:]
    P['mla_wuq'] = bf(jnp.concatenate([wuq, _rotate_half_cols(pe)], axis=-1).reshape(NB, Q, -1))
    return P


def _trunk(x, mod, kv_mod, pos0, h_prev, s0, past_ckv, past_kpe, W, P):
    B, T, D = x.shape
    M = B * T
    depth = W['ada_w'].shape[0]
    NA = W['rw_wr'].shape[0]
    H = W['kv_wuk'].shape[1]
    tab = _rope_table(pos0 + jnp.arange(T))
    xf = x.reshape(M, D)
    shifts, states = [], []
    v_first = None
    keys = vals = ckv = kpe = None
    for l in range(depth):
        m = mod[l]
        if l < NA:
            has_v = l > 0
            pre = rwkv_pre(xf.reshape(B, T, D), W['norm_mix_g'][l], m[:, 0], m[:, 1], h_prev[l], W['rw_mu'][l],
                           P['rw_w1'][l], P['rw_a1'][l], P['rw_g1'][l], P['rw_v1'][l - 1] if has_v else None)
            shifts.append(pre[-1].reshape(B, D))
            acts = [a.reshape(M, a.shape[-1]) for a in pre[:-1]]
            res = rwkv_mix(acts, v_first, P['rw_wr'][l], P['rw_wk'][l], P['rw_wv'][l], P['rw_w2'][l],
                           P['rw_a2'][l], P['rw_g2'][l], P['rw_v2'][l - 1] if has_v else None,
                           P['proj_vecs'][l], P['recur_vecs'][l], _block_diag_states(s0[l]), B, T)
            z, s_fin = res[0], res[1]
            if not has_v:
                v_first = res[2]
            states.append(_diag_states(s_fin))
            xf = matmul_res(z, P['rw_wo'][l], xf, m[:, 2], B, T)
        else:
            j = l - NA
            cq = wdq_norm(xf.reshape(B, T, D), W['norm_mix_g'][l], m[:, 0], m[:, 1], P['mla_wdq'][j], W['mla_q_g'][j])
            if past_ckv is None:
                q = wuq_rope(cq, P['mla_wuq'][j], tab, H, False)
                o = flash_prompt(q, keys, vals).reshape(M, H * V_DIM)
            else:
                q = wuq_rope(cq.reshape(1, M, -1), P['mla_wuq'][j], jnp.tile(tab, (B, 1)), H, True)
                o = attn_latent(q, keys, vals, P['kv_wuk'], P['kv_wuv'], B, T, pos0)
            xf = matmul_res(o, P['mla_wo'][j], xf, m[:, 2], B, T)
        xf = mlp_res(xf, W['norm_mlp_g'][l], m[:, 3], m[:, 4], m[:, 5], P['mlp_w1'], P['mlp_w2'], l,
                     W['final_g'], l == depth - 1, B, T)
        if l == NA - 1:
            ckv, kpe = latent(xf.reshape(B, T, D), W['kv_norm_g'], kv_mod[:, 0], kv_mod[:, 1],
                              P['kv_wd'], W['kv_lat_g'], tab)
            if past_ckv is None:
                keys, vals = kv_expand(ckv, kpe, P['kv_wuk'], P['kv_wuv'], H)
            else:
                keys = jnp.concatenate([past_ckv, ckv], axis=1)
                vals = jnp.concatenate([past_kpe, kpe], axis=1)
    return xf.reshape(B, T, D), ckv, kpe, jnp.stack(states), jnp.stack(shifts)


def kernel(x_prompt, x_sample, cache_ckv, cache_kpe, state_wkv, state_shift, c_prompt, c_sample, ada_w, ada_b, norm_mix_g, norm_mlp_g, mlp_w1, mlp_w2, rw_mu, rw_w0, rw_w1, rw_w2, rw_a0, rw_a1, rw_a2, rw_v0, rw_v1, rw_v2, rw_g1, rw_g2, rw_wr, rw_wk, rw_wv, rw_wo, rw_kk, rw_ka, rw_rk, rw_lnx_g, rw_lnx_b, kv_ada_w, kv_ada_b, kv_norm_g, kv_wd, kv_lat_g, kv_wuk, kv_wuv, mla_wdq, mla_q_g, mla_wuq, mla_wo, final_g):
    W = dict(ada_w=ada_w, ada_b=ada_b, norm_mix_g=norm_mix_g, norm_mlp_g=norm_mlp_g,
             mlp_w1=mlp_w1, mlp_w2=mlp_w2, rw_mu=rw_mu, rw_w0=rw_w0, rw_w1=rw_w1, rw_w2=rw_w2,
             rw_a0=rw_a0, rw_a1=rw_a1, rw_a2=rw_a2, rw_v0=rw_v0, rw_v1=rw_v1, rw_v2=rw_v2,
             rw_g1=rw_g1, rw_g2=rw_g2, rw_wr=rw_wr, rw_wk=rw_wk, rw_wv=rw_wv, rw_wo=rw_wo,
             rw_kk=rw_kk, rw_ka=rw_ka, rw_rk=rw_rk, rw_lnx_g=rw_lnx_g, rw_lnx_b=rw_lnx_b,
             kv_ada_w=kv_ada_w, kv_ada_b=kv_ada_b, kv_norm_g=kv_norm_g, kv_wd=kv_wd,
             kv_lat_g=kv_lat_g, kv_wuk=kv_wuk, kv_wuv=kv_wuv, mla_wdq=mla_wdq, mla_q_g=mla_q_g,
             mla_wuq=mla_wuq, mla_wo=mla_wo, final_g=final_g)
    P = _prepare(W)
    Bp, Tp, D = x_prompt.shape
    Bs = x_sample.shape[0]
    depth = ada_w.shape[0]
    NA = rw_wr.shape[0]
    c_all = jnp.concatenate([c_prompt, c_sample], axis=0)
    mod = ada_linear(c_all, ada_w, ada_b).reshape(depth, Bp + Bs, N_MOD, D)
    kv_mod = ada_linear(c_all, kv_ada_w[None], kv_ada_b[None]).reshape(Bp + Bs, 2, D)
    h0 = jnp.zeros((NA, Bp, D), F32)
    s0 = jnp.zeros((NA, Bp, D // RW_HEAD, RW_HEAD, RW_HEAD), F32)
    out_p = _trunk(x_prompt, mod[:, :Bp], kv_mod[:Bp], 0, h0, s0, None, None, W, P)
    out_s = _trunk(x_sample, mod[:, Bp:], kv_mod[Bp:], cache_ckv.shape[1], state_shift, state_wkv,
                   cache_ckv, cache_kpe, W, P)
    return (out_p[0], out_s[0]) + out_p[1:] + out_s[1:]
```

```python
import functools

import jax
import jax.numpy as jnp
from jax import lax
from jax.experimental import pallas as pl
from jax.experimental.pallas import tpu as pltpu

F32, BF16 = jnp.float32, jnp.bfloat16

RW_HEAD = 64
CHUNK = 64
GN_EPS = 64e-5
NOPE_DIM = 128
ROPE_DIM = 64
V_DIM = 128
ROPE_THETA = 10000.0
MLA_SCALE = (NOPE_DIM + ROPE_DIM) ** -0.5
LOG2E = 1.4426950408889634
V_ROWS = V_DIM + 16
NORM_EPS = 1e-6
N_MOD = 6

LANES = 128
VMEM_LIMIT = 56 * 1024 * 1024
LORA_PAD = 128


def _params(sem, vmem=VMEM_LIMIT):
    return pltpu.CompilerParams(dimension_semantics=sem, vmem_limit_bytes=vmem)


def _tiles(n, block):
    assert n % block == 0, (n, block)
    return n // block


def _dot(a, b):
    return jnp.dot(a, b, preferred_element_type=F32)


def _dot_nt(a, b):
    return lax.dot_general(a, b, (((1,), (1,)), ((), ())), preferred_element_type=F32)


def _dot_tn(a, b):
    return lax.dot_general(a, b, (((0,), (0,)), ((), ())), preferred_element_type=F32)


def _split2(x):
    hi = x.astype(BF16)
    return hi, (x - hi.astype(F32)).astype(BF16)


def _split3(x):
    hi = x.astype(BF16)
    r1 = x - hi.astype(F32)
    mid = r1.astype(BF16)
    return hi, mid, (r1 - mid.astype(F32)).astype(BF16)


def _sigmoid(x):
    return 1.0 / (1.0 + jnp.exp(-x))


def _blk(i, n):
    assert n & (n - 1) == 0
    return i >> (n.bit_length() - 1)


def _off(i, n):
    assert n & (n - 1) == 0
    return i & (n - 1)


def _lower_left(ri, cj, s):
    return (_blk(ri, 2 * s) == _blk(cj, 2 * s)) & (_off(ri, 2 * s) >= s) & (_off(cj, 2 * s) < s)


def _rms(x, g):
    return x * lax.rsqrt(jnp.mean(x * x, axis=-1, keepdims=True) + NORM_EPS) * g


def _ada_kernel(c_ref, w_ref, b_ref, o_ref):
    c = c_ref[...]
    cs = c * _sigmoid(c)
    o_ref[...] = _dot(cs.astype(BF16), w_ref[...].astype(BF16)) + b_ref[...]


def ada_linear(c, w, b):
    L, K, N = w.shape
    M = c.shape[0]
    bn = min(512, N)
    return pl.pallas_call(
        _ada_kernel,
        grid=(L, _tiles(N, bn)),
        in_specs=[pl.BlockSpec((M, K), lambda l, j: (0, 0)),
                  pl.BlockSpec((None, K, bn), lambda l, j: (l, 0, j)),
                  pl.BlockSpec((None, 1, bn), lambda l, j: (l, 0, j))],
        out_specs=pl.BlockSpec((None, M, bn), lambda l, j: (l, 0, j)),
        out_shape=jax.ShapeDtypeStruct((L, M, N), F32),
        compiler_params=_params(("parallel", "parallel")),
        name="ada_linear",
    )(c, w, b.reshape(L, 1, N))


def _rwkv_pre_kernel(*refs, has_v):
    if has_v:
        (x_ref, g_ref, sh_ref, sc_ref, hp_ref, mu_ref, w1_ref, a1_ref, g1_ref, v1_ref,
         xr_ref, xk_ref, xv_ref, tw_ref, av_ref, gg_ref, vv_ref, hl_ref, prev) = refs
    else:
        (x_ref, g_ref, sh_ref, sc_ref, hp_ref, mu_ref, w1_ref, a1_ref, g1_ref,
         xr_ref, xk_ref, xv_ref, tw_ref, av_ref, gg_ref, hl_ref, prev) = refs
    bt = x_ref.shape[0]
    h = _rms(x_ref[...], g_ref[...]) * (1.0 + sc_ref[...]) + sh_ref[...]

    @pl.when(pl.program_id(1) == 0)
    def _():
        prev[...] = hp_ref[...]

    row = lax.broadcasted_iota(jnp.int32, h.shape, 0)
    xx = jnp.where(row == 0, prev[...], pltpu.roll(h, 1, 0)) - h
    last = h[bt - 1:bt, :]
    prev[...] = last
    hl_ref[...] = last
    mu = mu_ref[...]

    def mix(i):
        return (h + xx * mu[i:i + 1, :]).astype(BF16)

    xr_ref[...] = mix(0)
    tw_ref[...] = jnp.tanh(_dot(mix(1), w1_ref[...])).astype(BF16)
    xk_ref[...] = mix(2)
    xv = mix(3)
    xv_ref[...] = xv
    if has_v:
        vv_ref[...] = _dot(xv, v1_ref[...]).astype(BF16)
    av_ref[...] = _dot(mix(4), a1_ref[...]).astype(BF16)
    gg_ref[...] = _sigmoid(_dot(mix(5), g1_ref[...])).astype(BF16)


def rwkv_pre(x, g, shift, scale, h_prev, mu, w1, a1, g1, v1):
    B, T, D = x.shape
    bt = min(512, T)
    has_v = v1 is not None
    row = lambda n: pl.BlockSpec((None, bt, n), lambda b, t: (b, t, 0))
    vec = pl.BlockSpec((None, 1, D), lambda b, t: (b, 0, 0))
    full = lambda a: pl.BlockSpec(a.shape, lambda b, t: (0, 0))
    lora = [w1, a1, g1] + ([v1] if has_v else [])
    outs = [(D, BF16)] * 3 + [(w1.shape[1], BF16), (a1.shape[1], BF16), (g1.shape[1], BF16)]
    if has_v:
        outs.append((v1.shape[1], BF16))
    res = pl.pallas_call(
        functools.partial(_rwkv_pre_kernel, has_v=has_v),
        grid=(B, _tiles(T, bt)),
        in_specs=[row(D), pl.BlockSpec((1, D), lambda b, t: (0, 0)), vec, vec, vec, full(mu)]
                 + [full(a) for a in lora],
        out_specs=[row(n) for n, _ in outs] + [vec],
        out_shape=[jax.ShapeDtypeStruct((B, T, n), dt) for n, dt in outs]
                  + [jax.ShapeDtypeStruct((B, 1, D), F32)],
        scratch_shapes=[pltpu.VMEM((1, D), F32)],
        compiler_params=_params(("parallel", "arbitrary")),
        name="rwkv_pre",
    )(x, g.reshape(1, D), shift.reshape(B, 1, D), scale.reshape(B, 1, D), h_prev.reshape(B, 1, D), mu, *lora)
    return res


def _rwkv_mix_kernel(*refs, C, NC, HP, nT, has_v):
    it = iter(refs)
    xr_ref, xk_ref, xv_ref, tw_ref, av_ref, gg_ref = (next(it) for _ in range(6))
    vv_ref, vf_ref = (next(it), next(it)) if has_v else (None, None)
    wr_ref, wk_ref, wv_ref, w2_ref, a2_ref, g2_ref = (next(it) for _ in range(6))
    v2_ref = next(it) if has_v else None
    pvec_ref, rvec_ref, s0_ref, z_ref, sout_ref = (next(it) for _ in range(5))
    vout_ref = None if has_v else next(it)
    ops_scr, s_scr = next(it), next(it)
    step = pl.program_id(1)
    C2, RB = 2 * C, NC * C

    @pl.when(step == 0)
    def _():
        ops_scr[...] = jnp.zeros(ops_scr.shape, F32)

    r, ld, k, v, kk, b, g = (ops_scr[i] for i in range(7))
    first_of_batch = _off(jnp.maximum(step - 1, 0), nT) == 0
    state = [jnp.where(first_of_batch, s0_ref[p], s_scr[p]) for p in range(HP)]

    lanes = [slice(p * LANES, (p + 1) * LANES) for p in range(HP)]
    same_head = (_blk(lax.broadcasted_iota(jnp.int32, (LANES, LANES), 0), RW_HEAD)
                 == _blk(lax.broadcasted_iota(jnp.int32, (LANES, LANES), 1), RW_HEAD))
    ones_bd = jnp.where(same_head, 1.0, 0.0).astype(BF16)
    pvec = pvec_ref[...]
    PW = 2 * LANES if HP % 2 == 0 else LANES
    ones_pw = jnp.where(_blk(lax.broadcasted_iota(jnp.int32, (PW, PW), 0), RW_HEAD)
                        == _blk(lax.broadcasted_iota(jnp.int32, (PW, PW), 1), RW_HEAD), 1.0, 0.0).astype(BF16)

    def project(q):
        sl = slice(q * PW, (q + 1) * PW)
        w0, a0, v0, k_k, k_a = (pvec[i:i + 1, sl] for i in range(5))
        ops_scr[0, :, sl] = _dot(xr_ref[...], wr_ref[:, sl])
        kraw = _dot(xk_ref[...], wk_ref[:, sl])
        vp = _dot(xv_ref[...], wv_ref[:, sl])
        nz = -(w0 + _dot(tw_ref[...], w2_ref[:, sl]))
        softplus = jnp.maximum(nz, 0.0) + jnp.log(1.0 + jnp.exp(-jnp.abs(nz)))
        ops_scr[1, :, sl] = -jnp.exp(-softplus - 0.5)
        a = _sigmoid(a0 + _dot(av_ref[...], a2_ref[:, sl]))
        if has_v:
            vp = vp + (vf_ref[:, sl] - vp) * _sigmoid(v0 + _dot(vv_ref[...], v2_ref[:, sl]))
        else:
            vout_ref[:, sl] = vp
        ops_scr[3, :, sl] = vp
        ops_scr[6, :, sl] = _dot(gg_ref[...], g2_ref[:, sl])
        kn = kraw * k_k
        kn = kn / jnp.maximum(jnp.sqrt(_dot((kn * kn).astype(BF16), ones_pw)), 1e-12)
        ops_scr[4, :, sl] = kn
        ops_scr[5, :, sl] = kn * a
        ops_scr[2, :, sl] = kraw * (1.0 + (a - 1.0) * k_a)

    n_units = HP * LANES // PW
    n_slots = 2 + (C.bit_length() - 2) + NC
    plan = iter([[u for u in range(n_units) if u * n_slots // n_units == i] for i in range(n_slots)])

    def fill():
        for u in next(plan):
            project(u)

    ti = lax.broadcasted_iota(jnp.int32, (RB, RB), 0)
    tj = lax.broadcasted_iota(jnp.int32, (RB, RB), 1)
    tri = jnp.where((_blk(ti, C) == _blk(tj, C)) & (ti >= tj), 1.0, 0.0).astype(BF16)
    cl = _dot(jnp.concatenate([tri] * 3, axis=1), jnp.concatenate(_split3(ld), axis=0))
    p_in = jnp.exp(cl)
    p_inv = jnp.exp(-cl)
    a_t = -(jnp.exp(cl - ld) * kk)
    r_t = p_in * r
    b_t = p_inv * b
    k_t = p_inv * k
    rows = [slice(c * C, (c + 1) * C) for c in range(NC)]
    cl_end = [cl[(c + 1) * C - 1:(c + 1) * C, :] for c in range(NC)]
    p_rem = [jnp.exp(cl_end[c] - cl[rows[c], :]) for c in range(NC)]
    p_end = [jnp.exp(cl_end[c]) for c in range(NC)]
    b_h = [p_rem[c] * b[rows[c], :] for c in range(NC)]
    k_h = [p_rem[c] * k[rows[c], :] for c in range(NC)]
    rvec = rvec_ref[...]
    r_k, lnx_g, lnx_b = (rvec[i:i + 1, :] for i in range(3))
    rk = r * k * r_k

    first = lax.broadcasted_iota(jnp.int32, (1, LANES), 1) < RW_HEAD

    def stack(x):
        return jnp.concatenate([jnp.where(first, x, 0.0), jnp.where(first, 0.0, x)], axis=0).astype(BF16)

    def fold(x):
        return x[0:C, :] + x[C:C2, :]

    ri = lax.broadcasted_iota(jnp.int32, (C2, C2), 0)
    cj = lax.broadcasted_iota(jnp.int32, (C2, C2), 1)
    same = _blk(ri, C) == _blk(cj, C)
    strict = same & (_off(ri, C) > _off(cj, C))
    incl = same & (_off(ri, C) >= _off(cj, C))
    eye = jnp.where(ri == cj, 1.0, 0.0)

    probs = [(c, p) for c in range(NC) for p in range(HP)]
    a_b = {cp: a_t[rows[cp[0]], lanes[cp[1]]].astype(BF16) for cp in probs}
    r_b = {cp: r_t[rows[cp[0]], lanes[cp[1]]].astype(BF16) for cp in probs}
    v_st = {cp: stack(v[rows[cp[0]], lanes[cp[1]]]) for cp in probs}
    low, rb_f, ak_f, rk_f = {}, {}, {}, {}
    if C2 == LANES:
        t_f64 = lax.broadcasted_iota(jnp.int32, (C, LANES), 0)
        s_f64 = _off(lax.broadcasted_iota(jnp.int32, (C, LANES), 1), C)
        strict_f, incl_f = t_f64 > s_f64, t_f64 >= s_f64
        for c, p in probs:
            q_st = jnp.concatenate([stack(a_t[rows[c], lanes[p]]), stack(r_t[rows[c], lanes[p]])], axis=0)
            w_st = jnp.concatenate([b_t[rows[c], lanes[p]], k_t[rows[c], lanes[p]]], axis=0).astype(BF16)
            gm = _dot_nt(q_st, w_st)
            gr = pltpu.roll(gm, C, 1)
            a0_, a1_, r0_, r1_ = (gm[i * C:(i + 1) * C, :] for i in range(4))
            a0r, a1r, r0r, r1r = (gr[i * C:(i + 1) * C, :] for i in range(4))
            low[c, p] = jnp.concatenate([jnp.where(first & strict_f, a0_, 0.0),
                                         jnp.where(strict_f & ~first, a1r, 0.0)], axis=0)
            ak_f[c, p] = jnp.where(strict_f, jnp.where(first, a0r, a1_), 0.0).astype(BF16)
            rb_f[c, p] = jnp.where(incl_f, jnp.where(first, r0_, r1r), 0.0).astype(BF16)
            rk_f[c, p] = jnp.where(incl_f, jnp.where(first, r0r, r1_), 0.0).astype(BF16)
    else:
        for c, p in probs:
            q_st = jnp.concatenate([stack(a_t[rows[c], lanes[p]]), stack(r_t[rows[c], lanes[p]])], axis=0)
            b_c = b_t[rows[c], lanes[p]].astype(BF16)
            k_c = k_t[rows[c], lanes[p]].astype(BF16)
            gb = _dot_nt(q_st, jnp.concatenate([b_c, b_c], axis=0))
            gk = _dot_nt(q_st, jnp.concatenate([k_c, k_c], axis=0))
            low[c, p] = jnp.where(strict, gb[0:C2, :], 0.0)
            rb_f[c, p] = fold(jnp.where(incl, gb[C2:, :], 0.0)).astype(BF16)
            ak_f[c, p] = fold(jnp.where(strict, gk[0:C2, :], 0.0)).astype(BF16)
            rk_f[c, p] = fold(jnp.where(incl, gk[C2:, :], 0.0)).astype(BF16)
    fill()
    akv = {cp: _dot(ak_f[cp], v_st[cp]) for cp in probs}

    first_level = _lower_left(ri, cj, 1)
    t_inv = {cp: eye + jnp.where(first_level, low[cp], 0.0) for cp in probs}
    s = 2
    while s < C:
        sel = _lower_left(ri, cj, s)
        tb = {cp: t_inv[cp].astype(BF16) for cp in probs}
        ls = {cp: jnp.where(sel, low[cp], 0.0).astype(BF16) for cp in probs}
        if s % 8 == 0:
            groups = range(C2 // (2 * s))

            def lower_rows(m):
                return jnp.concatenate([m[g * 2 * s + s:(g + 1) * 2 * s, :] for g in groups], axis=0)

            mid = {cp: _dot(lower_rows(t_inv[cp]).astype(BF16), ls[cp]).astype(BF16) for cp in probs}
            upd = {cp: _dot(mid[cp], tb[cp]) for cp in probs}
            t_inv = {cp: jnp.concatenate(
                [piece for g in groups for piece in (
                    t_inv[cp][g * 2 * s:g * 2 * s + s, :],
                    t_inv[cp][g * 2 * s + s:(g + 1) * 2 * s, :] + upd[cp][g * s:(g + 1) * s, :])], axis=0)
                for cp in probs}
        else:
            mid = {cp: _dot(tb[cp], ls[cp]).astype(BF16) for cp in probs}
            t_inv = {cp: t_inv[cp] + _dot(mid[cp], tb[cp]) for cp in probs}
        fill()
        s *= 2
    t_f = {cp: fold(t_inv[cp]).astype(BF16) for cp in probs}

    y = {}
    for c in range(NC):
        state_b = [st.astype(BF16) for st in state]
        x = [_dot_nt(a_b[c, p], state_b[p]) + akv[c, p] for p in range(HP)]
        y0 = [_dot_nt(r_b[c, p], state_b[p]) for p in range(HP)]
        u = [_dot(t_f[c, p], stack(x[p])) for p in range(HP)]
        for p in range(HP):
            if C2 == LANES:
                y[c, p] = y0[p] + _dot(jnp.concatenate([rb_f[c, p], rk_f[c, p]], axis=1),
                                       jnp.concatenate([stack(u[p]), v_st[c, p]], axis=0))
            else:
                y[c, p] = y0[p] + _dot(rb_f[c, p], stack(u[p])) + _dot(rk_f[c, p], v_st[c, p])
        upd = []
        for p in range(HP):
            uv = jnp.concatenate([u[p], v[rows[c], lanes[p]]], axis=0).astype(BF16)
            bk = jnp.concatenate([b_h[c][:, lanes[p]], k_h[c][:, lanes[p]]], axis=0).astype(BF16)
            upd.append(_dot_tn(uv, bk))
        state = [state[p] * p_end[c][:, lanes[p]] + jnp.where(same_head, upd[p], 0.0) for p in range(HP)]
        fill()
    for p in range(HP):
        s_scr[p] = state[p]
        sout_ref[p] = state[p]

    def seg(t, slices):
        if slices == 1:
            return _dot(t.astype(BF16), ones_bd)
        return _dot(jnp.concatenate(_split2(t), axis=1), jnp.concatenate([ones_bd, ones_bd], axis=0))

    y_all = [jnp.concatenate([y[c, p] for c in range(NC)], axis=0) for p in range(HP)]
    mean = [seg(y_all[p], 2) * (1.0 / RW_HEAD) for p in range(HP)]
    bonus = [seg(rk[:, lanes[p]], 1) * v[:, lanes[p]] for p in range(HP)]
    fill()
    d = [y_all[p] - mean[p] for p in range(HP)]
    var = [seg(d[p] * d[p], 1) * (1.0 / RW_HEAD) for p in range(HP)]
    for p in range(HP):
        yn = d[p] * lax.rsqrt(var[p] + GN_EPS) * lnx_g[:, lanes[p]] + lnx_b[:, lanes[p]]
        z_ref[:, lanes[p]] = ((yn + bonus[p]) * g[:, lanes[p]]).astype(z_ref.dtype)


def rwkv_mix(acts, v_first, wr, wk, wv, w2, a2, g2, v2, pvecs, rvecs, s0_bd, B, T):
    has_v = v2 is not None
    M, D = acts[0].shape
    HL = min(1024, D)
    HP = _tiles(HL, LANES)
    C = min(CHUNK, T)
    NC = next(n for n in (4, 2, 1) if T % (n * C) == 0)
    RB = NC * C
    nT = _tiles(T, RB)
    assert nT & (nT - 1) == 0
    S = _tiles(M, RB)
    proj_blk = lambda s: jnp.minimum(s, S - 1)
    rec_blk = lambda s: jnp.maximum(s - 1, 0)
    row = lambda a: pl.BlockSpec((RB, a.shape[1]), lambda j, s: (proj_blk(s), 0))
    col = lambda a: pl.BlockSpec((a.shape[0], HL), lambda j, s: (0, j), pipeline_mode=pl.Buffered(1))
    st = pl.BlockSpec((None, HP, LANES, LANES), lambda j, s: (rec_blk(s) // nT, j, 0, 0))
    weights = [wr, wk, wv, w2, a2, g2] + ([v2] if has_v else [])
    in_specs = ([row(a) for a in acts]
                + ([pl.BlockSpec((RB, HL), lambda j, s: (proj_blk(s), j))] if has_v else [])
                + [col(w) for w in weights] + [col(pvecs), col(rvecs), st])
    out_specs = [pl.BlockSpec((RB, HL), lambda j, s: (rec_blk(s), j)), st]
    out_shape = [jax.ShapeDtypeStruct((M, D), BF16), jax.ShapeDtypeStruct(s0_bd.shape, F32)]
    if not has_v:
        out_specs.append(pl.BlockSpec((RB, HL), lambda j, s: (proj_blk(s), j)))
        out_shape.append(jax.ShapeDtypeStruct((M, D), F32))
    res = pl.pallas_call(
        functools.partial(_rwkv_mix_kernel, C=C, NC=NC, HP=HP, nT=nT, has_v=has_v),
        grid=(_tiles(D, HL), S + 1),
        in_specs=in_specs, out_specs=out_specs, out_shape=out_shape,
        scratch_shapes=[pltpu.VMEM((7, RB, HL), F32), pltpu.VMEM((HP, LANES, LANES), F32)],
        compiler_params=_params(("parallel", "arbitrary")),
        name="rwkv_mix",
    )(*acts, *([v_first] if has_v else []), *weights, pvecs, rvecs, s0_bd)
    return res


def _gate_operand(gate, B, T, bm, bn):
    N = gate.shape[1]
    col = (lambda j: j) if bn < N else (lambda j: 0)
    if T % bm == 0:
        return gate.reshape(B, 1, N), pl.BlockSpec((None, 1, bn), lambda i, j: ((i * bm) // T, 0, col(j)))
    rows = jnp.broadcast_to(gate[:, None, :], (B, T, N)).reshape(B * T, N)
    return rows, pl.BlockSpec((bm, bn), lambda i, j: (i, col(j)))


def _matmul_res_kernel(a_ref, w_ref, x_ref, gate_ref, o_ref):
    o_ref[...] = x_ref[...] + gate_ref[...] * _dot(a_ref[...], w_ref[...])


def matmul_res(a, w, x, gate, B, T):
    M, K = a.shape
    N = w.shape[1]
    bm, bn = min(1024, M), min(1024 if K <= 2048 else 512, N)
    gate_arr, gate_spec = _gate_operand(gate, B, T, bm, bn)
    return pl.pallas_call(
        _matmul_res_kernel,
        grid=(_tiles(M, bm), _tiles(N, bn)),
        in_specs=[pl.BlockSpec((bm, K), lambda i, j: (i, 0)),
                  pl.BlockSpec((K, bn), lambda i, j: (0, j)),
                  pl.BlockSpec((bm, bn), lambda i, j: (i, j)), gate_spec],
        out_specs=pl.BlockSpec((bm, bn), lambda i, j: (i, j)),
        out_shape=jax.ShapeDtypeStruct((M, N), F32),
        compiler_params=_params(("parallel", "arbitrary")),
        name="matmul_res",
    )(a, w, x, gate_arr)


def _mlp_kernel(x_ref, g_ref, sh_ref, sc_ref, gate_ref, w1_ref, w2_ref, fg_ref, o_ref, h_scr, acc, *, final):
    f = pl.program_id(1)
    bm = x_ref.shape[0]

    def rows_of(ref, rs):
        return ref[...] if ref.shape[0] == 1 else ref[rs, :]

    def mlp(h):
        return _dot(jnp.square(jnp.maximum(_dot(h, w1_ref[...]), 0.0)).astype(BF16), w2_ref[...])

    @pl.when(f == 0)
    def _():
        parts = 4 if bm % 64 == 0 else 1
        for c in range(parts):
            rs = slice(c * bm // parts, (c + 1) * bm // parts)
            h = (_rms(x_ref[rs, :], g_ref[...]) * (1.0 + rows_of(sc_ref, rs)) + rows_of(sh_ref, rs)).astype(BF16)
            h_scr[rs, :] = h
            acc[rs, :] = mlp(h)

    @pl.when(f > 0)
    def _():
        acc[...] += mlp(h_scr[...])

    @pl.when(f == pl.num_programs(1) - 1)
    def _():
        y = x_ref[...] + gate_ref[...] * acc[...]
        o_ref[...] = _rms(y, fg_ref[...]) if final else y


def mlp_res(x, g, shift, scale, gate, w1, w2, layer, final_g, final, B, T):
    M, D = x.shape
    F = w1.shape[2]
    bm = 512 if T % 512 == 0 else min(256, M)
    bf = min(1024, F)
    (sh_arr, vec_spec), (sc_arr, _), (gate_arr, _) = (_gate_operand(v, B, T, bm, D) for v in (shift, scale, gate))
    return pl.pallas_call(
        functools.partial(_mlp_kernel, final=final),
        grid=(_tiles(M, bm), _tiles(F, bf)),
        in_specs=[pl.BlockSpec((bm, D), lambda i, f: (i, 0)),
                  pl.BlockSpec((1, D), lambda i, f: (0, 0)), vec_spec, vec_spec, vec_spec,
                  pl.BlockSpec((None, D, bf), lambda i, f: (layer, 0, f)),
                  pl.BlockSpec((None, bf, D), lambda i, f: (layer, f, 0)),
                  pl.BlockSpec((1, D), lambda i, f: (0, 0))],
        out_specs=pl.BlockSpec((bm, D), lambda i, f: (i, 0)),
        out_shape=jax.ShapeDtypeStruct((M, D), F32),
        scratch_shapes=[pltpu.VMEM((bm, D), BF16), pltpu.VMEM((bm, D), F32)],
        compiler_params=_params(("parallel", "arbitrary")),
        name="mlp_res",
    )(x, g.reshape(1, D), sh_arr, sc_arr, gate_arr, w1, w2, final_g.reshape(1, D))


def _rope_pair(t, tab):
    prod = t * tab
    return (prod + pltpu.roll(prod, ROPE_DIM, 1))[:, :ROPE_DIM]


def _latent_kernel(x_ref, gx_ref, sh_ref, sc_ref, w_ref, g_ref, tab_ref, ckv_ref, kpe_ref):
    h = (_rms(x_ref[...], gx_ref[...]) * (1.0 + sc_ref[...]) + sh_ref[...]).astype(BF16)
    acc = _dot(h, w_ref[...])
    R = g_ref.shape[1]
    ckv_ref[...] = _rms(acc[:, :R], g_ref[...])
    kpe_ref[...] = _rope_pair(acc[:, R:R + 2 * ROPE_DIM], tab_ref[...])


def latent(x, gx, shift, scale, w, g, tab):
    B, T, D = x.shape
    R = g.shape[0]
    bt = min(512, T)
    vec = pl.BlockSpec((None, 1, D), lambda b, t: (b, 0, 0))
    return pl.pallas_call(
        _latent_kernel,
        grid=(B, _tiles(T, bt)),
        in_specs=[pl.BlockSpec((None, bt, D), lambda b, t: (b, t, 0)),
                  pl.BlockSpec((1, D), lambda b, t: (0, 0)), vec, vec,
                  pl.BlockSpec(w.shape, lambda b, t: (0, 0)),
                  pl.BlockSpec((1, R), lambda b, t: (0, 0)),
                  pl.BlockSpec((bt, 2 * ROPE_DIM), lambda b, t: (t, 0))],
        out_specs=[pl.BlockSpec((None, bt, R), lambda b, t: (b, t, 0)),
                   pl.BlockSpec((None, bt, ROPE_DIM), lambda b, t: (b, t, 0))],
        out_shape=[jax.ShapeDtypeStruct((B, T, R), F32), jax.ShapeDtypeStruct((B, T, ROPE_DIM), F32)],
        compiler_params=_params(("parallel", "parallel")),
        name="latent",
    )(x, gx.reshape(1, D), shift.reshape(B, 1, D), scale.reshape(B, 1, D), w, g.reshape(1, R), tab)


def _kv_expand_kernel(c_ref, pe_ref, wuk_ref, wuv_ref, k_ref, vt_ref):
    c = c_ref[...].astype(BF16)
    kn = _dot(c, wuk_ref[...])
    vv = _dot(c, wuv_ref[...])
    pe = pe_ref[...].astype(BF16)
    for h in range(k_ref.shape[0]):
        k_ref[h, :, 0:NOPE_DIM] = kn[:, h * NOPE_DIM:(h + 1) * NOPE_DIM].astype(BF16)
        k_ref[h, :, NOPE_DIM:NOPE_DIM + ROPE_DIM] = pe
        vt_ref[h, 0:V_DIM, :] = vv[:, h * V_DIM:(h + 1) * V_DIM].T.astype(BF16)
        vt_ref[h, V_DIM:V_ROWS, :] = jnp.ones((V_ROWS - V_DIM, c.shape[0]), BF16)


def kv_expand(ckv, kpe, wuk, wuv, H):
    B, S, R = ckv.shape
    bt = min(1024, S)
    hb = min(8, H)
    DK = NOPE_DIM + ROPE_DIM
    return pl.pallas_call(
        _kv_expand_kernel,
        grid=(_tiles(H, hb), B, _tiles(S, bt)),
        in_specs=[pl.BlockSpec((None, bt, R), lambda j, b, t: (b, t, 0)),
                  pl.BlockSpec((None, bt, ROPE_DIM), lambda j, b, t: (b, t, 0)),
                  pl.BlockSpec((R, hb * NOPE_DIM), lambda j, b, t: (0, j)),
                  pl.BlockSpec((R, hb * V_DIM), lambda j, b, t: (0, j))],
        out_specs=[pl.BlockSpec((None, hb, bt, DK), lambda j, b, t: (b, j, t, 0)),
                   pl.BlockSpec((None, hb, V_ROWS, bt), lambda j, b, t: (b, j, 0, t))],
        out_shape=[jax.ShapeDtypeStruct((B, H, S, DK), BF16), jax.ShapeDtypeStruct((B, H, V_ROWS, S), BF16)],
        compiler_params=_params(("parallel", "parallel", "arbitrary")),
        name="kv_expand",
    )(ckv, kpe, wuk, wuv)


def _wdq_kernel(x_ref, gx_ref, sh_ref, sc_ref, w_ref, g_ref, o_ref):
    h = (_rms(x_ref[...], gx_ref[...]) * (1.0 + sc_ref[...]) + sh_ref[...]).astype(BF16)
    o_ref[...] = _rms(_dot(h, w_ref[...]), g_ref[...]).astype(BF16)


def wdq_norm(x, gx, shift, scale, w, g):
    B, T, D = x.shape
    R = w.shape[1]
    bt = min(1024, T)
    vec = pl.BlockSpec((None, 1, D), lambda b, t: (b, 0, 0))
    return pl.pallas_call(
        _wdq_kernel,
        grid=(B, _tiles(T, bt)),
        in_specs=[pl.BlockSpec((None, bt, D), lambda b, t: (b, t, 0)),
                  pl.BlockSpec((1, D), lambda b, t: (0, 0)), vec, vec,
                  pl.BlockSpec((D, R), lambda b, t: (0, 0)),
                  pl.BlockSpec((1, R), lambda b, t: (0, 0))],
        out_specs=pl.BlockSpec((None, bt, R), lambda b, t: (b, t, 0)),
        out_shape=jax.ShapeDtypeStruct((B, T, R), BF16),
        compiler_params=_params(("parallel", "parallel")),
        name="wdq_norm",
    )(x, gx.reshape(1, D), shift.reshape(B, 1, D), scale.reshape(B, 1, D), w, g.reshape(1, R))


def _wuq_kernel(c_ref, w_ref, tab_ref, q_ref, *, scale):
    acc = _dot(c_ref[...], w_ref[...])
    tab = tab_ref[...]
    W = NOPE_DIM + 2 * ROPE_DIM
    for h in range(q_ref.shape[0]):
        q_ref[h, :, 0:NOPE_DIM] = (acc[:, h * W:h * W + NOPE_DIM] * scale).astype(BF16)
        pe = _rope_pair(acc[:, h * W + NOPE_DIM:(h + 1) * W], tab)
        q_ref[h, :, NOPE_DIM:NOPE_DIM + ROPE_DIM] = (pe * scale).astype(BF16)
        if q_ref.shape[2] > NOPE_DIM + ROPE_DIM:
            pad = q_ref.shape[2] - NOPE_DIM - ROPE_DIM
            q_ref[h, :, NOPE_DIM + ROPE_DIM:] = jnp.zeros((q_ref.shape[1], pad), BF16)


def wuq_rope(cq, w, tab, H, head_major):
    B, T, R = cq.shape
    bt = min(1024, T)
    nt = _tiles(T, bt)
    hb = min(8, H)
    W = NOPE_DIM + 2 * ROPE_DIM
    DK = NOPE_DIM + ROPE_DIM
    if head_major:
        out_spec = pl.BlockSpec((hb, bt, DK), lambda j, b, t: (j, b * nt + t, 0))
        out_shape = (H, B * T, DK)
    else:
        out_spec = pl.BlockSpec((None, hb, bt, 2 * LANES), lambda j, b, t: (b, j, t, 0))
        out_shape = (B, H, T, 2 * LANES)
    return pl.pallas_call(
        functools.partial(_wuq_kernel, scale=MLA_SCALE if head_major else MLA_SCALE * LOG2E),
        grid=(_tiles(H, hb), B, nt),
        in_specs=[pl.BlockSpec((None, bt, R), lambda j, b, t: (b, t, 0)),
                  pl.BlockSpec((R, hb * W), lambda j, b, t: (0, j)),
                  pl.BlockSpec((bt, 2 * ROPE_DIM), lambda j, b, t: (t, 0))],
        out_specs=out_spec,
        out_shape=jax.ShapeDtypeStruct(out_shape, BF16),
        compiler_params=_params(("parallel", "parallel", "parallel")),
        name="wuq_rope",
    )(cq, w, tab)


def _visible(qpos, kpos):
    return _blk(kpos, CHUNK) <= _blk(qpos, CHUNK)


def _flash_kernel(q_ref, k_ref, vt_ref, o_ref, *scratch, tile, q_tiles):
    for t in range(q_tiles):
        rows = pl.ds(t * tile, tile)
        _flash_query_tile(pl.program_id(2) * q_tiles + t, q_ref.at[:, rows, :], k_ref, vt_ref, o_ref.at[rows, :],
                          *scratch, tile=tile)


def _flash_query_tile(qi, q_ref, k_ref, vt_ref, o_ref, qt_scr, s_a, s_b, m_scr, acc_scr, *, tile):
    hb = q_ref.shape[0]
    ha = hb // 2
    first, second = range(0, ha), range(ha, hb)
    qpos = qi * tile + lax.broadcasted_iota(jnp.int32, (1, tile), 1)
    m_scr[...] = jnp.full(m_scr.shape, -jnp.inf, F32)
    acc_scr[...] = jnp.zeros(acc_scr.shape, F32)
    DK = NOPE_DIM + ROPE_DIM
    for h in range(hb):
        qf = q_ref[h].astype(F32)
        qt_scr[h, 0:LANES, :] = qf[:, 0:LANES].T.astype(BF16)
        qt_scr[h, LANES:DK, :] = qf[:, LANES:2 * LANES].T[0:DK - LANES, :].astype(BF16)

    def scores(ki, heads, s_ref):
        start = pl.multiple_of(ki * tile, tile)
        for h in heads:
            s_ref[h - heads[0]] = _dot(k_ref[h, pl.ds(start, tile), :], qt_scr[h])

    def softmax_pv(ki, heads, s_ref, masked):
        start = pl.multiple_of(ki * tile, tile)
        for h in heads:
            s = s_ref[h - heads[0]]
            if masked:
                kpos = ki * tile + lax.broadcasted_iota(jnp.int32, (tile, 1), 0)
                s = jnp.where(_visible(qpos, kpos), s, -jnp.inf)
            m = m_scr[h]
            m_new = jnp.maximum(m, jnp.max(s, axis=0, keepdims=True))
            p = jnp.exp2(s - m_new).astype(BF16)
            acc_scr[h] = jnp.exp2(m - m_new) * acc_scr[h] + _dot(vt_ref[h, :, pl.ds(start, tile)], p)
            m_scr[h] = m_new

    def one_tile(ki):
        scores(ki, second, s_b)
        softmax_pv(ki, first, s_a, False)
        scores(ki + 1, first, s_a)
        softmax_pv(ki, second, s_b, False)

    def two_tiles(kk, carry):
        one_tile(2 * kk)
        one_tile(2 * kk + 1)
        return carry

    scores(0, first, s_a)
    lax.fori_loop(0, qi // 2, two_tiles, 0)

    @pl.when(qi % 2 == 1)
    def _():
        one_tile(qi - 1)

    scores(qi, second, s_b)
    softmax_pv(qi, first, s_a, True)
    softmax_pv(qi, second, s_b, True)
    for h in range(hb):
        acc = acc_scr[h]
        inv_l = 1.0 / acc[V_DIM:V_DIM + 1, :]
        o_ref[:, h * V_DIM:(h + 1) * V_DIM] = (acc[0:V_DIM, :] * inv_l).T.astype(o_ref.dtype)


def flash_prompt(q, k, vt):
    B, H, T, QW = q.shape
    DK = k.shape[3]
    tile = min(256, T)
    assert tile % CHUNK == 0 and T % tile == 0 and H % 2 == 0 and QW == 2 * LANES
    hb = next(n for n in (8, 4, 2) if H % n == 0)
    q_tiles = next(n for n in (4, 2, 1) if (T // tile) % n == 0)
    rows = q_tiles * tile
    return pl.pallas_call(
        functools.partial(_flash_kernel, tile=tile, q_tiles=q_tiles),
        grid=(B, _tiles(H, hb), _tiles(T, rows)),
        in_specs=[pl.BlockSpec((None, hb, rows, QW), lambda b, j, i: (b, j, i, 0)),
                  pl.BlockSpec((None, hb, T, DK), lambda b, j, i: (b, j, 0, 0)),
                  pl.BlockSpec((None, hb, V_ROWS, T), lambda b, j, i: (b, j, 0, 0))],
        out_specs=pl.BlockSpec((None, rows, hb * V_DIM), lambda b, j, i: (b, i, j)),
        out_shape=jax.ShapeDtypeStruct((B, T, H * V_DIM), BF16),
        scratch_shapes=[pltpu.VMEM((hb, DK, tile), BF16),
                        pltpu.VMEM((hb // 2, tile, tile), F32), pltpu.VMEM((hb // 2, tile, tile), F32),
                        pltpu.VMEM((hb, 1, tile), F32), pltpu.VMEM((hb, V_ROWS, tile), F32)],
        compiler_params=_params(("parallel", "parallel", "arbitrary")),
        name="flash_prompt",
    )(q, k, vt)


def _q_absorb_kernel(q_ref, wuk_ref, o_ref):
    o_ref[...] = _dot_nt(q_ref[:, 0:NOPE_DIM], wuk_ref[...]).astype(BF16)


def _attn_latent_kernel(ql_ref, q_ref, qpos_ref, c_ref, pe_ref, o_ref):
    H, T, R = ql_ref.shape
    S = c_ref.shape[0]
    c = c_ref[...].astype(BF16)
    s = (_dot_nt(ql_ref[...].reshape(H * T, R), c)
         + _dot_nt(q_ref[:, :, NOPE_DIM:NOPE_DIM + ROPE_DIM].reshape(H * T, ROPE_DIM), pe_ref[...].astype(BF16)))
    kpos = lax.broadcasted_iota(jnp.int32, (1, S), 1)
    s = jnp.where(_visible(qpos_ref[...], kpos), s, -jnp.inf)
    p = jnp.exp(s - jnp.max(s, axis=-1, keepdims=True))
    l = jnp.sum(p, axis=-1, keepdims=True)
    o_ref[...] = (_dot(p.astype(BF16), c) / l).astype(BF16).reshape(H, T, R)


def _o_expand_kernel(o_ref, wuv_ref, out_ref):
    out_ref[...] = _dot(o_ref[...], wuv_ref[...]).astype(BF16)


def attn_latent(q, ckv, kpe, wuk, wuv, B, T, q0):
    H, M, DK = q.shape
    S, R = ckv.shape[1:]
    q_lat = pl.pallas_call(
        _q_absorb_kernel,
        grid=(H,),
        in_specs=[pl.BlockSpec((None, M, DK), lambda h: (h, 0, 0)),
                  pl.BlockSpec((R, NOPE_DIM), lambda h: (0, h))],
        out_specs=pl.BlockSpec((None, M, R), lambda h: (h, 0, 0)),
        out_shape=jax.ShapeDtypeStruct((H, M, R), BF16),
        compiler_params=_params(("parallel",)),
        name="q_absorb",
    )(q, wuk)
    qpos = jnp.tile(q0 + jnp.arange(T, dtype=jnp.int32), H)[:, None]
    o_lat = pl.pallas_call(
        _attn_latent_kernel,
        grid=(B,),
        in_specs=[pl.BlockSpec((H, T, R), lambda b: (0, b, 0)),
                  pl.BlockSpec((H, T, DK), lambda b: (0, b, 0)),
                  pl.BlockSpec((H * T, 1), lambda b: (0, 0)),
                  pl.BlockSpec((None, S, R), lambda b: (b, 0, 0)),
                  pl.BlockSpec((None, S, ROPE_DIM), lambda b: (b, 0, 0))],
        out_specs=pl.BlockSpec((H, T, R), lambda b: (0, b, 0)),
        out_shape=jax.ShapeDtypeStruct((H, M, R), BF16),
        compiler_params=_params(("parallel",)),
        name="attn_latent",
    )(q_lat, q, qpos, ckv, kpe)
    return pl.pallas_call(
        _o_expand_kernel,
        grid=(H,),
        in_specs=[pl.BlockSpec((None, M, R), lambda h: (h, 0, 0)),
                  pl.BlockSpec((R, V_DIM), lambda h: (0, h))],
        out_specs=pl.BlockSpec((M, V_DIM), lambda h: (0, h)),
        out_shape=jax.ShapeDtypeStruct((M, H * V_DIM), BF16),
        compiler_params=_params(("parallel",)),
        name="o_expand",
    )(o_lat, wuv)


def _pad_cols(w, n):
    return jnp.pad(w, ((0, 0), (0, n - w.shape[1])))


def _pad_rows(w, n):
    return jnp.pad(w, ((0, n - w.shape[0]), (0, 0)))


def _rotate_half_cols(w):
    half = ROPE_DIM // 2
    return jnp.concatenate([-w[..., half:], w[..., :half]], axis=-1)


def _rope_table(pos):
    half = ROPE_DIM // 2
    inv = ROPE_THETA ** (-jnp.arange(half, dtype=F32) / half)
    ang = pos.astype(F32)[:, None] * inv[None, :]
    cos, sin = jnp.cos(ang), jnp.sin(ang)
    return jnp.concatenate([cos, cos, sin, sin], axis=-1)


def _block_diag_states(s):
    B, H, N, _ = s.shape
    s = s.reshape(B, H // 2, 2, N, N)
    z = jnp.zeros_like(s[:, :, 0])
    top = jnp.concatenate([s[:, :, 0], z], axis=-1)
    bot = jnp.concatenate([z, s[:, :, 1]], axis=-1)
    return jnp.concatenate([top, bot], axis=-2)


def _diag_states(s):
    B, HP = s.shape[:2]
    N = RW_HEAD
    return jnp.stack([s[:, :, :N, :N], s[:, :, N:, N:]], axis=2).reshape(B, 2 * HP, N, N)


def _prepare(W):
    D = W['rw_wr'].shape[1]
    P = {}
    bf = lambda a: a.astype(BF16)
    P['mlp_w1'], P['mlp_w2'] = bf(W['mlp_w1']), bf(W['mlp_w2'])
    for n in ('rw_wr', 'rw_wk', 'rw_wv', 'rw_wo', 'rw_g1', 'rw_g2', 'mla_wdq', 'mla_wo'):
        P[n] = [bf(W[n][l]) for l in range(W[n].shape[0])]
    NA = W['rw_wr'].shape[0]
    P['rw_w1'] = [bf(_pad_cols(W['rw_w1'][l], LORA_PAD)) for l in range(NA)]
    P['rw_w2'] = [bf(_pad_rows(W['rw_w2'][l], LORA_PAD)) for l in range(NA)]
    P['rw_a1'] = [bf(_pad_cols(W['rw_a1'][l], LORA_PAD)) for l in range(NA)]
    P['rw_a2'] = [bf(_pad_rows(W['rw_a2'][l], LORA_PAD)) for l in range(NA)]
    P['rw_v1'] = [bf(_pad_cols(W['rw_v1'][l], LORA_PAD)) for l in range(NA - 1)]
    P['rw_v2'] = [bf(_pad_rows(W['rw_v2'][l], LORA_PAD)) for l in range(NA - 1)]
    zeros = jnp.zeros((D,), F32)
    P['proj_vecs'], P['recur_vecs'] = [], []
    for l in range(NA):
        v0 = W['rw_v0'][l - 1] if l > 0 else zeros
        P['proj_vecs'].append(jnp.stack([W['rw_w0'][l], W['rw_a0'][l], v0, W['rw_kk'][l], W['rw_ka'][l],
                                         zeros, zeros, zeros]))
        P['recur_vecs'].append(jnp.stack([W['rw_rk'][l], W['rw_lnx_g'][l], W['rw_lnx_b'][l]] + [zeros] * 5))
    R = W['kv_lat_g'].shape[0]
    wd = W['kv_wd']
    P['kv_wd'] = bf(jnp.concatenate([wd, _rotate_half_cols(wd[:, R:])], axis=1))
    H = W['kv_wuk'].shape[1]
    P['kv_wuk'] = bf(W['kv_wuk'].reshape(R, H * NOPE_DIM))
    P['kv_wuv'] = bf(W['kv_wuv'].reshape(R, H * V_DIM))
    NB, Q = W['mla_wuq'].shape[:2]
    wuq = W['mla_wuq'].reshape(NB, Q, H, NOPE_DIM + ROPE_DIM)
    pe = wuq[..., NOPE_DIM:]
    P['mla_wuq'] = bf(jnp.concatenate([wuq, _rotate_half_cols(pe)], axis=-1).reshape(NB, Q, -1))
    return P


def _trunk(x, mod, kv_mod, pos0, h_prev, s0, past_ckv, past_kpe, W, P):
    B, T, D = x.shape
    M = B * T
    depth = W['ada_w'].shape[0]
    NA = W['rw_wr'].shape[0]
    H = W['kv_wuk'].shape[1]
    tab = _rope_table(pos0 + jnp.arange(T))
    xf = x.reshape(M, D)
    shifts, states = [], []
    v_first = None
    keys = vals = ckv = kpe = None
    for l in range(depth):
        m = mod[l]
        if l < NA:
            has_v = l > 0
            pre = rwkv_pre(xf.reshape(B, T, D), W['norm_mix_g'][l], m[:, 0], m[:, 1], h_prev[l], W['rw_mu'][l],
                           P['rw_w1'][l], P['rw_a1'][l], P['rw_g1'][l], P['rw_v1'][l - 1] if has_v else None)
            shifts.append(pre[-1].reshape(B, D))
            acts = [a.reshape(M, a.shape[-1]) for a in pre[:-1]]
            res = rwkv_mix(acts, v_first, P['rw_wr'][l], P['rw_wk'][l], P['rw_wv'][l], P['rw_w2'][l],
                           P['rw_a2'][l], P['rw_g2'][l], P['rw_v2'][l - 1] if has_v else None,
                           P['proj_vecs'][l], P['recur_vecs'][l], _block_diag_states(s0[l]), B, T)
            z, s_fin = res[0], res[1]
            if not has_v:
                v_first = res[2]
            states.append(_diag_states(s_fin))
            xf = matmul_res(z, P['rw_wo'][l], xf, m[:, 2], B, T)
        else:
            j = l - NA
            cq = wdq_norm(xf.reshape(B, T, D), W['norm_mix_g'][l], m[:, 0], m[:, 1], P['mla_wdq'][j], W['mla_q_g'][j])
            if past_ckv is None:
                q = wuq_rope(cq, P['mla_wuq'][j], tab, H, False)
                o = flash_prompt(q, keys, vals).reshape(M, H * V_DIM)
            else:
                q = wuq_rope(cq.reshape(1, M, -1), P['mla_wuq'][j], jnp.tile(tab, (B, 1)), H, True)
                o = attn_latent(q, keys, vals, P['kv_wuk'], P['kv_wuv'], B, T, pos0)
            xf = matmul_res(o, P['mla_wo'][j], xf, m[:, 2], B, T)
        xf = mlp_res(xf, W['norm_mlp_g'][l], m[:, 3], m[:, 4], m[:, 5], P['mlp_w1'], P['mlp_w2'], l,
                     W['final_g'], l == depth - 1, B, T)
        if l == NA - 1:
            ckv, kpe = latent(xf.reshape(B, T, D), W['kv_norm_g'], kv_mod[:, 0], kv_mod[:, 1],
                              P['kv_wd'], W['kv_lat_g'], tab)
            if past_ckv is None:
                keys, vals = kv_expand(ckv, kpe, P['kv_wuk'], P['kv_wuv'], H)
            else:
                keys = jnp.concatenate([past_ckv, ckv], axis=1)
                vals = jnp.concatenate([past_kpe, kpe], axis=1)
    return xf.reshape(B, T, D), ckv, kpe, jnp.stack(states), jnp.stack(shifts)


def kernel(x_prompt, x_sample, cache_ckv, cache_kpe, state_wkv, state_shift, c_prompt, c_sample, ada_w, ada_b, norm_mix_g, norm_mlp_g, mlp_w1, mlp_w2, rw_mu, rw_w0, rw_w1, rw_w2, rw_a0, rw_a1, rw_a2, rw_v0, rw_v1, rw_v2, rw_g1, rw_g2, rw_wr, rw_wk, rw_wv, rw_wo, rw_kk, rw_ka, rw_rk, rw_lnx_g, rw_lnx_b, kv_ada_w, kv_ada_b, kv_norm_g, kv_wd, kv_lat_g, kv_wuk, kv_wuv, mla_wdq, mla_q_g, mla_wuq, mla_wo, final_g):
    W = dict(ada_w=ada_w, ada_b=ada_b, norm_mix_g=norm_mix_g, norm_mlp_g=norm_mlp_g,
             mlp_w1=mlp_w1, mlp_w2=mlp_w2, rw_mu=rw_mu, rw_w0=rw_w0, rw_w1=rw_w1, rw_w2=rw_w2,
             rw_a0=rw_a0, rw_a1=rw_a1, rw_a2=rw_a2, rw_v0=rw_v0, rw_v1=rw_v1, rw_v2=rw_v2,
             rw_g1=rw_g1, rw_g2=rw_g2, rw_wr=rw_wr, rw_wk=rw_wk, rw_wv=rw_wv, rw_wo=rw_wo,
             rw_kk=rw_kk, rw_ka=rw_ka, rw_rk=rw_rk, rw_lnx_g=rw_lnx_g, rw_lnx_b=rw_lnx_b,
             kv_ada_w=kv_ada_w, kv_ada_b=kv_ada_b, kv_norm_g=kv_norm_g, kv_wd=kv_wd,
             kv_lat_g=kv_lat_g, kv_wuk=kv_wuk, kv_wuv=kv_wuv, mla_wdq=mla_wdq, mla_q_g=mla_q_g,
             mla_wuq=mla_wuq, mla_wo=mla_wo, final_g=final_g)
    P = _prepare(W)
    Bp, Tp, D = x_prompt.shape
    Bs = x_sample.shape[0]
    depth = ada_w.shape[0]
    NA = rw_wr.shape[0]
    c_all = jnp.concatenate([c_prompt, c_sample], axis=0)
    mod = ada_linear(c_all, ada_w, ada_b).reshape(depth, Bp + Bs, N_MOD, D)
    kv_mod = ada_linear(c_all, kv_ada_w[None], kv_ada_b[None]).reshape(Bp + Bs, 2, D)
    h0 = jnp.zeros((NA, Bp, D), F32)
    s0 = jnp.zeros((NA, Bp, D // RW_HEAD, RW_HEAD, RW_HEAD), F32)
    out_p = _trunk(x_prompt, mod[:, :Bp], kv_mod[:Bp], 0, h0, s0, None, None, W, P)
    out_s = _trunk(x_sample, mod[:, Bp:], kv_mod[Bp:], cache_ckv.shape[1], state_shift, state_wkv,
                   cache_ckv, cache_kpe, W, P)
    return (out_p[0], out_s[0]) + out_p[1:] + out_s[1:]
```
